```python
import math
import jax
import jax.numpy as jnp
from jax import lax
import numpy as np

D_MODEL = 1024
BATCH = 8
SEQ = 2048
DEPTH = 4

CHUNK = 64
N_META = 16
RMS_EPS = 1e-6
F32 = jnp.float32

N_AB_LAYERS = (DEPTH + 1) // 2
N_C_LAYERS = DEPTH // 2

S5_WIDTH = D_MODEL // 2
S5_GROUP = 16
S5_GROUPS = S5_WIDTH // S5_GROUP
S5_STATE = 64
S5_DT_MIN = 1e-3
S5_DT_MAX = 1e-1

LRU_WIDTH = D_MODEL // 2
LRU_HEADS = 8
LRU_HEAD_DIM = LRU_WIDTH // LRU_HEADS
LRU_CONV = 4
LRU_C = 8.0

AB_IN = S5_WIDTH + 2 * LRU_WIDTH
AB_MIX = S5_WIDTH + LRU_WIDTH

RWKV_HEAD = 64
RWKV_HEADS = D_MODEL // RWKV_HEAD
DECAY_LORA = 64
AAA_LORA = 64
MV_LORA = 32
GATE_LORA = 128
LNX_EPS = 64e-5

N_GROUPS = 4
EXPERTS_PER_GROUP = 4
N_EXPERTS = N_GROUPS * EXPERTS_PER_GROUP
TOP_K_IN_GROUP = 2
D_EXPERT = 512

kernel_name = 'hybrid_s5_rglru_rwkv7_hmoe_encoder'


def _rmsnorm(x, g):
    xf = x.astype(F32)
    return xf * lax.rsqrt(jnp.mean(xf * xf, axis=-1, keepdims=True) + RMS_EPS) * g.astype(F32)


def _linear_combine(left, right):
    a1, b1 = left
    a2, b2 = right
    return (a2 * a1, a2 * b1 + b2)


def _complex_linear_combine(left, right):
    a1r, a1i, b1r, b1i = left
    a2r, a2i, b2r, b2i = right
    return (a2r * a1r - a2i * a1i, a2r * a1i + a2i * a1r,
            a2r * b1r - a2i * b1i + b2r, a2r * b1i + a2i * b1r + b2i)


def _s5_mixer(u, lam_re, lam_im, log_dt, b_re, b_im, c_re, c_im, d_skip, w_glu, b_glu):
    bsz, L, _ = u.shape
    ug = u.reshape(bsz, L, S5_GROUPS, S5_GROUP)
    lr = lam_re.astype(F32)
    li = lam_im.astype(F32)
    dt = jnp.exp(log_dt.astype(F32))[:, None]
    mag = jnp.exp(lr * dt)
    abar_r = mag * jnp.cos(li * dt)
    abar_i = mag * jnp.sin(li * dt)
    den = lr * lr + li * li
    zr = ((abar_r - 1.0) * lr + abar_i * li) / den
    zi = (abar_i * lr - (abar_r - 1.0) * li) / den
    bbar_r = zr[..., None] * b_re - zi[..., None] * b_im
    bbar_i = zr[..., None] * b_im + zi[..., None] * b_re
    bu_r = jnp.einsum('blgh,gph->blgp', ug, bbar_r)
    bu_i = jnp.einsum('blgh,gph->blgp', ug, bbar_i)
    a_r = jnp.broadcast_to(abar_r, (1, L, S5_GROUPS, S5_STATE))
    a_i = jnp.broadcast_to(abar_i, (1, L, S5_GROUPS, S5_STATE))
    _, _, x_r, x_i = lax.associative_scan(_complex_linear_combine, (a_r, a_i, bu_r, bu_i), axis=1)
    y = (jnp.einsum('blgp,ghp->blgh', x_r, c_re) - jnp.einsum('blgp,ghp->blgh', x_i, c_im)
         + d_skip * ug)
    y = jax.nn.gelu(y.reshape(bsz, L, S5_WIDTH))
    return y * jax.nn.sigmoid(y @ w_glu + b_glu)


def _rglru_mixer(gate_in, rec_in, conv_w, conv_b, w_a, b_a, w_x, b_x, lam):
    bsz, L, _ = rec_in.shape
    xc = lax.conv_general_dilated(rec_in.astype(F32), conv_w.astype(F32)[:, None, :],
                                  window_strides=(1,), padding=[(LRU_CONV - 1, 0)],
                                  dimension_numbers=('NWC', 'WIO', 'NWC'),
                                  feature_group_count=LRU_WIDTH) + conv_b
    xh = xc.reshape(bsz, L, LRU_HEADS, LRU_HEAD_DIM)
    r = jax.nn.sigmoid(jnp.einsum('blhi,hij->blhj', xh, w_a) + b_a).reshape(bsz, L, LRU_WIDTH)
    i = jax.nn.sigmoid(jnp.einsum('blhi,hij->blhj', xh, w_x) + b_x).reshape(bsz, L, LRU_WIDTH)
    log_a = -LRU_C * r * jax.nn.softplus(-lam.astype(F32))
    a = jnp.exp(log_a)
    b = jnp.sqrt(-jnp.expm1(2.0 * log_a)) * (i * xc)
    _, h = lax.associative_scan(_linear_combine, (a, b), axis=1)
    return jax.nn.gelu(gate_in) * h


def _s5_rglru_layer(hn, w_in, w_out, norm_a, norm_b,
                    lam_re, lam_im, log_dt, b_re, b_im, c_re, c_im, d_skip, w_glu, b_glu,
                    conv_w, conv_b, w_a, b_a, w_x, b_x, lam):
    z = hn @ w_in
    u = z[..., :S5_WIDTH]
    gate = z[..., S5_WIDTH:S5_WIDTH + LRU_WIDTH]
    rec = z[..., S5_WIDTH + LRU_WIDTH:]
    ya = _rmsnorm(_s5_mixer(u, lam_re, lam_im, log_dt, b_re, b_im, c_re, c_im, d_skip,
                            w_glu, b_glu), norm_a)
    yb = _rmsnorm(_rglru_mixer(gate, rec, conv_w, conv_b, w_a, b_a, w_x, b_x, lam), norm_b)
    return jnp.concatenate([ya, yb], axis=-1) @ w_out


def _rwkv7_mixer(x, v_first, v_res, mu, w_r, w_k, w_v, w_o, w0, w_l1, w_l2,
                 a0, a_l1, a_l2, g_l1, g_l2, k_k, k_a, r_k, lnx_w, lnx_b):
    bsz, L, D = x.shape
    xx = jnp.pad(x, ((0, 0), (1, 0), (0, 0)))[:, :-1] - x
    xr, xw, xk, xv, xa, xg = [x + xx * mu[j] for j in range(6)]
    r = xr @ w_r
    k = xk @ w_k
    v = xv @ w_v
    w_log = -jax.nn.softplus(-(w0 + jnp.tanh(xw @ w_l1) @ w_l2)) - 0.5
    if v_first is None:
        v_first = v
    else:
        v0, v_l1, v_l2 = v_res
        v = v + (v_first - v) * jax.nn.sigmoid(v0 + (xv @ v_l1) @ v_l2)
    a = jax.nn.sigmoid(a0 + (xa @ a_l1) @ a_l2)
    g = jax.nn.sigmoid(xg @ g_l1) @ g_l2

    def heads(t):
        return t.reshape(bsz, L, RWKV_HEADS, RWKV_HEAD)

    kk = heads(k * k_k)
    kk = kk / jnp.maximum(jnp.sqrt(jnp.sum(kk * kk, axis=-1, keepdims=True)), 1e-12)
    k = k * (1.0 + (a - 1.0) * k_a)
    rh, kh, vh, ah = heads(r), heads(k), heads(v), heads(a)
    decay = jnp.exp(-jnp.exp(heads(w_log)))

    def tm(t):
        return jnp.swapaxes(t, 0, 1)

    def step(S, inp):
        r_t, w_t, k_t, v_t, a_t, b_t = inp
        sa = jnp.einsum('bhij,bhj->bhi', S, a_t)
        S = (S * w_t[:, :, None, :] + sa[..., None] * b_t[:, :, None, :]
             + v_t[..., None] * k_t[:, :, None, :])
        return S, jnp.einsum('bhij,bhj->bhi', S, r_t)

    S0 = jnp.zeros((bsz, RWKV_HEADS, RWKV_HEAD, RWKV_HEAD), F32)
    _, y = lax.scan(step, S0, (tm(rh), tm(decay), tm(kh), tm(vh), tm(-kk), tm(kk * ah)))
    y = tm(y)
    mean = jnp.mean(y, axis=-1, keepdims=True)
    var = jnp.mean(jnp.square(y - mean), axis=-1, keepdims=True)
    y = ((y - mean) * lax.rsqrt(var + LNX_EPS)).reshape(bsz, L, D) * lnx_w + lnx_b
    bonus = (jnp.sum(rh * kh * r_k, axis=-1, keepdims=True) * vh).reshape(bsz, L, D)
    return ((y + bonus) * g) @ w_o, v_first


def _hier_moe(x, wr_g, br_g, wr_e, br_e, w_gate, w_up, w_down):
    bsz, L, D = x.shape
    t = x.reshape(-1, D)
    lg = (t @ wr_g + br_g).astype(F32)
    pg = jax.nn.softmax(lg, axis=-1)
    g_idx = jnp.argmax(lg, axis=-1)
    pg_sel = jnp.max(pg, axis=-1, keepdims=True)
    le = (t @ wr_e + br_e).astype(F32).reshape(-1, N_GROUPS, EXPERTS_PER_GROUP)
    le = jnp.einsum('tge,tg->te', le, jax.nn.one_hot(g_idx, N_GROUPS, dtype=F32))
    top_v, top_i = lax.top_k(le, TOP_K_IN_GROUP)
    gate = jax.nn.softmax(top_v, axis=-1) * pg_sel
    eid = g_idx[:, None] * EXPERTS_PER_GROUP + top_i
    cw = jnp.sum(jax.nn.one_hot(eid, N_EXPERTS, dtype=F32) * gate[..., None], axis=1)
    out = jnp.zeros_like(t)
    for grp in range(N_GROUPS):
        sl = slice(grp * EXPERTS_PER_GROUP, (grp + 1) * EXPERTS_PER_GROUP)
        hid = (jax.nn.silu(jnp.einsum('td,edf->tef', t, w_gate[sl]))
               * jnp.einsum('td,edf->tef', t, w_up[sl]))
        out = out + jnp.einsum('tef,efd->td', hid * cw[:, sl, None], w_down[sl])
    return out.reshape(bsz, L, D)


def setup_inputs(seed: int = 0) -> dict:
    key = jax.random.key(seed)
    ks = iter(jax.random.split(key, 64))
    D = D_MODEL
    nab, nc = N_AB_LAYERS, N_C_LAYERS

    def nrm(shape, scale):
        return scale * jax.random.normal(next(ks), shape, F32)

    def unif(shape, lo, hi):
        return jax.random.uniform(next(ks), shape, F32, lo, hi)

    lam_im0 = math.pi * jnp.arange(S5_STATE, dtype=F32)
    ramp = jnp.linspace(0.0, 1.0, D, dtype=F32) ** 0.85
    inp = {}
    inp['x'] = nrm((BATCH, SEQ, D), 1.0)
    inp['meta_tokens'] = nrm((N_META, D), 1.0)
    inp['norm_mix'] = 1.0 + nrm((DEPTH, D), 0.02)
    inp['norm_ffn'] = 1.0 + nrm((DEPTH, D), 0.02)
    inp['norm_final'] = 1.0 + nrm((D,), 0.02)
    inp['ab_w_in'] = nrm((nab, D, AB_IN), D ** -0.5)
    inp['ab_w_out'] = nrm((nab, AB_MIX, D), AB_MIX ** -0.5)
    inp['ab_norm_a'] = 1.0 + nrm((nab, S5_WIDTH), 0.02)
    inp['ab_norm_b'] = 1.0 + nrm((nab, LRU_WIDTH), 0.02)
    inp['s5_lam_re'] = -0.5 + nrm((nab, S5_GROUPS, S5_STATE), 0.01)
    inp['s5_lam_im'] = lam_im0 + nrm((nab, S5_GROUPS, S5_STATE), 0.01)
    inp['s5_log_dt'] = unif((nab, S5_GROUPS), math.log(S5_DT_MIN), math.log(S5_DT_MAX))
    inp['s5_b_re'] = nrm((nab, S5_GROUPS, S5_STATE, S5_GROUP), (2 * S5_GROUP) ** -0.5)
    inp['s5_b_im'] = nrm((nab, S5_GROUPS, S5_STATE, S5_GROUP), (2 * S5_GROUP) ** -0.5)
    inp['s5_c_re'] = nrm((nab, S5_GROUPS, S5_GROUP, S5_STATE), S5_STATE ** -0.5)
    inp['s5_c_im'] = nrm((nab, S5_GROUPS, S5_GROUP, S5_STATE), S5_STATE ** -0.5)
    inp['s5_d'] = nrm((nab, S5_GROUPS, S5_GROUP), 1.0)
    inp['s5_w_glu'] = nrm((nab, S5_WIDTH, S5_WIDTH), S5_WIDTH ** -0.5)
    inp['s5_b_glu'] = nrm((nab, S5_WIDTH), 0.01)
    inp['lru_conv_w'] = nrm((nab, LRU_CONV, LRU_WIDTH), LRU_CONV ** -0.5)
    inp['lru_conv_b'] = nrm((nab, LRU_WIDTH), 0.01)
    inp['lru_w_a'] = nrm((nab, LRU_HEADS, LRU_HEAD_DIM, LRU_HEAD_DIM), LRU_HEAD_DIM ** -0.5)
    inp['lru_b_a'] = nrm((nab, LRU_HEADS, LRU_HEAD_DIM), 0.01)
    inp['lru_w_x'] = nrm((nab, LRU_HEADS, LRU_HEAD_DIM, LRU_HEAD_DIM), LRU_HEAD_DIM ** -0.5)
    inp['lru_b_x'] = nrm((nab, LRU_HEADS, LRU_HEAD_DIM), 0.01)
    s = unif((nab, LRU_WIDTH), 0.9, 0.999) ** (1.0 / LRU_C)
    inp['lru_lam'] = jnp.log(s) - jnp.log1p(-s)
    inp['rw_mu'] = unif((nc, 6, D), 0.0, 1.0)
    inp['rw_w_r'] = nrm((nc, D, D), D ** -0.5)
    inp['rw_w_k'] = nrm((nc, D, D), D ** -0.5)
    inp['rw_w_v'] = nrm((nc, D, D), D ** -0.5)
    inp['rw_w_o'] = nrm((nc, D, D), D ** -0.5)
    inp['rw_w0'] = -6.0 + 5.0 * ramp + 0.5 + nrm((nc, D), 0.01)
    inp['rw_w_l1'] = nrm((nc, D, DECAY_LORA), D ** -0.5)
    inp['rw_w_l2'] = nrm((nc, DECAY_LORA, D), 0.5 * DECAY_LORA ** -0.5)
    inp['rw_a0'] = nrm((nc, D), 0.01)
    inp['rw_a_l1'] = nrm((nc, D, AAA_LORA), D ** -0.5)
    inp['rw_a_l2'] = nrm((nc, AAA_LORA, D), 0.5 * AAA_LORA ** -0.5)
    inp['rw_v0'] = 1.0 + nrm((max(nc - 1, 0), D), 0.01)
    inp['rw_v_l1'] = nrm((max(nc - 1, 0), D, MV_LORA), D ** -0.5)
    inp['rw_v_l2'] = nrm((max(nc - 1, 0), MV_LORA, D), 0.5 * MV_LORA ** -0.5)
    inp['rw_g_l1'] = nrm((nc, D, GATE_LORA), D ** -0.5)
    inp['rw_g_l2'] = nrm((nc, GATE_LORA, D), GATE_LORA ** -0.5)
    inp['rw_k_k'] = 0.85 + nrm((nc, D), 0.01)
    inp['rw_k_a'] = 1.0 + nrm((nc, D), 0.01)
    inp['rw_r_k'] = nrm((nc, RWKV_HEADS, RWKV_HEAD), 0.1)
    inp['rw_lnx_w'] = 1.0 + nrm((nc, D), 0.02)
    inp['rw_lnx_b'] = nrm((nc, D), 0.01)
    inp['moe_router_g'] = nrm((DEPTH, D, N_GROUPS), D ** -0.5)
    inp['moe_router_g_b'] = nrm((DEPTH, N_GROUPS), 0.01)
    inp['moe_router_e'] = nrm((DEPTH, D, N_EXPERTS), D ** -0.5)
    inp['moe_router_e_b'] = nrm((DEPTH, N_EXPERTS), 0.01)
    inp['moe_w_gate'] = nrm((DEPTH, N_EXPERTS, D, D_EXPERT), D ** -0.5)
    inp['moe_w_up'] = nrm((DEPTH, N_EXPERTS, D, D_EXPERT), D ** -0.5)
    inp['moe_w_down'] = nrm((DEPTH, N_EXPERTS, D_EXPERT, D), D_EXPERT ** -0.5)
    return inp


def reference(x, meta_tokens, norm_mix, norm_ffn, norm_final,
              ab_w_in, ab_w_out, ab_norm_a, ab_norm_b,
              s5_lam_re, s5_lam_im, s5_log_dt, s5_b_re, s5_b_im, s5_c_re, s5_c_im, s5_d,
              s5_w_glu, s5_b_glu,
              lru_conv_w, lru_conv_b, lru_w_a, lru_b_a, lru_w_x, lru_b_x, lru_lam,
              rw_mu, rw_w_r, rw_w_k, rw_w_v, rw_w_o, rw_w0, rw_w_l1, rw_w_l2,
              rw_a0, rw_a_l1, rw_a_l2, rw_v0, rw_v_l1, rw_v_l2, rw_g_l1, rw_g_l2,
              rw_k_k, rw_k_a, rw_r_k, rw_lnx_w, rw_lnx_b,
              moe_router_g, moe_router_g_b, moe_router_e, moe_router_e_b,
              moe_w_gate, moe_w_up, moe_w_down):
    bsz = x.shape[0]
    meta = jnp.broadcast_to(meta_tokens.astype(F32)[None], (bsz, N_META, D_MODEL))
    h = jnp.concatenate([meta, x.astype(F32)], axis=1)
    v_first = None
    for layer in range(DEPTH):
        hn = _rmsnorm(h, norm_mix[layer])
        j = layer // 2
        if layer % 2 == 0:
            mix = _s5_rglru_layer(hn, ab_w_in[j], ab_w_out[j], ab_norm_a[j], ab_norm_b[j],
                                  s5_lam_re[j], s5_lam_im[j], s5_log_dt[j], s5_b_re[j], s5_b_im[j],
                                  s5_c_re[j], s5_c_im[j], s5_d[j], s5_w_glu[j], s5_b_glu[j],
                                  lru_conv_w[j], lru_conv_b[j], lru_w_a[j], lru_b_a[j],
                                  lru_w_x[j], lru_b_x[j], lru_lam[j])
        else:
            v_res = (rw_v0[j - 1], rw_v_l1[j - 1], rw_v_l2[j - 1]) if j > 0 else None
            mix, v_first = _rwkv7_mixer(hn, v_first, v_res, rw_mu[j], rw_w_r[j], rw_w_k[j],
                                        rw_w_v[j], rw_w_o[j], rw_w0[j], rw_w_l1[j], rw_w_l2[j],
                                        rw_a0[j], rw_a_l1[j], rw_a_l2[j], rw_g_l1[j], rw_g_l2[j],
                                        rw_k_k[j], rw_k_a[j], rw_r_k[j], rw_lnx_w[j], rw_lnx_b[j])
        h = h + mix
        h = h + _hier_moe(_rmsnorm(h, norm_ffn[layer]), moe_router_g[layer], moe_router_g_b[layer],
                          moe_router_e[layer], moe_router_e_b[layer], moe_w_gate[layer],
                          moe_w_up[layer], moe_w_down[layer])
    return _rmsnorm(h, norm_final)[:, N_META:].astype(x.dtype)
```

```python
import functools
import math

import jax
import jax.numpy as jnp
from jax import lax
from jax.experimental import pallas as pl
from jax.experimental.pallas import tpu as pltpu

F32 = jnp.float32
BF16 = jnp.bfloat16
HI = lax.Precision.HIGHEST

RMS_EPS = 1e-6
LNX_EPS = 64e-5
N_META = 16
SEQ_ALIGN = 64
S5_CHUNK = 16
RW_CHUNK = 64
RW_HEAD = 64
LRU_C = 8.0
N_GROUPS = 4
EXPERTS_PER_GROUP = 4
N_EXPERTS = N_GROUPS * EXPERTS_PER_GROUP
MOE_TILE = 256
VMEM_LIMIT = 56 * 1024 * 1024


def _cparams(sem):
    return pltpu.CompilerParams(dimension_semantics=sem, vmem_limit_bytes=VMEM_LIMIT)


def _pick_tile(n, target):
    best = 8
    for t in range(8, min(n, target) + 1, 8):
        if n % t == 0:
            best = t
    return best


def _const_spec(shape):
    nd = len(shape)
    return pl.BlockSpec(shape, lambda *_: (0,) * nd)


def _rms(x, g):
    return x * lax.rsqrt(jnp.mean(x * x, axis=-1, keepdims=True) + RMS_EPS) * g


def _gelu(x):
    return 0.5 * x * (1.0 + jnp.tanh(math.sqrt(2.0 / math.pi) * (x + 0.044715 * (x * x * x))))


def _sigmoid(x):
    return 1.0 / (1.0 + jnp.exp(-x))


def _softplus(x):
    return jnp.maximum(x, 0.0) + jnp.log(1.0 + jnp.exp(-jnp.abs(x)))


def _bdot(a, b):
    return jnp.dot(a.astype(BF16), b.astype(BF16), preferred_element_type=F32)


def _hdot(a, b):
    return jnp.dot(a, b, preferred_element_type=F32, precision=HI)


def _ab_in_kernel(h_ref, g_ref, w_ref, u_ref, gate_ref, rec_ref, *, s5w, lruw):
    xn = _rms(h_ref[...], g_ref[...])
    z = _bdot(xn, w_ref[...])
    u_ref[...] = z[:, :s5w].astype(u_ref.dtype)
    gate_ref[...] = z[:, s5w:s5w + lruw]
    rec_ref[...] = z[:, s5w + lruw:]


def _ab_in(h, g, w_in, s5w, lruw):
    tp, d = h.shape
    tm = _pick_tile(tp, 512)
    row = lambda n: pl.BlockSpec((tm, n), lambda i: (i, 0))
    return pl.pallas_call(
        functools.partial(_ab_in_kernel, s5w=s5w, lruw=lruw),
        grid=(tp // tm,),
        in_specs=[row(d), _const_spec((1, d)), _const_spec(w_in.shape)],
        out_specs=[row(s5w), row(lruw), row(lruw)],
        out_shape=[jax.ShapeDtypeStruct((tp, s5w), BF16),
                   jax.ShapeDtypeStruct((tp, lruw), F32),
                   jax.ShapeDtypeStruct((tp, lruw), F32)],
        compiler_params=_cparams(("parallel",)),
        name="ab_in",
    )(h, g.reshape(1, d), w_in.astype(BF16))


def _s5_tables(lam_re, lam_im, log_dt, b_re, b_im, c_re, c_im, d_skip):
    g, p = lam_re.shape
    hh = b_re.shape[-1]
    c = S5_CHUNK
    lr, li = lam_re.astype(F32), lam_im.astype(F32)
    dt = jnp.exp(log_dt.astype(F32))[:, None]
    mag = jnp.exp(lr * dt)
    abar_r = mag * jnp.cos(li * dt)
    abar_i = mag * jnp.sin(li * dt)
    den = lr * lr + li * li
    zr = ((abar_r - 1.0) * lr + abar_i * li) / den
    zi = (abar_i * lr - (abar_r - 1.0) * li) / den
    bbar_r = zr[..., None] * b_re - zi[..., None] * b_im
    bbar_i = zr[..., None] * b_im + zi[..., None] * b_re
    steps = jnp.arange(c + 1, dtype=F32)[:, None, None]
    pmag = jnp.exp(steps * (lr * dt))
    pw_r = pmag * jnp.cos(steps * (li * dt))
    pw_i = pmag * jnp.sin(steps * (li * dt))
    rev_r, rev_i = pw_r[c - 1::-1][:c], pw_i[c - 1::-1][:c]
    m1_r = rev_r[:, :, :, None] * bbar_r[None] - rev_i[:, :, :, None] * bbar_i[None]
    m1_i = rev_r[:, :, :, None] * bbar_i[None] + rev_i[:, :, :, None] * bbar_r[None]
    m1_r = jnp.transpose(m1_r, (1, 0, 3, 2)).reshape(g, c * hh, p)
    m1_i = jnp.transpose(m1_i, (1, 0, 3, 2)).reshape(g, c * hh, p)
    ca_r = c_re[None] * pw_r[:, :, None, :] - c_im[None] * pw_i[:, :, None, :]
    ca_i = c_re[None] * pw_i[:, :, None, :] + c_im[None] * pw_r[:, :, None, :]
    kern = (jnp.einsum('kghp,gpj->kghj', ca_r[:c], bbar_r, precision=HI)
            - jnp.einsum('kghp,gpj->kghj', ca_i[:c], bbar_i, precision=HI))
    kern = kern.at[0].add(d_skip[:, :, None] * jnp.eye(hh, dtype=F32)[None])
    s_idx = jnp.arange(c)[:, None]
    t_idx = jnp.arange(c)[None, :]
    tau = t_idx - s_idx
    toep = jnp.where((tau >= 0)[:, :, None, None, None], kern[jnp.clip(tau, 0, c - 1)], 0.0)
    toep = jnp.transpose(toep, (2, 0, 4, 1, 3)).reshape(g, c * hh, c * hh)
    m2_r = jnp.transpose(ca_r[1:], (1, 3, 0, 2)).reshape(g, p, c * hh)
    m2_i = -jnp.transpose(ca_i[1:], (1, 3, 0, 2)).reshape(g, p, c * hh)
    gp = g // 2
    n = c * hh
    m1 = jnp.zeros((gp, 2, n, 4, p), F32)
    toep_p = jnp.zeros((gp, 2, n, 2, n), F32)
    m2 = jnp.zeros((gp, 4, p, 2, n), F32)
    m1r2, m1i2 = m1_r.reshape(gp, 2, n, p), m1_i.reshape(gp, 2, n, p)
    tp2 = toep.reshape(gp, 2, n, n)
    m2r2, m2i2 = m2_r.reshape(gp, 2, p, n), m2_i.reshape(gp, 2, p, n)
    for gl in range(2):
        m1 = m1.at[:, gl, :, gl, :].set(m1r2[:, gl])
        m1 = m1.at[:, gl, :, 2 + gl, :].set(m1i2[:, gl])
        toep_p = toep_p.at[:, gl, :, gl, :].set(tp2[:, gl])
        m2 = m2.at[:, gl, :, gl, :].set(m2r2[:, gl])
        m2 = m2.at[:, 2 + gl, :, gl, :].set(m2i2[:, gl])
    m1 = m1.reshape(gp, 2 * n, 4 * p).astype(BF16)
    toep_p = toep_p.reshape(gp, 2 * n, 2 * n).astype(BF16)
    m2 = m2.reshape(gp, 4 * p, 2 * n).astype(BF16)
    adv_r = pw_r[c].reshape(gp, 1, 2 * p)
    adv_i = pw_i[c].reshape(gp, 1, 2 * p)
    return m1, toep_p, m2, adv_r, adv_i


def _s5_kernel(u_ref, m1_ref, tp_ref, m2_ref, ar_ref, ai_ref, y_ref, xe_ref, xin_ref, *, n_chunks, nb):
    u = u_ref[0]
    sw = ar_ref.shape[-1]
    xe_ref[...] = jnp.dot(u, m1_ref[0], preferred_element_type=F32)
    ar = jnp.broadcast_to(ar_ref[0], (nb, sw))
    ai = jnp.broadcast_to(ai_ref[0], (nb, sw))

    def body(c, carry):
        sr, si = carry
        off = pl.multiple_of(c * nb, nb)
        xin_ref[pl.ds(off, nb), :] = jnp.concatenate([sr, si], axis=1)
        e = xe_ref[pl.ds(off, nb), :]
        return (ar * sr - ai * si + e[:, :sw], ar * si + ai * sr + e[:, sw:])

    zero = jnp.zeros((nb, sw), F32)
    lax.fori_loop(0, n_chunks, body, (zero, zero))
    y_ref[0] = (jnp.dot(u, tp_ref[0], preferred_element_type=F32)
                + _bdot(xin_ref[...], m2_ref[0]))


def _s5_scan(u, tables, bsz, lp):
    m1, toep, m2, adv_r, adv_i = tables
    gp, kin, sw2 = m1.shape
    c = S5_CHUNK
    hh = kin // (2 * c)
    nc = lp // c
    rows = nc * bsz
    ug = u.reshape(bsz, nc, c, gp, 2, hh)
    ug = jnp.transpose(ug, (3, 1, 0, 4, 2, 5)).reshape(gp, rows, kin)
    y = pl.pallas_call(
        functools.partial(_s5_kernel, n_chunks=nc, nb=bsz),
        grid=(gp,),
        in_specs=[pl.BlockSpec((1, rows, kin), lambda i: (i, 0, 0)),
                  pl.BlockSpec((1, kin, sw2), lambda i: (i, 0, 0)),
                  pl.BlockSpec((1, kin, kin), lambda i: (i, 0, 0)),
                  pl.BlockSpec((1, sw2, kin), lambda i: (i, 0, 0)),
                  pl.BlockSpec((1, 1, sw2 // 2), lambda i: (i, 0, 0)),
                  pl.BlockSpec((1, 1, sw2 // 2), lambda i: (i, 0, 0))],
        out_specs=pl.BlockSpec((1, rows, kin), lambda i: (i, 0, 0)),
        out_shape=jax.ShapeDtypeStruct((gp, rows, kin), F32),
        scratch_shapes=[pltpu.VMEM((rows, sw2), F32), pltpu.VMEM((rows, sw2), F32)],
        compiler_params=_cparams(("parallel",)),
        name="s5_scan",
    )(ug, m1, toep, m2, adv_r, adv_i)
    y = y.reshape(gp, nc, bsz, 2, c, hh)
    return jnp.transpose(y, (2, 1, 4, 0, 3, 5)).reshape(bsz * lp, gp * 2 * hh)


def _lru_kernel(rec_ref, gate_ref, cw_ref, cb_ref, wa_ref, ba_ref, wx_ref, bx_ref, lam_ref,
                o_ref, ext_ref, a_ref, b_ref, h_ref, *, tl):
    t = pl.program_id(1)

    @pl.when(t == 0)
    def _():
        ext_ref[0:8, :] = jnp.zeros((8, ext_ref.shape[1]), F32)
        h_ref[...] = jnp.zeros(h_ref.shape, F32)

    x = rec_ref[...]
    ext_ref[8:, :] = x
    xc = cb_ref[...] + cw_ref[3:4, :] * x
    for k in range(3):
        xc = xc + cw_ref[k:k + 1, :] * ext_ref[5 + k:5 + k + tl, :]
    ext_ref[0:8, :] = x[tl - 8:, :]
    r = _sigmoid(_bdot(xc, wa_ref[...]) + ba_ref[...])
    i = _sigmoid(_bdot(xc, wx_ref[...]) + bx_ref[...])
    log_a = (-LRU_C) * r * _softplus(-lam_ref[...])
    a = jnp.exp(log_a)
    a_ref[...] = a
    b_ref[...] = jnp.sqrt(1.0 - a * a) * (i * xc)

    def body(j, h):
        off = pl.multiple_of(j * 8, 8)
        ab = a_ref[pl.ds(off, 8), :]
        bb = b_ref[pl.ds(off, 8), :]
        rows = []
        for q in range(8):
            h = ab[q:q + 1, :] * h + bb[q:q + 1, :]
            rows.append(h)
        o_ref[pl.ds(off, 8), :] = jnp.concatenate(rows, axis=0) * _gelu(gate_ref[pl.ds(off, 8), :])
        return h

    h_ref[...] = lax.fori_loop(0, tl // 8, body, h_ref[...])


def _lru(rec, gate, conv_w, conv_b, w_a, b_a, w_x, b_x, lam, bsz, lp):
    tp, w = rec.shape
    tl = _pick_tile(lp, 1056)
    nt = lp // tl
    heads, hd, _ = w_a.shape

    def dense(wb):
        eye = jnp.eye(heads, dtype=F32)
        return jnp.einsum('hij,hg->higj', wb, eye).reshape(w, w).astype(BF16)

    row = pl.BlockSpec((tl, w), lambda b, t: (b * nt + t, 0))
    vec = _const_spec((1, w))
    return pl.pallas_call(
        functools.partial(_lru_kernel, tl=tl),
        grid=(bsz, nt),
        in_specs=[row, row, _const_spec((4, w)), vec, _const_spec((w, w)), vec, _const_spec((w, w)), vec, vec],
        out_specs=row,
        out_shape=jax.ShapeDtypeStruct((tp, w), F32),
        scratch_shapes=[pltpu.VMEM((tl + 8, w), F32), pltpu.VMEM((tl, w), F32),
                        pltpu.VMEM((tl, w), F32), pltpu.VMEM((1, w), F32)],
        compiler_params=_cparams(("parallel", "arbitrary")),
        name="rglru",
    )(rec, gate, conv_w, conv_b.reshape(1, w), dense(w_a), b_a.reshape(1, w), dense(w_x), b_x.reshape(1, w),
      lam.reshape(1, w))


def _ab_out_kernel(y5_ref, lru_ref, h_ref, wglu_ref, bglu_ref, na_ref, nb_ref, wo_ref, o_ref, *, s5w):
    y = _gelu(y5_ref[...])
    ya = y * _sigmoid(_bdot(y, wglu_ref[...]) + bglu_ref[...])
    ya = _rms(ya, na_ref[...])
    yb = _rms(lru_ref[...], nb_ref[...])
    o_ref[...] = h_ref[...] + _bdot(ya, wo_ref[:s5w, :]) + _bdot(yb, wo_ref[s5w:, :])


def _ab_out(y5, lru, h, w_glu, b_glu, norm_a, norm_b, w_out):
    tp, d = h.shape
    s5w, lruw = y5.shape[1], lru.shape[1]
    tm = _pick_tile(tp, 512)
    row = lambda n: pl.BlockSpec((tm, n), lambda i: (i, 0))
    return pl.pallas_call(
        functools.partial(_ab_out_kernel, s5w=s5w),
        grid=(tp // tm,),
        in_specs=[row(s5w), row(lruw), row(d), _const_spec((s5w, s5w)), _const_spec((1, s5w)),
                  _const_spec((1, s5w)), _const_spec((1, lruw)), _const_spec(w_out.shape)],
        out_specs=row(d),
        out_shape=jax.ShapeDtypeStruct((tp, d), F32),
        compiler_params=_cparams(("parallel",)),
        name="ab_out",
    )(y5, lru, h, w_glu.astype(BF16), b_glu.reshape(1, s5w), norm_a.reshape(1, s5w),
      norm_b.reshape(1, lruw), w_out.astype(BF16))


def _route_kernel(h_ref, g_ref, wr_ref, br_ref, xn_ref, rt_ref):
    xn = _rms(h_ref[...], g_ref[...])
    xn_ref[...] = xn.astype(xn_ref.dtype)
    lg = _hdot(xn, wr_ref[...]) + br_ref[...]
    lane = lax.broadcasted_iota(jnp.int32, lg.shape, 1).astype(F32)
    big = float(lg.shape[1])
    neg = -jnp.inf
    gl = jnp.where(lane < N_GROUPS, lg, neg)
    mg = jnp.max(gl, axis=-1, keepdims=True)
    gidx = jnp.min(jnp.where(gl == mg, lane, big), axis=-1, keepdims=True)
    pg_sel = 1.0 / jnp.sum(jnp.exp(gl - mg), axis=-1, keepdims=True)
    lo = N_GROUPS + EXPERTS_PER_GROUP * gidx
    le = jnp.where(lane >= lo, jnp.where(lane < lo + EXPERTS_PER_GROUP, lg, neg), neg)
    v1 = jnp.max(le, axis=-1, keepdims=True)
    i1 = jnp.min(jnp.where(le == v1, lane, big), axis=-1, keepdims=True)
    le2 = jnp.where(lane == i1, neg, le)
    v2 = jnp.max(le2, axis=-1, keepdims=True)
    i2 = jnp.min(jnp.where(le2 == v2, lane, big), axis=-1, keepdims=True)
    e2 = jnp.exp(v2 - v1)
    w1 = pg_sel / (1.0 + e2)
    w2 = w1 * e2
    rt_ref[...] = jnp.where(lane == 0.0, w1, jnp.where(lane == 1.0, w2, jnp.where(
        lane == 2.0, i1 - N_GROUPS, jnp.where(lane == 3.0, i2 - N_GROUPS, 0.0))))


def _route(h, g, wr_g, br_g, wr_e, br_e):
    tp, d = h.shape
    tm = _pick_tile(tp, 512)
    nl = 128
    wr = jnp.zeros((d, nl), F32).at[:, :N_GROUPS].set(wr_g).at[:, N_GROUPS:N_GROUPS + N_EXPERTS].set(wr_e)
    br = jnp.zeros((1, nl), F32).at[0, :N_GROUPS].set(br_g).at[0, N_GROUPS:N_GROUPS + N_EXPERTS].set(br_e)
    row = lambda n: pl.BlockSpec((tm, n), lambda i: (i, 0))
    return pl.pallas_call(
        _route_kernel,
        grid=(tp // tm,),
        in_specs=[row(d), _const_spec((1, d)), _const_spec((d, nl)), _const_spec((1, nl))],
        out_specs=[row(d), row(nl)],
        out_shape=[jax.ShapeDtypeStruct((tp, d), BF16), jax.ShapeDtypeStruct((tp, nl), F32)],
        compiler_params=_cparams(("parallel",)),
        name="moe_route",
    )(h, g.reshape(1, d), wr, br)


def _gmm_kernel(te_ref, tv_ref, x_ref, wg_ref, wu_ref, wd_ref, o_ref):
    i = pl.program_id(0)

    @pl.when(tv_ref[i] != 0)
    def _():
        x = x_ref[...]
        hg = jnp.dot(x, wg_ref[0], preferred_element_type=F32)
        hu = jnp.dot(x, wu_ref[0], preferred_element_type=F32)
        hid = hg * _sigmoid(hg) * hu
        o_ref[...] = _bdot(hid, wd_ref[0])

    @pl.when(tv_ref[i] == 0)
    def _():
        o_ref[...] = jnp.zeros(o_ref.shape, o_ref.dtype)


def _moe(h, g, wr_g, br_g, wr_e, br_e, w_gate, w_up, w_down):
    tp, d = h.shape
    f = w_gate.shape[-1]
    xn, rt = _route(h, g, wr_g, br_g, wr_e, br_e)
    gate = rt[:, 0:2]
    eid = rt[:, 2:4].astype(jnp.int32)
    tmm = MOE_TILE
    na = 2 * tp
    e_flat = eid.reshape(na)
    onehot = (e_flat[:, None] == jnp.arange(N_EXPERTS, dtype=jnp.int32)[None, :]).astype(jnp.int32)
    csum = jnp.cumsum(onehot, axis=0)
    counts = csum[-1]
    rank = jnp.sum(onehot * csum, axis=1) - 1
    padded = ((counts + tmm - 1) // tmm) * tmm
    ends = jnp.cumsum(padded)
    starts = ends - padded
    dest = starts[e_flat] + rank
    n_tiles = -(-na // tmm) + N_EXPERTS
    nrows = n_tiles * tmm
    tile_e = jnp.searchsorted(ends, jnp.arange(n_tiles, dtype=jnp.int32) * tmm, side='right').astype(jnp.int32)
    tile_v = (tile_e < N_EXPERTS).astype(jnp.int32)
    tile_e = jnp.minimum(tile_e, N_EXPERTS - 1)
    src = jnp.zeros((nrows,), jnp.int32).at[dest].set(jnp.arange(na, dtype=jnp.int32) // 2)
    xs = jnp.take(xn, src, axis=0)
    ys = pl.pallas_call(
        _gmm_kernel,
        grid_spec=pltpu.PrefetchScalarGridSpec(
            num_scalar_prefetch=2,
            grid=(n_tiles,),
            in_specs=[pl.BlockSpec((tmm, d), lambda i, te, tv: (i, 0)),
                      pl.BlockSpec((1, d, f), lambda i, te, tv: (te[i], 0, 0)),
                      pl.BlockSpec((1, d, f), lambda i, te, tv: (te[i], 0, 0)),
                      pl.BlockSpec((1, f, d), lambda i, te, tv: (te[i], 0, 0))],
            out_specs=pl.BlockSpec((tmm, d), lambda i, te, tv: (i, 0)),
        ),
        out_shape=jax.ShapeDtypeStruct((nrows, d), F32),
        compiler_params=_cparams(("arbitrary",)),
        name="moe_gmm",
    )(tile_e, tile_v, xs, w_gate.astype(BF16), w_up.astype(BF16), w_down.astype(BF16))
    d2 = dest.reshape(tp, 2)
    return h + gate[:, 0:1] * jnp.take(ys, d2[:, 0], axis=0) + gate[:, 1:2] * jnp.take(ys, d2[:, 1], axis=0)


def _rw_pre_kernel(*refs, tm, lp, has_vres):
    if has_vres:
        (h_ref, hp_ref, g_ref, mu_ref, wr_ref, wk_ref, wv_ref, w0_ref, wl1_ref, wl2_ref,
         a0_ref, al1_ref, al2_ref, gl1_ref, gl2_ref, vf_ref, v0_ref, vl1_ref, vl2_ref,
         r_ref, k_ref, v_ref, w_ref, a_ref, gg_ref) = refs
    else:
        (h_ref, hp_ref, g_ref, mu_ref, wr_ref, wk_ref, wv_ref, w0_ref, wl1_ref, wl2_ref,
         a0_ref, al1_ref, al2_ref, gl1_ref, gl2_ref,
         r_ref, k_ref, v_ref, w_ref, a_ref, gg_ref) = refs
    i = pl.program_id(0)
    g = g_ref[...]
    x = _rms(h_ref[...], g)
    xp8 = _rms(hp_ref[...], g)
    row = lax.broadcasted_iota(jnp.int32, x.shape, 0)
    prev = jnp.where(row == 0, jnp.broadcast_to(xp8[7:8, :], x.shape), pltpu.roll(x, 1, axis=0))
    first = lax.rem(lp - lax.rem(i * tm, lp), lp)
    prev = jnp.where(row == first, 0.0, prev)
    xx = prev - x
    xr, xw, xk, xv, xa, xg = [x + xx * mu_ref[j:j + 1, :] for j in range(6)]
    r_ref[...] = _bdot(xr, wr_ref[...])
    k_ref[...] = _bdot(xk, wk_ref[...])
    v = _bdot(xv, wv_ref[...])
    if has_vres:
        mix = _sigmoid(v0_ref[...] + _bdot(_bdot(xv, vl1_ref[...]), vl2_ref[...]))
        v = v + (vf_ref[...] - v) * mix
    v_ref[...] = v
    w_ref[...] = -_softplus(-(w0_ref[...] + _bdot(jnp.tanh(_bdot(xw, wl1_ref[...])), wl2_ref[...]))) - 0.5
    a_ref[...] = _sigmoid(a0_ref[...] + _bdot(_bdot(xa, al1_ref[...]), al2_ref[...]))
    gg_ref[...] = _bdot(_sigmoid(_bdot(xg, gl1_ref[...])), gl2_ref[...])


def _rw_pre(h, g, mu, w_r, w_k, w_v, w0, w_l1, w_l2, a0, a_l1, a_l2, g_l1, g_l2, lp, v_first, v_res):
    tp, d = h.shape
    tm = _pick_tile(tp, min(256, lp))
    has_vres = v_res is not None
    row = pl.BlockSpec((tm, d), lambda i: (i, 0))
    prev8 = pl.BlockSpec((8, d), lambda i: (jnp.maximum(i * (tm // 8) - 1, 0), 0))
    vec = _const_spec((1, d))
    mu8 = jnp.zeros((8, d), F32).at[:6].set(mu)
    bf = lambda w: w.astype(BF16)
    ins = [h, h, g.reshape(1, d), mu8, bf(w_r), bf(w_k), bf(w_v), w0.reshape(1, d), bf(w_l1), bf(w_l2),
           a0.reshape(1, d), bf(a_l1), bf(a_l2), bf(g_l1), bf(g_l2)]
    specs = [row, prev8, vec, _const_spec((8, d))] + [_const_spec((d, d))] * 3 + [
        vec, _const_spec(w_l1.shape), _const_spec(w_l2.shape),
        vec, _const_spec(a_l1.shape), _const_spec(a_l2.shape), _const_spec(g_l1.shape), _const_spec(g_l2.shape)]
    if has_vres:
        v0, v_l1, v_l2 = v_res
        ins += [v_first, v0.reshape(1, d), bf(v_l1), bf(v_l2)]
        specs += [row, vec, _const_spec(v_l1.shape), _const_spec(v_l2.shape)]
    return pl.pallas_call(
        functools.partial(_rw_pre_kernel, tm=tm, lp=lp, has_vres=has_vres),
        grid=(tp // tm,),
        in_specs=specs,
        out_specs=[row] * 6,
        out_shape=[jax.ShapeDtypeStruct((tp, d), F32)] * 6,
        compiler_params=_cparams(("parallel",)),
        name="rwkv_pre",
    )(*ins)


def _rw_scan_kernel(r_ref, w_ref, k_ref, v_ref, a_ref, kk_ref, ka_ref, rk_ref, lw_ref, lb_ref,
                    o_ref, g_scr, h_scr, q_scr, y0_scr, *, n_chunks):
    c = RW_CHUNK
    nl = 2 * RW_HEAD
    lane = lax.broadcasted_iota(jnp.int32, (1, nl), 1)
    m0 = (lane < RW_HEAD).astype(F32)
    m1 = 1.0 - m0
    ri = lax.broadcasted_iota(jnp.int32, (nl, nl), 0)
    ci = lax.broadcasted_iota(jnp.int32, (nl, nl), 1)
    same_head = ((ri < RW_HEAD) == (ci < RW_HEAD)).astype(F32)
    eye_l = (ri == ci).astype(F32)
    ti = lax.broadcasted_iota(jnp.int32, (c, c), 0)
    si = lax.broadcasted_iota(jnp.int32, (c, c), 1)
    tril_incl = (si <= ti).astype(F32)
    tril_strict = (si < ti).astype(F32)
    eye_c = (si == ti).astype(F32)
    kk_w, ka_w, rk_w = kk_ref[...], ka_ref[...], rk_ref[...]

    def prep(ch, _):
        off = pl.multiple_of(ch * c, c)
        r = r_ref[pl.ds(off, c), :]
        k = k_ref[pl.ds(off, c), :]
        v = v_ref[pl.ds(off, c), :]
        a = a_ref[pl.ds(off, c), :]
        logw = -jnp.exp(w_ref[pl.ds(off, c), :])
        kk = k * kk_w
        ss = _hdot(kk * kk, same_head)
        kk = kk / jnp.maximum(jnp.sqrt(ss), 1e-12)
        kmod = k * (1.0 + (a - 1.0) * ka_w)
        cum = _hdot(tril_incl, logw)
        p_incl = jnp.exp(cum)
        p_excl = jnp.exp(cum - logw)
        p_inv = jnp.exp(-cum)
        p_end = jnp.exp(cum[c - 1:c, :] - cum)
        at = -kk * p_excl
        rt = r * p_incl
        bt = kk * a * p_inv
        kt = kmod * p_inv
        lhs = jnp.concatenate([at, rt], axis=0)
        rhs = jnp.concatenate([bt, kt], axis=0)
        w_acc = jnp.zeros((c, nl), F32)
        u0_acc = jnp.zeros((c, nl), F32)
        q_acc = rt
        y0_acc = jnp.zeros((c, nl), F32)
        for hm in (m0, m1):
            amat = lax.dot_general(lhs * hm, rhs, (((1,), (1,)), ((), ())),
                                   preferred_element_type=F32, precision=HI)
            a_ab = amat[:c, :c] * tril_strict
            a_ak = amat[:c, c:] * tril_strict
            a_rb = amat[c:, :c] * tril_incl
            a_rk = amat[c:, c:] * tril_incl
            tinv = eye_c + a_ab
            lpow = a_ab
            for _ in range(5):
                lpow = _hdot(lpow, lpow)
                tinv = tinv + _hdot(lpow, tinv)
            w_h = _hdot(tinv, at)
            u0_h = _hdot(tinv, _hdot(a_ak, v))
            w_acc = w_acc + w_h * hm
            u0_acc = u0_acc + u0_h * hm
            q_acc = q_acc + _hdot(a_rb, w_h) * hm
            y0_acc = y0_acc + (_hdot(a_rb, u0_h) + _hdot(a_rk, v)) * hm
        bhat_t = (kk * a * p_end).T
        khat_t = (kmod * p_end).T
        g_scr[ch] = eye_l * p_incl[c - 1:c, :] + same_head * _hdot(bhat_t, w_acc)
        h_scr[ch] = same_head * (_hdot(bhat_t, u0_acc) + _hdot(khat_t, v))
        q_scr[ch] = q_acc
        y0_scr[ch] = y0_acc
        return 0

    lax.fori_loop(0, n_chunks, prep, 0)
    lw, lb = lw_ref[...], lb_ref[...]
    avg = same_head * (1.0 / RW_HEAD)

    def seq(ch, st):
        off = pl.multiple_of(ch * c, c)
        both = _hdot(jnp.concatenate([q_scr[ch], g_scr[ch]], axis=0), st)
        y = both[:c] + y0_scr[ch]
        st = both[c:] + h_scr[ch]
        mean = _hdot(y, avg)
        dev = y - mean
        var = _hdot(dev * dev, avg)
        yn = dev * lax.rsqrt(var + LNX_EPS) * lw + lb
        r = r_ref[pl.ds(off, c), :]
        k = k_ref[pl.ds(off, c), :]
        a = a_ref[pl.ds(off, c), :]
        kmod = k * (1.0 + (a - 1.0) * ka_w)
        bonus = _hdot(r * kmod * rk_w, same_head) * v_ref[pl.ds(off, c), :]
        o_ref[pl.ds(off, c), :] = yn + bonus
        return st

    lax.fori_loop(0, n_chunks, seq, jnp.zeros((nl, nl), F32))


def _rw_scan(r, w, k, v, a, k_k, k_a, r_k, lnx_w, lnx_b, bsz, lp):
    tp, d = r.shape
    nl = 2 * RW_HEAD
    nhp = d // nl
    nc = lp // RW_CHUNK
    seq = pl.BlockSpec((lp, nl), lambda b, hp: (b, hp))
    vec = pl.BlockSpec((1, nl), lambda b, hp: (0, hp))
    c = RW_CHUNK
    return pl.pallas_call(
        functools.partial(_rw_scan_kernel, n_chunks=nc),
        grid=(bsz, nhp),
        in_specs=[seq] * 5 + [vec] * 5,
        out_specs=seq,
        out_shape=jax.ShapeDtypeStruct((tp, d), F32),
        scratch_shapes=[pltpu.VMEM((nc, nl, nl), F32), pltpu.VMEM((nc, nl, nl), F32),
                        pltpu.VMEM((nc, c, nl), F32), pltpu.VMEM((nc, c, nl), F32)],
        compiler_params=_cparams(("parallel", "parallel")),
        name="rwkv_scan",
    )(r, w, k, v, a, k_k.reshape(1, d), k_a.reshape(1, d), r_k.reshape(1, d),
      lnx_w.reshape(1, d), lnx_b.reshape(1, d))


def _rw_out_kernel(y_ref, g_ref, h_ref, wo_ref, o_ref):
    o_ref[...] = h_ref[...] + _bdot(y_ref[...] * g_ref[...], wo_ref[...])


def _rw_out(y, g, h, w_o):
    tp, d = h.shape
    tm = _pick_tile(tp, 512)
    row = pl.BlockSpec((tm, d), lambda i: (i, 0))
    return pl.pallas_call(
        _rw_out_kernel,
        grid=(tp // tm,),
        in_specs=[row, row, row, _const_spec((d, d))],
        out_specs=row,
        out_shape=jax.ShapeDtypeStruct((tp, d), F32),
        compiler_params=_cparams(("parallel",)),
        name="rwkv_out",
    )(y, g, h, w_o.astype(BF16))


def _final_norm_kernel(h_ref, g_ref, o_ref):
    o_ref[...] = _rms(h_ref[...], g_ref[...]).astype(o_ref.dtype)


def _final_norm(h, g, dtype):
    tp, d = h.shape
    tm = _pick_tile(tp, 512)
    row = pl.BlockSpec((tm, d), lambda i: (i, 0))
    return pl.pallas_call(
        _final_norm_kernel,
        grid=(tp // tm,),
        in_specs=[row, _const_spec((1, d))],
        out_specs=row,
        out_shape=jax.ShapeDtypeStruct((tp, d), dtype),
        compiler_params=_cparams(("parallel",)),
        name="final_norm",
    )(h, g.reshape(1, d))


def kernel(x, meta_tokens, norm_mix, norm_ffn, norm_final, ab_w_in, ab_w_out, ab_norm_a, ab_norm_b, s5_lam_re, s5_lam_im, s5_log_dt, s5_b_re, s5_b_im, s5_c_re, s5_c_im, s5_d, s5_w_glu, s5_b_glu, lru_conv_w, lru_conv_b, lru_w_a, lru_b_a, lru_w_x, lru_b_x, lru_lam, rw_mu, rw_w_r, rw_w_k, rw_w_v, rw_w_o, rw_w0, rw_w_l1, rw_w_l2, rw_a0, rw_a_l1, rw_a_l2, rw_v0, rw_v_l1, rw_v_l2, rw_g_l1, rw_g_l2, rw_k_k, rw_k_a, rw_r_k, rw_lnx_w, rw_lnx_b, moe_router_g, moe_router_g_b, moe_router_e, moe_router_e_b, moe_w_gate, moe_w_up, moe_w_down):
    bsz, seq, d = x.shape
    n_meta = meta_tokens.shape[0]
    depth = norm_mix.shape[0]
    ltot = n_meta + seq
    lp = -(-ltot // SEQ_ALIGN) * SEQ_ALIGN
    s5w = s5_w_glu.shape[-1]
    lruw = lru_lam.shape[-1]
    meta = jnp.broadcast_to(meta_tokens.astype(F32)[None], (bsz, n_meta, d))
    h = jnp.concatenate([meta, x.astype(F32), jnp.zeros((bsz, lp - ltot, d), F32)], axis=1).reshape(bsz * lp, d)
    v_first = None
    for layer in range(depth):
        j = layer // 2
        if layer % 2 == 0:
            u, gate, rec = _ab_in(h, norm_mix[layer], ab_w_in[j], s5w, lruw)
            tables = _s5_tables(s5_lam_re[j], s5_lam_im[j], s5_log_dt[j], s5_b_re[j], s5_b_im[j],
                                s5_c_re[j], s5_c_im[j], s5_d[j])
            y5 = _s5_scan(u, tables, bsz, lp)
            lru = _lru(rec, gate, lru_conv_w[j], lru_conv_b[j], lru_w_a[j], lru_b_a[j], lru_w_x[j], lru_b_x[j],
                       lru_lam[j], bsz, lp)
            h = _ab_out(y5, lru, h, s5_w_glu[j], s5_b_glu[j], ab_norm_a[j], ab_norm_b[j], ab_w_out[j])
        else:
            v_res = (rw_v0[j - 1], rw_v_l1[j - 1], rw_v_l2[j - 1]) if j > 0 else None
            r, k, v, w, a, g = _rw_pre(h, norm_mix[layer], rw_mu[j], rw_w_r[j], rw_w_k[j], rw_w_v[j], rw_w0[j],
                                       rw_w_l1[j], rw_w_l2[j], rw_a0[j], rw_a_l1[j], rw_a_l2[j], rw_g_l1[j],
                                       rw_g_l2[j], lp, v_first, v_res)
            if v_first is None:
                v_first = v
            y = _rw_scan(r, w, k, v, a, rw_k_k[j], rw_k_a[j], rw_r_k[j].reshape(-1), rw_lnx_w[j], rw_lnx_b[j],
                         bsz, lp)
            h = _rw_out(y, g, h, rw_w_o[j])
        h = _moe(h, norm_ffn[layer], moe_router_g[layer], moe_router_g_b[layer], moe_router_e[layer],
                 moe_router_e_b[layer], moe_w_gate[layer], moe_w_up[layer], moe_w_down[layer])
    out = _final_norm(h, norm_final, x.dtype)
    return out.reshape(bsz, lp, d)[:, n_meta:ltot]
```

```python
import functools
import math

import jax
import jax.numpy as jnp
from jax import lax
from jax.experimental import pallas as pl
from jax.experimental.pallas import tpu as pltpu

F32 = jnp.float32
BF16 = jnp.bfloat16
HI = lax.Precision.HIGHEST

RMS_EPS = 1e-6
LNX_EPS = 64e-5
N_META = 16
SEQ_ALIGN = 64
S5_CHUNK = 16
RW_CHUNK = 64
RW_HEAD = 64
RW_UNROLL = 3
LRU_C = 8.0
N_GROUPS = 4
EXPERTS_PER_GROUP = 4
N_EXPERTS = N_GROUPS * EXPERTS_PER_GROUP
MOE_TILE = 256
VMEM_LIMIT = 56 * 1024 * 1024


def _cparams(sem):
    return pltpu.CompilerParams(dimension_semantics=sem, vmem_limit_bytes=VMEM_LIMIT)


def _pick_tile(n, target):
    best = 8
    for t in range(8, min(n, target) + 1, 8):
        if n % t == 0:
            best = t
    return best


def _const_spec(shape):
    nd = len(shape)
    return pl.BlockSpec(shape, lambda *_: (0,) * nd)


def _rms(x, g):
    return x * lax.rsqrt(jnp.mean(x * x, axis=-1, keepdims=True) + RMS_EPS) * g


def _gelu(x):
    return 0.5 * x * (1.0 + jnp.tanh(math.sqrt(2.0 / math.pi) * (x + 0.044715 * (x * x * x))))


def _sigmoid(x):
    return 1.0 / (1.0 + jnp.exp(-x))


def _softplus(x):
    return jnp.maximum(x, 0.0) + jnp.log(1.0 + jnp.exp(-jnp.abs(x)))


def _bdot(a, b):
    return jnp.dot(a.astype(BF16), b.astype(BF16), preferred_element_type=F32)


def _hdot(a, b):
    return jnp.dot(a, b, preferred_element_type=F32, precision=HI)


def _dot3(a, b):
    m = a.shape[0]
    a_hi = a.astype(BF16)
    b_hi = b.astype(BF16)
    a_lo = (a - a_hi.astype(F32)).astype(BF16)
    b_lo = (b - b_hi.astype(F32)).astype(BF16)
    dot = functools.partial(jnp.dot, preferred_element_type=F32)
    top = dot(jnp.concatenate([a_hi, a_lo], axis=0), b_hi)
    return top[:m] + (top[m:] + dot(a_hi, b_lo))


def _split_dot_many(xs, m, rhs=False, pieces=3):
    accs = [None] * len(xs)
    xs = list(xs)
    for _ in range(pieces):
        his = [x.astype(BF16) for x in xs]
        parts = [jnp.dot(m, hi, preferred_element_type=F32) if rhs else jnp.dot(hi, m, preferred_element_type=F32)
                 for hi in his]
        accs = [p if a is None else a + p for a, p in zip(accs, parts)]
        xs = [x - hi.astype(F32) for x, hi in zip(xs, his)]
    return accs


def _ab_in_kernel(h_ref, g_ref, w_ref, u_ref, gate_ref, rec_ref, *, s5w, lruw):
    xn = _rms(h_ref[...], g_ref[...])
    z = _bdot(xn, w_ref[...])
    u_ref[...] = z[:, :s5w].astype(u_ref.dtype)
    gate_ref[...] = z[:, s5w:s5w + lruw]
    rec_ref[...] = z[:, s5w + lruw:]


def _ab_in(h, g, w_in, s5w, lruw):
    tp, d = h.shape
    tm = _pick_tile(tp, 512)
    row = lambda n: pl.BlockSpec((tm, n), lambda i: (i, 0))
    return pl.pallas_call(
        functools.partial(_ab_in_kernel, s5w=s5w, lruw=lruw),
        grid=(tp // tm,),
        in_specs=[row(d), _const_spec((1, d)), _const_spec(w_in.shape)],
        out_specs=[row(s5w), row(lruw), row(lruw)],
        out_shape=[jax.ShapeDtypeStruct((tp, s5w), BF16),
                   jax.ShapeDtypeStruct((tp, lruw), F32),
                   jax.ShapeDtypeStruct((tp, lruw), F32)],
        compiler_params=_cparams(("parallel",)),
        name="ab_in",
    )(h, g.reshape(1, d), w_in.astype(BF16))


def _s5_tables(lam_re, lam_im, log_dt, b_re, b_im, c_re, c_im, d_skip):
    g, p = lam_re.shape
    hh = b_re.shape[-1]
    c = S5_CHUNK
    lr, li = lam_re.astype(F32), lam_im.astype(F32)
    dt = jnp.exp(log_dt.astype(F32))[:, None]
    mag = jnp.exp(lr * dt)
    abar_r = mag * jnp.cos(li * dt)
    abar_i = mag * jnp.sin(li * dt)
    den = lr * lr + li * li
    zr = ((abar_r - 1.0) * lr + abar_i * li) / den
    zi = (abar_i * lr - (abar_r - 1.0) * li) / den
    bbar_r = zr[..., None] * b_re - zi[..., None] * b_im
    bbar_i = zr[..., None] * b_im + zi[..., None] * b_re
    steps = jnp.arange(c + 1, dtype=F32)[:, None, None]
    pmag = jnp.exp(steps * (lr * dt))
    pw_r = pmag * jnp.cos(steps * (li * dt))
    pw_i = pmag * jnp.sin(steps * (li * dt))
    rev_r, rev_i = pw_r[c - 1::-1][:c], pw_i[c - 1::-1][:c]
    m1_r = rev_r[:, :, :, None] * bbar_r[None] - rev_i[:, :, :, None] * bbar_i[None]
    m1_i = rev_r[:, :, :, None] * bbar_i[None] + rev_i[:, :, :, None] * bbar_r[None]
    m1_r = jnp.transpose(m1_r, (1, 0, 3, 2)).reshape(g, c * hh, p)
    m1_i = jnp.transpose(m1_i, (1, 0, 3, 2)).reshape(g, c * hh, p)
    ca_r = c_re[None] * pw_r[:, :, None, :] - c_im[None] * pw_i[:, :, None, :]
    ca_i = c_re[None] * pw_i[:, :, None, :] + c_im[None] * pw_r[:, :, None, :]
    kern = (jnp.einsum('kghp,gpj->kghj', ca_r[:c], bbar_r, precision=HI)
            - jnp.einsum('kghp,gpj->kghj', ca_i[:c], bbar_i, precision=HI))
    kern = kern.at[0].add(d_skip[:, :, None] * jnp.eye(hh, dtype=F32)[None])
    s_idx = jnp.arange(c)[:, None]
    t_idx = jnp.arange(c)[None, :]
    tau = t_idx - s_idx
    toep = jnp.where((tau >= 0)[:, :, None, None, None], kern[jnp.clip(tau, 0, c - 1)], 0.0)
    toep = jnp.transpose(toep, (2, 0, 4, 1, 3)).reshape(g, c * hh, c * hh)
    m2_r = jnp.transpose(ca_r[1:], (1, 3, 0, 2)).reshape(g, p, c * hh)
    m2_i = -jnp.transpose(ca_i[1:], (1, 3, 0, 2)).reshape(g, p, c * hh)
    gp = g // 2
    n = c * hh
    m1 = jnp.zeros((gp, 2, n, 4, p), F32)
    toep_p = jnp.zeros((gp, 2, n, 2, n), F32)
    m2 = jnp.zeros((gp, 4, p, 2, n), F32)
    m1r2, m1i2 = m1_r.reshape(gp, 2, n, p), m1_i.reshape(gp, 2, n, p)
    tp2 = toep.reshape(gp, 2, n, n)
    m2r2, m2i2 = m2_r.reshape(gp, 2, p, n), m2_i.reshape(gp, 2, p, n)
    for gl in range(2):
        m1 = m1.at[:, gl, :, gl, :].set(m1r2[:, gl])
        m1 = m1.at[:, gl, :, 2 + gl, :].set(m1i2[:, gl])
        toep_p = toep_p.at[:, gl, :, gl, :].set(tp2[:, gl])
        m2 = m2.at[:, gl, :, gl, :].set(m2r2[:, gl])
        m2 = m2.at[:, 2 + gl, :, gl, :].set(m2i2[:, gl])
    m1 = m1.reshape(gp, 2 * n, 4 * p).astype(BF16)
    toep_p = toep_p.reshape(gp, 2 * n, 2 * n).astype(BF16)
    m2 = m2.reshape(gp, 4 * p, 2 * n).astype(BF16)
    adv_r = pw_r[c].reshape(gp, 1, 2 * p)
    adv_i = pw_i[c].reshape(gp, 1, 2 * p)
    return m1, toep_p, m2, adv_r, adv_i


def _s5_kernel(u_ref, m1_ref, tp_ref, m2_ref, ar_ref, ai_ref, y_ref, xe_ref, xin_ref, *, n_chunks, nb):
    u = u_ref[0]
    sw = ar_ref.shape[-1]
    xe_ref[...] = jnp.dot(u, m1_ref[0], preferred_element_type=F32)
    ar = jnp.broadcast_to(ar_ref[0], (nb, sw))
    ai = jnp.broadcast_to(ai_ref[0], (nb, sw))

    def body(c, carry):
        sr, si = carry
        off = pl.multiple_of(c * nb, nb)
        xin_ref[pl.ds(off, nb), :] = jnp.concatenate([sr, si], axis=1)
        e = xe_ref[pl.ds(off, nb), :]
        return (ar * sr - ai * si + e[:, :sw], ar * si + ai * sr + e[:, sw:])

    zero = jnp.zeros((nb, sw), F32)
    lax.fori_loop(0, n_chunks, body, (zero, zero))
    y_ref[0] = (jnp.dot(u, tp_ref[0], preferred_element_type=F32)
                + _bdot(xin_ref[...], m2_ref[0]))


def _s5_scan(u, tables, bsz, lp):
    m1, toep, m2, adv_r, adv_i = tables
    gp, kin, sw2 = m1.shape
    c = S5_CHUNK
    hh = kin // (2 * c)
    nc = lp // c
    rows = nc * bsz
    ug = u.reshape(bsz, nc, c, gp, 2, hh)
    ug = jnp.transpose(ug, (3, 1, 0, 4, 2, 5)).reshape(gp, rows, kin)
    y = pl.pallas_call(
        functools.partial(_s5_kernel, n_chunks=nc, nb=bsz),
        grid=(gp,),
        in_specs=[pl.BlockSpec((1, rows, kin), lambda i: (i, 0, 0)),
                  pl.BlockSpec((1, kin, sw2), lambda i: (i, 0, 0)),
                  pl.BlockSpec((1, kin, kin), lambda i: (i, 0, 0)),
                  pl.BlockSpec((1, sw2, kin), lambda i: (i, 0, 0)),
                  pl.BlockSpec((1, 1, sw2 // 2), lambda i: (i, 0, 0)),
                  pl.BlockSpec((1, 1, sw2 // 2), lambda i: (i, 0, 0))],
        out_specs=pl.BlockSpec((1, rows, kin), lambda i: (i, 0, 0)),
        out_shape=jax.ShapeDtypeStruct((gp, rows, kin), F32),
        scratch_shapes=[pltpu.VMEM((rows, sw2), F32), pltpu.VMEM((rows, sw2), F32)],
        compiler_params=_cparams(("parallel",)),
        name="s5_scan",
    )(ug, m1, toep, m2, adv_r, adv_i)
    y = y.reshape(gp, nc, bsz, 2, c, hh)
    return jnp.transpose(y, (2, 1, 4, 0, 3, 5)).reshape(bsz * lp, gp * 2 * hh)


def _lru_kernel(rec_ref, gate_ref, cw_ref, cb_ref, wa_ref, ba_ref, wx_ref, bx_ref, lam_ref,
                o_ref, ext_ref, a_ref, b_ref, h_ref, *, tl):
    t = pl.program_id(1)

    @pl.when(t == 0)
    def _():
        ext_ref[0:8, :] = jnp.zeros((8, ext_ref.shape[1]), F32)
        h_ref[...] = jnp.zeros(h_ref.shape, F32)

    x = rec_ref[...]
    ext_ref[8:, :] = x
    xc = cb_ref[...] + cw_ref[3:4, :] * x
    for k in range(3):
        xc = xc + cw_ref[k:k + 1, :] * ext_ref[5 + k:5 + k + tl, :]
    ext_ref[0:8, :] = x[tl - 8:, :]
    r = _sigmoid(_bdot(xc, wa_ref[...]) + ba_ref[...])
    i = _sigmoid(_bdot(xc, wx_ref[...]) + bx_ref[...])
    log_a = (-LRU_C) * r * _softplus(-lam_ref[...])
    a = jnp.exp(log_a)
    a_ref[...] = a
    b_ref[...] = jnp.sqrt(1.0 - a * a) * (i * xc)

    def body(j, h):
        off = pl.multiple_of(j * 8, 8)
        ab = a_ref[pl.ds(off, 8), :]
        bb = b_ref[pl.ds(off, 8), :]
        rows = []
        for q in range(8):
            h = ab[q:q + 1, :] * h + bb[q:q + 1, :]
            rows.append(h)
        o_ref[pl.ds(off, 8), :] = jnp.concatenate(rows, axis=0) * _gelu(gate_ref[pl.ds(off, 8), :])
        return h

    h_ref[...] = lax.fori_loop(0, tl // 8, body, h_ref[...])


def _lru(rec, gate, conv_w, conv_b, w_a, b_a, w_x, b_x, lam, bsz, lp):
    tp, w = rec.shape
    tl = _pick_tile(lp, 1056)
    nt = lp // tl
    heads, hd, _ = w_a.shape

    def dense(wb):
        eye = jnp.eye(heads, dtype=F32)
        return jnp.einsum('hij,hg->higj', wb, eye).reshape(w, w).astype(BF16)

    row = pl.BlockSpec((tl, w), lambda b, t: (b * nt + t, 0))
    vec = _const_spec((1, w))
    return pl.pallas_call(
        functools.partial(_lru_kernel, tl=tl),
        grid=(bsz, nt),
        in_specs=[row, row, _const_spec((4, w)), vec, _const_spec((w, w)), vec, _const_spec((w, w)), vec, vec],
        out_specs=row,
        out_shape=jax.ShapeDtypeStruct((tp, w), F32),
        scratch_shapes=[pltpu.VMEM((tl + 8, w), F32), pltpu.VMEM((tl, w), F32),
                        pltpu.VMEM((tl, w), F32), pltpu.VMEM((1, w), F32)],
        compiler_params=_cparams(("parallel", "arbitrary")),
        name="rglru",
    )(rec, gate, conv_w, conv_b.reshape(1, w), dense(w_a), b_a.reshape(1, w), dense(w_x), b_x.reshape(1, w),
      lam.reshape(1, w))


def _ab_out_kernel(y5_ref, lru_ref, h_ref, wglu_ref, bglu_ref, na_ref, nb_ref, wo_ref, o_ref, *, s5w):
    y = _gelu(y5_ref[...])
    ya = y * _sigmoid(_bdot(y, wglu_ref[...]) + bglu_ref[...])
    ya = _rms(ya, na_ref[...])
    yb = _rms(lru_ref[...], nb_ref[...])
    o_ref[...] = h_ref[...] + _bdot(ya, wo_ref[:s5w, :]) + _bdot(yb, wo_ref[s5w:, :])


def _ab_out(y5, lru, h, w_glu, b_glu, norm_a, norm_b, w_out):
    tp, d = h.shape
    s5w, lruw = y5.shape[1], lru.shape[1]
    tm = _pick_tile(tp, 512)
    row = lambda n: pl.BlockSpec((tm, n), lambda i: (i, 0))
    return pl.pallas_call(
        functools.partial(_ab_out_kernel, s5w=s5w),
        grid=(tp // tm,),
        in_specs=[row(s5w), row(lruw), row(d), _const_spec((s5w, s5w)), _const_spec((1, s5w)),
                  _const_spec((1, s5w)), _const_spec((1, lruw)), _const_spec(w_out.shape)],
        out_specs=row(d),
        out_shape=jax.ShapeDtypeStruct((tp, d), F32),
        compiler_params=_cparams(("parallel",)),
        name="ab_out",
    )(y5, lru, h, w_glu.astype(BF16), b_glu.reshape(1, s5w), norm_a.reshape(1, s5w),
      norm_b.reshape(1, lruw), w_out.astype(BF16))


def _route_kernel(h_ref, g_ref, wr_ref, br_ref, xn_ref, rt_ref):
    xn = _rms(h_ref[...], g_ref[...])
    xn_ref[...] = xn.astype(xn_ref.dtype)
    lg = _hdot(xn, wr_ref[...]) + br_ref[...]
    lane = lax.broadcasted_iota(jnp.int32, lg.shape, 1).astype(F32)
    big = float(lg.shape[1])
    neg = -jnp.inf
    gl = jnp.where(lane < N_GROUPS, lg, neg)
    mg = jnp.max(gl, axis=-1, keepdims=True)
    gidx = jnp.min(jnp.where(gl == mg, lane, big), axis=-1, keepdims=True)
    pg_sel = 1.0 / jnp.sum(jnp.exp(gl - mg), axis=-1, keepdims=True)
    lo = N_GROUPS + EXPERTS_PER_GROUP * gidx
    le = jnp.where(lane >= lo, jnp.where(lane < lo + EXPERTS_PER_GROUP, lg, neg), neg)
    v1 = jnp.max(le, axis=-1, keepdims=True)
    i1 = jnp.min(jnp.where(le == v1, lane, big), axis=-1, keepdims=True)
    le2 = jnp.where(lane == i1, neg, le)
    v2 = jnp.max(le2, axis=-1, keepdims=True)
    i2 = jnp.min(jnp.where(le2 == v2, lane, big), axis=-1, keepdims=True)
    e2 = jnp.exp(v2 - v1)
    w1 = pg_sel / (1.0 + e2)
    w2 = w1 * e2
    rt_ref[...] = jnp.where(lane == 0.0, w1, jnp.where(lane == 1.0, w2, jnp.where(
        lane == 2.0, i1 - N_GROUPS, jnp.where(lane == 3.0, i2 - N_GROUPS, 0.0))))


def _route(h, g, wr_g, br_g, wr_e, br_e):
    tp, d = h.shape
    tm = _pick_tile(tp, 512)
    nl = 128
    wr = jnp.zeros((d, nl), F32).at[:, :N_GROUPS].set(wr_g).at[:, N_GROUPS:N_GROUPS + N_EXPERTS].set(wr_e)
    br = jnp.zeros((1, nl), F32).at[0, :N_GROUPS].set(br_g).at[0, N_GROUPS:N_GROUPS + N_EXPERTS].set(br_e)
    row = lambda n: pl.BlockSpec((tm, n), lambda i: (i, 0))
    return pl.pallas_call(
        _route_kernel,
        grid=(tp // tm,),
        in_specs=[row(d), _const_spec((1, d)), _const_spec((d, nl)), _const_spec((1, nl))],
        out_specs=[row(d), row(nl)],
        out_shape=[jax.ShapeDtypeStruct((tp, d), F32), jax.ShapeDtypeStruct((tp, nl), F32)],
        compiler_params=_cparams(("parallel",)),
        name="moe_route",
    )(h, g.reshape(1, d), wr, br)


def _gmm_kernel(te_ref, tv_ref, x_ref, wg_ref, wu_ref, wd_ref, o_ref):
    i = pl.program_id(0)

    @pl.when(tv_ref[i] != 0)
    def _():
        x = x_ref[...].astype(BF16)
        hg = jnp.dot(x, wg_ref[0], preferred_element_type=F32)
        hu = jnp.dot(x, wu_ref[0], preferred_element_type=F32)
        hid = hg * _sigmoid(hg) * hu
        o_ref[...] = _bdot(hid, wd_ref[0])

    @pl.when(tv_ref[i] == 0)
    def _():
        o_ref[...] = jnp.zeros(o_ref.shape, o_ref.dtype)


def _moe(h, g, wr_g, br_g, wr_e, br_e, w_gate, w_up, w_down):
    tp, d = h.shape
    f = w_gate.shape[-1]
    xn, rt = _route(h, g, wr_g, br_g, wr_e, br_e)
    gate = rt[:, 0:2]
    eid = rt[:, 2:4].astype(jnp.int32)
    tmm = MOE_TILE
    na = 2 * tp
    e_flat = eid.reshape(na)
    onehot = (e_flat[:, None] == jnp.arange(N_EXPERTS, dtype=jnp.int32)[None, :]).astype(jnp.int32)
    csum = jnp.cumsum(onehot, axis=0)
    counts = csum[-1]
    rank = jnp.sum(onehot * csum, axis=1) - 1
    padded = ((counts + tmm - 1) // tmm) * tmm
    ends = jnp.cumsum(padded)
    starts = ends - padded
    dest = starts[e_flat] + rank
    n_tiles = -(-na // tmm) + N_EXPERTS
    nrows = n_tiles * tmm
    tile_start = jnp.arange(n_tiles, dtype=jnp.int32) * tmm
    tile_e = jnp.sum((ends[None, :] <= tile_start[:, None]).astype(jnp.int32), axis=1)
    tile_v = (tile_e < N_EXPERTS).astype(jnp.int32)
    tile_e = jnp.minimum(tile_e, N_EXPERTS - 1)
    src = jnp.zeros((nrows,), jnp.int32).at[dest].set(jnp.arange(na, dtype=jnp.int32) // 2)
    xs = jnp.take(xn, src, axis=0)
    ys = pl.pallas_call(
        _gmm_kernel,
        grid_spec=pltpu.PrefetchScalarGridSpec(
            num_scalar_prefetch=2,
            grid=(n_tiles,),
            in_specs=[pl.BlockSpec((tmm, d), lambda i, te, tv: (i, 0)),
                      pl.BlockSpec((1, d, f), lambda i, te, tv: (te[i], 0, 0)),
                      pl.BlockSpec((1, d, f), lambda i, te, tv: (te[i], 0, 0)),
                      pl.BlockSpec((1, f, d), lambda i, te, tv: (te[i], 0, 0))],
            out_specs=pl.BlockSpec((tmm, d), lambda i, te, tv: (i, 0)),
        ),
        out_shape=jax.ShapeDtypeStruct((nrows, d), F32),
        compiler_params=_cparams(("arbitrary",)),
        name="moe_gmm",
    )(tile_e, tile_v, xs, w_gate.astype(BF16), w_up.astype(BF16), w_down.astype(BF16))
    d2 = dest.reshape(tp, 2)
    return h + gate[:, 0:1] * jnp.take(ys, d2[:, 0], axis=0) + gate[:, 1:2] * jnp.take(ys, d2[:, 1], axis=0)


def _rw_pre_kernel(*refs, tm, lp, has_vres):
    if has_vres:
        (h_ref, hp_ref, g_ref, mu_ref, wr_ref, wk_ref, wv_ref, w0_ref, wl1_ref, wl2_ref,
         a0_ref, al1_ref, al2_ref, gl1_ref, gl2_ref, vf_ref, v0_ref, vl1_ref, vl2_ref,
         r_ref, k_ref, v_ref, w_ref, a_ref, gg_ref) = refs
    else:
        (h_ref, hp_ref, g_ref, mu_ref, wr_ref, wk_ref, wv_ref, w0_ref, wl1_ref, wl2_ref,
         a0_ref, al1_ref, al2_ref, gl1_ref, gl2_ref,
         r_ref, k_ref, v_ref, w_ref, a_ref, gg_ref) = refs
    i = pl.program_id(0)
    g = g_ref[...]
    x = _rms(h_ref[...], g)
    xp8 = _rms(hp_ref[...], g)
    row = lax.broadcasted_iota(jnp.int32, x.shape, 0)
    prev = jnp.where(row == 0, jnp.broadcast_to(xp8[7:8, :], x.shape), pltpu.roll(x, 1, axis=0))
    first = lax.rem(lp - lax.rem(i * tm, lp), lp)
    prev = jnp.where(row == first, 0.0, prev)
    xx = prev - x
    xr, xw, xk, xv, xa, xg = [x + xx * mu_ref[j:j + 1, :] for j in range(6)]
    r_ref[...] = _bdot(xr, wr_ref[...])
    k_ref[...] = _bdot(xk, wk_ref[...])
    v = _bdot(xv, wv_ref[...])
    if has_vres:
        mix = _sigmoid(v0_ref[...] + _bdot(_bdot(xv, vl1_ref[...]), vl2_ref[...]))
        v = v + (vf_ref[...] - v) * mix
    v_ref[...] = v
    w_ref[...] = -_softplus(-(w0_ref[...] + _bdot(jnp.tanh(_bdot(xw, wl1_ref[...])), wl2_ref[...]))) - 0.5
    a_ref[...] = _sigmoid(a0_ref[...] + _bdot(_bdot(xa, al1_ref[...]), al2_ref[...]))
    gg_ref[...] = _bdot(_sigmoid(_bdot(xg, gl1_ref[...])), gl2_ref[...])


def _rw_pre(h, g, mu, w_r, w_k, w_v, w0, w_l1, w_l2, a0, a_l1, a_l2, g_l1, g_l2, lp, v_first, v_res):
    tp, d = h.shape
    tm = _pick_tile(tp, min(256, lp))
    has_vres = v_res is not None
    row = pl.BlockSpec((tm, d), lambda i: (i, 0))
    prev8 = pl.BlockSpec((8, d), lambda i: (jnp.maximum(i * (tm // 8) - 1, 0), 0))
    vec = _const_spec((1, d))
    mu8 = jnp.zeros((8, d), F32).at[:6].set(mu)
    bf = lambda w: w.astype(BF16)
    ins = [h, h, g.reshape(1, d), mu8, bf(w_r), bf(w_k), bf(w_v), w0.reshape(1, d), bf(w_l1), bf(w_l2),
           a0.reshape(1, d), bf(a_l1), bf(a_l2), bf(g_l1), bf(g_l2)]
    specs = [row, prev8, vec, _const_spec((8, d))] + [_const_spec((d, d))] * 3 + [
        vec, _const_spec(w_l1.shape), _const_spec(w_l2.shape),
        vec, _const_spec(a_l1.shape), _const_spec(a_l2.shape), _const_spec(g_l1.shape), _const_spec(g_l2.shape)]
    if has_vres:
        v0, v_l1, v_l2 = v_res
        ins += [v_first, v0.reshape(1, d), bf(v_l1), bf(v_l2)]
        specs += [row, vec, _const_spec(v_l1.shape), _const_spec(v_l2.shape)]
    return pl.pallas_call(
        functools.partial(_rw_pre_kernel, tm=tm, lp=lp, has_vres=has_vres),
        grid=(tp // tm,),
        in_specs=specs,
        out_specs=[row] * 6,
        out_shape=[jax.ShapeDtypeStruct((tp, d), F32)] * 6,
        compiler_params=_cparams(("parallel",)),
        name="rwkv_pre",
    )(*ins)


def _rw_scan_kernel(r_ref, w_ref, k_ref, v_ref, a_ref, kk_ref, ka_ref, rk_ref, lw_ref, lb_ref,
                    o_ref, g_scr, h_scr, q_scr, y0_scr, *, n_chunks, unroll):
    c = RW_CHUNK
    nl = 2 * RW_HEAD
    c2 = 2 * c
    ri = lax.broadcasted_iota(jnp.int32, (c2, nl), 0)
    ci = lax.broadcasted_iota(jnp.int32, (c2, nl), 1)
    own = ((ri >= c) == (ci >= RW_HEAD)).astype(F32)
    t_in = jnp.bitwise_and(ri, c - 1)
    s_in = jnp.bitwise_and(ci, c - 1)
    strict = jnp.where(s_in < t_in, own, 0.0)
    incl = jnp.where(s_in <= t_in, own, 0.0)
    eye_l = (ri == ci).astype(F32)
    same_head = own.astype(BF16)
    ti = lax.broadcasted_iota(jnp.int32, (c, c), 0)
    si = lax.broadcasted_iota(jnp.int32, (c, c), 1)
    tril_c = (si <= ti).astype(BF16)
    kk_w, ka_w, rk_w = kk_ref[...], ka_ref[...], rk_ref[...]

    def stack(x):
        return jnp.concatenate([x, x], axis=0) * own

    def offset(ch):
        return ch * c if isinstance(ch, int) else pl.multiple_of(ch * c, c)

    def groups(fn):
        def body(i, _):
            fn([i * unroll + q for q in range(unroll)])
            return 0
        lax.fori_loop(0, n_chunks // unroll, body, 0)
        if n_chunks % unroll:
            fn(list(range(n_chunks - n_chunks % unroll, n_chunks)))

    def prep(chs):
        offs = [offset(ch) for ch in chs]
        r = [r_ref[pl.ds(o, c), :] for o in offs]
        k = [k_ref[pl.ds(o, c), :] for o in offs]
        v = [v_ref[pl.ds(o, c), :] for o in offs]
        a = [a_ref[pl.ds(o, c), :] for o in offs]
        logw = [-jnp.exp(w_ref[pl.ds(o, c), :]) for o in offs]
        kk = [x * kk_w for x in k]
        ss = _split_dot_many([x * x for x in kk], same_head)
        kk = [x / jnp.maximum(jnp.sqrt(q), 1e-12) for x, q in zip(kk, ss)]
        kmod = [x * (1.0 + (y - 1.0) * ka_w) for x, y in zip(k, a)]
        cum = _split_dot_many(logw, tril_c, rhs=True)
        p_incl = [jnp.exp(x) for x in cum]
        p_inv = [jnp.exp(-x) for x in cum]
        p_end = [jnp.exp(x[c - 1:c, :] - x) for x in cum]
        kka = [x * y for x, y in zip(kk, a)]
        a_s = [stack(-x * jnp.exp(y - z)) for x, y, z in zip(kk, cum, logw)]
        r_s = [stack(x * y) for x, y in zip(r, p_incl)]
        v_s = [stack(x).astype(BF16) for x in v]
        lhs = [jnp.concatenate([x, y], axis=0).astype(BF16) for x, y in zip(a_s, r_s)]
        rhs = [jnp.concatenate([stack(x * z), stack(y * z)], axis=0).astype(BF16) for x, y, z in zip(kka, kmod, p_inv)]
        big = [lax.dot_general(x, y, (((1,), (1,)), ((), ())), preferred_element_type=F32) for x, y in zip(lhs, rhs)]
        lpow = [(x[:c2, :c2] * strict).astype(BF16) for x in big]
        a_ak = [x[:c2, c2:] * strict for x in big]
        a_rb = [x[c2:, :c2] * incl for x in big]
        a_rk = [x[c2:, c2:] * incl for x in big]
        x = [jnp.concatenate([p, _bdot(q, w)], axis=1) for p, q, w in zip(a_s, a_ak, v_s)]
        x = [p + _bdot(q, p) for p, q in zip(x, lpow)]
        for _ in range(5):
            lpow = [jnp.dot(q, q, preferred_element_type=F32).astype(BF16) for q in lpow]
            x = [p + _bdot(q, p) for p, q in zip(x, lpow)]
        xb = [p.astype(BF16) for p in x]
        qy = [jnp.concatenate([p, _bdot(q, w)], axis=1) + _bdot(u, z)
              for p, q, w, u, z in zip(r_s, a_rk, v_s, a_rb, xb)]
        bh_t = [stack(p * q).T for p, q in zip(kka, p_end)]
        kh_t = [stack(p * q).T for p, q in zip(kmod, p_end)]
        gh = [_bdot(p, z) + jnp.concatenate([eye_l * q[c - 1:c, :], _bdot(u, w)], axis=1)
              for p, z, q, u, w in zip(bh_t, xb, p_incl, kh_t, v_s)]
        for ch, p, q in zip(chs, qy, gh):
            q_scr[ch] = p[:c, :nl] + p[c:, :nl]
            y0_scr[ch] = p[:c, nl:] + p[c:, nl:]
            g_scr[ch] = q[:, :nl]
            h_scr[ch] = q[:, nl:]

    groups(prep)

    def seq(ch, st):
        both = _dot3(jnp.concatenate([q_scr[ch], g_scr[ch]], axis=0), st)
        o_ref[pl.ds(offset(ch), c), :] = both[:c] + y0_scr[ch]
        return both[c:] + h_scr[ch]

    lax.fori_loop(0, n_chunks, seq, jnp.zeros((nl, nl), F32))
    lw, lb = lw_ref[...], lb_ref[...]
    avg = (own * (1.0 / RW_HEAD)).astype(BF16)

    def post(chs):
        offs = [offset(ch) for ch in chs]
        y = [o_ref[pl.ds(o, c), :] for o in offs]
        mean = _split_dot_many(y, avg)
        dev = [p - q for p, q in zip(y, mean)]
        var = _split_dot_many([p * p for p in dev], avg)
        kmod = [k_ref[pl.ds(o, c), :] * (1.0 + (a_ref[pl.ds(o, c), :] - 1.0) * ka_w) for o in offs]
        dots = _split_dot_many([r_ref[pl.ds(o, c), :] * p * rk_w for o, p in zip(offs, kmod)], same_head)
        for o, p, q, z in zip(offs, dev, var, dots):
            o_ref[pl.ds(o, c), :] = p * lax.rsqrt(q + LNX_EPS) * lw + lb + z * v_ref[pl.ds(o, c), :]

    groups(post)


def _rw_scan(r, w, k, v, a, k_k, k_a, r_k, lnx_w, lnx_b, bsz, lp):
    tp, d = r.shape
    nl = 2 * RW_HEAD
    nhp = d // nl
    nc = lp // RW_CHUNK
    seq = pl.BlockSpec((lp, nl), lambda b, hp: (b, hp))
    vec = pl.BlockSpec((1, nl), lambda b, hp: (0, hp))
    c = RW_CHUNK
    return pl.pallas_call(
        functools.partial(_rw_scan_kernel, n_chunks=nc, unroll=RW_UNROLL),
        grid=(bsz, nhp),
        in_specs=[seq] * 5 + [vec] * 5,
        out_specs=seq,
        out_shape=jax.ShapeDtypeStruct((tp, d), F32),
        scratch_shapes=[pltpu.VMEM((nc, nl, nl), F32), pltpu.VMEM((nc, nl, nl), F32),
                        pltpu.VMEM((nc, c, nl), F32), pltpu.VMEM((nc, c, nl), F32)],
        compiler_params=_cparams(("parallel", "parallel")),
        name="rwkv_scan",
    )(r, w, k, v, a, k_k.reshape(1, d), k_a.reshape(1, d), r_k.reshape(1, d),
      lnx_w.reshape(1, d), lnx_b.reshape(1, d))


def _rw_out_kernel(y_ref, g_ref, h_ref, wo_ref, o_ref):
    o_ref[...] = h_ref[...] + _bdot(y_ref[...] * g_ref[...], wo_ref[...])


def _rw_out(y, g, h, w_o):
    tp, d = h.shape
    tm = _pick_tile(tp, 512)
    row = pl.BlockSpec((tm, d), lambda i: (i, 0))
    return pl.pallas_call(
        _rw_out_kernel,
        grid=(tp // tm,),
        in_specs=[row, row, row, _const_spec((d, d))],
        out_specs=row,
        out_shape=jax.ShapeDtypeStruct((tp, d), F32),
        compiler_params=_cparams(("parallel",)),
        name="rwkv_out",
    )(y, g, h, w_o.astype(BF16))


def _final_norm_kernel(h_ref, g_ref, o_ref):
    o_ref[...] = _rms(h_ref[...], g_ref[...]).astype(o_ref.dtype)


def _final_norm(h, g, dtype):
    tp, d = h.shape
    tm = _pick_tile(tp, 512)
    row = pl.BlockSpec((tm, d), lambda i: (i, 0))
    return pl.pallas_call(
        _final_norm_kernel,
        grid=(tp // tm,),
        in_specs=[row, _const_spec((1, d))],
        out_specs=row,
        out_shape=jax.ShapeDtypeStruct((tp, d), dtype),
        compiler_params=_cparams(("parallel",)),
        name="final_norm",
    )(h, g.reshape(1, d))


def kernel(x, meta_tokens, norm_mix, norm_ffn, norm_final, ab_w_in, ab_w_out, ab_norm_a, ab_norm_b, s5_lam_re, s5_lam_im, s5_log_dt, s5_b_re, s5_b_im, s5_c_re, s5_c_im, s5_d, s5_w_glu, s5_b_glu, lru_conv_w, lru_conv_b, lru_w_a, lru_b_a, lru_w_x, lru_b_x, lru_lam, rw_mu, rw_w_r, rw_w_k, rw_w_v, rw_w_o, rw_w0, rw_w_l1, rw_w_l2, rw_a0, rw_a_l1, rw_a_l2, rw_v0, rw_v_l1, rw_v_l2, rw_g_l1, rw_g_l2, rw_k_k, rw_k_a, rw_r_k, rw_lnx_w, rw_lnx_b, moe_router_g, moe_router_g_b, moe_router_e, moe_router_e_b, moe_w_gate, moe_w_up, moe_w_down):
    bsz, seq, d = x.shape
    n_meta = meta_tokens.shape[0]
    depth = norm_mix.shape[0]
    ltot = n_meta + seq
    lp = -(-ltot // SEQ_ALIGN) * SEQ_ALIGN
    s5w = s5_w_glu.shape[-1]
    lruw = lru_lam.shape[-1]
    meta = jnp.broadcast_to(meta_tokens.astype(F32)[None], (bsz, n_meta, d))
    h = jnp.concatenate([meta, x.astype(F32), jnp.zeros((bsz, lp - ltot, d), F32)], axis=1).reshape(bsz * lp, d)
    v_first = None
    for layer in range(depth):
        j = layer // 2
        if layer % 2 == 0:
            u, gate, rec = _ab_in(h, norm_mix[layer], ab_w_in[j], s5w, lruw)
            tables = _s5_tables(s5_lam_re[j], s5_lam_im[j], s5_log_dt[j], s5_b_re[j], s5_b_im[j],
                                s5_c_re[j], s5_c_im[j], s5_d[j])
            y5 = _s5_scan(u, tables, bsz, lp)
            lru = _lru(rec, gate, lru_conv_w[j], lru_conv_b[j], lru_w_a[j], lru_b_a[j], lru_w_x[j], lru_b_x[j],
                       lru_lam[j], bsz, lp)
            h = _ab_out(y5, lru, h, s5_w_glu[j], s5_b_glu[j], ab_norm_a[j], ab_norm_b[j], ab_w_out[j])
        else:
            v_res = (rw_v0[j - 1], rw_v_l1[j - 1], rw_v_l2[j - 1]) if j > 0 else None
            r, k, v, w, a, g = _rw_pre(h, norm_mix[layer], rw_mu[j], rw_w_r[j], rw_w_k[j], rw_w_v[j], rw_w0[j],
                                       rw_w_l1[j], rw_w_l2[j], rw_a0[j], rw_a_l1[j], rw_a_l2[j], rw_g_l1[j],
                                       rw_g_l2[j], lp, v_first, v_res)
            if v_first is None:
                v_first = v
            y = _rw_scan(r, w, k, v, a, rw_k_k[j], rw_k_a[j], rw_r_k[j].reshape(-1), rw_lnx_w[j], rw_lnx_b[j],
                         bsz, lp)
            h = _rw_out(y, g, h, rw_w_o[j])
        h = _moe(h, norm_ffn[layer], moe_router_g[layer], moe_router_g_b[layer], moe_router_e[layer],
                 moe_router_e_b[layer], moe_w_gate[layer], moe_w_up[layer], moe_w_down[layer])
    out = _final_norm(h, norm_final, x.dtype)
    return out.reshape(bsz, lp, d)[:, n_meta:ltot]
```

```python
import functools
import math

import jax
import jax.numpy as jnp
from jax import lax
from jax.experimental import pallas as pl
from jax.experimental.pallas import tpu as pltpu

F32 = jnp.float32
BF16 = jnp.bfloat16
HI = lax.Precision.HIGHEST

RMS_EPS = 1e-6
LNX_EPS = 64e-5
N_META = 16
SEQ_ALIGN = 64
S5_CHUNK = 16
S5_LANES = 128
RW_CHUNK = 64
RW_HEAD = 64
RW_UNROLL = 11
LRU_C = 8.0
N_GROUPS = 4
EXPERTS_PER_GROUP = 4
N_EXPERTS = N_GROUPS * EXPERTS_PER_GROUP
MOE_TILE = 256
VMEM_LIMIT = 56 * 1024 * 1024


def _cparams(sem):
    return pltpu.CompilerParams(dimension_semantics=sem, vmem_limit_bytes=VMEM_LIMIT)


def _pick_tile(n, target):
    best = 8
    for t in range(8, min(n, target) + 1, 8):
        if n % t == 0:
            best = t
    return best


def _const_spec(shape):
    nd = len(shape)
    return pl.BlockSpec(shape, lambda *_: (0,) * nd)


def _rms(x, g):
    return x * lax.rsqrt(jnp.mean(x * x, axis=-1, keepdims=True) + RMS_EPS) * g


def _gelu(x):
    return 0.5 * x * (1.0 + jnp.tanh(math.sqrt(2.0 / math.pi) * (x + 0.044715 * (x * x * x))))


def _sigmoid(x):
    return 1.0 / (1.0 + jnp.exp(-x))


def _softplus(x):
    return jnp.maximum(x, 0.0) + jnp.log(1.0 + jnp.exp(-jnp.abs(x)))


def _bdot(a, b):
    return jnp.dot(a.astype(BF16), b.astype(BF16), preferred_element_type=F32)


def _hdot(a, b):
    return jnp.dot(a, b, preferred_element_type=F32, precision=HI)


def _dot3(a, b):
    m = a.shape[0]
    a_hi = a.astype(BF16)
    b_hi = b.astype(BF16)
    a_lo = (a - a_hi.astype(F32)).astype(BF16)
    b_lo = (b - b_hi.astype(F32)).astype(BF16)
    dot = functools.partial(jnp.dot, preferred_element_type=F32)
    top = dot(jnp.concatenate([a_hi, a_lo], axis=0), b_hi)
    return top[:m] + (top[m:] + dot(a_hi, b_lo))


def _split_dot_many(xs, m, rhs=False, pieces=3):
    accs = [None] * len(xs)
    xs = list(xs)
    for _ in range(pieces):
        his = [x.astype(BF16) for x in xs]
        parts = [jnp.dot(m, hi, preferred_element_type=F32) if rhs else jnp.dot(hi, m, preferred_element_type=F32)
                 for hi in his]
        accs = [p if a is None else a + p for a, p in zip(accs, parts)]
        xs = [x - hi.astype(F32) for x, hi in zip(xs, his)]
    return accs


def _ab_in_kernel(h_ref, g_ref, w_ref, u_ref, gate_ref, rec_ref, *, s5w, lruw):
    xn = _rms(h_ref[...], g_ref[...])
    z = _bdot(xn, w_ref[...])
    u_ref[...] = z[:, :s5w].astype(u_ref.dtype)
    gate_ref[...] = z[:, s5w:s5w + lruw]
    rec_ref[...] = z[:, s5w + lruw:]


def _ab_in(h, g, w_in, s5w, lruw):
    tp, d = h.shape
    tm = _pick_tile(tp, 512)
    row = lambda n: pl.BlockSpec((tm, n), lambda i: (i, 0))
    return pl.pallas_call(
        functools.partial(_ab_in_kernel, s5w=s5w, lruw=lruw),
        grid=(tp // tm,),
        in_specs=[row(d), _const_spec((1, d)), _const_spec(w_in.shape)],
        out_specs=[row(s5w), row(lruw), row(lruw)],
        out_shape=[jax.ShapeDtypeStruct((tp, s5w), BF16),
                   jax.ShapeDtypeStruct((tp, lruw), F32),
                   jax.ShapeDtypeStruct((tp, lruw), F32)],
        compiler_params=_cparams(("parallel",)),
        name="ab_in",
    )(h, g.reshape(1, d), w_in.astype(BF16))


def _s5_tables(lam_re, lam_im, log_dt, b_re, b_im, c_re, c_im, d_skip):
    g, p = lam_re.shape
    hh = b_re.shape[-1]
    c = S5_CHUNK
    lr, li = lam_re.astype(F32), lam_im.astype(F32)
    dt = jnp.exp(log_dt.astype(F32))[:, None]
    mag = jnp.exp(lr * dt)
    abar_r = mag * jnp.cos(li * dt)
    abar_i = mag * jnp.sin(li * dt)
    den = lr * lr + li * li
    zr = ((abar_r - 1.0) * lr + abar_i * li) / den
    zi = (abar_i * lr - (abar_r - 1.0) * li) / den
    bbar_r = zr[..., None] * b_re - zi[..., None] * b_im
    bbar_i = zr[..., None] * b_im + zi[..., None] * b_re
    def powers(steps):
        st = steps.astype(F32)[:, None, None]
        pmag = jnp.exp(st * (lr * dt))
        return pmag * jnp.cos(st * (li * dt)), pmag * jnp.sin(st * (li * dt))

    def c_times(pw_r, pw_i):
        return (c_re[None] * pw_r[:, :, None, :] - c_im[None] * pw_i[:, :, None, :],
                c_re[None] * pw_i[:, :, None, :] + c_im[None] * pw_r[:, :, None, :])

    down = (c - 1) - jnp.arange(c)
    rev_r, rev_i = powers(down)
    m1_r = rev_r[:, :, :, None] * bbar_r[None] - rev_i[:, :, :, None] * bbar_i[None]
    m1_i = rev_r[:, :, :, None] * bbar_i[None] + rev_i[:, :, :, None] * bbar_r[None]
    m1_r = jnp.transpose(m1_r, (1, 0, 3, 2))
    m1_i = jnp.transpose(m1_i, (1, 0, 3, 2))
    car, cai = c_times(rev_r, rev_i)
    kern = (jnp.einsum('kghp,gpj->kghj', car, bbar_r, precision=HI)
            - jnp.einsum('kghp,gpj->kghj', cai, bbar_i, precision=HI))
    is_tau0 = (down == 0).astype(F32)[:, None, None, None]
    kern = kern + is_tau0 * (d_skip[:, :, None] * jnp.eye(hh, dtype=F32)[None])[None]
    ca_r, ca_i = c_times(*powers(jnp.arange(1, c + 1)))
    m2_r = jnp.transpose(ca_r, (1, 3, 0, 2))
    m2_i = -jnp.transpose(ca_i, (1, 3, 0, 2))
    adv_r, adv_i = powers(jnp.full((1,), c))
    gb = S5_LANES // hh
    nj = g // gb
    eye = jnp.eye(gb, dtype=F32)
    blk = lambda x: x.reshape((nj, gb) + x.shape[1:])
    m1 = jnp.stack([jnp.transpose(blk(m), (0, 2, 1, 3, 4))[:, :, :, :, None, :] * eye[None, None, :, None, :, None]
                    for m in (m1_r, m1_i)], axis=4)
    m1 = m1.reshape(nj, c * gb * hh, 2 * gb * p).astype(BF16)
    krev = jnp.transpose(kern, (1, 0, 3, 2))
    krev = jnp.transpose(blk(krev), (0, 2, 1, 3, 4))[:, :, :, :, None, :] * eye[None, None, :, None, :, None]
    krev = krev.reshape(nj, c * gb * hh, gb * hh).astype(BF16)
    m2 = jnp.stack([blk(m)[:, :, :, :, None, :] * eye[None, :, None, None, :, None] for m in (m2_r, m2_i)], axis=1)
    m2 = m2.reshape(nj, 2 * gb * p, c * gb * hh).astype(BF16)
    return m1, krev, m2, adv_r.reshape(nj, 1, gb * p), adv_i.reshape(nj, 1, gb * p)


def _s5_kernel(u_ref, m1_ref, kr_ref, m2_ref, ar_ref, ai_ref, y_ref, xe_ref, xin_ref, st_ref, *, n_chunks, nb):
    nl = S5_LANES
    csz = S5_CHUNK
    sw = ar_ref.shape[-1]

    @pl.when(pl.program_id(1) == 0)
    def _():
        st_ref[...] = jnp.zeros(st_ref.shape, F32)

    u = u_ref[0]
    xe_ref[...] = jnp.dot(u, m1_ref[0], preferred_element_type=F32)
    ar = jnp.broadcast_to(ar_ref[0], (nb, sw))
    ai = jnp.broadcast_to(ai_ref[0], (nb, sw))

    def body(c, carry):
        sr, si = carry
        off = pl.multiple_of(c * nb, nb)
        xin_ref[pl.ds(off, nb), :] = jnp.concatenate([sr, si], axis=1)
        e = xe_ref[pl.ds(off, nb), :]
        return (ar * sr - ai * si + e[:, :sw], ar * si + ai * sr + e[:, sw:])

    sr, si = lax.fori_loop(0, n_chunks, body, (st_ref[:, :sw], st_ref[:, sw:]))
    st_ref[:, :sw] = sr
    st_ref[:, sw:] = si
    y_ref[0] = _bdot(xin_ref[...], m2_ref[0])
    for t in range(csz):
        y_ref[0, :, t * nl:(t + 1) * nl] += jnp.dot(u[:, :(t + 1) * nl], kr_ref[0, (csz - 1 - t) * nl:, :],
                                                    preferred_element_type=F32)


def _s5_scan(u, tables, bsz, lp):
    m1, krev, m2, adv_r, adv_i = tables
    nj, kin, sw2 = m1.shape
    c = S5_CHUNK
    nl = S5_LANES
    nc = lp // c
    cpt = max(d for d in range(1, nc + 1) if nc % d == 0 and d * bsz <= 512)
    rows = cpt * bsz
    ug = jnp.transpose(u.reshape(bsz, nc, c, nj, nl), (3, 1, 0, 2, 4)).reshape(nj, nc * bsz, kin)
    y = pl.pallas_call(
        functools.partial(_s5_kernel, n_chunks=cpt, nb=bsz),
        grid=(nj, nc // cpt),
        in_specs=[pl.BlockSpec((1, rows, kin), lambda j, r: (j, r, 0)),
                  pl.BlockSpec((1, kin, sw2), lambda j, r: (j, 0, 0)),
                  pl.BlockSpec((1, kin, nl), lambda j, r: (j, 0, 0)),
                  pl.BlockSpec((1, sw2, kin), lambda j, r: (j, 0, 0)),
                  pl.BlockSpec((1, 1, sw2 // 2), lambda j, r: (j, 0, 0)),
                  pl.BlockSpec((1, 1, sw2 // 2), lambda j, r: (j, 0, 0))],
        out_specs=pl.BlockSpec((1, rows, kin), lambda j, r: (j, r, 0)),
        out_shape=jax.ShapeDtypeStruct((nj, nc * bsz, kin), F32),
        scratch_shapes=[pltpu.VMEM((rows, sw2), F32), pltpu.VMEM((rows, sw2), F32), pltpu.VMEM((bsz, sw2), F32)],
        compiler_params=_cparams(("parallel", "arbitrary")),
        name="s5_scan",
    )(ug, m1, krev, m2, adv_r, adv_i)
    y = jnp.transpose(y.reshape(nj, nc, bsz, c, nl), (2, 1, 3, 0, 4))
    return y.reshape(bsz * lp, nj * nl)


def _lru_kernel(rec_ref, gate_ref, cw_ref, cb_ref, wa_ref, ba_ref, wx_ref, bx_ref, lam_ref,
                o_ref, ext_ref, a_ref, b_ref, h_ref, *, tl):
    t = pl.program_id(1)

    @pl.when(t == 0)
    def _():
        ext_ref[0:8, :] = jnp.zeros((8, ext_ref.shape[1]), F32)
        h_ref[...] = jnp.zeros(h_ref.shape, F32)

    x = rec_ref[...]
    ext_ref[8:, :] = x
    xc = cb_ref[...] + cw_ref[3:4, :] * x
    for k in range(3):
        xc = xc + cw_ref[k:k + 1, :] * ext_ref[5 + k:5 + k + tl, :]
    ext_ref[0:8, :] = x[tl - 8:, :]
    r = _sigmoid(_bdot(xc, wa_ref[...]) + ba_ref[...])
    i = _sigmoid(_bdot(xc, wx_ref[...]) + bx_ref[...])
    log_a = (-LRU_C) * r * _softplus(-lam_ref[...])
    a = jnp.exp(log_a)
    a_ref[...] = a
    b_ref[...] = jnp.sqrt(1.0 - a * a) * (i * xc)

    def body(j, h):
        off = pl.multiple_of(j * 8, 8)
        ab = a_ref[pl.ds(off, 8), :]
        bb = b_ref[pl.ds(off, 8), :]
        rows = []
        for q in range(8):
            h = ab[q:q + 1, :] * h + bb[q:q + 1, :]
            rows.append(h)
        o_ref[pl.ds(off, 8), :] = jnp.concatenate(rows, axis=0) * _gelu(gate_ref[pl.ds(off, 8), :])
        return h

    h_ref[...] = lax.fori_loop(0, tl // 8, body, h_ref[...])


def _lru(rec, gate, conv_w, conv_b, w_a, b_a, w_x, b_x, lam, bsz, lp):
    tp, w = rec.shape
    tl = _pick_tile(lp, 1056)
    nt = lp // tl
    heads, hd, _ = w_a.shape

    def dense(wb):
        eye = jnp.eye(heads, dtype=F32)
        return jnp.einsum('hij,hg->higj', wb, eye).reshape(w, w).astype(BF16)

    row = pl.BlockSpec((tl, w), lambda b, t: (b * nt + t, 0))
    vec = _const_spec((1, w))
    return pl.pallas_call(
        functools.partial(_lru_kernel, tl=tl),
        grid=(bsz, nt),
        in_specs=[row, row, _const_spec((4, w)), vec, _const_spec((w, w)), vec, _const_spec((w, w)), vec, vec],
        out_specs=row,
        out_shape=jax.ShapeDtypeStruct((tp, w), F32),
        scratch_shapes=[pltpu.VMEM((tl + 8, w), F32), pltpu.VMEM((tl, w), F32),
                        pltpu.VMEM((tl, w), F32), pltpu.VMEM((1, w), F32)],
        compiler_params=_cparams(("parallel", "arbitrary")),
        name="rglru",
    )(rec, gate, conv_w, conv_b.reshape(1, w), dense(w_a), b_a.reshape(1, w), dense(w_x), b_x.reshape(1, w),
      lam.reshape(1, w))


def _ab_out_kernel(y5_ref, lru_ref, h_ref, wglu_ref, bglu_ref, na_ref, nb_ref, wo_ref, o_ref, *, s5w):
    y = _gelu(y5_ref[...])
    ya = y * _sigmoid(_bdot(y, wglu_ref[...]) + bglu_ref[...])
    ya = _rms(ya, na_ref[...])
    yb = _rms(lru_ref[...], nb_ref[...])
    o_ref[...] = h_ref[...] + _bdot(ya, wo_ref[:s5w, :]) + _bdot(yb, wo_ref[s5w:, :])


def _ab_out(y5, lru, h, w_glu, b_glu, norm_a, norm_b, w_out):
    tp, d = h.shape
    s5w, lruw = y5.shape[1], lru.shape[1]
    tm = _pick_tile(tp, 512)
    row = lambda n: pl.BlockSpec((tm, n), lambda i: (i, 0))
    return pl.pallas_call(
        functools.partial(_ab_out_kernel, s5w=s5w),
        grid=(tp // tm,),
        in_specs=[row(s5w), row(lruw), row(d), _const_spec((s5w, s5w)), _const_spec((1, s5w)),
                  _const_spec((1, s5w)), _const_spec((1, lruw)), _const_spec(w_out.shape)],
        out_specs=row(d),
        out_shape=jax.ShapeDtypeStruct((tp, d), F32),
        compiler_params=_cparams(("parallel",)),
        name="ab_out",
    )(y5, lru, h, w_glu.astype(BF16), b_glu.reshape(1, s5w), norm_a.reshape(1, s5w),
      norm_b.reshape(1, lruw), w_out.astype(BF16))


def _route_kernel(h_ref, g_ref, wr_ref, br_ref, xn_ref, rt_ref):
    xn = _rms(h_ref[...], g_ref[...])
    xn_ref[...] = xn.astype(xn_ref.dtype)
    lg = _hdot(xn, wr_ref[...]) + br_ref[...]
    lane = lax.broadcasted_iota(jnp.int32, lg.shape, 1).astype(F32)
    big = float(lg.shape[1])
    neg = -jnp.inf
    gl = jnp.where(lane < N_GROUPS, lg, neg)
    mg = jnp.max(gl, axis=-1, keepdims=True)
    gidx = jnp.min(jnp.where(gl == mg, lane, big), axis=-1, keepdims=True)
    pg_sel = 1.0 / jnp.sum(jnp.exp(gl - mg), axis=-1, keepdims=True)
    lo = N_GROUPS + EXPERTS_PER_GROUP * gidx
    le = jnp.where(lane >= lo, jnp.where(lane < lo + EXPERTS_PER_GROUP, lg, neg), neg)
    v1 = jnp.max(le, axis=-1, keepdims=True)
    i1 = jnp.min(jnp.where(le == v1, lane, big), axis=-1, keepdims=True)
    le2 = jnp.where(lane == i1, neg, le)
    v2 = jnp.max(le2, axis=-1, keepdims=True)
    i2 = jnp.min(jnp.where(le2 == v2, lane, big), axis=-1, keepdims=True)
    e2 = jnp.exp(v2 - v1)
    w1 = pg_sel / (1.0 + e2)
    w2 = w1 * e2
    rt_ref[...] = jnp.where(lane == 0.0, w1, jnp.where(lane == 1.0, w2, jnp.where(
        lane == 2.0, i1 - N_GROUPS, jnp.where(lane == 3.0, i2 - N_GROUPS, 0.0))))


def _route(h, g, wr_g, br_g, wr_e, br_e):
    tp, d = h.shape
    tm = _pick_tile(tp, 512)
    nl = 128
    wr = jnp.zeros((d, nl), F32).at[:, :N_GROUPS].set(wr_g).at[:, N_GROUPS:N_GROUPS + N_EXPERTS].set(wr_e)
    br = jnp.zeros((1, nl), F32).at[0, :N_GROUPS].set(br_g).at[0, N_GROUPS:N_GROUPS + N_EXPERTS].set(br_e)
    row = lambda n: pl.BlockSpec((tm, n), lambda i: (i, 0))
    return pl.pallas_call(
        _route_kernel,
        grid=(tp // tm,),
        in_specs=[row(d), _const_spec((1, d)), _const_spec((d, nl)), _const_spec((1, nl))],
        out_specs=[row(d), row(nl)],
        out_shape=[jax.ShapeDtypeStruct((tp, d), F32), jax.ShapeDtypeStruct((tp, nl), F32)],
        compiler_params=_cparams(("parallel",)),
        name="moe_route",
    )(h, g.reshape(1, d), wr, br)


def _gmm_kernel(te_ref, tv_ref, x_ref, wg_ref, wu_ref, wd_ref, o_ref, wg_bf, wu_bf, wd_bf):
    i = pl.program_id(0)

    @pl.when(jnp.logical_or(i == 0, te_ref[i] != te_ref[jnp.maximum(i - 1, 0)]))
    def _():
        wg_bf[...] = wg_ref[0, 0].astype(BF16)
        wu_bf[...] = wu_ref[0, 0].astype(BF16)
        wd_bf[...] = wd_ref[0, 0].astype(BF16)

    @pl.when(tv_ref[i] != 0)
    def _():
        x = x_ref[...].astype(BF16)
        hg = jnp.dot(x, wg_bf[...], preferred_element_type=F32)
        hu = jnp.dot(x, wu_bf[...], preferred_element_type=F32)
        hid = hg * _sigmoid(hg) * hu
        o_ref[...] = _bdot(hid, wd_bf[...])

    @pl.when(tv_ref[i] == 0)
    def _():
        o_ref[...] = jnp.zeros(o_ref.shape, o_ref.dtype)


def _moe(h, g, wr_g, br_g, wr_e, br_e, w_gate, w_up, w_down, layer):
    tp, d = h.shape
    f = w_gate.shape[-1]
    xn, rt = _route(h, g, wr_g, br_g, wr_e, br_e)
    gate = rt[:, 0:2]
    eid = rt[:, 2:4].astype(jnp.int32)
    tmm = MOE_TILE
    na = 2 * tp
    e_flat = eid.reshape(na)
    onehot = (e_flat[:, None] == jnp.arange(N_EXPERTS, dtype=jnp.int32)[None, :]).astype(jnp.int32)
    csum = jnp.cumsum(onehot, axis=0)
    counts = csum[-1]
    rank = jnp.sum(onehot * csum, axis=1) - 1
    padded = ((counts + tmm - 1) // tmm) * tmm
    ends = jnp.cumsum(padded)
    starts = ends - padded
    dest = starts[e_flat] + rank
    n_tiles = -(-na // tmm) + N_EXPERTS
    nrows = n_tiles * tmm
    tile_start = jnp.arange(n_tiles, dtype=jnp.int32) * tmm
    tile_e = jnp.sum((ends[None, :] <= tile_start[:, None]).astype(jnp.int32), axis=1)
    tile_v = (tile_e < N_EXPERTS).astype(jnp.int32)
    tile_e = jnp.minimum(tile_e, N_EXPERTS - 1)
    src = jnp.zeros((nrows,), jnp.int32).at[dest].set(jnp.arange(na, dtype=jnp.int32) // 2)
    xs = jnp.take(xn, src, axis=0)
    ys = pl.pallas_call(
        _gmm_kernel,
        grid_spec=pltpu.PrefetchScalarGridSpec(
            num_scalar_prefetch=2,
            grid=(n_tiles,),
            in_specs=[pl.BlockSpec((tmm, d), lambda i, te, tv: (i, 0)),
                      pl.BlockSpec((1, 1, d, f), lambda i, te, tv: (layer, te[i], 0, 0)),
                      pl.BlockSpec((1, 1, d, f), lambda i, te, tv: (layer, te[i], 0, 0)),
                      pl.BlockSpec((1, 1, f, d), lambda i, te, tv: (layer, te[i], 0, 0))],
            out_specs=pl.BlockSpec((tmm, d), lambda i, te, tv: (i, 0)),
            scratch_shapes=[pltpu.VMEM((d, f), BF16), pltpu.VMEM((d, f), BF16), pltpu.VMEM((f, d), BF16)],
        ),
        out_shape=jax.ShapeDtypeStruct((nrows, d), F32),
        compiler_params=_cparams(("arbitrary",)),
        name="moe_gmm",
    )(tile_e, tile_v, xs, w_gate, w_up, w_down)
    d2 = dest.reshape(tp, 2)
    return h + gate[:, 0:1] * jnp.take(ys, d2[:, 0], axis=0) + gate[:, 1:2] * jnp.take(ys, d2[:, 1], axis=0)


def _rw_pre_kernel(*refs, tm, lp, has_vres):
    if has_vres:
        (h_ref, hp_ref, g_ref, mu_ref, wr_ref, wk_ref, wv_ref, w0_ref, wl1_ref, wl2_ref,
         a0_ref, al1_ref, al2_ref, gl1_ref, gl2_ref, vf_ref, v0_ref, vl1_ref, vl2_ref,
         r_ref, k_ref, v_ref, w_ref, a_ref, gg_ref) = refs
    else:
        (h_ref, hp_ref, g_ref, mu_ref, wr_ref, wk_ref, wv_ref, w0_ref, wl1_ref, wl2_ref,
         a0_ref, al1_ref, al2_ref, gl1_ref, gl2_ref,
         r_ref, k_ref, v_ref, w_ref, a_ref, gg_ref) = refs
    i = pl.program_id(0)
    g = g_ref[...]
    x = _rms(h_ref[...], g)
    xp8 = _rms(hp_ref[...], g)
    row = lax.broadcasted_iota(jnp.int32, x.shape, 0)
    prev = jnp.where(row == 0, jnp.broadcast_to(xp8[7:8, :], x.shape), pltpu.roll(x, 1, axis=0))
    first = lax.rem(lp - lax.rem(i * tm, lp), lp)
    prev = jnp.where(row == first, 0.0, prev)
    xx = prev - x
    xr, xw, xk, xv, xa, xg = [x + xx * mu_ref[j:j + 1, :] for j in range(6)]
    r_ref[...] = _bdot(xr, wr_ref[...])
    k_ref[...] = _bdot(xk, wk_ref[...])
    v = _bdot(xv, wv_ref[...])
    if has_vres:
        mix = _sigmoid(v0_ref[...] + _bdot(_bdot(xv, vl1_ref[...]), vl2_ref[...]))
        v = v + (vf_ref[...] - v) * mix
    v_ref[...] = v
    w_ref[...] = -_softplus(-(w0_ref[...] + _bdot(jnp.tanh(_bdot(xw, wl1_ref[...])), wl2_ref[...]))) - 0.5
    a_ref[...] = _sigmoid(a0_ref[...] + _bdot(_bdot(xa, al1_ref[...]), al2_ref[...]))
    gg_ref[...] = _bdot(_sigmoid(_bdot(xg, gl1_ref[...])), gl2_ref[...])


def _rw_pre(h, g, mu, w_r, w_k, w_v, w0, w_l1, w_l2, a0, a_l1, a_l2, g_l1, g_l2, lp, v_first, v_res):
    tp, d = h.shape
    tm = _pick_tile(tp, min(256, lp))
    has_vres = v_res is not None
    row = pl.BlockSpec((tm, d), lambda i: (i, 0))
    prev8 = pl.BlockSpec((8, d), lambda i: (jnp.maximum(i * (tm // 8) - 1, 0), 0))
    vec = _const_spec((1, d))
    mu8 = jnp.zeros((8, d), F32).at[:6].set(mu)
    bf = lambda w: w.astype(BF16)
    ins = [h, h, g.reshape(1, d), mu8, bf(w_r), bf(w_k), bf(w_v), w0.reshape(1, d), bf(w_l1), bf(w_l2),
           a0.reshape(1, d), bf(a_l1), bf(a_l2), bf(g_l1), bf(g_l2)]
    specs = [row, prev8, vec, _const_spec((8, d))] + [_const_spec((d, d))] * 3 + [
        vec, _const_spec(w_l1.shape), _const_spec(w_l2.shape),
        vec, _const_spec(a_l1.shape), _const_spec(a_l2.shape), _const_spec(g_l1.shape), _const_spec(g_l2.shape)]
    if has_vres:
        v0, v_l1, v_l2 = v_res
        ins += [v_first, v0.reshape(1, d), bf(v_l1), bf(v_l2)]
        specs += [row, vec, _const_spec(v_l1.shape), _const_spec(v_l2.shape)]
    return pl.pallas_call(
        functools.partial(_rw_pre_kernel, tm=tm, lp=lp, has_vres=has_vres),
        grid=(tp // tm,),
        in_specs=specs,
        out_specs=[row] * 6,
        out_shape=[jax.ShapeDtypeStruct((tp, d), F32)] * 6,
        compiler_params=_cparams(("parallel",)),
        name="rwkv_pre",
    )(*ins)


def _rw_scan_kernel(r_ref, w_ref, k_ref, v_ref, a_ref, kk_ref, ka_ref, rk_ref, lw_ref, lb_ref,
                    o_ref, g_scr, h_scr, q_scr, y0_scr, *, n_chunks, unroll):
    c = RW_CHUNK
    nl = 2 * RW_HEAD
    c2 = 2 * c
    ri = lax.broadcasted_iota(jnp.int32, (c2, nl), 0)
    ci = lax.broadcasted_iota(jnp.int32, (c2, nl), 1)
    own = ((ri >= c) == (ci >= RW_HEAD)).astype(F32)
    t_in = jnp.bitwise_and(ri, c - 1)
    s_in = jnp.bitwise_and(ci, c - 1)
    strict = jnp.where(s_in < t_in, own, 0.0)
    incl = jnp.where(s_in <= t_in, own, 0.0)
    eye_l = (ri == ci).astype(F32)
    same_head = own.astype(BF16)
    ti = lax.broadcasted_iota(jnp.int32, (c, c), 0)
    si = lax.broadcasted_iota(jnp.int32, (c, c), 1)
    tril_c = (si <= ti).astype(BF16)
    kk_w, ka_w, rk_w = kk_ref[...], ka_ref[...], rk_ref[...]

    def stack(x):
        return jnp.concatenate([x, x], axis=0) * own

    def offset(ch):
        return ch * c if isinstance(ch, int) else pl.multiple_of(ch * c, c)

    def groups(fn):
        def body(i, _):
            fn([i * unroll + q for q in range(unroll)])
            return 0
        lax.fori_loop(0, n_chunks // unroll, body, 0)
        if n_chunks % unroll:
            fn(list(range(n_chunks - n_chunks % unroll, n_chunks)))

    def prep(chs):
        offs = [offset(ch) for ch in chs]
        r = [r_ref[pl.ds(o, c), :] for o in offs]
        k = [k_ref[pl.ds(o, c), :] for o in offs]
        v = [v_ref[pl.ds(o, c), :] for o in offs]
        a = [a_ref[pl.ds(o, c), :] for o in offs]
        logw = [-jnp.exp(w_ref[pl.ds(o, c), :]) for o in offs]
        kk = [x * kk_w for x in k]
        ss = _split_dot_many([x * x for x in kk], same_head)
        kk = [x / jnp.maximum(jnp.sqrt(q), 1e-12) for x, q in zip(kk, ss)]
        kmod = [x * (1.0 + (y - 1.0) * ka_w) for x, y in zip(k, a)]
        cum = _split_dot_many(logw, tril_c, rhs=True)
        p_incl = [jnp.exp(x) for x in cum]
        p_inv = [jnp.exp(-x) for x in cum]
        p_end = [jnp.exp(x[c - 1:c, :] - x) for x in cum]
        kka = [x * y for x, y in zip(kk, a)]
        a_s = [stack(-x * jnp.exp(y - z)) for x, y, z in zip(kk, cum, logw)]
        r_s = [stack(x * y) for x, y in zip(r, p_incl)]
        v_s = [stack(x).astype(BF16) for x in v]
        lhs = [jnp.concatenate([x, y], axis=0).astype(BF16) for x, y in zip(a_s, r_s)]
        rhs = [jnp.concatenate([stack(x * z), stack(y * z)], axis=0).astype(BF16) for x, y, z in zip(kka, kmod, p_inv)]
        big = [lax.dot_general(x, y, (((1,), (1,)), ((), ())), preferred_element_type=F32) for x, y in zip(lhs, rhs)]
        lpow = [(x[:c2, :c2] * strict).astype(BF16) for x in big]
        a_ak = [x[:c2, c2:] * strict for x in big]
        a_rb = [x[c2:, :c2] * incl for x in big]
        a_rk = [x[c2:, c2:] * incl for x in big]
        x = [jnp.concatenate([p, _bdot(q, w)], axis=1) for p, q, w in zip(a_s, a_ak, v_s)]
        x = [p + _bdot(q, p) for p, q in zip(x, lpow)]
        for _ in range(5):
            lpow = [jnp.dot(q, q, preferred_element_type=F32).astype(BF16) for q in lpow]
            x = [p + _bdot(q, p) for p, q in zip(x, lpow)]
        xb = [p.astype(BF16) for p in x]
        qy = [jnp.concatenate([p, _bdot(q, w)], axis=1) + _bdot(u, z)
              for p, q, w, u, z in zip(r_s, a_rk, v_s, a_rb, xb)]
        bh_t = [stack(p * q).T for p, q in zip(kka, p_end)]
        kh_t = [stack(p * q).T for p, q in zip(kmod, p_end)]
        gh = [_bdot(p, z) + jnp.concatenate([eye_l * q[c - 1:c, :], _bdot(u, w)], axis=1)
              for p, z, q, u, w in zip(bh_t, xb, p_incl, kh_t, v_s)]
        for ch, p, q in zip(chs, qy, gh):
            q_scr[ch] = p[:c, :nl] + p[c:, :nl]
            y0_scr[ch] = p[:c, nl:] + p[c:, nl:]
            g_scr[ch] = q[:, :nl]
            h_scr[ch] = q[:, nl:]

    groups(prep)

    def seq(ch, st):
        both = _dot3(jnp.concatenate([q_scr[ch], g_scr[ch]], axis=0), st)
        o_ref[pl.ds(offset(ch), c), :] = both[:c] + y0_scr[ch]
        return both[c:] + h_scr[ch]

    lax.fori_loop(0, n_chunks, seq, jnp.zeros((nl, nl), F32))
    lw, lb = lw_ref[...], lb_ref[...]

    def post(chs):
        offs = [offset(ch) for ch in chs]
        y = [o_ref[pl.ds(o, c), :] for o in offs]
        kmod = [k_ref[pl.ds(o, c), :] * (1.0 + (a_ref[pl.ds(o, c), :] - 1.0) * ka_w) for o in offs]
        rk = [r_ref[pl.ds(o, c), :] * p * rk_w for o, p in zip(offs, kmod)]
        sums = _split_dot_many([jnp.concatenate([p, p * p, q], axis=0) for p, q in zip(y, rk)], same_head)
        for o, p, s in zip(offs, y, sums):
            mean = s[:c] * (1.0 / RW_HEAD)
            var = s[c:2 * c] * (1.0 / RW_HEAD) - mean * mean
            o_ref[pl.ds(o, c), :] = ((p - mean) * lax.rsqrt(var + LNX_EPS) * lw + lb
                                     + s[2 * c:] * v_ref[pl.ds(o, c), :])

    groups(post)


def _rw_scan(r, w, k, v, a, k_k, k_a, r_k, lnx_w, lnx_b, bsz, lp):
    tp, d = r.shape
    nl = 2 * RW_HEAD
    nhp = d // nl
    nc = lp // RW_CHUNK
    seq = pl.BlockSpec((lp, nl), lambda b, hp: (b, hp))
    vec = pl.BlockSpec((1, nl), lambda b, hp: (0, hp))
    c = RW_CHUNK
    return pl.pallas_call(
        functools.partial(_rw_scan_kernel, n_chunks=nc, unroll=RW_UNROLL),
        grid=(bsz, nhp),
        in_specs=[seq] * 5 + [vec] * 5,
        out_specs=seq,
        out_shape=jax.ShapeDtypeStruct((tp, d), F32),
        scratch_shapes=[pltpu.VMEM((nc, nl, nl), F32), pltpu.VMEM((nc, nl, nl), F32),
                        pltpu.VMEM((nc, c, nl), F32), pltpu.VMEM((nc, c, nl), F32)],
        compiler_params=_cparams(("parallel", "parallel")),
        name="rwkv_scan",
    )(r, w, k, v, a, k_k.reshape(1, d), k_a.reshape(1, d), r_k.reshape(1, d),
      lnx_w.reshape(1, d), lnx_b.reshape(1, d))


def _rw_out_kernel(y_ref, g_ref, h_ref, wo_ref, o_ref):
    o_ref[...] = h_ref[...] + _bdot(y_ref[...] * g_ref[...], wo_ref[...])


def _rw_out(y, g, h, w_o):
    tp, d = h.shape
    tm = _pick_tile(tp, 512)
    row = pl.BlockSpec((tm, d), lambda i: (i, 0))
    return pl.pallas_call(
        _rw_out_kernel,
        grid=(tp // tm,),
        in_specs=[row, row, row, _const_spec((d, d))],
        out_specs=row,
        out_shape=jax.ShapeDtypeStruct((tp, d), F32),
        compiler_params=_cparams(("parallel",)),
        name="rwkv_out",
    )(y, g, h, w_o.astype(BF16))


def _final_norm_kernel(h_ref, g_ref, o_ref):
    o_ref[...] = _rms(h_ref[...], g_ref[...]).astype(o_ref.dtype)


def _final_norm(h, g, dtype):
    tp, d = h.shape
    tm = _pick_tile(tp, 512)
    row = pl.BlockSpec((tm, d), lambda i: (i, 0))
    return pl.pallas_call(
        _final_norm_kernel,
        grid=(tp // tm,),
        in_specs=[row, _const_spec((1, d))],
        out_specs=row,
        out_shape=jax.ShapeDtypeStruct((tp, d), dtype),
        compiler_params=_cparams(("parallel",)),
        name="final_norm",
    )(h, g.reshape(1, d))


def kernel(x, meta_tokens, norm_mix, norm_ffn, norm_final, ab_w_in, ab_w_out, ab_norm_a, ab_norm_b, s5_lam_re, s5_lam_im, s5_log_dt, s5_b_re, s5_b_im, s5_c_re, s5_c_im, s5_d, s5_w_glu, s5_b_glu, lru_conv_w, lru_conv_b, lru_w_a, lru_b_a, lru_w_x, lru_b_x, lru_lam, rw_mu, rw_w_r, rw_w_k, rw_w_v, rw_w_o, rw_w0, rw_w_l1, rw_w_l2, rw_a0, rw_a_l1, rw_a_l2, rw_v0, rw_v_l1, rw_v_l2, rw_g_l1, rw_g_l2, rw_k_k, rw_k_a, rw_r_k, rw_lnx_w, rw_lnx_b, moe_router_g, moe_router_g_b, moe_router_e, moe_router_e_b, moe_w_gate, moe_w_up, moe_w_down):
    bsz, seq, d = x.shape
    n_meta = meta_tokens.shape[0]
    depth = norm_mix.shape[0]
    ltot = n_meta + seq
    lp = -(-ltot // SEQ_ALIGN) * SEQ_ALIGN
    s5w = s5_w_glu.shape[-1]
    lruw = lru_lam.shape[-1]
    meta = jnp.broadcast_to(meta_tokens.astype(F32)[None], (bsz, n_meta, d))
    h = jnp.concatenate([meta, x.astype(F32), jnp.zeros((bsz, lp - ltot, d), F32)], axis=1).reshape(bsz * lp, d)
    v_first = None
    for layer in range(depth):
        j = layer // 2
        if layer % 2 == 0:
            u, gate, rec = _ab_in(h, norm_mix[layer], ab_w_in[j], s5w, lruw)
            tables = _s5_tables(s5_lam_re[j], s5_lam_im[j], s5_log_dt[j], s5_b_re[j], s5_b_im[j],
                                s5_c_re[j], s5_c_im[j], s5_d[j])
            y5 = _s5_scan(u, tables, bsz, lp)
            lru = _lru(rec, gate, lru_conv_w[j], lru_conv_b[j], lru_w_a[j], lru_b_a[j], lru_w_x[j], lru_b_x[j],
                       lru_lam[j], bsz, lp)
            h = _ab_out(y5, lru, h, s5_w_glu[j], s5_b_glu[j], ab_norm_a[j], ab_norm_b[j], ab_w_out[j])
        else:
            v_res = (rw_v0[j - 1], rw_v_l1[j - 1], rw_v_l2[j - 1]) if j > 0 else None
            r, k, v, w, a, g = _rw_pre(h, norm_mix[layer], rw_mu[j], rw_w_r[j], rw_w_k[j], rw_w_v[j], rw_w0[j],
                                       rw_w_l1[j], rw_w_l2[j], rw_a0[j], rw_a_l1[j], rw_a_l2[j], rw_g_l1[j],
                                       rw_g_l2[j], lp, v_first, v_res)
            if v_first is None:
                v_first = v
            y = _rw_scan(r, w, k, v, a, rw_k_k[j], rw_k_a[j], rw_r_k[j].reshape(-1), rw_lnx_w[j], rw_lnx_b[j],
                         bsz, lp)
            h = _rw_out(y, g, h, rw_w_o[j])
        h = _moe(h, norm_ffn[layer], moe_router_g[layer], moe_router_g_b[layer], moe_router_e[layer],
                 moe_router_e_b[layer], moe_w_gate, moe_w_up, moe_w_down, layer)
    out = _final_norm(h, norm_final, x.dtype)
    return out.reshape(bsz, lp, d)[:, n_meta:ltot]
```

```python
import functools
import math

import jax
import jax.numpy as jnp
from jax import lax
from jax.experimental import pallas as pl
from jax.experimental.pallas import tpu as pltpu

F32 = jnp.float32
BF16 = jnp.bfloat16
HI = lax.Precision.HIGHEST

RMS_EPS = 1e-6
LNX_EPS = 64e-5
N_META = 16
SEQ_ALIGN = 64
S5_CHUNK = 16
S5_LANES = 128
RW_CHUNK = 64
RW_HEAD = 64
RW_UNROLL = 11
LRU_C = 8.0
N_GROUPS = 4
EXPERTS_PER_GROUP = 4
N_EXPERTS = N_GROUPS * EXPERTS_PER_GROUP
MOE_TILE = 256
MOE_SPLIT = 2
VMEM_LIMIT = 56 * 1024 * 1024


def _cparams(sem):
    return pltpu.CompilerParams(dimension_semantics=sem, vmem_limit_bytes=VMEM_LIMIT)


def _pick_tile(n, target):
    best = 8
    for t in range(8, min(n, target) + 1, 8):
        if n % t == 0:
            best = t
    return best


def _const_spec(shape):
    nd = len(shape)
    return pl.BlockSpec(shape, lambda *_: (0,) * nd)


def _rms(x, g):
    return x * lax.rsqrt(jnp.mean(x * x, axis=-1, keepdims=True) + RMS_EPS) * g


def _gelu(x):
    return 0.5 * x * (1.0 + jnp.tanh(math.sqrt(2.0 / math.pi) * (x + 0.044715 * (x * x * x))))


def _sigmoid(x):
    return 1.0 / (1.0 + jnp.exp(-x))


def _softplus(x):
    return jnp.maximum(x, 0.0) + jnp.log(1.0 + jnp.exp(-jnp.abs(x)))


def _bdot(a, b):
    return jnp.dot(a.astype(BF16), b.astype(BF16), preferred_element_type=F32)


def _hdot(a, b):
    return jnp.dot(a, b, preferred_element_type=F32, precision=HI)


def _dot3(a, b):
    m = a.shape[0]
    a_hi = a.astype(BF16)
    b_hi = b.astype(BF16)
    a_lo = (a - a_hi.astype(F32)).astype(BF16)
    b_lo = (b - b_hi.astype(F32)).astype(BF16)
    dot = functools.partial(jnp.dot, preferred_element_type=F32)
    top = dot(jnp.concatenate([a_hi, a_lo], axis=0), b_hi)
    return top[:m] + (top[m:] + dot(a_hi, b_lo))


def _split_dot_many(xs, m, rhs=False, pieces=3):
    accs = [None] * len(xs)
    xs = list(xs)
    for _ in range(pieces):
        his = [x.astype(BF16) for x in xs]
        parts = [jnp.dot(m, hi, preferred_element_type=F32) if rhs else jnp.dot(hi, m, preferred_element_type=F32)
                 for hi in his]
        accs = [p if a is None else a + p for a, p in zip(accs, parts)]
        xs = [x - hi.astype(F32) for x, hi in zip(xs, his)]
    return accs


def _ab_in_kernel(h_ref, g_ref, w_ref, u_ref, gate_ref, rec_ref, *, s5w, lruw):
    xn = _rms(h_ref[...], g_ref[...])
    z = _bdot(xn, w_ref[...])
    u_ref[...] = z[:, :s5w].astype(u_ref.dtype)
    gate_ref[...] = z[:, s5w:s5w + lruw]
    rec_ref[...] = z[:, s5w + lruw:]


def _ab_in(h, g, w_in, s5w, lruw):
    tp, d = h.shape
    tm = _pick_tile(tp, 512)
    row = lambda n: pl.BlockSpec((tm, n), lambda i: (i, 0))
    return pl.pallas_call(
        functools.partial(_ab_in_kernel, s5w=s5w, lruw=lruw),
        grid=(tp // tm,),
        in_specs=[row(d), _const_spec((1, d)), _const_spec(w_in.shape)],
        out_specs=[row(s5w), row(lruw), row(lruw)],
        out_shape=[jax.ShapeDtypeStruct((tp, s5w), BF16),
                   jax.ShapeDtypeStruct((tp, lruw), F32),
                   jax.ShapeDtypeStruct((tp, lruw), F32)],
        compiler_params=_cparams(("parallel",)),
        name="ab_in",
    )(h, g.reshape(1, d), w_in.astype(BF16))


def _s5_tables(lam_re, lam_im, log_dt, b_re, b_im, c_re, c_im, d_skip):
    g, p = lam_re.shape
    hh = b_re.shape[-1]
    c = S5_CHUNK
    lr, li = lam_re.astype(F32), lam_im.astype(F32)
    dt = jnp.exp(log_dt.astype(F32))[:, None]
    mag = jnp.exp(lr * dt)
    abar_r = mag * jnp.cos(li * dt)
    abar_i = mag * jnp.sin(li * dt)
    den = lr * lr + li * li
    zr = ((abar_r - 1.0) * lr + abar_i * li) / den
    zi = (abar_i * lr - (abar_r - 1.0) * li) / den
    bbar_r = zr[..., None] * b_re - zi[..., None] * b_im
    bbar_i = zr[..., None] * b_im + zi[..., None] * b_re
    def powers(steps):
        st = steps.astype(F32)[:, None, None]
        pmag = jnp.exp(st * (lr * dt))
        return pmag * jnp.cos(st * (li * dt)), pmag * jnp.sin(st * (li * dt))

    def c_times(pw_r, pw_i):
        return (c_re[None] * pw_r[:, :, None, :] - c_im[None] * pw_i[:, :, None, :],
                c_re[None] * pw_i[:, :, None, :] + c_im[None] * pw_r[:, :, None, :])

    down = (c - 1) - jnp.arange(c)
    rev_r, rev_i = powers(down)
    m1_r = rev_r[:, :, :, None] * bbar_r[None] - rev_i[:, :, :, None] * bbar_i[None]
    m1_i = rev_r[:, :, :, None] * bbar_i[None] + rev_i[:, :, :, None] * bbar_r[None]
    m1_r = jnp.transpose(m1_r, (1, 0, 3, 2))
    m1_i = jnp.transpose(m1_i, (1, 0, 3, 2))
    car, cai = c_times(rev_r, rev_i)
    kern = (jnp.einsum('kghp,gpj->kghj', car, bbar_r, precision=HI)
            - jnp.einsum('kghp,gpj->kghj', cai, bbar_i, precision=HI))
    is_tau0 = (down == 0).astype(F32)[:, None, None, None]
    kern = kern + is_tau0 * (d_skip[:, :, None] * jnp.eye(hh, dtype=F32)[None])[None]
    ca_r, ca_i = c_times(*powers(jnp.arange(1, c + 1)))
    m2_r = jnp.transpose(ca_r, (1, 3, 0, 2))
    m2_i = -jnp.transpose(ca_i, (1, 3, 0, 2))
    adv_r, adv_i = powers(jnp.full((1,), c))
    gb = S5_LANES // hh
    nj = g // gb
    eye = jnp.eye(gb, dtype=F32)
    blk = lambda x: x.reshape((nj, gb) + x.shape[1:])
    m1 = jnp.stack([jnp.transpose(blk(m), (0, 2, 1, 3, 4))[:, :, :, :, None, :] * eye[None, None, :, None, :, None]
                    for m in (m1_r, m1_i)], axis=4)
    m1 = m1.reshape(nj, c * gb * hh, 2 * gb * p).astype(BF16)
    krev = jnp.transpose(kern, (1, 0, 3, 2))
    krev = jnp.transpose(blk(krev), (0, 2, 1, 3, 4))[:, :, :, :, None, :] * eye[None, None, :, None, :, None]
    krev = krev.reshape(nj, c * gb * hh, gb * hh).astype(BF16)
    m2 = jnp.stack([blk(m)[:, :, :, :, None, :] * eye[None, :, None, None, :, None] for m in (m2_r, m2_i)], axis=1)
    m2 = m2.reshape(nj, 2 * gb * p, c * gb * hh).astype(BF16)
    return m1, krev, m2, adv_r.reshape(nj, 1, gb * p), adv_i.reshape(nj, 1, gb * p)


def _s5_kernel(u_ref, m1_ref, kr_ref, m2_ref, ar_ref, ai_ref, y_ref, xe_ref, xin_ref, st_ref, *, n_chunks, nb):
    nl = S5_LANES
    csz = S5_CHUNK
    sw = ar_ref.shape[-1]

    @pl.when(pl.program_id(1) == 0)
    def _():
        st_ref[...] = jnp.zeros(st_ref.shape, F32)

    u = u_ref[0]
    xe_ref[...] = jnp.dot(u, m1_ref[0], preferred_element_type=F32)
    ar = jnp.broadcast_to(ar_ref[0], (nb, sw))
    ai = jnp.broadcast_to(ai_ref[0], (nb, sw))

    def body(c, carry):
        sr, si = carry
        off = pl.multiple_of(c * nb, nb)
        xin_ref[pl.ds(off, nb), :] = jnp.concatenate([sr, si], axis=1)
        e = xe_ref[pl.ds(off, nb), :]
        return (ar * sr - ai * si + e[:, :sw], ar * si + ai * sr + e[:, sw:])

    sr, si = lax.fori_loop(0, n_chunks, body, (st_ref[:, :sw], st_ref[:, sw:]))
    st_ref[:, :sw] = sr
    st_ref[:, sw:] = si
    y_ref[0] = _bdot(xin_ref[...], m2_ref[0])
    for t in range(csz):
        y_ref[0, :, t * nl:(t + 1) * nl] += jnp.dot(u[:, :(t + 1) * nl], kr_ref[0, (csz - 1 - t) * nl:, :],
                                                    preferred_element_type=F32)


def _s5_scan(u, tables, bsz, lp):
    m1, krev, m2, adv_r, adv_i = tables
    nj, kin, sw2 = m1.shape
    c = S5_CHUNK
    nl = S5_LANES
    nc = lp // c
    cpt = max(d for d in range(1, nc + 1) if nc % d == 0 and d * bsz <= 512)
    rows = cpt * bsz
    ug = jnp.transpose(u.reshape(bsz, nc, c, nj, nl), (3, 1, 0, 2, 4)).reshape(nj, nc * bsz, kin)
    y = pl.pallas_call(
        functools.partial(_s5_kernel, n_chunks=cpt, nb=bsz),
        grid=(nj, nc // cpt),
        in_specs=[pl.BlockSpec((1, rows, kin), lambda j, r: (j, r, 0)),
                  pl.BlockSpec((1, kin, sw2), lambda j, r: (j, 0, 0)),
                  pl.BlockSpec((1, kin, nl), lambda j, r: (j, 0, 0)),
                  pl.BlockSpec((1, sw2, kin), lambda j, r: (j, 0, 0)),
                  pl.BlockSpec((1, 1, sw2 // 2), lambda j, r: (j, 0, 0)),
                  pl.BlockSpec((1, 1, sw2 // 2), lambda j, r: (j, 0, 0))],
        out_specs=pl.BlockSpec((1, rows, kin), lambda j, r: (j, r, 0)),
        out_shape=jax.ShapeDtypeStruct((nj, nc * bsz, kin), F32),
        scratch_shapes=[pltpu.VMEM((rows, sw2), F32), pltpu.VMEM((rows, sw2), F32), pltpu.VMEM((bsz, sw2), F32)],
        compiler_params=_cparams(("parallel", "arbitrary")),
        name="s5_scan",
    )(ug, m1, krev, m2, adv_r, adv_i)
    y = jnp.transpose(y.reshape(nj, nc, bsz, c, nl), (2, 1, 3, 0, 4))
    return y.reshape(bsz * lp, nj * nl)


def _lru_kernel(rec_ref, gate_ref, cw_ref, cb_ref, wa_ref, ba_ref, wx_ref, bx_ref, lam_ref,
                o_ref, ext_ref, a_ref, b_ref, h_ref, *, tl):
    t = pl.program_id(1)

    @pl.when(t == 0)
    def _():
        ext_ref[0:8, :] = jnp.zeros((8, ext_ref.shape[1]), F32)
        h_ref[...] = jnp.zeros(h_ref.shape, F32)

    x = rec_ref[...]
    ext_ref[8:, :] = x
    xc = cb_ref[...] + cw_ref[3:4, :] * x
    for k in range(3):
        xc = xc + cw_ref[k:k + 1, :] * ext_ref[5 + k:5 + k + tl, :]
    ext_ref[0:8, :] = x[tl - 8:, :]
    r = _sigmoid(_bdot(xc, wa_ref[...]) + ba_ref[...])
    i = _sigmoid(_bdot(xc, wx_ref[...]) + bx_ref[...])
    log_a = (-LRU_C) * r * _softplus(-lam_ref[...])
    a = jnp.exp(log_a)
    a_ref[...] = a
    b_ref[...] = jnp.sqrt(1.0 - a * a) * (i * xc)

    def body(j, h):
        off = pl.multiple_of(j * 8, 8)
        ab = a_ref[pl.ds(off, 8), :]
        bb = b_ref[pl.ds(off, 8), :]
        rows = []
        for q in range(8):
            h = ab[q:q + 1, :] * h + bb[q:q + 1, :]
            rows.append(h)
        o_ref[pl.ds(off, 8), :] = jnp.concatenate(rows, axis=0) * _gelu(gate_ref[pl.ds(off, 8), :])
        return h

    h_ref[...] = lax.fori_loop(0, tl // 8, body, h_ref[...])


def _lru(rec, gate, conv_w, conv_b, w_a, b_a, w_x, b_x, lam, bsz, lp):
    tp, w = rec.shape
    tl = _pick_tile(lp, 1056)
    nt = lp // tl
    heads, hd, _ = w_a.shape

    def dense(wb):
        eye = jnp.eye(heads, dtype=F32)
        return jnp.einsum('hij,hg->higj', wb, eye).reshape(w, w).astype(BF16)

    row = pl.BlockSpec((tl, w), lambda b, t: (b * nt + t, 0))
    vec = _const_spec((1, w))
    return pl.pallas_call(
        functools.partial(_lru_kernel, tl=tl),
        grid=(bsz, nt),
        in_specs=[row, row, _const_spec((4, w)), vec, _const_spec((w, w)), vec, _const_spec((w, w)), vec, vec],
        out_specs=row,
        out_shape=jax.ShapeDtypeStruct((tp, w), F32),
        scratch_shapes=[pltpu.VMEM((tl + 8, w), F32), pltpu.VMEM((tl, w), F32),
                        pltpu.VMEM((tl, w), F32), pltpu.VMEM((1, w), F32)],
        compiler_params=_cparams(("parallel", "arbitrary")),
        name="rglru",
    )(rec, gate, conv_w, conv_b.reshape(1, w), dense(w_a), b_a.reshape(1, w), dense(w_x), b_x.reshape(1, w),
      lam.reshape(1, w))


def _ab_out_kernel(y5_ref, lru_ref, h_ref, wglu_ref, bglu_ref, na_ref, nb_ref, wo_ref, o_ref, *, s5w):
    y = _gelu(y5_ref[...])
    ya = y * _sigmoid(_bdot(y, wglu_ref[...]) + bglu_ref[...])
    ya = _rms(ya, na_ref[...])
    yb = _rms(lru_ref[...], nb_ref[...])
    o_ref[...] = h_ref[...] + _bdot(ya, wo_ref[:s5w, :]) + _bdot(yb, wo_ref[s5w:, :])


def _ab_out(y5, lru, h, w_glu, b_glu, norm_a, norm_b, w_out):
    tp, d = h.shape
    s5w, lruw = y5.shape[1], lru.shape[1]
    tm = _pick_tile(tp, 512)
    row = lambda n: pl.BlockSpec((tm, n), lambda i: (i, 0))
    return pl.pallas_call(
        functools.partial(_ab_out_kernel, s5w=s5w),
        grid=(tp // tm,),
        in_specs=[row(s5w), row(lruw), row(d), _const_spec((s5w, s5w)), _const_spec((1, s5w)),
                  _const_spec((1, s5w)), _const_spec((1, lruw)), _const_spec(w_out.shape)],
        out_specs=row(d),
        out_shape=jax.ShapeDtypeStruct((tp, d), F32),
        compiler_params=_cparams(("parallel",)),
        name="ab_out",
    )(y5, lru, h, w_glu.astype(BF16), b_glu.reshape(1, s5w), norm_a.reshape(1, s5w),
      norm_b.reshape(1, lruw), w_out.astype(BF16))


def _route_kernel(h_ref, g_ref, wr_ref, br_ref, xn_ref, rt_ref):
    xn = _rms(h_ref[...], g_ref[...])
    xn_ref[...] = xn.astype(xn_ref.dtype)
    lg = _hdot(xn, wr_ref[...]) + br_ref[...]
    lane = lax.broadcasted_iota(jnp.int32, lg.shape, 1).astype(F32)
    big = float(lg.shape[1])
    neg = -jnp.inf
    gl = jnp.where(lane < N_GROUPS, lg, neg)
    mg = jnp.max(gl, axis=-1, keepdims=True)
    gidx = jnp.min(jnp.where(gl == mg, lane, big), axis=-1, keepdims=True)
    pg_sel = 1.0 / jnp.sum(jnp.exp(gl - mg), axis=-1, keepdims=True)
    lo = N_GROUPS + EXPERTS_PER_GROUP * gidx
    le = jnp.where(lane >= lo, jnp.where(lane < lo + EXPERTS_PER_GROUP, lg, neg), neg)
    v1 = jnp.max(le, axis=-1, keepdims=True)
    i1 = jnp.min(jnp.where(le == v1, lane, big), axis=-1, keepdims=True)
    le2 = jnp.where(lane == i1, neg, le)
    v2 = jnp.max(le2, axis=-1, keepdims=True)
    i2 = jnp.min(jnp.where(le2 == v2, lane, big), axis=-1, keepdims=True)
    e2 = jnp.exp(v2 - v1)
    w1 = pg_sel / (1.0 + e2)
    w2 = w1 * e2
    rt_ref[...] = jnp.where(lane == 0.0, w1, jnp.where(lane == 1.0, w2, jnp.where(
        lane == 2.0, i1 - N_GROUPS, jnp.where(lane == 3.0, i2 - N_GROUPS, 0.0))))


def _route(h, g, wr_g, br_g, wr_e, br_e, row0, tp):
    d = h.shape[1]
    tm = _pick_tile(math.gcd(tp, row0) if row0 else tp, 512)
    blk0 = row0 // tm
    nl = 128
    wr = jnp.zeros((d, nl), F32).at[:, :N_GROUPS].set(wr_g).at[:, N_GROUPS:N_GROUPS + N_EXPERTS].set(wr_e)
    br = jnp.zeros((1, nl), F32).at[0, :N_GROUPS].set(br_g).at[0, N_GROUPS:N_GROUPS + N_EXPERTS].set(br_e)
    row = lambda n: pl.BlockSpec((tm, n), lambda i: (i, 0))
    return pl.pallas_call(
        _route_kernel,
        grid=(tp // tm,),
        in_specs=[pl.BlockSpec((tm, d), lambda i: (i + blk0, 0)), _const_spec((1, d)), _const_spec((d, nl)),
                  _const_spec((1, nl))],
        out_specs=[row(d), row(nl)],
        out_shape=[jax.ShapeDtypeStruct((tp, d), F32), jax.ShapeDtypeStruct((tp, nl), F32)],
        compiler_params=_cparams(("parallel",)),
        name="moe_route",
    )(h, g.reshape(1, d), wr, br)


def _gmm_kernel(te_ref, tv_ref, x_ref, wg_ref, wu_ref, wd_ref, o_ref, wg_bf, wu_bf, wd_bf):
    i = pl.program_id(0)

    @pl.when(jnp.logical_or(i == 0, te_ref[i] != te_ref[jnp.maximum(i - 1, 0)]))
    def _():
        wg_bf[...] = wg_ref[0, 0].astype(BF16)
        wu_bf[...] = wu_ref[0, 0].astype(BF16)
        wd_bf[...] = wd_ref[0, 0].astype(BF16)

    @pl.when(tv_ref[i] != 0)
    def _():
        x = x_ref[...].astype(BF16)
        hg = jnp.dot(x, wg_bf[...], preferred_element_type=F32)
        hu = jnp.dot(x, wu_bf[...], preferred_element_type=F32)
        hid = hg * _sigmoid(hg) * hu
        o_ref[...] = _bdot(hid, wd_bf[...])

    @pl.when(tv_ref[i] == 0)
    def _():
        o_ref[...] = jnp.zeros(o_ref.shape, o_ref.dtype)


def _moe(h, g, wr_g, br_g, wr_e, br_e, w_gate, w_up, w_down, layer, row0, tp):
    d = h.shape[1]
    f = w_gate.shape[-1]
    xn, rt = _route(h, g, wr_g, br_g, wr_e, br_e, row0, tp)
    gate = rt[:, 0:2]
    eid = rt[:, 2:4].astype(jnp.int32)
    tmm = MOE_TILE
    na = 2 * tp
    e_flat = eid.reshape(na)
    onehot = (e_flat[:, None] == jnp.arange(N_EXPERTS, dtype=jnp.int32)[None, :]).astype(jnp.int32)
    csum = jnp.cumsum(onehot, axis=0)
    counts = csum[-1]
    rank = jnp.sum(onehot * csum, axis=1) - 1
    padded = ((counts + tmm - 1) // tmm) * tmm
    ends = jnp.cumsum(padded)
    starts = ends - padded
    dest = starts[e_flat] + rank
    n_tiles = -(-na // tmm) + N_EXPERTS
    nrows = n_tiles * tmm
    tile_start = jnp.arange(n_tiles, dtype=jnp.int32) * tmm
    tile_e = jnp.sum((ends[None, :] <= tile_start[:, None]).astype(jnp.int32), axis=1)
    tile_v = (tile_e < N_EXPERTS).astype(jnp.int32)
    tile_e = jnp.minimum(tile_e, N_EXPERTS - 1)
    src = (jnp.arange(nrows, dtype=jnp.int32) % tp).at[dest].set(jnp.arange(na, dtype=jnp.int32) // 2)
    xs = jnp.take(xn, src, axis=0)
    ys = pl.pallas_call(
        _gmm_kernel,
        grid_spec=pltpu.PrefetchScalarGridSpec(
            num_scalar_prefetch=2,
            grid=(n_tiles,),
            in_specs=[pl.BlockSpec((tmm, d), lambda i, te, tv: (i, 0)),
                      pl.BlockSpec((1, 1, d, f), lambda i, te, tv: (layer, te[i], 0, 0)),
                      pl.BlockSpec((1, 1, d, f), lambda i, te, tv: (layer, te[i], 0, 0)),
                      pl.BlockSpec((1, 1, f, d), lambda i, te, tv: (layer, te[i], 0, 0))],
            out_specs=pl.BlockSpec((tmm, d), lambda i, te, tv: (i, 0)),
            scratch_shapes=[pltpu.VMEM((d, f), BF16), pltpu.VMEM((d, f), BF16), pltpu.VMEM((f, d), BF16)],
        ),
        out_shape=jax.ShapeDtypeStruct((nrows, d), F32),
        compiler_params=_cparams(("arbitrary",)),
        name="moe_gmm",
    )(tile_e, tile_v, xs, w_gate, w_up, w_down)
    d2 = dest.reshape(tp, 2)
    return (lax.slice_in_dim(h, row0, row0 + tp, axis=0)
            + gate[:, 0:1] * jnp.take(ys, d2[:, 0], axis=0) + gate[:, 1:2] * jnp.take(ys, d2[:, 1], axis=0))


def _rw_pre_kernel(*refs, tm, lp, has_vres):
    if has_vres:
        (h_ref, hp_ref, g_ref, mu_ref, wr_ref, wk_ref, wv_ref, w0_ref, wl1_ref, wl2_ref,
         a0_ref, al1_ref, al2_ref, gl1_ref, gl2_ref, vf_ref, v0_ref, vl1_ref, vl2_ref,
         r_ref, k_ref, v_ref, w_ref, a_ref, gg_ref) = refs
    else:
        (h_ref, hp_ref, g_ref, mu_ref, wr_ref, wk_ref, wv_ref, w0_ref, wl1_ref, wl2_ref,
         a0_ref, al1_ref, al2_ref, gl1_ref, gl2_ref,
         r_ref, k_ref, v_ref, w_ref, a_ref, gg_ref) = refs
    i = pl.program_id(0)
    g = g_ref[...]
    x = _rms(h_ref[...], g)
    xp8 = _rms(hp_ref[...], g)
    row = lax.broadcasted_iota(jnp.int32, x.shape, 0)
    prev = jnp.where(row == 0, jnp.broadcast_to(xp8[7:8, :], x.shape), pltpu.roll(x, 1, axis=0))
    first = lax.rem(lp - lax.rem(i * tm, lp), lp)
    prev = jnp.where(row == first, 0.0, prev)
    xx = prev - x
    xr, xw, xk, xv, xa, xg = [x + xx * mu_ref[j:j + 1, :] for j in range(6)]
    r_ref[...] = _bdot(xr, wr_ref[...])
    k_ref[...] = _bdot(xk, wk_ref[...])
    v = _bdot(xv, wv_ref[...])
    if has_vres:
        mix = _sigmoid(v0_ref[...] + _bdot(_bdot(xv, vl1_ref[...]), vl2_ref[...]))
        v = v + (vf_ref[...] - v) * mix
    v_ref[...] = v
    w_ref[...] = -_softplus(-(w0_ref[...] + _bdot(jnp.tanh(_bdot(xw, wl1_ref[...])), wl2_ref[...]))) - 0.5
    a_ref[...] = _sigmoid(a0_ref[...] + _bdot(_bdot(xa, al1_ref[...]), al2_ref[...]))
    gg_ref[...] = _bdot(_sigmoid(_bdot(xg, gl1_ref[...])), gl2_ref[...])


def _rw_pre(h, g, mu, w_r, w_k, w_v, w0, w_l1, w_l2, a0, a_l1, a_l2, g_l1, g_l2, lp, v_first, v_res):
    tp, d = h.shape
    tm = _pick_tile(tp, min(256, lp))
    has_vres = v_res is not None
    row = pl.BlockSpec((tm, d), lambda i: (i, 0))
    prev8 = pl.BlockSpec((8, d), lambda i: (jnp.maximum(i * (tm // 8) - 1, 0), 0))
    vec = _const_spec((1, d))
    mu8 = jnp.zeros((8, d), F32).at[:6].set(mu)
    bf = lambda w: w.astype(BF16)
    ins = [h, h, g.reshape(1, d), mu8, bf(w_r), bf(w_k), bf(w_v), w0.reshape(1, d), bf(w_l1), bf(w_l2),
           a0.reshape(1, d), bf(a_l1), bf(a_l2), bf(g_l1), bf(g_l2)]
    specs = [row, prev8, vec, _const_spec((8, d))] + [_const_spec((d, d))] * 3 + [
        vec, _const_spec(w_l1.shape), _const_spec(w_l2.shape),
        vec, _const_spec(a_l1.shape), _const_spec(a_l2.shape), _const_spec(g_l1.shape), _const_spec(g_l2.shape)]
    if has_vres:
        v0, v_l1, v_l2 = v_res
        ins += [v_first, v0.reshape(1, d), bf(v_l1), bf(v_l2)]
        specs += [row, vec, _const_spec(v_l1.shape), _const_spec(v_l2.shape)]
    return pl.pallas_call(
        functools.partial(_rw_pre_kernel, tm=tm, lp=lp, has_vres=has_vres),
        grid=(tp // tm,),
        in_specs=specs,
        out_specs=[row] * 6,
        out_shape=[jax.ShapeDtypeStruct((tp, d), F32)] * 6,
        compiler_params=_cparams(("parallel",)),
        name="rwkv_pre",
    )(*ins)


def _rw_scan_kernel(r_ref, w_ref, k_ref, v_ref, a_ref, kk_ref, ka_ref, rk_ref, lw_ref, lb_ref,
                    o_ref, g_scr, h_scr, q_scr, y0_scr, *, n_chunks, unroll):
    c = RW_CHUNK
    nl = 2 * RW_HEAD
    c2 = 2 * c
    ri = lax.broadcasted_iota(jnp.int32, (c2, nl), 0)
    ci = lax.broadcasted_iota(jnp.int32, (c2, nl), 1)
    own = ((ri >= c) == (ci >= RW_HEAD)).astype(F32)
    t_in = jnp.bitwise_and(ri, c - 1)
    s_in = jnp.bitwise_and(ci, c - 1)
    strict = jnp.where(s_in < t_in, own, 0.0)
    incl = jnp.where(s_in <= t_in, own, 0.0)
    eye_l = (ri == ci).astype(F32)
    same_head = own.astype(BF16)
    ti = lax.broadcasted_iota(jnp.int32, (c, c), 0)
    si = lax.broadcasted_iota(jnp.int32, (c, c), 1)
    tril_c = (si <= ti).astype(BF16)
    kk_w, ka_w, rk_w = kk_ref[...], ka_ref[...], rk_ref[...]

    def stack(x):
        return jnp.concatenate([x, x], axis=0) * own

    def offset(ch):
        return ch * c if isinstance(ch, int) else pl.multiple_of(ch * c, c)

    def groups(fn):
        def body(i, _):
            fn([i * unroll + q for q in range(unroll)])
            return 0
        lax.fori_loop(0, n_chunks // unroll, body, 0)
        if n_chunks % unroll:
            fn(list(range(n_chunks - n_chunks % unroll, n_chunks)))

    def prep(chs):
        offs = [offset(ch) for ch in chs]
        r = [r_ref[pl.ds(o, c), :] for o in offs]
        k = [k_ref[pl.ds(o, c), :] for o in offs]
        v = [v_ref[pl.ds(o, c), :] for o in offs]
        a = [a_ref[pl.ds(o, c), :] for o in offs]
        logw = [-jnp.exp(w_ref[pl.ds(o, c), :]) for o in offs]
        kk = [x * kk_w for x in k]
        ss = _split_dot_many([x * x for x in kk], same_head)
        kk = [x / jnp.maximum(jnp.sqrt(q), 1e-12) for x, q in zip(kk, ss)]
        kmod = [x * (1.0 + (y - 1.0) * ka_w) for x, y in zip(k, a)]
        cum = _split_dot_many(logw, tril_c, rhs=True)
        p_incl = [jnp.exp(x) for x in cum]
        p_inv = [jnp.exp(-x) for x in cum]
        p_end = [jnp.exp(x[c - 1:c, :] - x) for x in cum]
        kka = [x * y for x, y in zip(kk, a)]
        a_s = [stack(-x * jnp.exp(y - z)) for x, y, z in zip(kk, cum, logw)]
        r_s = [stack(x * y) for x, y in zip(r, p_incl)]
        v_s = [stack(x).astype(BF16) for x in v]
        lhs = [jnp.concatenate([x, y], axis=0).astype(BF16) for x, y in zip(a_s, r_s)]
        rhs = [jnp.concatenate([stack(x * z), stack(y * z)], axis=0).astype(BF16) for x, y, z in zip(kka, kmod, p_inv)]
        big = [lax.dot_general(x, y, (((1,), (1,)), ((), ())), preferred_element_type=F32) for x, y in zip(lhs, rhs)]
        lpow = [(x[:c2, :c2] * strict).astype(BF16) for x in big]
        a_ak = [x[:c2, c2:] * strict for x in big]
        a_rb = [x[c2:, :c2] * incl for x in big]
        a_rk = [x[c2:, c2:] * incl for x in big]
        x = [jnp.concatenate([p, _bdot(q, w)], axis=1) for p, q, w in zip(a_s, a_ak, v_s)]
        x = [p + _bdot(q, p) for p, q in zip(x, lpow)]
        for _ in range(5):
            lpow = [jnp.dot(q, q, preferred_element_type=F32).astype(BF16) for q in lpow]
            x = [p + _bdot(q, p) for p, q in zip(x, lpow)]
        xb = [p.astype(BF16) for p in x]
        qy = [jnp.concatenate([p, _bdot(q, w)], axis=1) + _bdot(u, z)
              for p, q, w, u, z in zip(r_s, a_rk, v_s, a_rb, xb)]
        bh_t = [stack(p * q).T for p, q in zip(kka, p_end)]
        kh_t = [stack(p * q).T for p, q in zip(kmod, p_end)]
        gh = [_bdot(p, z) + jnp.concatenate([eye_l * q[c - 1:c, :], _bdot(u, w)], axis=1)
              for p, z, q, u, w in zip(bh_t, xb, p_incl, kh_t, v_s)]
        for ch, p, q in zip(chs, qy, gh):
            q_scr[ch] = p[:c, :nl] + p[c:, :nl]
            y0_scr[ch] = p[:c, nl:] + p[c:, nl:]
            g_scr[ch] = q[:, :nl]
            h_scr[ch] = q[:, nl:]

    groups(prep)

    def seq(ch, st):
        both = _dot3(jnp.concatenate([q_scr[ch], g_scr[ch]], axis=0), st)
        o_ref[pl.ds(offset(ch), c), :] = both[:c] + y0_scr[ch]
        return both[c:] + h_scr[ch]

    lax.fori_loop(0, n_chunks, seq, jnp.zeros((nl, nl), F32))
    lw, lb = lw_ref[...], lb_ref[...]

    def post(chs):
        offs = [offset(ch) for ch in chs]
        y = [o_ref[pl.ds(o, c), :] for o in offs]
        kmod = [k_ref[pl.ds(o, c), :] * (1.0 + (a_ref[pl.ds(o, c), :] - 1.0) * ka_w) for o in offs]
        rk = [r_ref[pl.ds(o, c), :] * p * rk_w for o, p in zip(offs, kmod)]
        sums = _split_dot_many([jnp.concatenate([p, p * p, q], axis=0) for p, q in zip(y, rk)], same_head)
        for o, p, s in zip(offs, y, sums):
            mean = s[:c] * (1.0 / RW_HEAD)
            var = s[c:2 * c] * (1.0 / RW_HEAD) - mean * mean
            o_ref[pl.ds(o, c), :] = ((p - mean) * lax.rsqrt(var + LNX_EPS) * lw + lb
                                     + s[2 * c:] * v_ref[pl.ds(o, c), :])

    groups(post)


def _rw_scan(r, w, k, v, a, k_k, k_a, r_k, lnx_w, lnx_b, bsz, lp):
    tp, d = r.shape
    nl = 2 * RW_HEAD
    nhp = d // nl
    nc = lp // RW_CHUNK
    seq = pl.BlockSpec((lp, nl), lambda b, hp: (b, hp))
    vec = pl.BlockSpec((1, nl), lambda b, hp: (0, hp))
    c = RW_CHUNK
    return pl.pallas_call(
        functools.partial(_rw_scan_kernel, n_chunks=nc, unroll=RW_UNROLL),
        grid=(bsz, nhp),
        in_specs=[seq] * 5 + [vec] * 5,
        out_specs=seq,
        out_shape=jax.ShapeDtypeStruct((tp, d), F32),
        scratch_shapes=[pltpu.VMEM((nc, nl, nl), F32), pltpu.VMEM((nc, nl, nl), F32),
                        pltpu.VMEM((nc, c, nl), F32), pltpu.VMEM((nc, c, nl), F32)],
        compiler_params=_cparams(("parallel", "parallel")),
        name="rwkv_scan",
    )(r, w, k, v, a, k_k.reshape(1, d), k_a.reshape(1, d), r_k.reshape(1, d),
      lnx_w.reshape(1, d), lnx_b.reshape(1, d))


def _rw_out_kernel(y_ref, g_ref, h_ref, wo_ref, o_ref):
    o_ref[...] = h_ref[...] + _bdot(y_ref[...] * g_ref[...], wo_ref[...])


def _rw_out(y, g, h, w_o):
    tp, d = h.shape
    tm = _pick_tile(tp, 512)
    row = pl.BlockSpec((tm, d), lambda i: (i, 0))
    return pl.pallas_call(
        _rw_out_kernel,
        grid=(tp // tm,),
        in_specs=[row, row, row, _const_spec((d, d))],
        out_specs=row,
        out_shape=jax.ShapeDtypeStruct((tp, d), F32),
        compiler_params=_cparams(("parallel",)),
        name="rwkv_out",
    )(y, g, h, w_o.astype(BF16))


def _final_norm_kernel(h_ref, g_ref, o_ref):
    o_ref[...] = _rms(h_ref[...], g_ref[...]).astype(o_ref.dtype)


def _final_norm(h, g, dtype):
    tp, d = h.shape
    tm = _pick_tile(tp, 512)
    row = pl.BlockSpec((tm, d), lambda i: (i, 0))
    return pl.pallas_call(
        _final_norm_kernel,
        grid=(tp // tm,),
        in_specs=[row, _const_spec((1, d))],
        out_specs=row,
        out_shape=jax.ShapeDtypeStruct((tp, d), dtype),
        compiler_params=_cparams(("parallel",)),
        name="final_norm",
    )(h, g.reshape(1, d))


def kernel(x, meta_tokens, norm_mix, norm_ffn, norm_final, ab_w_in, ab_w_out, ab_norm_a, ab_norm_b, s5_lam_re, s5_lam_im, s5_log_dt, s5_b_re, s5_b_im, s5_c_re, s5_c_im, s5_d, s5_w_glu, s5_b_glu, lru_conv_w, lru_conv_b, lru_w_a, lru_b_a, lru_w_x, lru_b_x, lru_lam, rw_mu, rw_w_r, rw_w_k, rw_w_v, rw_w_o, rw_w0, rw_w_l1, rw_w_l2, rw_a0, rw_a_l1, rw_a_l2, rw_v0, rw_v_l1, rw_v_l2, rw_g_l1, rw_g_l2, rw_k_k, rw_k_a, rw_r_k, rw_lnx_w, rw_lnx_b, moe_router_g, moe_router_g_b, moe_router_e, moe_router_e_b, moe_w_gate, moe_w_up, moe_w_down):
    bsz, seq, d = x.shape
    n_meta = meta_tokens.shape[0]
    depth = norm_mix.shape[0]
    ltot = n_meta + seq
    lp = -(-ltot // SEQ_ALIGN) * SEQ_ALIGN
    s5w = s5_w_glu.shape[-1]
    lruw = lru_lam.shape[-1]
    meta = jnp.broadcast_to(meta_tokens.astype(F32)[None], (bsz, n_meta, d))
    h = jnp.concatenate([meta, x.astype(F32), jnp.zeros((bsz, lp - ltot, d), F32)], axis=1).reshape(bsz * lp, d)
    v_first = None
    for layer in range(depth):
        j = layer // 2
        if layer % 2 == 0:
            u, gate, rec = _ab_in(h, norm_mix[layer], ab_w_in[j], s5w, lruw)
            tables = _s5_tables(s5_lam_re[j], s5_lam_im[j], s5_log_dt[j], s5_b_re[j], s5_b_im[j],
                                s5_c_re[j], s5_c_im[j], s5_d[j])
            y5 = _s5_scan(u, tables, bsz, lp)
            lru = _lru(rec, gate, lru_conv_w[j], lru_conv_b[j], lru_w_a[j], lru_b_a[j], lru_w_x[j], lru_b_x[j],
                       lru_lam[j], bsz, lp)
            h = _ab_out(y5, lru, h, s5_w_glu[j], s5_b_glu[j], ab_norm_a[j], ab_norm_b[j], ab_w_out[j])
        else:
            v_res = (rw_v0[j - 1], rw_v_l1[j - 1], rw_v_l2[j - 1]) if j > 0 else None
            r, k, v, w, a, g = _rw_pre(h, norm_mix[layer], rw_mu[j], rw_w_r[j], rw_w_k[j], rw_w_v[j], rw_w0[j],
                                       rw_w_l1[j], rw_w_l2[j], rw_a0[j], rw_a_l1[j], rw_a_l2[j], rw_g_l1[j],
                                       rw_g_l2[j], lp, v_first, v_res)
            if v_first is None:
                v_first = v
            y = _rw_scan(r, w, k, v, a, rw_k_k[j], rw_k_a[j], rw_r_k[j].reshape(-1), rw_lnx_w[j], rw_lnx_b[j],
                         bsz, lp)
            h = _rw_out(y, g, h, rw_w_o[j])
        part = (bsz * lp) // MOE_SPLIT
        h = jnp.concatenate([_moe(h, norm_ffn[layer], moe_router_g[layer], moe_router_g_b[layer], moe_router_e[layer],
                                  moe_router_e_b[layer], moe_w_gate, moe_w_up, moe_w_down, layer, s * part, part)
                             for s in range(MOE_SPLIT)], axis=0)
    out = _final_norm(h, norm_final, x.dtype)
    return out.reshape(bsz, lp, d)[:, n_meta:ltot]
```

```python
import functools
import math

import jax
import jax.numpy as jnp
from jax import lax
from jax.experimental import pallas as pl
from jax.experimental.pallas import tpu as pltpu

F32 = jnp.float32
BF16 = jnp.bfloat16
HI = lax.Precision.HIGHEST

RMS_EPS = 1e-6
LNX_EPS = 64e-5
N_META = 16
SEQ_ALIGN = 64
S5_CHUNK = 16
S5_LANES = 128
RW_CHUNK = 64
RW_HEAD = 64
RW_UNROLL = 11
LRU_C = 8.0
N_GROUPS = 4
EXPERTS_PER_GROUP = 4
N_EXPERTS = N_GROUPS * EXPERTS_PER_GROUP
MOE_TILE = 512
MOE_SPLIT = 1
VMEM_LIMIT = 56 * 1024 * 1024


def _cparams(sem):
    return pltpu.CompilerParams(dimension_semantics=sem, vmem_limit_bytes=VMEM_LIMIT)


def _pick_tile(n, target):
    best = 8
    for t in range(8, min(n, target) + 1, 8):
        if n % t == 0:
            best = t
    return best


def _const_spec(shape):
    nd = len(shape)
    return pl.BlockSpec(shape, lambda *_: (0,) * nd)


def _rms(x, g):
    return x * lax.rsqrt(jnp.mean(x * x, axis=-1, keepdims=True) + RMS_EPS) * g


def _gelu(x):
    return 0.5 * x * (1.0 + jnp.tanh(math.sqrt(2.0 / math.pi) * (x + 0.044715 * (x * x * x))))


def _sigmoid(x):
    return 1.0 / (1.0 + jnp.exp(-x))


def _softplus(x):
    return jnp.maximum(x, 0.0) + jnp.log(1.0 + jnp.exp(-jnp.abs(x)))


def _bdot(a, b):
    return jnp.dot(a.astype(BF16), b.astype(BF16), preferred_element_type=F32)


def _hdot(a, b):
    return jnp.dot(a, b, preferred_element_type=F32, precision=HI)


def _dot3(a, b):
    m = a.shape[0]
    a_hi = a.astype(BF16)
    b_hi = b.astype(BF16)
    a_lo = (a - a_hi.astype(F32)).astype(BF16)
    b_lo = (b - b_hi.astype(F32)).astype(BF16)
    dot = functools.partial(jnp.dot, preferred_element_type=F32)
    top = dot(jnp.concatenate([a_hi, a_lo], axis=0), b_hi)
    return top[:m] + (top[m:] + dot(a_hi, b_lo))


def _split_dot_many(xs, m, rhs=False, pieces=3):
    accs = [None] * len(xs)
    xs = list(xs)
    for _ in range(pieces):
        his = [x.astype(BF16) for x in xs]
        parts = [jnp.dot(m, hi, preferred_element_type=F32) if rhs else jnp.dot(hi, m, preferred_element_type=F32)
                 for hi in his]
        accs = [p if a is None else a + p for a, p in zip(accs, parts)]
        xs = [x - hi.astype(F32) for x, hi in zip(xs, his)]
    return accs


def _ab_in_kernel(h_ref, g_ref, w_ref, u_ref, gate_ref, rec_ref, *, s5w, lruw):
    xn = _rms(h_ref[...], g_ref[...])
    z = _bdot(xn, w_ref[...])
    u_ref[...] = z[:, :s5w].astype(u_ref.dtype)
    gate_ref[...] = z[:, s5w:s5w + lruw]
    rec_ref[...] = z[:, s5w + lruw:]


def _ab_in(h, g, w_in, s5w, lruw):
    tp, d = h.shape
    tm = _pick_tile(tp, 512)
    row = lambda n: pl.BlockSpec((tm, n), lambda i: (i, 0))
    return pl.pallas_call(
        functools.partial(_ab_in_kernel, s5w=s5w, lruw=lruw),
        grid=(tp // tm,),
        in_specs=[row(d), _const_spec((1, d)), _const_spec(w_in.shape)],
        out_specs=[row(s5w), row(lruw), row(lruw)],
        out_shape=[jax.ShapeDtypeStruct((tp, s5w), BF16),
                   jax.ShapeDtypeStruct((tp, lruw), F32),
                   jax.ShapeDtypeStruct((tp, lruw), F32)],
        compiler_params=_cparams(("parallel",)),
        name="ab_in",
    )(h, g.reshape(1, d), w_in.astype(BF16))


def _s5_tables(lam_re, lam_im, log_dt, b_re, b_im, c_re, c_im, d_skip):
    g, p = lam_re.shape
    hh = b_re.shape[-1]
    c = S5_CHUNK
    lr, li = lam_re.astype(F32), lam_im.astype(F32)
    dt = jnp.exp(log_dt.astype(F32))[:, None]
    mag = jnp.exp(lr * dt)
    abar_r = mag * jnp.cos(li * dt)
    abar_i = mag * jnp.sin(li * dt)
    den = lr * lr + li * li
    zr = ((abar_r - 1.0) * lr + abar_i * li) / den
    zi = (abar_i * lr - (abar_r - 1.0) * li) / den
    bbar_r = zr[..., None] * b_re - zi[..., None] * b_im
    bbar_i = zr[..., None] * b_im + zi[..., None] * b_re
    def powers(steps):
        st = steps.astype(F32)[:, None, None]
        pmag = jnp.exp(st * (lr * dt))
        return pmag * jnp.cos(st * (li * dt)), pmag * jnp.sin(st * (li * dt))

    def c_times(pw_r, pw_i):
        return (c_re[None] * pw_r[:, :, None, :] - c_im[None] * pw_i[:, :, None, :],
                c_re[None] * pw_i[:, :, None, :] + c_im[None] * pw_r[:, :, None, :])

    down = (c - 1) - jnp.arange(c)
    rev_r, rev_i = powers(down)
    m1_r = rev_r[:, :, :, None] * bbar_r[None] - rev_i[:, :, :, None] * bbar_i[None]
    m1_i = rev_r[:, :, :, None] * bbar_i[None] + rev_i[:, :, :, None] * bbar_r[None]
    m1_r = jnp.transpose(m1_r, (1, 0, 3, 2))
    m1_i = jnp.transpose(m1_i, (1, 0, 3, 2))
    car, cai = c_times(rev_r, rev_i)
    kern = (jnp.einsum('kghp,gpj->kghj', car, bbar_r, precision=HI)
            - jnp.einsum('kghp,gpj->kghj', cai, bbar_i, precision=HI))
    is_tau0 = (down == 0).astype(F32)[:, None, None, None]
    kern = kern + is_tau0 * (d_skip[:, :, None] * jnp.eye(hh, dtype=F32)[None])[None]
    ca_r, ca_i = c_times(*powers(jnp.arange(1, c + 1)))
    m2_r = jnp.transpose(ca_r, (1, 3, 0, 2))
    m2_i = -jnp.transpose(ca_i, (1, 3, 0, 2))
    adv_r, adv_i = powers(jnp.full((1,), c))
    gb = S5_LANES // hh
    nj = g // gb
    eye = jnp.eye(gb, dtype=F32)
    blk = lambda x: x.reshape((nj, gb) + x.shape[1:])
    m1 = jnp.stack([jnp.transpose(blk(m), (0, 2, 1, 3, 4))[:, :, :, :, None, :] * eye[None, None, :, None, :, None]
                    for m in (m1_r, m1_i)], axis=4)
    m1 = m1.reshape(nj, c * gb * hh, 2 * gb * p).astype(BF16)
    krev = jnp.transpose(kern, (1, 0, 3, 2))
    krev = jnp.transpose(blk(krev), (0, 2, 1, 3, 4))[:, :, :, :, None, :] * eye[None, None, :, None, :, None]
    krev = krev.reshape(nj, c * gb * hh, gb * hh).astype(BF16)
    m2 = jnp.stack([blk(m)[:, :, :, :, None, :] * eye[None, :, None, None, :, None] for m in (m2_r, m2_i)], axis=1)
    m2 = m2.reshape(nj, 2 * gb * p, c * gb * hh).astype(BF16)
    return m1, krev, m2, adv_r.reshape(nj, 1, gb * p), adv_i.reshape(nj, 1, gb * p)


def _s5_kernel(u_ref, m1_ref, kr_ref, m2_ref, ar_ref, ai_ref, y_ref, xe_ref, xin_ref, st_ref, *, n_chunks, nb):
    nl = S5_LANES
    csz = S5_CHUNK
    sw = ar_ref.shape[-1]

    @pl.when(pl.program_id(1) == 0)
    def _():
        st_ref[...] = jnp.zeros(st_ref.shape, F32)

    u = u_ref[0]
    xe_ref[...] = jnp.dot(u, m1_ref[0], preferred_element_type=F32)
    ar = jnp.broadcast_to(ar_ref[0], (nb, sw))
    ai = jnp.broadcast_to(ai_ref[0], (nb, sw))

    def body(c, carry):
        sr, si = carry
        off = pl.multiple_of(c * nb, nb)
        xin_ref[pl.ds(off, nb), :] = jnp.concatenate([sr, si], axis=1)
        e = xe_ref[pl.ds(off, nb), :]
        return (ar * sr - ai * si + e[:, :sw], ar * si + ai * sr + e[:, sw:])

    sr, si = lax.fori_loop(0, n_chunks, body, (st_ref[:, :sw], st_ref[:, sw:]))
    st_ref[:, :sw] = sr
    st_ref[:, sw:] = si
    y_ref[0] = _bdot(xin_ref[...], m2_ref[0])
    for t in range(csz):
        y_ref[0, :, t * nl:(t + 1) * nl] += jnp.dot(u[:, :(t + 1) * nl], kr_ref[0, (csz - 1 - t) * nl:, :],
                                                    preferred_element_type=F32)


def _s5_scan(u, tables, bsz, lp):
    m1, krev, m2, adv_r, adv_i = tables
    nj, kin, sw2 = m1.shape
    c = S5_CHUNK
    nl = S5_LANES
    nc = lp // c
    cpt = max(d for d in range(1, nc + 1) if nc % d == 0 and d * bsz <= 512)
    rows = cpt * bsz
    ug = jnp.transpose(u.reshape(bsz, nc, c, nj, nl), (3, 1, 0, 2, 4)).reshape(nj, nc * bsz, kin)
    y = pl.pallas_call(
        functools.partial(_s5_kernel, n_chunks=cpt, nb=bsz),
        grid=(nj, nc // cpt),
        in_specs=[pl.BlockSpec((1, rows, kin), lambda j, r: (j, r, 0)),
                  pl.BlockSpec((1, kin, sw2), lambda j, r: (j, 0, 0)),
                  pl.BlockSpec((1, kin, nl), lambda j, r: (j, 0, 0)),
                  pl.BlockSpec((1, sw2, kin), lambda j, r: (j, 0, 0)),
                  pl.BlockSpec((1, 1, sw2 // 2), lambda j, r: (j, 0, 0)),
                  pl.BlockSpec((1, 1, sw2 // 2), lambda j, r: (j, 0, 0))],
        out_specs=pl.BlockSpec((1, rows, kin), lambda j, r: (j, r, 0)),
        out_shape=jax.ShapeDtypeStruct((nj, nc * bsz, kin), F32),
        scratch_shapes=[pltpu.VMEM((rows, sw2), F32), pltpu.VMEM((rows, sw2), F32), pltpu.VMEM((bsz, sw2), F32)],
        compiler_params=_cparams(("parallel", "arbitrary")),
        name="s5_scan",
    )(ug, m1, krev, m2, adv_r, adv_i)
    y = jnp.transpose(y.reshape(nj, nc, bsz, c, nl), (2, 1, 3, 0, 4))
    return y.reshape(bsz * lp, nj * nl)


def _lru_kernel(rec_ref, gate_ref, cw_ref, cb_ref, wa_ref, ba_ref, wx_ref, bx_ref, lam_ref,
                o_ref, ext_ref, a_ref, b_ref, h_ref, *, tl):
    t = pl.program_id(1)

    @pl.when(t == 0)
    def _():
        ext_ref[0:8, :] = jnp.zeros((8, ext_ref.shape[1]), F32)
        h_ref[...] = jnp.zeros(h_ref.shape, F32)

    x = rec_ref[...]
    ext_ref[8:, :] = x
    xc = cb_ref[...] + cw_ref[3:4, :] * x
    for k in range(3):
        xc = xc + cw_ref[k:k + 1, :] * ext_ref[5 + k:5 + k + tl, :]
    ext_ref[0:8, :] = x[tl - 8:, :]
    r = _sigmoid(_bdot(xc, wa_ref[...]) + ba_ref[...])
    i = _sigmoid(_bdot(xc, wx_ref[...]) + bx_ref[...])
    log_a = (-LRU_C) * r * _softplus(-lam_ref[...])
    a = jnp.exp(log_a)
    a_ref[...] = a
    b_ref[...] = jnp.sqrt(1.0 - a * a) * (i * xc)

    def body(j, h):
        off = pl.multiple_of(j * 8, 8)
        ab = a_ref[pl.ds(off, 8), :]
        bb = b_ref[pl.ds(off, 8), :]
        rows = []
        for q in range(8):
            h = ab[q:q + 1, :] * h + bb[q:q + 1, :]
            rows.append(h)
        o_ref[pl.ds(off, 8), :] = jnp.concatenate(rows, axis=0) * _gelu(gate_ref[pl.ds(off, 8), :])
        return h

    h_ref[...] = lax.fori_loop(0, tl // 8, body, h_ref[...])


def _lru(rec, gate, conv_w, conv_b, w_a, b_a, w_x, b_x, lam, bsz, lp):
    tp, w = rec.shape
    tl = _pick_tile(lp, 1056)
    nt = lp // tl
    heads, hd, _ = w_a.shape

    def dense(wb):
        eye = jnp.eye(heads, dtype=F32)
        return jnp.einsum('hij,hg->higj', wb, eye).reshape(w, w).astype(BF16)

    row = pl.BlockSpec((tl, w), lambda b, t: (b * nt + t, 0))
    vec = _const_spec((1, w))
    return pl.pallas_call(
        functools.partial(_lru_kernel, tl=tl),
        grid=(bsz, nt),
        in_specs=[row, row, _const_spec((4, w)), vec, _const_spec((w, w)), vec, _const_spec((w, w)), vec, vec],
        out_specs=row,
        out_shape=jax.ShapeDtypeStruct((tp, w), F32),
        scratch_shapes=[pltpu.VMEM((tl + 8, w), F32), pltpu.VMEM((tl, w), F32),
                        pltpu.VMEM((tl, w), F32), pltpu.VMEM((1, w), F32)],
        compiler_params=_cparams(("parallel", "arbitrary")),
        name="rglru",
    )(rec, gate, conv_w, conv_b.reshape(1, w), dense(w_a), b_a.reshape(1, w), dense(w_x), b_x.reshape(1, w),
      lam.reshape(1, w))


def _ab_out_kernel(y5_ref, lru_ref, h_ref, wglu_ref, bglu_ref, na_ref, nb_ref, wo_ref, o_ref, *, s5w):
    y = _gelu(y5_ref[...])
    ya = y * _sigmoid(_bdot(y, wglu_ref[...]) + bglu_ref[...])
    ya = _rms(ya, na_ref[...])
    yb = _rms(lru_ref[...], nb_ref[...])
    o_ref[...] = h_ref[...] + _bdot(ya, wo_ref[:s5w, :]) + _bdot(yb, wo_ref[s5w:, :])


def _ab_out(y5, lru, h, w_glu, b_glu, norm_a, norm_b, w_out):
    tp, d = h.shape
    s5w, lruw = y5.shape[1], lru.shape[1]
    tm = _pick_tile(tp, 512)
    row = lambda n: pl.BlockSpec((tm, n), lambda i: (i, 0))
    return pl.pallas_call(
        functools.partial(_ab_out_kernel, s5w=s5w),
        grid=(tp // tm,),
        in_specs=[row(s5w), row(lruw), row(d), _const_spec((s5w, s5w)), _const_spec((1, s5w)),
                  _const_spec((1, s5w)), _const_spec((1, lruw)), _const_spec(w_out.shape)],
        out_specs=row(d),
        out_shape=jax.ShapeDtypeStruct((tp, d), F32),
        compiler_params=_cparams(("parallel",)),
        name="ab_out",
    )(y5, lru, h, w_glu.astype(BF16), b_glu.reshape(1, s5w), norm_a.reshape(1, s5w),
      norm_b.reshape(1, lruw), w_out.astype(BF16))


def _route_kernel(h_ref, g_ref, wr_ref, br_ref, xn_ref, rt_ref):
    xn = _rms(h_ref[...], g_ref[...])
    xn_ref[...] = xn.astype(xn_ref.dtype)
    lg = _hdot(xn, wr_ref[...]) + br_ref[...]
    lane = lax.broadcasted_iota(jnp.int32, lg.shape, 1).astype(F32)
    big = float(lg.shape[1])
    neg = -jnp.inf
    gl = jnp.where(lane < N_GROUPS, lg, neg)
    mg = jnp.max(gl, axis=-1, keepdims=True)
    gidx = jnp.min(jnp.where(gl == mg, lane, big), axis=-1, keepdims=True)
    pg_sel = 1.0 / jnp.sum(jnp.exp(gl - mg), axis=-1, keepdims=True)
    lo = N_GROUPS + EXPERTS_PER_GROUP * gidx
    le = jnp.where(lane >= lo, jnp.where(lane < lo + EXPERTS_PER_GROUP, lg, neg), neg)
    v1 = jnp.max(le, axis=-1, keepdims=True)
    i1 = jnp.min(jnp.where(le == v1, lane, big), axis=-1, keepdims=True)
    le2 = jnp.where(lane == i1, neg, le)
    v2 = jnp.max(le2, axis=-1, keepdims=True)
    i2 = jnp.min(jnp.where(le2 == v2, lane, big), axis=-1, keepdims=True)
    e2 = jnp.exp(v2 - v1)
    w1 = pg_sel / (1.0 + e2)
    w2 = w1 * e2
    rt_ref[...] = jnp.where(lane == 0.0, w1, jnp.where(lane == 1.0, w2, jnp.where(
        lane == 2.0, i1 - N_GROUPS, jnp.where(lane == 3.0, i2 - N_GROUPS, 0.0))))


def _route(h, g, wr_g, br_g, wr_e, br_e, row0, tp):
    d = h.shape[1]
    tm = _pick_tile(math.gcd(tp, row0) if row0 else tp, 512)
    blk0 = row0 // tm
    nl = 128
    wr = jnp.zeros((d, nl), F32).at[:, :N_GROUPS].set(wr_g).at[:, N_GROUPS:N_GROUPS + N_EXPERTS].set(wr_e)
    br = jnp.zeros((1, nl), F32).at[0, :N_GROUPS].set(br_g).at[0, N_GROUPS:N_GROUPS + N_EXPERTS].set(br_e)
    row = lambda n: pl.BlockSpec((tm, n), lambda i: (i, 0))
    return pl.pallas_call(
        _route_kernel,
        grid=(tp // tm,),
        in_specs=[pl.BlockSpec((tm, d), lambda i: (i + blk0, 0)), _const_spec((1, d)), _const_spec((d, nl)),
                  _const_spec((1, nl))],
        out_specs=[row(d), row(nl)],
        out_shape=[jax.ShapeDtypeStruct((tp, d), F32), jax.ShapeDtypeStruct((tp, nl), F32)],
        compiler_params=_cparams(("parallel",)),
        name="moe_route",
    )(h, g.reshape(1, d), wr, br)


def _gmm_kernel(te_ref, tv_ref, xa_ref, xb_ref, wg_ref, wu_ref, wd_ref, o_ref, wg_bf, wu_bf, wd_bf, *, n_half):
    i = pl.program_id(0)

    @pl.when(jnp.logical_or(i == 0, te_ref[i] != te_ref[jnp.maximum(i - 1, 0)]))
    def _():
        wg_bf[...] = wg_ref[0, 0].astype(BF16)
        wu_bf[...] = wu_ref[0, 0].astype(BF16)
        wd_bf[...] = wd_ref[0, 0].astype(BF16)

    @pl.when(tv_ref[i] != 0)
    def _():
        x = jnp.where(i < n_half, xa_ref[...], xb_ref[...]).astype(BF16)
        hg = jnp.dot(x, wg_bf[...], preferred_element_type=F32)
        hu = jnp.dot(x, wu_bf[...], preferred_element_type=F32)
        hid = hg * _sigmoid(hg) * hu
        o_ref[...] = _bdot(hid, wd_bf[...])

    @pl.when(tv_ref[i] == 0)
    def _():
        o_ref[...] = jnp.zeros(o_ref.shape, o_ref.dtype)


def _moe(h, g, wr_g, br_g, wr_e, br_e, w_gate, w_up, w_down, layer, row0, tp):
    d = h.shape[1]
    f = w_gate.shape[-1]
    xn, rt = _route(h, g, wr_g, br_g, wr_e, br_e, row0, tp)
    gate = rt[:, 0:2]
    eid = rt[:, 2:4].astype(jnp.int32)
    tmm = MOE_TILE
    na = 2 * tp
    e_flat = eid.reshape(na)
    onehot = (e_flat[:, None] == jnp.arange(N_EXPERTS, dtype=jnp.int32)[None, :]).astype(jnp.int32)
    csum = jnp.cumsum(onehot, axis=0)
    counts = csum[-1]
    padded = ((counts + tmm - 1) // tmm) * tmm
    ends = jnp.cumsum(padded)
    starts = ends - padded
    dest = jnp.sum(onehot * (csum + (starts - 1)[None, :]), axis=1)
    n_half = (-(-na // tmm) + N_EXPERTS + 1) // 2
    n_tiles = 2 * n_half
    nrows = n_tiles * tmm
    tile_start = jnp.arange(n_tiles, dtype=jnp.int32) * tmm
    tile_e = jnp.sum((ends[None, :] <= tile_start[:, None]).astype(jnp.int32), axis=1)
    tile_v = (tile_e < N_EXPERTS).astype(jnp.int32)
    tile_e = jnp.minimum(tile_e, N_EXPERTS - 1)
    src = (jnp.arange(nrows, dtype=jnp.int32) % tp).at[dest].set(
        jnp.arange(na, dtype=jnp.int32) // 2, mode="promise_in_bounds", unique_indices=True)
    take = lambda a, i: a.at[i].get(mode="promise_in_bounds")
    xs_a = take(xn, src[:n_half * tmm])
    xs_b = take(xn, src[n_half * tmm:])
    ys = pl.pallas_call(
        functools.partial(_gmm_kernel, n_half=n_half),
        grid_spec=pltpu.PrefetchScalarGridSpec(
            num_scalar_prefetch=2,
            grid=(n_tiles,),
            in_specs=[pl.BlockSpec((tmm, d), lambda i, te, tv: (jnp.minimum(i, n_half - 1), 0)),
                      pl.BlockSpec((tmm, d), lambda i, te, tv: (jnp.maximum(i - n_half, 0), 0)),
                      pl.BlockSpec((1, 1, d, f), lambda i, te, tv: (layer, te[i], 0, 0)),
                      pl.BlockSpec((1, 1, d, f), lambda i, te, tv: (layer, te[i], 0, 0)),
                      pl.BlockSpec((1, 1, f, d), lambda i, te, tv: (layer, te[i], 0, 0))],
            out_specs=pl.BlockSpec((tmm, d), lambda i, te, tv: (i, 0)),
            scratch_shapes=[pltpu.VMEM((d, f), BF16), pltpu.VMEM((d, f), BF16), pltpu.VMEM((f, d), BF16)],
        ),
        out_shape=jax.ShapeDtypeStruct((nrows, d), F32),
        compiler_params=_cparams(("arbitrary",)),
        name="moe_gmm",
    )(tile_e, tile_v, xs_a, xs_b, w_gate, w_up, w_down)
    d2 = dest.reshape(tp, 2)
    return (lax.slice_in_dim(h, row0, row0 + tp, axis=0)
            + gate[:, 0:1] * take(ys, d2[:, 0]) + gate[:, 1:2] * take(ys, d2[:, 1]))


def _rw_pre_kernel(*refs, tm, lp, has_vres):
    if has_vres:
        (h_ref, hp_ref, g_ref, mu_ref, wr_ref, wk_ref, wv_ref, w0_ref, wl1_ref, wl2_ref,
         a0_ref, al1_ref, al2_ref, gl1_ref, gl2_ref, vf_ref, v0_ref, vl1_ref, vl2_ref,
         r_ref, k_ref, v_ref, w_ref, a_ref, gg_ref) = refs
    else:
        (h_ref, hp_ref, g_ref, mu_ref, wr_ref, wk_ref, wv_ref, w0_ref, wl1_ref, wl2_ref,
         a0_ref, al1_ref, al2_ref, gl1_ref, gl2_ref,
         r_ref, k_ref, v_ref, w_ref, a_ref, gg_ref) = refs
    i = pl.program_id(0)
    g = g_ref[...]
    x = _rms(h_ref[...], g)
    xp8 = _rms(hp_ref[...], g)
    row = lax.broadcasted_iota(jnp.int32, x.shape, 0)
    prev = jnp.where(row == 0, jnp.broadcast_to(xp8[7:8, :], x.shape), pltpu.roll(x, 1, axis=0))
    first = lax.rem(lp - lax.rem(i * tm, lp), lp)
    prev = jnp.where(row == first, 0.0, prev)
    xx = prev - x
    xr, xw, xk, xv, xa, xg = [x + xx * mu_ref[j:j + 1, :] for j in range(6)]
    r_ref[...] = _bdot(xr, wr_ref[...])
    k_ref[...] = _bdot(xk, wk_ref[...])
    v = _bdot(xv, wv_ref[...])
    if has_vres:
        mix = _sigmoid(v0_ref[...] + _bdot(_bdot(xv, vl1_ref[...]), vl2_ref[...]))
        v = v + (vf_ref[...] - v) * mix
    v_ref[...] = v
    w_ref[...] = -_softplus(-(w0_ref[...] + _bdot(jnp.tanh(_bdot(xw, wl1_ref[...])), wl2_ref[...]))) - 0.5
    a_ref[...] = _sigmoid(a0_ref[...] + _bdot(_bdot(xa, al1_ref[...]), al2_ref[...]))
    gg_ref[...] = _bdot(_sigmoid(_bdot(xg, gl1_ref[...])), gl2_ref[...])


def _rw_pre(h, g, mu, w_r, w_k, w_v, w0, w_l1, w_l2, a0, a_l1, a_l2, g_l1, g_l2, lp, v_first, v_res):
    tp, d = h.shape
    tm = _pick_tile(tp, min(256, lp))
    has_vres = v_res is not None
    row = pl.BlockSpec((tm, d), lambda i: (i, 0))
    prev8 = pl.BlockSpec((8, d), lambda i: (jnp.maximum(i * (tm // 8) - 1, 0), 0))
    vec = _const_spec((1, d))
    mu8 = jnp.zeros((8, d), F32).at[:6].set(mu)
    bf = lambda w: w.astype(BF16)
    ins = [h, h, g.reshape(1, d), mu8, bf(w_r), bf(w_k), bf(w_v), w0.reshape(1, d), bf(w_l1), bf(w_l2),
           a0.reshape(1, d), bf(a_l1), bf(a_l2), bf(g_l1), bf(g_l2)]
    specs = [row, prev8, vec, _const_spec((8, d))] + [_const_spec((d, d))] * 3 + [
        vec, _const_spec(w_l1.shape), _const_spec(w_l2.shape),
        vec, _const_spec(a_l1.shape), _const_spec(a_l2.shape), _const_spec(g_l1.shape), _const_spec(g_l2.shape)]
    if has_vres:
        v0, v_l1, v_l2 = v_res
        ins += [v_first, v0.reshape(1, d), bf(v_l1), bf(v_l2)]
        specs += [row, vec, _const_spec(v_l1.shape), _const_spec(v_l2.shape)]
    return pl.pallas_call(
        functools.partial(_rw_pre_kernel, tm=tm, lp=lp, has_vres=has_vres),
        grid=(tp // tm,),
        in_specs=specs,
        out_specs=[row] * 6,
        out_shape=[jax.ShapeDtypeStruct((tp, d), F32)] * 6,
        compiler_params=_cparams(("parallel",)),
        name="rwkv_pre",
    )(*ins)


def _rw_scan_kernel(r_ref, w_ref, k_ref, v_ref, a_ref, kk_ref, ka_ref, rk_ref, lw_ref, lb_ref,
                    o_ref, g_scr, h_scr, q_scr, y0_scr, *, n_chunks, unroll):
    c = RW_CHUNK
    nl = 2 * RW_HEAD
    c2 = 2 * c
    ri = lax.broadcasted_iota(jnp.int32, (c2, nl), 0)
    ci = lax.broadcasted_iota(jnp.int32, (c2, nl), 1)
    own = ((ri >= c) == (ci >= RW_HEAD)).astype(F32)
    t_in = jnp.bitwise_and(ri, c - 1)
    s_in = jnp.bitwise_and(ci, c - 1)
    strict = jnp.where(s_in < t_in, own, 0.0)
    incl = jnp.where(s_in <= t_in, own, 0.0)
    eye_l = (ri == ci).astype(F32)
    same_head = own.astype(BF16)
    ti = lax.broadcasted_iota(jnp.int32, (c, c), 0)
    si = lax.broadcasted_iota(jnp.int32, (c, c), 1)
    tril_c = (si <= ti).astype(BF16)
    kk_w, ka_w, rk_w = kk_ref[...], ka_ref[...], rk_ref[...]

    def stack(x):
        return jnp.concatenate([x, x], axis=0) * own

    def offset(ch):
        return ch * c if isinstance(ch, int) else pl.multiple_of(ch * c, c)

    def groups(fn):
        def body(i, _):
            fn([i * unroll + q for q in range(unroll)])
            return 0
        lax.fori_loop(0, n_chunks // unroll, body, 0)
        if n_chunks % unroll:
            fn(list(range(n_chunks - n_chunks % unroll, n_chunks)))

    def prep(chs):
        offs = [offset(ch) for ch in chs]
        r = [r_ref[pl.ds(o, c), :] for o in offs]
        k = [k_ref[pl.ds(o, c), :] for o in offs]
        v = [v_ref[pl.ds(o, c), :] for o in offs]
        a = [a_ref[pl.ds(o, c), :] for o in offs]
        logw = [-jnp.exp(w_ref[pl.ds(o, c), :]) for o in offs]
        kk = [x * kk_w for x in k]
        ss = _split_dot_many([x * x for x in kk], same_head)
        kk = [x / jnp.maximum(jnp.sqrt(q), 1e-12) for x, q in zip(kk, ss)]
        kmod = [x * (1.0 + (y - 1.0) * ka_w) for x, y in zip(k, a)]
        cum = _split_dot_many(logw, tril_c, rhs=True)
        p_incl = [jnp.exp(x) for x in cum]
        p_inv = [jnp.exp(-x) for x in cum]
        p_end = [jnp.exp(x[c - 1:c, :] - x) for x in cum]
        kka = [x * y for x, y in zip(kk, a)]
        a_s = [stack(-x * jnp.exp(y - z)) for x, y, z in zip(kk, cum, logw)]
        r_s = [stack(x * y) for x, y in zip(r, p_incl)]
        v_s = [stack(x).astype(BF16) for x in v]
        lhs = [jnp.concatenate([x, y], axis=0).astype(BF16) for x, y in zip(a_s, r_s)]
        rhs = [jnp.concatenate([stack(x * z), stack(y * z)], axis=0).astype(BF16) for x, y, z in zip(kka, kmod, p_inv)]
        big = [lax.dot_general(x, y, (((1,), (1,)), ((), ())), preferred_element_type=F32) for x, y in zip(lhs, rhs)]
        lpow = [(x[:c2, :c2] * strict).astype(BF16) for x in big]
        a_ak = [x[:c2, c2:] * strict for x in big]
        a_rb = [x[c2:, :c2] * incl for x in big]
        a_rk = [x[c2:, c2:] * incl for x in big]
        x = [jnp.concatenate([p, _bdot(q, w)], axis=1) for p, q, w in zip(a_s, a_ak, v_s)]
        x = [p + _bdot(q, p) for p, q in zip(x, lpow)]
        for _ in range(5):
            lpow = [jnp.dot(q, q, preferred_element_type=F32).astype(BF16) for q in lpow]
            x = [p + _bdot(q, p) for p, q in zip(x, lpow)]
        xb = [p.astype(BF16) for p in x]
        qy = [jnp.concatenate([p, _bdot(q, w)], axis=1) + _bdot(u, z)
              for p, q, w, u, z in zip(r_s, a_rk, v_s, a_rb, xb)]
        bh_t = [stack(p * q).T for p, q in zip(kka, p_end)]
        kh_t = [stack(p * q).T for p, q in zip(kmod, p_end)]
        gh = [_bdot(p, z) + jnp.concatenate([eye_l * q[c - 1:c, :], _bdot(u, w)], axis=1)
              for p, z, q, u, w in zip(bh_t, xb, p_incl, kh_t, v_s)]
        for ch, p, q in zip(chs, qy, gh):
            q_scr[ch] = p[:c, :nl] + p[c:, :nl]
            y0_scr[ch] = p[:c, nl:] + p[c:, nl:]
            g_scr[ch] = q[:, :nl]
            h_scr[ch] = q[:, nl:]

    groups(prep)

    def seq(ch, st):
        both = _dot3(jnp.concatenate([q_scr[ch], g_scr[ch]], axis=0), st)
        o_ref[pl.ds(offset(ch), c), :] = both[:c] + y0_scr[ch]
        return both[c:] + h_scr[ch]

    lax.fori_loop(0, n_chunks, seq, jnp.zeros((nl, nl), F32))
    lw, lb = lw_ref[...], lb_ref[...]

    def post(chs):
        offs = [offset(ch) for ch in chs]
        y = [o_ref[pl.ds(o, c), :] for o in offs]
        kmod = [k_ref[pl.ds(o, c), :] * (1.0 + (a_ref[pl.ds(o, c), :] - 1.0) * ka_w) for o in offs]
        rk = [r_ref[pl.ds(o, c), :] * p * rk_w for o, p in zip(offs, kmod)]
        sums = _split_dot_many([jnp.concatenate([p, p * p, q], axis=0) for p, q in zip(y, rk)], same_head)
        for o, p, s in zip(offs, y, sums):
            mean = s[:c] * (1.0 / RW_HEAD)
            var = s[c:2 * c] * (1.0 / RW_HEAD) - mean * mean
            o_ref[pl.ds(o, c), :] = ((p - mean) * lax.rsqrt(var + LNX_EPS) * lw + lb
                                     + s[2 * c:] * v_ref[pl.ds(o, c), :])

    groups(post)


def _rw_scan(r, w, k, v, a, k_k, k_a, r_k, lnx_w, lnx_b, bsz, lp):
    tp, d = r.shape
    nl = 2 * RW_HEAD
    nhp = d // nl
    nc = lp // RW_CHUNK
    seq = pl.BlockSpec((lp, nl), lambda b, hp: (b, hp))
    vec = pl.BlockSpec((1, nl), lambda b, hp: (0, hp))
    c = RW_CHUNK
    return pl.pallas_call(
        functools.partial(_rw_scan_kernel, n_chunks=nc, unroll=RW_UNROLL),
        grid=(bsz, nhp),
        in_specs=[seq] * 5 + [vec] * 5,
        out_specs=seq,
        out_shape=jax.ShapeDtypeStruct((tp, d), F32),
        scratch_shapes=[pltpu.VMEM((nc, nl, nl), F32), pltpu.VMEM((nc, nl, nl), F32),
                        pltpu.VMEM((nc, c, nl), F32), pltpu.VMEM((nc, c, nl), F32)],
        compiler_params=_cparams(("parallel", "parallel")),
        name="rwkv_scan",
    )(r, w, k, v, a, k_k.reshape(1, d), k_a.reshape(1, d), r_k.reshape(1, d),
      lnx_w.reshape(1, d), lnx_b.reshape(1, d))


def _rw_out_kernel(y_ref, g_ref, h_ref, wo_ref, o_ref):
    o_ref[...] = h_ref[...] + _bdot(y_ref[...] * g_ref[...], wo_ref[...])


def _rw_out(y, g, h, w_o):
    tp, d = h.shape
    tm = _pick_tile(tp, 512)
    row = pl.BlockSpec((tm, d), lambda i: (i, 0))
    return pl.pallas_call(
        _rw_out_kernel,
        grid=(tp // tm,),
        in_specs=[row, row, row, _const_spec((d, d))],
        out_specs=row,
        out_shape=jax.ShapeDtypeStruct((tp, d), F32),
        compiler_params=_cparams(("parallel",)),
        name="rwkv_out",
    )(y, g, h, w_o.astype(BF16))


def _final_norm_kernel(h_ref, g_ref, o_ref):
    o_ref[...] = _rms(h_ref[...], g_ref[...]).astype(o_ref.dtype)


def _final_norm(h, g, dtype):
    tp, d = h.shape
    tm = _pick_tile(tp, 512)
    row = pl.BlockSpec((tm, d), lambda i: (i, 0))
    return pl.pallas_call(
        _final_norm_kernel,
        grid=(tp // tm,),
        in_specs=[row, _const_spec((1, d))],
        out_specs=row,
        out_shape=jax.ShapeDtypeStruct((tp, d), dtype),
        compiler_params=_cparams(("parallel",)),
        name="final_norm",
    )(h, g.reshape(1, d))


def kernel(x, meta_tokens, norm_mix, norm_ffn, norm_final, ab_w_in, ab_w_out, ab_norm_a, ab_norm_b, s5_lam_re, s5_lam_im, s5_log_dt, s5_b_re, s5_b_im, s5_c_re, s5_c_im, s5_d, s5_w_glu, s5_b_glu, lru_conv_w, lru_conv_b, lru_w_a, lru_b_a, lru_w_x, lru_b_x, lru_lam, rw_mu, rw_w_r, rw_w_k, rw_w_v, rw_w_o, rw_w0, rw_w_l1, rw_w_l2, rw_a0, rw_a_l1, rw_a_l2, rw_v0, rw_v_l1, rw_v_l2, rw_g_l1, rw_g_l2, rw_k_k, rw_k_a, rw_r_k, rw_lnx_w, rw_lnx_b, moe_router_g, moe_router_g_b, moe_router_e, moe_router_e_b, moe_w_gate, moe_w_up, moe_w_down):
    bsz, seq, d = x.shape
    n_meta = meta_tokens.shape[0]
    depth = norm_mix.shape[0]
    ltot = n_meta + seq
    lp = -(-ltot // SEQ_ALIGN) * SEQ_ALIGN
    s5w = s5_w_glu.shape[-1]
    lruw = lru_lam.shape[-1]
    meta = jnp.broadcast_to(meta_tokens.astype(F32)[None], (bsz, n_meta, d))
    h = jnp.concatenate([meta, x.astype(F32), jnp.zeros((bsz, lp - ltot, d), F32)], axis=1).reshape(bsz * lp, d)
    v_first = None
    for layer in range(depth):
        j = layer // 2
        if layer % 2 == 0:
            u, gate, rec = _ab_in(h, norm_mix[layer], ab_w_in[j], s5w, lruw)
            tables = _s5_tables(s5_lam_re[j], s5_lam_im[j], s5_log_dt[j], s5_b_re[j], s5_b_im[j],
                                s5_c_re[j], s5_c_im[j], s5_d[j])
            y5 = _s5_scan(u, tables, bsz, lp)
            lru = _lru(rec, gate, lru_conv_w[j], lru_conv_b[j], lru_w_a[j], lru_b_a[j], lru_w_x[j], lru_b_x[j],
                       lru_lam[j], bsz, lp)
            h = _ab_out(y5, lru, h, s5_w_glu[j], s5_b_glu[j], ab_norm_a[j], ab_norm_b[j], ab_w_out[j])
        else:
            v_res = (rw_v0[j - 1], rw_v_l1[j - 1], rw_v_l2[j - 1]) if j > 0 else None
            r, k, v, w, a, g = _rw_pre(h, norm_mix[layer], rw_mu[j], rw_w_r[j], rw_w_k[j], rw_w_v[j], rw_w0[j],
                                       rw_w_l1[j], rw_w_l2[j], rw_a0[j], rw_a_l1[j], rw_a_l2[j], rw_g_l1[j],
                                       rw_g_l2[j], lp, v_first, v_res)
            if v_first is None:
                v_first = v
            y = _rw_scan(r, w, k, v, a, rw_k_k[j], rw_k_a[j], rw_r_k[j].reshape(-1), rw_lnx_w[j], rw_lnx_b[j],
                         bsz, lp)
            h = _rw_out(y, g, h, rw_w_o[j])
        part = (bsz * lp) // MOE_SPLIT
        h = jnp.concatenate([_moe(h, norm_ffn[layer], moe_router_g[layer], moe_router_g_b[layer], moe_router_e[layer],
                                  moe_router_e_b[layer], moe_w_gate, moe_w_up, moe_w_down, layer, s * part, part)
                             for s in range(MOE_SPLIT)], axis=0)
    out = _final_norm(h, norm_final, x.dtype)
    return out.reshape(bsz, lp, d)[:, n_meta:ltot]
```

```python
import functools
import math

import jax
import jax.numpy as jnp
from jax import lax
from jax.experimental import pallas as pl
from jax.experimental.pallas import tpu as pltpu

F32 = jnp.float32
BF16 = jnp.bfloat16
HI = lax.Precision.HIGHEST

RMS_EPS = 1e-6
LNX_EPS = 64e-5
N_META = 16
SEQ_ALIGN = 64
S5_CHUNK = 16
S5_LANES = 128
RW_CHUNK = 64
RW_HEAD = 64
RW_PAIRS = 2
RW_UNROLL = 11
LRU_C = 8.0
N_GROUPS = 4
EXPERTS_PER_GROUP = 4
N_EXPERTS = N_GROUPS * EXPERTS_PER_GROUP
MOE_TILE = 512
MOE_SPLIT = 1
VMEM_LIMIT = 56 * 1024 * 1024


def _cparams(sem):
    return pltpu.CompilerParams(dimension_semantics=sem, vmem_limit_bytes=VMEM_LIMIT)


def _pick_tile(n, target):
    best = 8
    for t in range(8, min(n, target) + 1, 8):
        if n % t == 0:
            best = t
    return best


def _const_spec(shape):
    nd = len(shape)
    return pl.BlockSpec(shape, lambda *_: (0,) * nd)


def _rms(x, g):
    return x * lax.rsqrt(jnp.mean(x * x, axis=-1, keepdims=True) + RMS_EPS) * g


def _gelu(x):
    return 0.5 * x * (1.0 + jnp.tanh(math.sqrt(2.0 / math.pi) * (x + 0.044715 * (x * x * x))))


def _sigmoid(x):
    return 1.0 / (1.0 + jnp.exp(-x))


def _softplus(x):
    return jnp.maximum(x, 0.0) + jnp.log(1.0 + jnp.exp(-jnp.abs(x)))


def _bdot(a, b):
    return jnp.dot(a.astype(BF16), b.astype(BF16), preferred_element_type=F32)


def _hdot(a, b):
    return jnp.dot(a, b, preferred_element_type=F32, precision=HI)


def _dot3(a, b):
    m = a.shape[0]
    a_hi = a.astype(BF16)
    b_hi = b.astype(BF16)
    a_lo = (a - a_hi.astype(F32)).astype(BF16)
    b_lo = (b - b_hi.astype(F32)).astype(BF16)
    dot = functools.partial(jnp.dot, preferred_element_type=F32)
    top = dot(jnp.concatenate([a_hi, a_lo], axis=0), b_hi)
    return top[:m] + (top[m:] + dot(a_hi, b_lo))


def _dot3_many(a_list, b_list):
    m = a_list[0].shape[0]
    dot = functools.partial(jnp.dot, preferred_element_type=F32)
    a_hi = [a.astype(BF16) for a in a_list]
    b_hi = [b.astype(BF16) for b in b_list]
    a_lo = [(a - h.astype(F32)).astype(BF16) for a, h in zip(a_list, a_hi)]
    b_lo = [(b - h.astype(F32)).astype(BF16) for b, h in zip(b_list, b_hi)]
    top = [dot(jnp.concatenate([h, l], axis=0), b) for h, l, b in zip(a_hi, a_lo, b_hi)]
    low = [dot(h, b) for h, b in zip(a_hi, b_lo)]
    return [t[:m] + (t[m:] + l) for t, l in zip(top, low)]


def _split_dot_many(xs, m, rhs=False, pieces=3):
    accs = [None] * len(xs)
    xs = list(xs)
    for _ in range(pieces):
        his = [x.astype(BF16) for x in xs]
        parts = [jnp.dot(m, hi, preferred_element_type=F32) if rhs else jnp.dot(hi, m, preferred_element_type=F32)
                 for hi in his]
        accs = [p if a is None else a + p for a, p in zip(accs, parts)]
        xs = [x - hi.astype(F32) for x, hi in zip(xs, his)]
    return accs


def _ab_in_kernel(h_ref, g_ref, w_ref, u_ref, gate_ref, rec_ref, *, s5w, lruw):
    xn = _rms(h_ref[...], g_ref[...])
    z = _bdot(xn, w_ref[...])
    u_ref[...] = z[:, :s5w].astype(u_ref.dtype)
    gate_ref[...] = z[:, s5w:s5w + lruw]
    rec_ref[...] = z[:, s5w + lruw:]


def _ab_in(h, g, w_in, s5w, lruw):
    tp, d = h.shape
    tm = _pick_tile(tp, 512)
    row = lambda n: pl.BlockSpec((tm, n), lambda i: (i, 0))
    return pl.pallas_call(
        functools.partial(_ab_in_kernel, s5w=s5w, lruw=lruw),
        grid=(tp // tm,),
        in_specs=[row(d), _const_spec((1, d)), _const_spec(w_in.shape)],
        out_specs=[row(s5w), row(lruw), row(lruw)],
        out_shape=[jax.ShapeDtypeStruct((tp, s5w), BF16),
                   jax.ShapeDtypeStruct((tp, lruw), F32),
                   jax.ShapeDtypeStruct((tp, lruw), F32)],
        compiler_params=_cparams(("parallel",)),
        name="ab_in",
    )(h, g.reshape(1, d), w_in.astype(BF16))


def _s5_tables(lam_re, lam_im, log_dt, b_re, b_im, c_re, c_im, d_skip):
    g, p = lam_re.shape
    hh = b_re.shape[-1]
    c = S5_CHUNK
    lr, li = lam_re.astype(F32), lam_im.astype(F32)
    dt = jnp.exp(log_dt.astype(F32))[:, None]
    mag = jnp.exp(lr * dt)
    abar_r = mag * jnp.cos(li * dt)
    abar_i = mag * jnp.sin(li * dt)
    den = lr * lr + li * li
    zr = ((abar_r - 1.0) * lr + abar_i * li) / den
    zi = (abar_i * lr - (abar_r - 1.0) * li) / den
    bbar_r = zr[..., None] * b_re - zi[..., None] * b_im
    bbar_i = zr[..., None] * b_im + zi[..., None] * b_re
    def powers(steps):
        st = steps.astype(F32)[:, None, None]
        pmag = jnp.exp(st * (lr * dt))
        return pmag * jnp.cos(st * (li * dt)), pmag * jnp.sin(st * (li * dt))

    def c_times(pw_r, pw_i):
        return (c_re[None] * pw_r[:, :, None, :] - c_im[None] * pw_i[:, :, None, :],
                c_re[None] * pw_i[:, :, None, :] + c_im[None] * pw_r[:, :, None, :])

    down = (c - 1) - jnp.arange(c)
    rev_r, rev_i = powers(down)
    m1_r = rev_r[:, :, :, None] * bbar_r[None] - rev_i[:, :, :, None] * bbar_i[None]
    m1_i = rev_r[:, :, :, None] * bbar_i[None] + rev_i[:, :, :, None] * bbar_r[None]
    m1_r = jnp.transpose(m1_r, (1, 0, 3, 2))
    m1_i = jnp.transpose(m1_i, (1, 0, 3, 2))
    car, cai = c_times(rev_r, rev_i)
    kern = (jnp.einsum('kghp,gpj->kghj', car, bbar_r, precision=HI)
            - jnp.einsum('kghp,gpj->kghj', cai, bbar_i, precision=HI))
    is_tau0 = (down == 0).astype(F32)[:, None, None, None]
    kern = kern + is_tau0 * (d_skip[:, :, None] * jnp.eye(hh, dtype=F32)[None])[None]
    ca_r, ca_i = c_times(*powers(jnp.arange(1, c + 1)))
    m2_r = jnp.transpose(ca_r, (1, 3, 0, 2))
    m2_i = -jnp.transpose(ca_i, (1, 3, 0, 2))
    adv_r, adv_i = powers(jnp.full((1,), c))
    gb = S5_LANES // hh
    nj = g // gb
    eye = jnp.eye(gb, dtype=F32)
    blk = lambda x: x.reshape((nj, gb) + x.shape[1:])
    m1 = jnp.stack([jnp.transpose(blk(m), (0, 2, 1, 3, 4))[:, :, :, :, None, :] * eye[None, None, :, None, :, None]
                    for m in (m1_r, m1_i)], axis=4)
    m1 = m1.reshape(nj, c * gb * hh, 2 * gb * p).astype(BF16)
    krev = jnp.transpose(kern, (1, 0, 3, 2))
    krev = jnp.transpose(blk(krev), (0, 2, 1, 3, 4))[:, :, :, :, None, :] * eye[None, None, :, None, :, None]
    krev = krev.reshape(nj, c * gb * hh, gb * hh).astype(BF16)
    m2 = jnp.stack([blk(m)[:, :, :, :, None, :] * eye[None, :, None, None, :, None] for m in (m2_r, m2_i)], axis=1)
    m2 = m2.reshape(nj, 2 * gb * p, c * gb * hh).astype(BF16)
    return m1, krev, m2, adv_r.reshape(nj, 1, gb * p), adv_i.reshape(nj, 1, gb * p)


def _s5_kernel(u_ref, m1_ref, kr_ref, m2_ref, ar_ref, ai_ref, y_ref, xe_ref, xin_ref, st_ref, *, n_chunks, nb):
    nl = S5_LANES
    csz = S5_CHUNK
    sw = ar_ref.shape[-1]

    @pl.when(pl.program_id(1) == 0)
    def _():
        st_ref[...] = jnp.zeros(st_ref.shape, F32)

    u = u_ref[0]
    xe_ref[...] = jnp.dot(u, m1_ref[0], preferred_element_type=F32)
    ar = jnp.broadcast_to(ar_ref[0], (nb, sw))
    ai = jnp.broadcast_to(ai_ref[0], (nb, sw))

    def body(c, carry):
        sr, si = carry
        off = pl.multiple_of(c * nb, nb)
        xin_ref[pl.ds(off, nb), :] = jnp.concatenate([sr, si], axis=1)
        e = xe_ref[pl.ds(off, nb), :]
        return (ar * sr - ai * si + e[:, :sw], ar * si + ai * sr + e[:, sw:])

    sr, si = lax.fori_loop(0, n_chunks, body, (st_ref[:, :sw], st_ref[:, sw:]))
    st_ref[:, :sw] = sr
    st_ref[:, sw:] = si
    y_ref[0] = _bdot(xin_ref[...], m2_ref[0])
    for t in range(csz):
        y_ref[0, :, t * nl:(t + 1) * nl] += jnp.dot(u[:, :(t + 1) * nl], kr_ref[0, (csz - 1 - t) * nl:, :],
                                                    preferred_element_type=F32)


def _s5_scan(u, tables, bsz, lp):
    m1, krev, m2, adv_r, adv_i = tables
    nj, kin, sw2 = m1.shape
    c = S5_CHUNK
    nl = S5_LANES
    nc = lp // c
    cpt = max(d for d in range(1, nc + 1) if nc % d == 0 and d * bsz <= 512)
    rows = cpt * bsz
    ug = jnp.transpose(u.reshape(bsz, nc, c, nj, nl), (3, 1, 0, 2, 4)).reshape(nj, nc * bsz, kin)
    y = pl.pallas_call(
        functools.partial(_s5_kernel, n_chunks=cpt, nb=bsz),
        grid=(nj, nc // cpt),
        in_specs=[pl.BlockSpec((1, rows, kin), lambda j, r: (j, r, 0)),
                  pl.BlockSpec((1, kin, sw2), lambda j, r: (j, 0, 0)),
                  pl.BlockSpec((1, kin, nl), lambda j, r: (j, 0, 0)),
                  pl.BlockSpec((1, sw2, kin), lambda j, r: (j, 0, 0)),
                  pl.BlockSpec((1, 1, sw2 // 2), lambda j, r: (j, 0, 0)),
                  pl.BlockSpec((1, 1, sw2 // 2), lambda j, r: (j, 0, 0))],
        out_specs=pl.BlockSpec((1, rows, kin), lambda j, r: (j, r, 0)),
        out_shape=jax.ShapeDtypeStruct((nj, nc * bsz, kin), F32),
        scratch_shapes=[pltpu.VMEM((rows, sw2), F32), pltpu.VMEM((rows, sw2), F32), pltpu.VMEM((bsz, sw2), F32)],
        compiler_params=_cparams(("parallel", "arbitrary")),
        name="s5_scan",
    )(ug, m1, krev, m2, adv_r, adv_i)
    y = jnp.transpose(y.reshape(nj, nc, bsz, c, nl), (2, 1, 3, 0, 4))
    return y.reshape(bsz * lp, nj * nl)


def _lru_kernel(rec_ref, gate_ref, cw_ref, cb_ref, wa_ref, ba_ref, wx_ref, bx_ref, lam_ref,
                o_ref, ext_ref, a_ref, b_ref, h_ref, *, tl):
    t = pl.program_id(1)

    @pl.when(t == 0)
    def _():
        ext_ref[0:8, :] = jnp.zeros((8, ext_ref.shape[1]), F32)
        h_ref[...] = jnp.zeros(h_ref.shape, F32)

    x = rec_ref[...]
    ext_ref[8:, :] = x
    xc = cb_ref[...] + cw_ref[3:4, :] * x
    for k in range(3):
        xc = xc + cw_ref[k:k + 1, :] * ext_ref[5 + k:5 + k + tl, :]
    ext_ref[0:8, :] = x[tl - 8:, :]
    r = _sigmoid(_bdot(xc, wa_ref[...]) + ba_ref[...])
    i = _sigmoid(_bdot(xc, wx_ref[...]) + bx_ref[...])
    log_a = (-LRU_C) * r * _softplus(-lam_ref[...])
    a = jnp.exp(log_a)
    a_ref[...] = a
    b_ref[...] = jnp.sqrt(1.0 - a * a) * (i * xc)

    def body(j, h):
        off = pl.multiple_of(j * 8, 8)
        ab = a_ref[pl.ds(off, 8), :]
        bb = b_ref[pl.ds(off, 8), :]
        rows = []
        for q in range(8):
            h = ab[q:q + 1, :] * h + bb[q:q + 1, :]
            rows.append(h)
        o_ref[pl.ds(off, 8), :] = jnp.concatenate(rows, axis=0) * _gelu(gate_ref[pl.ds(off, 8), :])
        return h

    h_ref[...] = lax.fori_loop(0, tl // 8, body, h_ref[...])


def _lru(rec, gate, conv_w, conv_b, w_a, b_a, w_x, b_x, lam, bsz, lp):
    tp, w = rec.shape
    tl = _pick_tile(lp, 1056)
    nt = lp // tl
    heads, hd, _ = w_a.shape

    def dense(wb):
        eye = jnp.eye(heads, dtype=F32)
        return jnp.einsum('hij,hg->higj', wb, eye).reshape(w, w).astype(BF16)

    row = pl.BlockSpec((tl, w), lambda b, t: (b * nt + t, 0))
    vec = _const_spec((1, w))
    return pl.pallas_call(
        functools.partial(_lru_kernel, tl=tl),
        grid=(bsz, nt),
        in_specs=[row, row, _const_spec((4, w)), vec, _const_spec((w, w)), vec, _const_spec((w, w)), vec, vec],
        out_specs=row,
        out_shape=jax.ShapeDtypeStruct((tp, w), F32),
        scratch_shapes=[pltpu.VMEM((tl + 8, w), F32), pltpu.VMEM((tl, w), F32),
                        pltpu.VMEM((tl, w), F32), pltpu.VMEM((1, w), F32)],
        compiler_params=_cparams(("parallel", "arbitrary")),
        name="rglru",
    )(rec, gate, conv_w, conv_b.reshape(1, w), dense(w_a), b_a.reshape(1, w), dense(w_x), b_x.reshape(1, w),
      lam.reshape(1, w))


def _ab_out_kernel(y5_ref, lru_ref, h_ref, wglu_ref, bglu_ref, na_ref, nb_ref, wo_ref, o_ref, *, s5w):
    y = _gelu(y5_ref[...])
    ya = y * _sigmoid(_bdot(y, wglu_ref[...]) + bglu_ref[...])
    ya = _rms(ya, na_ref[...])
    yb = _rms(lru_ref[...], nb_ref[...])
    o_ref[...] = h_ref[...] + _bdot(ya, wo_ref[:s5w, :]) + _bdot(yb, wo_ref[s5w:, :])


def _ab_out(y5, lru, h, w_glu, b_glu, norm_a, norm_b, w_out):
    tp, d = h.shape
    s5w, lruw = y5.shape[1], lru.shape[1]
    tm = _pick_tile(tp, 512)
    row = lambda n: pl.BlockSpec((tm, n), lambda i: (i, 0))
    return pl.pallas_call(
        functools.partial(_ab_out_kernel, s5w=s5w),
        grid=(tp // tm,),
        in_specs=[row(s5w), row(lruw), row(d), _const_spec((s5w, s5w)), _const_spec((1, s5w)),
                  _const_spec((1, s5w)), _const_spec((1, lruw)), _const_spec(w_out.shape)],
        out_specs=row(d),
        out_shape=jax.ShapeDtypeStruct((tp, d), F32),
        compiler_params=_cparams(("parallel",)),
        name="ab_out",
    )(y5, lru, h, w_glu.astype(BF16), b_glu.reshape(1, s5w), norm_a.reshape(1, s5w),
      norm_b.reshape(1, lruw), w_out.astype(BF16))


def _route_kernel(h_ref, g_ref, wr_ref, br_ref, xn_ref, rt_ref):
    xn = _rms(h_ref[...], g_ref[...])
    xn_ref[...] = xn.astype(xn_ref.dtype)
    lg = _hdot(xn, wr_ref[...]) + br_ref[...]
    lane = lax.broadcasted_iota(jnp.int32, lg.shape, 1).astype(F32)
    big = float(lg.shape[1])
    neg = -jnp.inf
    gl = jnp.where(lane < N_GROUPS, lg, neg)
    mg = jnp.max(gl, axis=-1, keepdims=True)
    gidx = jnp.min(jnp.where(gl == mg, lane, big), axis=-1, keepdims=True)
    pg_sel = 1.0 / jnp.sum(jnp.exp(gl - mg), axis=-1, keepdims=True)
    lo = N_GROUPS + EXPERTS_PER_GROUP * gidx
    le = jnp.where(lane >= lo, jnp.where(lane < lo + EXPERTS_PER_GROUP, lg, neg), neg)
    v1 = jnp.max(le, axis=-1, keepdims=True)
    i1 = jnp.min(jnp.where(le == v1, lane, big), axis=-1, keepdims=True)
    le2 = jnp.where(lane == i1, neg, le)
    v2 = jnp.max(le2, axis=-1, keepdims=True)
    i2 = jnp.min(jnp.where(le2 == v2, lane, big), axis=-1, keepdims=True)
    e2 = jnp.exp(v2 - v1)
    w1 = pg_sel / (1.0 + e2)
    w2 = w1 * e2
    rt_ref[...] = jnp.where(lane == 0.0, w1, jnp.where(lane == 1.0, w2, jnp.where(
        lane == 2.0, i1 - N_GROUPS, jnp.where(lane == 3.0, i2 - N_GROUPS, 0.0))))


def _route(h, g, wr_g, br_g, wr_e, br_e, row0, tp):
    d = h.shape[1]
    tm = _pick_tile(math.gcd(tp, row0) if row0 else tp, 512)
    blk0 = row0 // tm
    nl = 128
    wr = jnp.zeros((d, nl), F32).at[:, :N_GROUPS].set(wr_g).at[:, N_GROUPS:N_GROUPS + N_EXPERTS].set(wr_e)
    br = jnp.zeros((1, nl), F32).at[0, :N_GROUPS].set(br_g).at[0, N_GROUPS:N_GROUPS + N_EXPERTS].set(br_e)
    row = lambda n: pl.BlockSpec((tm, n), lambda i: (i, 0))
    return pl.pallas_call(
        _route_kernel,
        grid=(tp // tm,),
        in_specs=[pl.BlockSpec((tm, d), lambda i: (i + blk0, 0)), _const_spec((1, d)), _const_spec((d, nl)),
                  _const_spec((1, nl))],
        out_specs=[row(d), row(nl)],
        out_shape=[jax.ShapeDtypeStruct((tp, d), F32), jax.ShapeDtypeStruct((tp, nl), F32)],
        compiler_params=_cparams(("parallel",)),
        name="moe_route",
    )(h, g.reshape(1, d), wr, br)


def _gmm_kernel(te_ref, tv_ref, xa_ref, xb_ref, wg_ref, wu_ref, wd_ref, o_ref, wg_bf, wu_bf, wd_bf, *, n_half):
    i = pl.program_id(0)

    @pl.when(jnp.logical_or(i == 0, te_ref[i] != te_ref[jnp.maximum(i - 1, 0)]))
    def _():
        wg_bf[...] = wg_ref[0, 0].astype(BF16)
        wu_bf[...] = wu_ref[0, 0].astype(BF16)
        wd_bf[...] = wd_ref[0, 0].astype(BF16)

    @pl.when(tv_ref[i] != 0)
    def _():
        x = jnp.where(i < n_half, xa_ref[...], xb_ref[...]).astype(BF16)
        hg = jnp.dot(x, wg_bf[...], preferred_element_type=F32)
        hu = jnp.dot(x, wu_bf[...], preferred_element_type=F32)
        hid = hg * _sigmoid(hg) * hu
        o_ref[...] = _bdot(hid, wd_bf[...])

    @pl.when(tv_ref[i] == 0)
    def _():
        o_ref[...] = jnp.zeros(o_ref.shape, o_ref.dtype)


def _moe(h, g, wr_g, br_g, wr_e, br_e, w_gate, w_up, w_down, layer, row0, tp):
    d = h.shape[1]
    f = w_gate.shape[-1]
    xn, rt = _route(h, g, wr_g, br_g, wr_e, br_e, row0, tp)
    gate = rt[:, 0:2]
    eid = rt[:, 2:4].astype(jnp.int32)
    tmm = MOE_TILE
    na = 2 * tp
    e_flat = eid.reshape(na)
    onehot = (e_flat[:, None] == jnp.arange(N_EXPERTS, dtype=jnp.int32)[None, :]).astype(jnp.int32)
    csum = jnp.cumsum(onehot, axis=0)
    counts = csum[-1]
    padded = ((counts + tmm - 1) // tmm) * tmm
    ends = jnp.cumsum(padded)
    starts = ends - padded
    dest = jnp.sum(onehot * (csum + (starts - 1)[None, :]), axis=1)
    n_half = (-(-na // tmm) + N_EXPERTS + 1) // 2
    n_tiles = 2 * n_half
    nrows = n_tiles * tmm
    tile_start = jnp.arange(n_tiles, dtype=jnp.int32) * tmm
    tile_e = jnp.sum((ends[None, :] <= tile_start[:, None]).astype(jnp.int32), axis=1)
    tile_v = (tile_e < N_EXPERTS).astype(jnp.int32)
    tile_e = jnp.minimum(tile_e, N_EXPERTS - 1)
    src = (jnp.arange(nrows, dtype=jnp.int32) % tp).at[dest].set(
        jnp.arange(na, dtype=jnp.int32) // 2, mode="promise_in_bounds", unique_indices=True)
    take = lambda a, i: a.at[i].get(mode="promise_in_bounds")
    xs_a = take(xn, src[:n_half * tmm])
    xs_b = take(xn, src[n_half * tmm:])
    ys = pl.pallas_call(
        functools.partial(_gmm_kernel, n_half=n_half),
        grid_spec=pltpu.PrefetchScalarGridSpec(
            num_scalar_prefetch=2,
            grid=(n_tiles,),
            in_specs=[pl.BlockSpec((tmm, d), lambda i, te, tv: (jnp.minimum(i, n_half - 1), 0)),
                      pl.BlockSpec((tmm, d), lambda i, te, tv: (jnp.maximum(i - n_half, 0), 0)),
                      pl.BlockSpec((1, 1, d, f), lambda i, te, tv: (layer, te[i], 0, 0)),
                      pl.BlockSpec((1, 1, d, f), lambda i, te, tv: (layer, te[i], 0, 0)),
                      pl.BlockSpec((1, 1, f, d), lambda i, te, tv: (layer, te[i], 0, 0))],
            out_specs=pl.BlockSpec((tmm, d), lambda i, te, tv: (i, 0)),
            scratch_shapes=[pltpu.VMEM((d, f), BF16), pltpu.VMEM((d, f), BF16), pltpu.VMEM((f, d), BF16)],
        ),
        out_shape=jax.ShapeDtypeStruct((nrows, d), F32),
        compiler_params=_cparams(("arbitrary",)),
        name="moe_gmm",
    )(tile_e, tile_v, xs_a, xs_b, w_gate, w_up, w_down)
    d2 = dest.reshape(tp, 2)
    return (lax.slice_in_dim(h, row0, row0 + tp, axis=0)
            + gate[:, 0:1] * take(ys, d2[:, 0]) + gate[:, 1:2] * take(ys, d2[:, 1]))


def _rw_pre_kernel(*refs, tm, lp, has_vres):
    if has_vres:
        (h_ref, hp_ref, g_ref, mu_ref, wr_ref, wk_ref, wv_ref, w0_ref, wl1_ref, wl2_ref,
         a0_ref, al1_ref, al2_ref, gl1_ref, gl2_ref, vf_ref, v0_ref, vl1_ref, vl2_ref,
         r_ref, k_ref, v_ref, w_ref, a_ref, gg_ref) = refs
    else:
        (h_ref, hp_ref, g_ref, mu_ref, wr_ref, wk_ref, wv_ref, w0_ref, wl1_ref, wl2_ref,
         a0_ref, al1_ref, al2_ref, gl1_ref, gl2_ref,
         r_ref, k_ref, v_ref, w_ref, a_ref, gg_ref) = refs
    i = pl.program_id(0)
    g = g_ref[...]
    x = _rms(h_ref[...], g)
    xp8 = _rms(hp_ref[...], g)
    row = lax.broadcasted_iota(jnp.int32, x.shape, 0)
    prev = jnp.where(row == 0, jnp.broadcast_to(xp8[7:8, :], x.shape), pltpu.roll(x, 1, axis=0))
    first = lax.rem(lp - lax.rem(i * tm, lp), lp)
    prev = jnp.where(row == first, 0.0, prev)
    xx = prev - x
    xr, xw, xk, xv, xa, xg = [x + xx * mu_ref[j:j + 1, :] for j in range(6)]
    r_ref[...] = _bdot(xr, wr_ref[...]).astype(r_ref.dtype)
    k_ref[...] = _bdot(xk, wk_ref[...]).astype(k_ref.dtype)
    v = _bdot(xv, wv_ref[...])
    if has_vres:
        mix = _sigmoid(v0_ref[...] + _bdot(_bdot(xv, vl1_ref[...]), vl2_ref[...]))
        v = v + (vf_ref[...] - v) * mix
    v_ref[...] = v.astype(v_ref.dtype)
    w_ref[...] = -_softplus(-(w0_ref[...] + _bdot(jnp.tanh(_bdot(xw, wl1_ref[...])), wl2_ref[...]))) - 0.5
    a_ref[...] = _sigmoid(a0_ref[...] + _bdot(_bdot(xa, al1_ref[...]), al2_ref[...]))
    gg_ref[...] = _bdot(_sigmoid(_bdot(xg, gl1_ref[...])), gl2_ref[...]).astype(gg_ref.dtype)


def _rw_pre(h, g, mu, w_r, w_k, w_v, w0, w_l1, w_l2, a0, a_l1, a_l2, g_l1, g_l2, lp, v_first, v_res):
    tp, d = h.shape
    tm = _pick_tile(tp, min(256, lp))
    has_vres = v_res is not None
    row = pl.BlockSpec((tm, d), lambda i: (i, 0))
    prev8 = pl.BlockSpec((8, d), lambda i: (jnp.maximum(i * (tm // 8) - 1, 0), 0))
    vec = _const_spec((1, d))
    mu8 = jnp.zeros((8, d), F32).at[:6].set(mu)
    bf = lambda w: w.astype(BF16)
    ins = [h, h, g.reshape(1, d), mu8, bf(w_r), bf(w_k), bf(w_v), w0.reshape(1, d), bf(w_l1), bf(w_l2),
           a0.reshape(1, d), bf(a_l1), bf(a_l2), bf(g_l1), bf(g_l2)]
    specs = [row, prev8, vec, _const_spec((8, d))] + [_const_spec((d, d))] * 3 + [
        vec, _const_spec(w_l1.shape), _const_spec(w_l2.shape),
        vec, _const_spec(a_l1.shape), _const_spec(a_l2.shape), _const_spec(g_l1.shape), _const_spec(g_l2.shape)]
    if has_vres:
        v0, v_l1, v_l2 = v_res
        ins += [v_first, v0.reshape(1, d), bf(v_l1), bf(v_l2)]
        specs += [row, vec, _const_spec(v_l1.shape), _const_spec(v_l2.shape)]
    return pl.pallas_call(
        functools.partial(_rw_pre_kernel, tm=tm, lp=lp, has_vres=has_vres),
        grid=(tp // tm,),
        in_specs=specs,
        out_specs=[row] * 6,
        out_shape=[jax.ShapeDtypeStruct((tp, d), dt) for dt in (BF16, BF16, BF16, F32, F32, BF16)],
        compiler_params=_cparams(("parallel",)),
        name="rwkv_pre",
    )(*ins)


def _rw_scan_kernel(r_ref, w_ref, k_ref, v_ref, a_ref, kk_ref, ka_ref, rk_ref, lw_ref, lb_ref,
                    o_ref, g_scr, h_scr, q_scr, y0_scr, *, n_chunks, unroll, pairs):
    c = RW_CHUNK
    nl = 2 * RW_HEAD
    c2 = 2 * c
    ri = lax.broadcasted_iota(jnp.int32, (c2, nl), 0)
    ci = lax.broadcasted_iota(jnp.int32, (c2, nl), 1)
    own = ((ri >= c) == (ci >= RW_HEAD)).astype(F32)
    t_in = jnp.bitwise_and(ri, c - 1)
    s_in = jnp.bitwise_and(ci, c - 1)
    strict = jnp.where(s_in < t_in, own, 0.0)
    incl = jnp.where(s_in <= t_in, own, 0.0)
    causal2 = jnp.concatenate([strict, incl], axis=0)
    eye_l = (ri == ci).astype(F32)
    same_head = own.astype(BF16)
    ti = lax.broadcasted_iota(jnp.int32, (c, c), 0)
    si = lax.broadcasted_iota(jnp.int32, (c, c), 1)
    tril_c = (si <= ti).astype(BF16)
    lanes = lambda p: slice(p * nl, (p + 1) * nl)

    def stack(x):
        return jnp.concatenate([x, x], axis=0) * own

    def offset(ch):
        return ch * c if isinstance(ch, int) else pl.multiple_of(ch * c, c)

    def groups(fn):
        for p in range(pairs):
            def body(i, _):
                fn([i * unroll + q for q in range(unroll)], p)
                return 0
            lax.fori_loop(0, n_chunks // unroll, body, 0)
            if n_chunks % unroll:
                fn(list(range(n_chunks - n_chunks % unroll, n_chunks)), p)

    def prep(chs, p):
        ln = lanes(p)
        kk_w, ka_w = kk_ref[:, ln], ka_ref[:, ln]
        offs = [offset(ch) for ch in chs]
        r = [r_ref[pl.ds(o, c), ln] for o in offs]
        k = [k_ref[pl.ds(o, c), ln] for o in offs]
        v = [v_ref[pl.ds(o, c), ln] for o in offs]
        a = [a_ref[pl.ds(o, c), ln] for o in offs]
        logw = [-jnp.exp(w_ref[pl.ds(o, c), ln]) for o in offs]
        kk = [x * kk_w for x in k]
        ss = _split_dot_many([x * x for x in kk], same_head, pieces=2)
        kk = [x / jnp.maximum(jnp.sqrt(q), 1e-12) for x, q in zip(kk, ss)]
        kmod = [x * (1.0 + (y - 1.0) * ka_w) for x, y in zip(k, a)]
        cum = _split_dot_many(logw, tril_c, rhs=True, pieces=2)
        p_incl = [jnp.exp(x) for x in cum]
        p_inv = [jnp.exp(-x) for x in cum]
        p_end = [jnp.exp(x[c - 1:c, :] - x) for x in cum]
        kka = [x * y for x, y in zip(kk, a)]
        a_s = [stack(-x * jnp.exp(y - z)) for x, y, z in zip(kk, cum, logw)]
        r_s = [stack(x * y) for x, y in zip(r, p_incl)]
        v_s = [stack(x).astype(BF16) for x in v]
        lhs = [jnp.concatenate([x, y], axis=0).astype(BF16) for x, y in zip(a_s, r_s)]
        rhs = [jnp.concatenate([stack(x * z), stack(y * z)], axis=0).astype(BF16) for x, y, z in zip(kka, kmod, p_inv)]
        big = [lax.dot_general(x, y, (((1,), (1,)), ((), ())), preferred_element_type=F32) for x, y in zip(lhs, rhs)]
        lpow = [(x[:c2, :c2] * strict).astype(BF16) for x in big]
        a_rb = [x[c2:, :c2] * incl for x in big]
        avk = [_bdot(x[:, c2:] * causal2, w) for x, w in zip(big, v_s)]
        x = [jnp.concatenate([p, q[:c2]], axis=1) for p, q in zip(a_s, avk)]
        x = [p + _bdot(q, p) for p, q in zip(x, lpow)]
        for _ in range(5):
            lpow = [jnp.dot(q, q, preferred_element_type=F32).astype(BF16) for q in lpow]
            x = [p + _bdot(q, p) for p, q in zip(x, lpow)]
        xb = [p.astype(BF16) for p in x]
        bh_t = [stack(p * q).T for p, q in zip(kka, p_end)]
        kh_t = [stack(p * q).T for p, q in zip(kmod, p_end)]
        both = [_bdot(jnp.concatenate([u, p], axis=0), z) for u, p, z in zip(a_rb, bh_t, xb)]
        qy = [jnp.concatenate([p, q[c2:]], axis=1) + z[:c2] for p, q, z in zip(r_s, avk, both)]
        gh = [z[c2:] + jnp.concatenate([eye_l * q[c - 1:c, :], _bdot(u, w)], axis=1)
              for z, q, u, w in zip(both, p_incl, kh_t, v_s)]
        for ch, u, q in zip(chs, qy, gh):
            q_scr[p, ch] = u[:c, :nl] + u[c:, :nl]
            y0_scr[p, ch] = u[:c, nl:] + u[c:, nl:]
            g_scr[p, ch] = q[:, :nl]
            h_scr[p, ch] = q[:, nl:]

    groups(prep)

    def seq(ch, states):
        both = _dot3_many([jnp.concatenate([q_scr[p, ch], g_scr[p, ch]], axis=0) for p in range(pairs)], states)
        for p in range(pairs):
            o_ref[pl.ds(offset(ch), c), lanes(p)] = both[p][:c] + y0_scr[p, ch]
        return tuple(both[p][c:] + h_scr[p, ch] for p in range(pairs))

    lax.fori_loop(0, n_chunks, seq, tuple(jnp.zeros((nl, nl), F32) for _ in range(pairs)))

    def post(chs, p):
        ln = lanes(p)
        ka_w, rk_w, lw, lb = ka_ref[:, ln], rk_ref[:, ln], lw_ref[:, ln], lb_ref[:, ln]
        offs = [offset(ch) for ch in chs]
        y = [o_ref[pl.ds(o, c), ln] for o in offs]
        kmod = [k_ref[pl.ds(o, c), ln] * (1.0 + (a_ref[pl.ds(o, c), ln] - 1.0) * ka_w) for o in offs]
        rk = [r_ref[pl.ds(o, c), ln] * u * rk_w for o, u in zip(offs, kmod)]
        sums = _split_dot_many([jnp.concatenate([u, u * u, q], axis=0) for u, q in zip(y, rk)], same_head)
        for o, u, s in zip(offs, y, sums):
            mean = s[:c] * (1.0 / RW_HEAD)
            var = s[c:2 * c] * (1.0 / RW_HEAD) - mean * mean
            o_ref[pl.ds(o, c), ln] = ((u - mean) * lax.rsqrt(var + LNX_EPS) * lw + lb
                                      + s[2 * c:] * v_ref[pl.ds(o, c), ln])

    groups(post)


def _rw_scan(r, w, k, v, a, k_k, k_a, r_k, lnx_w, lnx_b, bsz, lp):
    tp, d = r.shape
    nl = 2 * RW_HEAD
    pairs = RW_PAIRS
    bw = pairs * nl
    nhp = d // bw
    nc = lp // RW_CHUNK
    seq = pl.BlockSpec((lp, bw), lambda b, hp: (b, hp))
    vec = pl.BlockSpec((1, bw), lambda b, hp: (0, hp))
    c = RW_CHUNK
    return pl.pallas_call(
        functools.partial(_rw_scan_kernel, n_chunks=nc, unroll=RW_UNROLL, pairs=pairs),
        grid=(bsz, nhp),
        in_specs=[seq] * 5 + [vec] * 5,
        out_specs=seq,
        out_shape=jax.ShapeDtypeStruct((tp, d), F32),
        scratch_shapes=[pltpu.VMEM((pairs, nc, nl, nl), F32), pltpu.VMEM((pairs, nc, nl, nl), F32),
                        pltpu.VMEM((pairs, nc, c, nl), F32), pltpu.VMEM((pairs, nc, c, nl), F32)],
        compiler_params=_cparams(("parallel", "parallel")),
        name="rwkv_scan",
    )(r, w, k, v, a, k_k.reshape(1, d), k_a.reshape(1, d), r_k.reshape(1, d),
      lnx_w.reshape(1, d), lnx_b.reshape(1, d))


def _rw_out_kernel(y_ref, g_ref, h_ref, wo_ref, o_ref):
    o_ref[...] = h_ref[...] + _bdot(y_ref[...] * g_ref[...], wo_ref[...])


def _rw_out(y, g, h, w_o):
    tp, d = h.shape
    tm = _pick_tile(tp, 512)
    row = pl.BlockSpec((tm, d), lambda i: (i, 0))
    return pl.pallas_call(
        _rw_out_kernel,
        grid=(tp // tm,),
        in_specs=[row, row, row, _const_spec((d, d))],
        out_specs=row,
        out_shape=jax.ShapeDtypeStruct((tp, d), F32),
        compiler_params=_cparams(("parallel",)),
        name="rwkv_out",
    )(y, g, h, w_o.astype(BF16))


def _final_norm_kernel(h_ref, g_ref, o_ref):
    o_ref[...] = _rms(h_ref[...], g_ref[...]).astype(o_ref.dtype)


def _final_norm(h, g, dtype):
    tp, d = h.shape
    tm = _pick_tile(tp, 512)
    row = pl.BlockSpec((tm, d), lambda i: (i, 0))
    return pl.pallas_call(
        _final_norm_kernel,
        grid=(tp // tm,),
        in_specs=[row, _const_spec((1, d))],
        out_specs=row,
        out_shape=jax.ShapeDtypeStruct((tp, d), dtype),
        compiler_params=_cparams(("parallel",)),
        name="final_norm",
    )(h, g.reshape(1, d))


def kernel(x, meta_tokens, norm_mix, norm_ffn, norm_final, ab_w_in, ab_w_out, ab_norm_a, ab_norm_b, s5_lam_re, s5_lam_im, s5_log_dt, s5_b_re, s5_b_im, s5_c_re, s5_c_im, s5_d, s5_w_glu, s5_b_glu, lru_conv_w, lru_conv_b, lru_w_a, lru_b_a, lru_w_x, lru_b_x, lru_lam, rw_mu, rw_w_r, rw_w_k, rw_w_v, rw_w_o, rw_w0, rw_w_l1, rw_w_l2, rw_a0, rw_a_l1, rw_a_l2, rw_v0, rw_v_l1, rw_v_l2, rw_g_l1, rw_g_l2, rw_k_k, rw_k_a, rw_r_k, rw_lnx_w, rw_lnx_b, moe_router_g, moe_router_g_b, moe_router_e, moe_router_e_b, moe_w_gate, moe_w_up, moe_w_down):
    bsz, seq, d = x.shape
    n_meta = meta_tokens.shape[0]
    depth = norm_mix.shape[0]
    ltot = n_meta + seq
    lp = -(-ltot // SEQ_ALIGN) * SEQ_ALIGN
    s5w = s5_w_glu.shape[-1]
    lruw = lru_lam.shape[-1]
    meta = jnp.broadcast_to(meta_tokens.astype(F32)[None], (bsz, n_meta, d))
    h = jnp.concatenate([meta, x.astype(F32), jnp.zeros((bsz, lp - ltot, d), F32)], axis=1).reshape(bsz * lp, d)
    v_first = None
    for layer in range(depth):
        j = layer // 2
        if layer % 2 == 0:
            u, gate, rec = _ab_in(h, norm_mix[layer], ab_w_in[j], s5w, lruw)
            tables = _s5_tables(s5_lam_re[j], s5_lam_im[j], s5_log_dt[j], s5_b_re[j], s5_b_im[j],
                                s5_c_re[j], s5_c_im[j], s5_d[j])
            y5 = _s5_scan(u, tables, bsz, lp)
            lru = _lru(rec, gate, lru_conv_w[j], lru_conv_b[j], lru_w_a[j], lru_b_a[j], lru_w_x[j], lru_b_x[j],
                       lru_lam[j], bsz, lp)
            h = _ab_out(y5, lru, h, s5_w_glu[j], s5_b_glu[j], ab_norm_a[j], ab_norm_b[j], ab_w_out[j])
        else:
            v_res = (rw_v0[j - 1], rw_v_l1[j - 1], rw_v_l2[j - 1]) if j > 0 else None
            r, k, v, w, a, g = _rw_pre(h, norm_mix[layer], rw_mu[j], rw_w_r[j], rw_w_k[j], rw_w_v[j], rw_w0[j],
                                       rw_w_l1[j], rw_w_l2[j], rw_a0[j], rw_a_l1[j], rw_a_l2[j], rw_g_l1[j],
                                       rw_g_l2[j], lp, v_first, v_res)
            if v_first is None:
                v_first = v
            y = _rw_scan(r, w, k, v, a, rw_k_k[j], rw_k_a[j], rw_r_k[j].reshape(-1), rw_lnx_w[j], rw_lnx_b[j],
                         bsz, lp)
            h = _rw_out(y, g, h, rw_w_o[j])
        part = (bsz * lp) // MOE_SPLIT
        h = jnp.concatenate([_moe(h, norm_ffn[layer], moe_router_g[layer], moe_router_g_b[layer], moe_router_e[layer],
                                  moe_router_e_b[layer], moe_w_gate, moe_w_up, moe_w_down, layer, s * part, part)
                             for s in range(MOE_SPLIT)], axis=0)
    out = _final_norm(h, norm_final, x.dtype)
    return out.reshape(bsz, lp, d)[:, n_meta:ltot]
```

```python
import functools
import math

import jax
import jax.numpy as jnp
from jax import lax
from jax.experimental import pallas as pl
from jax.experimental.pallas import tpu as pltpu

F32 = jnp.float32
BF16 = jnp.bfloat16
HI = lax.Precision.HIGHEST

RMS_EPS = 1e-6
LNX_EPS = 64e-5
N_META = 16
SEQ_ALIGN = 64
S5_CHUNK = 16
S5_LANES = 128
RW_CHUNK = 64
RW_HEAD = 64
RW_PAIRS = 2
RW_UNROLL = 11
LRU_C = 8.0
N_GROUPS = 4
EXPERTS_PER_GROUP = 4
N_EXPERTS = N_GROUPS * EXPERTS_PER_GROUP
MOE_TILE = 512
ROUTE_LANES = 128
VMEM_LIMIT = 56 * 1024 * 1024


def _cparams(sem):
    return pltpu.CompilerParams(dimension_semantics=sem, vmem_limit_bytes=VMEM_LIMIT)


def _pick_tile(n, target):
    best = 8
    for t in range(8, min(n, target) + 1, 8):
        if n % t == 0:
            best = t
    return best


def _const_spec(shape):
    nd = len(shape)
    return pl.BlockSpec(shape, lambda *_: (0,) * nd)


def _rms(x, g):
    return x * lax.rsqrt(jnp.mean(x * x, axis=-1, keepdims=True) + RMS_EPS) * g


def _gelu(x):
    return 0.5 * x * (1.0 + jnp.tanh(math.sqrt(2.0 / math.pi) * (x + 0.044715 * (x * x * x))))


def _sigmoid(x):
    return 1.0 / (1.0 + jnp.exp(-x))


def _softplus(x):
    return jnp.maximum(x, 0.0) + jnp.log(1.0 + jnp.exp(-jnp.abs(x)))


def _bdot(a, b):
    return jnp.dot(a.astype(BF16), b.astype(BF16), preferred_element_type=F32)


def _hdot(a, b):
    return jnp.dot(a, b, preferred_element_type=F32, precision=HI)


def _dot3(a, b):
    m = a.shape[0]
    a_hi = a.astype(BF16)
    b_hi = b.astype(BF16)
    a_lo = (a - a_hi.astype(F32)).astype(BF16)
    b_lo = (b - b_hi.astype(F32)).astype(BF16)
    dot = functools.partial(jnp.dot, preferred_element_type=F32)
    top = dot(jnp.concatenate([a_hi, a_lo], axis=0), b_hi)
    return top[:m] + (top[m:] + dot(a_hi, b_lo))


def _dot3_many(a_list, b_list):
    m = a_list[0].shape[0]
    dot = functools.partial(jnp.dot, preferred_element_type=F32)
    a_hi = [a.astype(BF16) for a in a_list]
    b_hi = [b.astype(BF16) for b in b_list]
    a_lo = [(a - h.astype(F32)).astype(BF16) for a, h in zip(a_list, a_hi)]
    b_lo = [(b - h.astype(F32)).astype(BF16) for b, h in zip(b_list, b_hi)]
    top = [dot(jnp.concatenate([h, l], axis=0), b) for h, l, b in zip(a_hi, a_lo, b_hi)]
    low = [dot(h, b) for h, b in zip(a_hi, b_lo)]
    return [t[:m] + (t[m:] + l) for t, l in zip(top, low)]


def _split_dot_many(xs, m, rhs=False, pieces=3):
    accs = [None] * len(xs)
    xs = list(xs)
    for _ in range(pieces):
        his = [x.astype(BF16) for x in xs]
        parts = [jnp.dot(m, hi, preferred_element_type=F32) if rhs else jnp.dot(hi, m, preferred_element_type=F32)
                 for hi in his]
        accs = [p if a is None else a + p for a, p in zip(accs, parts)]
        xs = [x - hi.astype(F32) for x, hi in zip(xs, his)]
    return accs


def _ab_in_kernel(h_ref, g_ref, w_ref, u_ref, gate_ref, rec_ref, *, s5w, lruw):
    xn = _rms(h_ref[...], g_ref[...])
    z = _bdot(xn, w_ref[...])
    u_ref[...] = z[:, :s5w].astype(u_ref.dtype)
    gate_ref[...] = z[:, s5w:s5w + lruw]
    rec_ref[...] = z[:, s5w + lruw:]


def _ab_in(h, g, w_in, s5w, lruw):
    tp, d = h.shape
    tm = _pick_tile(tp, 512)
    row = lambda n: pl.BlockSpec((tm, n), lambda i: (i, 0))
    return pl.pallas_call(
        functools.partial(_ab_in_kernel, s5w=s5w, lruw=lruw),
        grid=(tp // tm,),
        in_specs=[row(d), _const_spec((1, d)), _const_spec(w_in.shape)],
        out_specs=[row(s5w), row(lruw), row(lruw)],
        out_shape=[jax.ShapeDtypeStruct((tp, s5w), BF16),
                   jax.ShapeDtypeStruct((tp, lruw), F32),
                   jax.ShapeDtypeStruct((tp, lruw), F32)],
        compiler_params=_cparams(("parallel",)),
        name="ab_in",
    )(h, g.reshape(1, d), w_in.astype(BF16))


def _s5_tables(lam_re, lam_im, log_dt, b_re, b_im, c_re, c_im, d_skip):
    g, p = lam_re.shape
    hh = b_re.shape[-1]
    c = S5_CHUNK
    lr, li = lam_re.astype(F32), lam_im.astype(F32)
    dt = jnp.exp(log_dt.astype(F32))[:, None]
    mag = jnp.exp(lr * dt)
    abar_r = mag * jnp.cos(li * dt)
    abar_i = mag * jnp.sin(li * dt)
    den = lr * lr + li * li
    zr = ((abar_r - 1.0) * lr + abar_i * li) / den
    zi = (abar_i * lr - (abar_r - 1.0) * li) / den
    bbar_r = zr[..., None] * b_re - zi[..., None] * b_im
    bbar_i = zr[..., None] * b_im + zi[..., None] * b_re
    bbr_t, bbi_t = jnp.swapaxes(bbar_r, 1, 2), jnp.swapaxes(bbar_i, 1, 2)
    cr_t, ci_t = jnp.swapaxes(c_re, 1, 2), jnp.swapaxes(c_im, 1, 2)

    def powers(steps):
        st = steps.astype(F32)[None, :, None]
        pmag = jnp.exp(st * (lr * dt)[:, None, :])
        ang = st * (li * dt)[:, None, :]
        return pmag * jnp.cos(ang), pmag * jnp.sin(ang)

    down = (c - 1) - jnp.arange(c)
    rev_r, rev_i = powers(down)
    m1_r = rev_r[:, :, None, :] * bbr_t[:, None] - rev_i[:, :, None, :] * bbi_t[:, None]
    m1_i = rev_r[:, :, None, :] * bbi_t[:, None] + rev_i[:, :, None, :] * bbr_t[:, None]
    car = c_re[:, None] * rev_r[:, :, None, :] - c_im[:, None] * rev_i[:, :, None, :]
    cai = c_re[:, None] * rev_i[:, :, None, :] + c_im[:, None] * rev_r[:, :, None, :]
    kern = (jnp.einsum('gqhp,gpj->gqjh', car, bbar_r, precision=HI)
            - jnp.einsum('gqhp,gpj->gqjh', cai, bbar_i, precision=HI))
    is_tau0 = (down == 0).astype(F32)[None, :, None, None]
    kern = kern + is_tau0 * (d_skip[:, None, None, :] * jnp.eye(hh, dtype=F32)[None, None])
    up_r, up_i = powers(jnp.arange(1, c + 1))
    up_r, up_i = jnp.swapaxes(up_r, 1, 2)[..., None], jnp.swapaxes(up_i, 1, 2)[..., None]
    m2_r = cr_t[:, :, None, :] * up_r - ci_t[:, :, None, :] * up_i
    m2_i = -(cr_t[:, :, None, :] * up_i + ci_t[:, :, None, :] * up_r)
    adv_r, adv_i = powers(jnp.full((1,), c))
    gb = S5_LANES // hh
    nj = g // gb

    def rows_sgh(x):
        w = x.shape[-1]
        return jnp.transpose(x.reshape(nj, gb, c, hh, w), (0, 2, 1, 3, 4)).reshape(nj, c * gb * hh, w)

    def place(base, spread, row_group, col_group):
        out = jnp.einsum('jrw,wc->jrc', base.astype(BF16), spread.astype(BF16), preferred_element_type=F32)
        rg = row_group(lax.broadcasted_iota(jnp.int32, out.shape[1:], 0))
        cg = col_group(lax.broadcasted_iota(jnp.int32, out.shape[1:], 1))
        return jnp.where(rg == cg, out, 0.0)

    grp_sgh = lambda r: (r // hh) % gb
    rep = lambda w: jnp.tile(jnp.eye(w, dtype=F32), (1, gb))
    m1 = jnp.concatenate([place(rows_sgh(m), rep(p), grp_sgh, lambda col: col // p) for m in (m1_r, m1_i)], axis=-1)
    krev = place(rows_sgh(kern), rep(hh), grp_sgh, lambda col: col // hh)
    ri = lax.broadcasted_iota(jnp.int32, (c * hh, c * gb * hh), 0)
    ci = lax.broadcasted_iota(jnp.int32, (c * hh, c * gb * hh), 1)
    spread_th = ((ri // hh == ci // (gb * hh)) & (ri % hh == ci % hh)).astype(F32)
    m2 = jnp.concatenate([place(m.reshape(nj, gb * p, c * hh), spread_th, lambda r: r // p,
                                lambda col: (col // hh) % gb) for m in (m2_r, m2_i)], axis=1)
    return (m1.astype(BF16), krev.astype(BF16), m2.astype(BF16),
            adv_r.reshape(nj, 1, gb * p), adv_i.reshape(nj, 1, gb * p))


def _s5_kernel(u_ref, m1_ref, kr_ref, m2_ref, ar_ref, ai_ref, y_ref, xe_ref, xin_ref, st_ref, *, n_chunks, nb):
    nl = S5_LANES
    csz = S5_CHUNK
    sw = ar_ref.shape[-1]

    @pl.when(pl.program_id(1) == 0)
    def _():
        st_ref[...] = jnp.zeros(st_ref.shape, F32)

    u = u_ref[0]
    xe_ref[...] = jnp.dot(u, m1_ref[0], preferred_element_type=F32)
    ar = jnp.broadcast_to(ar_ref[0], (nb, sw))
    ai = jnp.broadcast_to(ai_ref[0], (nb, sw))

    def body(c, carry):
        sr, si = carry
        off = pl.multiple_of(c * nb, nb)
        xin_ref[pl.ds(off, nb), :] = jnp.concatenate([sr, si], axis=1)
        e = xe_ref[pl.ds(off, nb), :]
        return (ar * sr - ai * si + e[:, :sw], ar * si + ai * sr + e[:, sw:])

    sr, si = lax.fori_loop(0, n_chunks, body, (st_ref[:, :sw], st_ref[:, sw:]))
    st_ref[:, :sw] = sr
    st_ref[:, sw:] = si
    y_ref[0] = _bdot(xin_ref[...], m2_ref[0])
    for t in range(csz):
        y_ref[0, :, t * nl:(t + 1) * nl] += jnp.dot(u[:, :(t + 1) * nl], kr_ref[0, (csz - 1 - t) * nl:, :],
                                                    preferred_element_type=F32)


def _s5_scan(u, tables, bsz, lp):
    m1, krev, m2, adv_r, adv_i = tables
    nj, kin, sw2 = m1.shape
    c = S5_CHUNK
    nl = S5_LANES
    nc = lp // c
    cpt = max(d for d in range(1, nc + 1) if nc % d == 0 and d * bsz <= 512)
    rows = cpt * bsz
    ug = jnp.transpose(u.reshape(bsz, nc, c, nj, nl), (3, 1, 0, 2, 4)).reshape(nj, nc * bsz, kin)
    y = pl.pallas_call(
        functools.partial(_s5_kernel, n_chunks=cpt, nb=bsz),
        grid=(nj, nc // cpt),
        in_specs=[pl.BlockSpec((1, rows, kin), lambda j, r: (j, r, 0)),
                  pl.BlockSpec((1, kin, sw2), lambda j, r: (j, 0, 0)),
                  pl.BlockSpec((1, kin, nl), lambda j, r: (j, 0, 0)),
                  pl.BlockSpec((1, sw2, kin), lambda j, r: (j, 0, 0)),
                  pl.BlockSpec((1, 1, sw2 // 2), lambda j, r: (j, 0, 0)),
                  pl.BlockSpec((1, 1, sw2 // 2), lambda j, r: (j, 0, 0))],
        out_specs=pl.BlockSpec((1, rows, kin), lambda j, r: (j, r, 0)),
        out_shape=jax.ShapeDtypeStruct((nj, nc * bsz, kin), F32),
        scratch_shapes=[pltpu.VMEM((rows, sw2), F32), pltpu.VMEM((rows, sw2), F32), pltpu.VMEM((bsz, sw2), F32)],
        compiler_params=_cparams(("parallel", "arbitrary")),
        name="s5_scan",
    )(ug, m1, krev, m2, adv_r, adv_i)
    y = jnp.transpose(y.reshape(nj, nc, bsz, c, nl), (2, 1, 3, 0, 4))
    return y.reshape(bsz * lp, nj * nl)


def _lru_kernel(rec_ref, gate_ref, cw_ref, cb_ref, wa_ref, ba_ref, wx_ref, bx_ref, lam_ref,
                o_ref, ext_ref, a_ref, b_ref, h_ref, *, tl):
    t = pl.program_id(1)

    @pl.when(t == 0)
    def _():
        ext_ref[0:8, :] = jnp.zeros((8, ext_ref.shape[1]), F32)
        h_ref[...] = jnp.zeros(h_ref.shape, F32)

    x = rec_ref[...]
    ext_ref[8:, :] = x
    xc = cb_ref[...] + cw_ref[3:4, :] * x
    for k in range(3):
        xc = xc + cw_ref[k:k + 1, :] * ext_ref[5 + k:5 + k + tl, :]
    ext_ref[0:8, :] = x[tl - 8:, :]
    r = _sigmoid(_bdot(xc, wa_ref[...]) + ba_ref[...])
    i = _sigmoid(_bdot(xc, wx_ref[...]) + bx_ref[...])
    log_a = (-LRU_C) * r * _softplus(-lam_ref[...])
    a = jnp.exp(log_a)
    a_ref[...] = a
    b_ref[...] = jnp.sqrt(1.0 - a * a) * (i * xc)

    def body(j, h):
        off = pl.multiple_of(j * 8, 8)
        ab = a_ref[pl.ds(off, 8), :]
        bb = b_ref[pl.ds(off, 8), :]
        rows = []
        for q in range(8):
            h = ab[q:q + 1, :] * h + bb[q:q + 1, :]
            rows.append(h)
        o_ref[pl.ds(off, 8), :] = jnp.concatenate(rows, axis=0) * _gelu(gate_ref[pl.ds(off, 8), :])
        return h

    h_ref[...] = lax.fori_loop(0, tl // 8, body, h_ref[...])


def _lru(rec, gate, conv_w, conv_b, w_a, b_a, w_x, b_x, lam, bsz, lp):
    tp, w = rec.shape
    tl = _pick_tile(lp, 1056)
    nt = lp // tl
    heads, hd, _ = w_a.shape

    def dense(wb):
        eye = jnp.eye(heads, dtype=F32)
        return jnp.einsum('hij,hg->higj', wb, eye).reshape(w, w).astype(BF16)

    row = pl.BlockSpec((tl, w), lambda b, t: (b * nt + t, 0))
    vec = _const_spec((1, w))
    return pl.pallas_call(
        functools.partial(_lru_kernel, tl=tl),
        grid=(bsz, nt),
        in_specs=[row, row, _const_spec((4, w)), vec, _const_spec((w, w)), vec, _const_spec((w, w)), vec, vec],
        out_specs=row,
        out_shape=jax.ShapeDtypeStruct((tp, w), F32),
        scratch_shapes=[pltpu.VMEM((tl + 8, w), F32), pltpu.VMEM((tl, w), F32),
                        pltpu.VMEM((tl, w), F32), pltpu.VMEM((1, w), F32)],
        compiler_params=_cparams(("parallel", "arbitrary")),
        name="rglru",
    )(rec, gate, conv_w, conv_b.reshape(1, w), dense(w_a), b_a.reshape(1, w), dense(w_x), b_x.reshape(1, w),
      lam.reshape(1, w))


def _ab_out_kernel(y5_ref, lru_ref, h_ref, wglu_ref, bglu_ref, na_ref, nb_ref, wo_ref, gf_ref, wr_ref, br_ref,
                   o_ref, xn_ref, rt_ref, *, s5w):
    y = _gelu(y5_ref[...])
    ya = y * _sigmoid(_bdot(y, wglu_ref[...]) + bglu_ref[...])
    ya = _rms(ya, na_ref[...])
    yb = _rms(lru_ref[...], nb_ref[...])
    h = h_ref[...] + _bdot(ya, wo_ref[:s5w, :]) + _bdot(yb, wo_ref[s5w:, :])
    o_ref[...] = h
    xn_ref[...], rt_ref[...] = _route_tile(h, gf_ref[...], wr_ref[...], br_ref[...])


def _ab_out(y5, lru, h, w_glu, b_glu, norm_a, norm_b, w_out, g_ffn, router):
    tp, d = h.shape
    s5w, lruw = y5.shape[1], lru.shape[1]
    wr, br = router
    tm = _pick_tile(tp, 512)
    row = lambda n: pl.BlockSpec((tm, n), lambda i: (i, 0))
    return pl.pallas_call(
        functools.partial(_ab_out_kernel, s5w=s5w),
        grid=(tp // tm,),
        in_specs=[row(s5w), row(lruw), row(d), _const_spec((s5w, s5w)), _const_spec((1, s5w)),
                  _const_spec((1, s5w)), _const_spec((1, lruw)), _const_spec(w_out.shape),
                  _const_spec((1, d)), _const_spec(wr.shape), _const_spec(br.shape)],
        out_specs=[row(d), row(d), row(ROUTE_LANES)],
        out_shape=[jax.ShapeDtypeStruct((tp, d), F32), jax.ShapeDtypeStruct((tp, d), F32),
                   jax.ShapeDtypeStruct((tp, ROUTE_LANES), F32)],
        compiler_params=_cparams(("parallel",)),
        name="ab_out",
    )(y5, lru, h, w_glu.astype(BF16), b_glu.reshape(1, s5w), norm_a.reshape(1, s5w),
      norm_b.reshape(1, lruw), w_out.astype(BF16), g_ffn.reshape(1, d), wr, br)


def _route_tile(h, g, wr, br):
    xn = _rms(h, g)
    lg = _hdot(xn, wr) + br
    lane = lax.broadcasted_iota(jnp.int32, lg.shape, 1).astype(F32)
    big = float(lg.shape[1])
    neg = -jnp.inf
    gl = jnp.where(lane < N_GROUPS, lg, neg)
    mg = jnp.max(gl, axis=-1, keepdims=True)
    gidx = jnp.min(jnp.where(gl == mg, lane, big), axis=-1, keepdims=True)
    pg_sel = 1.0 / jnp.sum(jnp.exp(gl - mg), axis=-1, keepdims=True)
    lo = N_GROUPS + EXPERTS_PER_GROUP * gidx
    le = jnp.where(lane >= lo, jnp.where(lane < lo + EXPERTS_PER_GROUP, lg, neg), neg)
    v1 = jnp.max(le, axis=-1, keepdims=True)
    i1 = jnp.min(jnp.where(le == v1, lane, big), axis=-1, keepdims=True)
    le2 = jnp.where(lane == i1, neg, le)
    v2 = jnp.max(le2, axis=-1, keepdims=True)
    i2 = jnp.min(jnp.where(le2 == v2, lane, big), axis=-1, keepdims=True)
    e2 = jnp.exp(v2 - v1)
    w1 = pg_sel / (1.0 + e2)
    w2 = w1 * e2
    rt = jnp.where(lane == 0.0, w1, jnp.where(lane == 1.0, w2, jnp.where(
        lane == 2.0, i1 - N_GROUPS, jnp.where(lane == 3.0, i2 - N_GROUPS, 0.0))))
    return xn, rt


def _router_table(wr_g, br_g, wr_e, br_e):
    d = wr_g.shape[0]
    wr = jnp.zeros((d, ROUTE_LANES), F32).at[:, :N_GROUPS].set(wr_g).at[:, N_GROUPS:N_GROUPS + N_EXPERTS].set(wr_e)
    br = jnp.zeros((1, ROUTE_LANES), F32).at[0, :N_GROUPS].set(br_g).at[0, N_GROUPS:N_GROUPS + N_EXPERTS].set(br_e)
    return wr, br


def _gmm_kernel(te_ref, tv_ref, xa_ref, xb_ref, wg_ref, wu_ref, wd_ref, o_ref, wg_bf, wu_bf, wd_bf, *, n_half):
    i = pl.program_id(0)

    @pl.when(jnp.logical_or(i == 0, te_ref[i] != te_ref[jnp.maximum(i - 1, 0)]))
    def _():
        wg_bf[...] = wg_ref[0, 0].astype(BF16)
        wu_bf[...] = wu_ref[0, 0].astype(BF16)
        wd_bf[...] = wd_ref[0, 0].astype(BF16)

    @pl.when(tv_ref[i] != 0)
    def _():
        x = jnp.where(i < n_half, xa_ref[...], xb_ref[...]).astype(BF16)
        hg = jnp.dot(x, wg_bf[...], preferred_element_type=F32)
        hu = jnp.dot(x, wu_bf[...], preferred_element_type=F32)
        hid = hg * _sigmoid(hg) * hu
        o_ref[...] = _bdot(hid, wd_bf[...])

    @pl.when(tv_ref[i] == 0)
    def _():
        o_ref[...] = jnp.zeros(o_ref.shape, o_ref.dtype)


def _moe(h, xn, rt, w_gate, w_up, w_down, layer):
    tp, d = h.shape
    f = w_gate.shape[-1]
    gate = rt[:, 0:2]
    eid = rt[:, 2:4].astype(jnp.int32)
    tmm = MOE_TILE
    na = 2 * tp
    e_flat = eid.reshape(na)
    onehot = (e_flat[:, None] == jnp.arange(N_EXPERTS, dtype=jnp.int32)[None, :]).astype(jnp.int32)
    csum = jnp.cumsum(onehot, axis=0)
    counts = csum[-1]
    padded = ((counts + tmm - 1) // tmm) * tmm
    ends = jnp.cumsum(padded)
    starts = ends - padded
    dest = jnp.sum(onehot * (csum + (starts - 1)[None, :]), axis=1)
    n_half = (-(-na // tmm) + N_EXPERTS + 1) // 2
    n_tiles = 2 * n_half
    nrows = n_tiles * tmm
    tile_start = jnp.arange(n_tiles, dtype=jnp.int32) * tmm
    tile_e = jnp.sum((ends[None, :] <= tile_start[:, None]).astype(jnp.int32), axis=1)
    tile_v = (tile_e < N_EXPERTS).astype(jnp.int32)
    tile_e = jnp.minimum(tile_e, N_EXPERTS - 1)
    src = (jnp.arange(nrows, dtype=jnp.int32) % tp).at[dest].set(
        jnp.arange(na, dtype=jnp.int32) // 2, mode="promise_in_bounds", unique_indices=True)
    take = lambda a, i: a.at[i].get(mode="promise_in_bounds")
    xs_a = take(xn, src[:n_half * tmm])
    xs_b = take(xn, src[n_half * tmm:])
    ys = pl.pallas_call(
        functools.partial(_gmm_kernel, n_half=n_half),
        grid_spec=pltpu.PrefetchScalarGridSpec(
            num_scalar_prefetch=2,
            grid=(n_tiles,),
            in_specs=[pl.BlockSpec((tmm, d), lambda i, te, tv: (jnp.minimum(i, n_half - 1), 0)),
                      pl.BlockSpec((tmm, d), lambda i, te, tv: (jnp.maximum(i - n_half, 0), 0)),
                      pl.BlockSpec((1, 1, d, f), lambda i, te, tv: (layer, te[i], 0, 0)),
                      pl.BlockSpec((1, 1, d, f), lambda i, te, tv: (layer, te[i], 0, 0)),
                      pl.BlockSpec((1, 1, f, d), lambda i, te, tv: (layer, te[i], 0, 0))],
            out_specs=pl.BlockSpec((tmm, d), lambda i, te, tv: (i, 0)),
            scratch_shapes=[pltpu.VMEM((d, f), BF16), pltpu.VMEM((d, f), BF16), pltpu.VMEM((f, d), BF16)],
        ),
        out_shape=jax.ShapeDtypeStruct((nrows, d), F32),
        compiler_params=_cparams(("arbitrary",)),
        name="moe_gmm",
    )(tile_e, tile_v, xs_a, xs_b, w_gate, w_up, w_down)
    d2 = dest.reshape(tp, 2)
    return h + gate[:, 0:1] * take(ys, d2[:, 0]) + gate[:, 1:2] * take(ys, d2[:, 1])


def _rw_pre_kernel(*refs, tm, lp, has_vres):
    if has_vres:
        (h_ref, hp_ref, g_ref, mu_ref, wr_ref, wk_ref, wv_ref, w0_ref, wl1_ref, wl2_ref,
         a0_ref, al1_ref, al2_ref, gl1_ref, gl2_ref, vf_ref, v0_ref, vl1_ref, vl2_ref,
         r_ref, k_ref, v_ref, w_ref, a_ref, gg_ref) = refs
    else:
        (h_ref, hp_ref, g_ref, mu_ref, wr_ref, wk_ref, wv_ref, w0_ref, wl1_ref, wl2_ref,
         a0_ref, al1_ref, al2_ref, gl1_ref, gl2_ref,
         r_ref, k_ref, v_ref, w_ref, a_ref, gg_ref) = refs
    i = pl.program_id(0)
    g = g_ref[...]
    x = _rms(h_ref[...], g)
    xp8 = _rms(hp_ref[...], g)
    row = lax.broadcasted_iota(jnp.int32, x.shape, 0)
    prev = jnp.where(row == 0, jnp.broadcast_to(xp8[7:8, :], x.shape), pltpu.roll(x, 1, axis=0))
    first = lax.rem(lp - lax.rem(i * tm, lp), lp)
    prev = jnp.where(row == first, 0.0, prev)
    xx = prev - x
    xr, xw, xk, xv, xa, xg = [x + xx * mu_ref[j:j + 1, :] for j in range(6)]
    r_ref[...] = _bdot(xr, wr_ref[...]).astype(r_ref.dtype)
    k_ref[...] = _bdot(xk, wk_ref[...]).astype(k_ref.dtype)
    v = _bdot(xv, wv_ref[...])
    if has_vres:
        mix = _sigmoid(v0_ref[...] + _bdot(_bdot(xv, vl1_ref[...]), vl2_ref[...]))
        v = v + (vf_ref[...] - v) * mix
    v_ref[...] = v.astype(v_ref.dtype)
    w_ref[...] = -_softplus(-(w0_ref[...] + _bdot(jnp.tanh(_bdot(xw, wl1_ref[...])), wl2_ref[...]))) - 0.5
    a_ref[...] = _sigmoid(a0_ref[...] + _bdot(_bdot(xa, al1_ref[...]), al2_ref[...]))
    gg_ref[...] = _bdot(_sigmoid(_bdot(xg, gl1_ref[...])), gl2_ref[...]).astype(gg_ref.dtype)


def _rw_pre(h, g, mu, w_r, w_k, w_v, w0, w_l1, w_l2, a0, a_l1, a_l2, g_l1, g_l2, lp, v_first, v_res):
    tp, d = h.shape
    tm = _pick_tile(tp, min(512, lp))
    has_vres = v_res is not None
    row = pl.BlockSpec((tm, d), lambda i: (i, 0))
    prev8 = pl.BlockSpec((8, d), lambda i: (jnp.maximum(i * (tm // 8) - 1, 0), 0))
    vec = _const_spec((1, d))
    mu8 = jnp.zeros((8, d), F32).at[:6].set(mu)
    bf = lambda w: w.astype(BF16)
    ins = [h, h, g.reshape(1, d), mu8, bf(w_r), bf(w_k), bf(w_v), w0.reshape(1, d), bf(w_l1), bf(w_l2),
           a0.reshape(1, d), bf(a_l1), bf(a_l2), bf(g_l1), bf(g_l2)]
    specs = [row, prev8, vec, _const_spec((8, d))] + [_const_spec((d, d))] * 3 + [
        vec, _const_spec(w_l1.shape), _const_spec(w_l2.shape),
        vec, _const_spec(a_l1.shape), _const_spec(a_l2.shape), _const_spec(g_l1.shape), _const_spec(g_l2.shape)]
    if has_vres:
        v0, v_l1, v_l2 = v_res
        ins += [v_first, v0.reshape(1, d), bf(v_l1), bf(v_l2)]
        specs += [row, vec, _const_spec(v_l1.shape), _const_spec(v_l2.shape)]
    return pl.pallas_call(
        functools.partial(_rw_pre_kernel, tm=tm, lp=lp, has_vres=has_vres),
        grid=(tp // tm,),
        in_specs=specs,
        out_specs=[row] * 6,
        out_shape=[jax.ShapeDtypeStruct((tp, d), dt) for dt in (BF16, BF16, BF16, F32, F32, BF16)],
        compiler_params=_cparams(("parallel",)),
        name="rwkv_pre",
    )(*ins)


def _rw_scan_kernel(r_ref, w_ref, k_ref, v_ref, a_ref, kk_ref, ka_ref, rk_ref, lw_ref, lb_ref,
                    o_ref, g_scr, h_scr, q_scr, y0_scr, *, n_chunks, unroll, pairs):
    c = RW_CHUNK
    nl = 2 * RW_HEAD
    c2 = 2 * c
    ri = lax.broadcasted_iota(jnp.int32, (c2, nl), 0)
    ci = lax.broadcasted_iota(jnp.int32, (c2, nl), 1)
    own = ((ri >= c) == (ci >= RW_HEAD)).astype(F32)
    t_in = jnp.bitwise_and(ri, c - 1)
    s_in = jnp.bitwise_and(ci, c - 1)
    strict = jnp.where(s_in < t_in, own, 0.0)
    incl = jnp.where(s_in <= t_in, own, 0.0)
    causal2 = jnp.concatenate([strict, incl], axis=0)
    eye_l = (ri == ci).astype(F32)
    same_head = own.astype(BF16)
    ti = lax.broadcasted_iota(jnp.int32, (c, c), 0)
    si = lax.broadcasted_iota(jnp.int32, (c, c), 1)
    tril_c = (si <= ti).astype(BF16)
    lanes = lambda p: slice(p * nl, (p + 1) * nl)

    def stack(x):
        return jnp.concatenate([x, x], axis=0) * own

    def offset(ch):
        return ch * c if isinstance(ch, int) else pl.multiple_of(ch * c, c)

    def groups(fn):
        for p in range(pairs):
            def body(i, _):
                fn([i * unroll + q for q in range(unroll)], p)
                return 0
            lax.fori_loop(0, n_chunks // unroll, body, 0)
            if n_chunks % unroll:
                fn(list(range(n_chunks - n_chunks % unroll, n_chunks)), p)

    def prep(chs, p):
        ln = lanes(p)
        kk_w, ka_w = kk_ref[:, ln], ka_ref[:, ln]
        offs = [offset(ch) for ch in chs]
        r = [r_ref[pl.ds(o, c), ln] for o in offs]
        k = [k_ref[pl.ds(o, c), ln] for o in offs]
        v = [v_ref[pl.ds(o, c), ln] for o in offs]
        a = [a_ref[pl.ds(o, c), ln] for o in offs]
        logw = [-jnp.exp(w_ref[pl.ds(o, c), ln]) for o in offs]
        kk = [x * kk_w for x in k]
        ss = _split_dot_many([x * x for x in kk], same_head, pieces=2)
        kk = [x / jnp.maximum(jnp.sqrt(q), 1e-12) for x, q in zip(kk, ss)]
        kmod = [x * (1.0 + (y - 1.0) * ka_w) for x, y in zip(k, a)]
        cum = _split_dot_many(logw, tril_c, rhs=True, pieces=2)
        p_incl = [jnp.exp(x) for x in cum]
        p_inv = [jnp.exp(-x) for x in cum]
        p_end = [jnp.exp(x[c - 1:c, :] - x) for x in cum]
        kka = [x * y for x, y in zip(kk, a)]
        a_s = [stack(-x * jnp.exp(y - z)) for x, y, z in zip(kk, cum, logw)]
        r_s = [stack(x * y) for x, y in zip(r, p_incl)]
        v_s = [stack(x).astype(BF16) for x in v]
        lhs = [jnp.concatenate([x, y], axis=0).astype(BF16) for x, y in zip(a_s, r_s)]
        rhs = [jnp.concatenate([stack(x * z), stack(y * z)], axis=0).astype(BF16) for x, y, z in zip(kka, kmod, p_inv)]
        big = [lax.dot_general(x, y, (((1,), (1,)), ((), ())), preferred_element_type=F32) for x, y in zip(lhs, rhs)]
        lpow = [(x[:c2, :c2] * strict).astype(BF16) for x in big]
        a_rb = [x[c2:, :c2] * incl for x in big]
        avk = [_bdot(x[:, c2:] * causal2, w) for x, w in zip(big, v_s)]
        x = [jnp.concatenate([p, q[:c2]], axis=1) for p, q in zip(a_s, avk)]
        x = [p + _bdot(q, p) for p, q in zip(x, lpow)]
        for _ in range(5):
            lpow = [jnp.dot(q, q, preferred_element_type=F32).astype(BF16) for q in lpow]
            x = [p + _bdot(q, p) for p, q in zip(x, lpow)]
        xb = [p.astype(BF16) for p in x]
        bh_t = [stack(p * q).T for p, q in zip(kka, p_end)]
        kh_t = [stack(p * q).T for p, q in zip(kmod, p_end)]
        both = [_bdot(jnp.concatenate([u, p], axis=0), z) for u, p, z in zip(a_rb, bh_t, xb)]
        qy = [jnp.concatenate([p, q[c2:]], axis=1) + z[:c2] for p, q, z in zip(r_s, avk, both)]
        gh = [z[c2:] + jnp.concatenate([eye_l * q[c - 1:c, :], _bdot(u, w)], axis=1)
              for z, q, u, w in zip(both, p_incl, kh_t, v_s)]
        for ch, u, q in zip(chs, qy, gh):
            q_scr[p, ch] = u[:c, :nl] + u[c:, :nl]
            y0_scr[p, ch] = u[:c, nl:] + u[c:, nl:]
            g_scr[p, ch] = q[:, :nl]
            h_scr[p, ch] = q[:, nl:]

    groups(prep)

    def seq(ch, states):
        both = _dot3_many([jnp.concatenate([q_scr[p, ch], g_scr[p, ch]], axis=0) for p in range(pairs)], states)
        for p in range(pairs):
            o_ref[pl.ds(offset(ch), c), lanes(p)] = both[p][:c] + y0_scr[p, ch]
        return tuple(both[p][c:] + h_scr[p, ch] for p in range(pairs))

    lax.fori_loop(0, n_chunks, seq, tuple(jnp.zeros((nl, nl), F32) for _ in range(pairs)))

    def post(chs, p):
        ln = lanes(p)
        ka_w, rk_w, lw, lb = ka_ref[:, ln], rk_ref[:, ln], lw_ref[:, ln], lb_ref[:, ln]
        offs = [offset(ch) for ch in chs]
        y = [o_ref[pl.ds(o, c), ln] for o in offs]
        kmod = [k_ref[pl.ds(o, c), ln] * (1.0 + (a_ref[pl.ds(o, c), ln] - 1.0) * ka_w) for o in offs]
        rk = [r_ref[pl.ds(o, c), ln] * u * rk_w for o, u in zip(offs, kmod)]
        sums = _split_dot_many([jnp.concatenate([u, u * u, q], axis=0) for u, q in zip(y, rk)], same_head)
        for o, u, s in zip(offs, y, sums):
            mean = s[:c] * (1.0 / RW_HEAD)
            var = s[c:2 * c] * (1.0 / RW_HEAD) - mean * mean
            o_ref[pl.ds(o, c), ln] = ((u - mean) * lax.rsqrt(var + LNX_EPS) * lw + lb
                                      + s[2 * c:] * v_ref[pl.ds(o, c), ln])

    groups(post)


def _rw_scan(r, w, k, v, a, k_k, k_a, r_k, lnx_w, lnx_b, bsz, lp):
    tp, d = r.shape
    nl = 2 * RW_HEAD
    pairs = RW_PAIRS
    bw = pairs * nl
    nhp = d // bw
    nc = lp // RW_CHUNK
    seq = pl.BlockSpec((lp, bw), lambda b, hp: (b, hp))
    vec = pl.BlockSpec((1, bw), lambda b, hp: (0, hp))
    c = RW_CHUNK
    return pl.pallas_call(
        functools.partial(_rw_scan_kernel, n_chunks=nc, unroll=RW_UNROLL, pairs=pairs),
        grid=(bsz, nhp),
        in_specs=[seq] * 5 + [vec] * 5,
        out_specs=seq,
        out_shape=jax.ShapeDtypeStruct((tp, d), F32),
        scratch_shapes=[pltpu.VMEM((pairs, nc, nl, nl), F32), pltpu.VMEM((pairs, nc, nl, nl), F32),
                        pltpu.VMEM((pairs, nc, c, nl), F32), pltpu.VMEM((pairs, nc, c, nl), F32)],
        compiler_params=_cparams(("parallel", "parallel")),
        name="rwkv_scan",
    )(r, w, k, v, a, k_k.reshape(1, d), k_a.reshape(1, d), r_k.reshape(1, d),
      lnx_w.reshape(1, d), lnx_b.reshape(1, d))


def _rw_out_kernel(y_ref, g_ref, h_ref, wo_ref, gf_ref, wr_ref, br_ref, o_ref, xn_ref, rt_ref):
    h = h_ref[...] + _bdot(y_ref[...] * g_ref[...], wo_ref[...])
    o_ref[...] = h
    xn_ref[...], rt_ref[...] = _route_tile(h, gf_ref[...], wr_ref[...], br_ref[...])


def _rw_out(y, g, h, w_o, g_ffn, router):
    tp, d = h.shape
    wr, br = router
    tm = _pick_tile(tp, 512)
    row = lambda n: pl.BlockSpec((tm, n), lambda i: (i, 0))
    return pl.pallas_call(
        _rw_out_kernel,
        grid=(tp // tm,),
        in_specs=[row(d), row(d), row(d), _const_spec((d, d)), _const_spec((1, d)), _const_spec(wr.shape),
                  _const_spec(br.shape)],
        out_specs=[row(d), row(d), row(ROUTE_LANES)],
        out_shape=[jax.ShapeDtypeStruct((tp, d), F32), jax.ShapeDtypeStruct((tp, d), F32),
                   jax.ShapeDtypeStruct((tp, ROUTE_LANES), F32)],
        compiler_params=_cparams(("parallel",)),
        name="rwkv_out",
    )(y, g, h, w_o.astype(BF16), g_ffn.reshape(1, d), wr, br)


def _final_norm_kernel(h_ref, g_ref, o_ref, *, first, rows):
    o_ref[0] = _rms(h_ref[first:first + rows, :], g_ref[...]).astype(o_ref.dtype)


def _final_norm(h, g, dtype, bsz, lp, first, rows):
    d = h.shape[1]
    return pl.pallas_call(
        functools.partial(_final_norm_kernel, first=first, rows=rows),
        grid=(bsz,),
        in_specs=[pl.BlockSpec((lp, d), lambda b: (b, 0)), _const_spec((1, d))],
        out_specs=pl.BlockSpec((1, rows, d), lambda b: (b, 0, 0)),
        out_shape=jax.ShapeDtypeStruct((bsz, rows, d), dtype),
        compiler_params=_cparams(("parallel",)),
        name="final_norm",
    )(h, g.reshape(1, d))


def kernel(x, meta_tokens, norm_mix, norm_ffn, norm_final, ab_w_in, ab_w_out, ab_norm_a, ab_norm_b, s5_lam_re, s5_lam_im, s5_log_dt, s5_b_re, s5_b_im, s5_c_re, s5_c_im, s5_d, s5_w_glu, s5_b_glu, lru_conv_w, lru_conv_b, lru_w_a, lru_b_a, lru_w_x, lru_b_x, lru_lam, rw_mu, rw_w_r, rw_w_k, rw_w_v, rw_w_o, rw_w0, rw_w_l1, rw_w_l2, rw_a0, rw_a_l1, rw_a_l2, rw_v0, rw_v_l1, rw_v_l2, rw_g_l1, rw_g_l2, rw_k_k, rw_k_a, rw_r_k, rw_lnx_w, rw_lnx_b, moe_router_g, moe_router_g_b, moe_router_e, moe_router_e_b, moe_w_gate, moe_w_up, moe_w_down):
    bsz, seq, d = x.shape
    n_meta = meta_tokens.shape[0]
    depth = norm_mix.shape[0]
    ltot = n_meta + seq
    lp = -(-ltot // SEQ_ALIGN) * SEQ_ALIGN
    s5w = s5_w_glu.shape[-1]
    lruw = lru_lam.shape[-1]
    meta = jnp.broadcast_to(meta_tokens.astype(F32)[None], (bsz, n_meta, d))
    h = jnp.concatenate([meta, x.astype(F32), jnp.zeros((bsz, lp - ltot, d), F32)], axis=1).reshape(bsz * lp, d)
    v_first = None
    for layer in range(depth):
        j = layer // 2
        router = _router_table(moe_router_g[layer], moe_router_g_b[layer], moe_router_e[layer], moe_router_e_b[layer])
        if layer % 2 == 0:
            u, gate, rec = _ab_in(h, norm_mix[layer], ab_w_in[j], s5w, lruw)
            tables = _s5_tables(s5_lam_re[j], s5_lam_im[j], s5_log_dt[j], s5_b_re[j], s5_b_im[j],
                                s5_c_re[j], s5_c_im[j], s5_d[j])
            y5 = _s5_scan(u, tables, bsz, lp)
            lru = _lru(rec, gate, lru_conv_w[j], lru_conv_b[j], lru_w_a[j], lru_b_a[j], lru_w_x[j], lru_b_x[j],
                       lru_lam[j], bsz, lp)
            h, xn, rt = _ab_out(y5, lru, h, s5_w_glu[j], s5_b_glu[j], ab_norm_a[j], ab_norm_b[j], ab_w_out[j],
                                norm_ffn[layer], router)
        else:
            v_res = (rw_v0[j - 1], rw_v_l1[j - 1], rw_v_l2[j - 1]) if j > 0 else None
            r, k, v, w, a, g = _rw_pre(h, norm_mix[layer], rw_mu[j], rw_w_r[j], rw_w_k[j], rw_w_v[j], rw_w0[j],
                                       rw_w_l1[j], rw_w_l2[j], rw_a0[j], rw_a_l1[j], rw_a_l2[j], rw_g_l1[j],
                                       rw_g_l2[j], lp, v_first, v_res)
            if v_first is None:
                v_first = v
            y = _rw_scan(r, w, k, v, a, rw_k_k[j], rw_k_a[j], rw_r_k[j].reshape(-1), rw_lnx_w[j], rw_lnx_b[j],
                         bsz, lp)
            h, xn, rt = _rw_out(y, g, h, rw_w_o[j], norm_ffn[layer], router)
        h = _moe(h, xn, rt, moe_w_gate, moe_w_up, moe_w_down, layer)
    return _final_norm(h, norm_final, x.dtype, bsz, lp, n_meta, seq)
```

```python
import functools
import math

import jax
import jax.numpy as jnp
from jax import lax
from jax.experimental import pallas as pl
from jax.experimental.pallas import tpu as pltpu

F32 = jnp.float32
BF16 = jnp.bfloat16
HI = lax.Precision.HIGHEST

RMS_EPS = 1e-6
LNX_EPS = 64e-5
N_META = 16
SEQ_ALIGN = 64
S5_CHUNK = 16
S5_LANES = 128
RW_CHUNK = 64
RW_HEAD = 64
RW_PAIRS = 2
RW_UNROLL = 11
LRU_C = 8.0
N_GROUPS = 4
EXPERTS_PER_GROUP = 4
N_EXPERTS = N_GROUPS * EXPERTS_PER_GROUP
MOE_TILE = 512
ROUTE_LANES = 128
VMEM_LIMIT = 56 * 1024 * 1024


def _cparams(sem):
    return pltpu.CompilerParams(dimension_semantics=sem, vmem_limit_bytes=VMEM_LIMIT)


def _pick_tile(n, target):
    best = 8
    for t in range(8, min(n, target) + 1, 8):
        if n % t == 0:
            best = t
    return best


def _const_spec(shape):
    nd = len(shape)
    return pl.BlockSpec(shape, lambda *_: (0,) * nd)


def _rms(x, g):
    return x * lax.rsqrt(jnp.mean(x * x, axis=-1, keepdims=True) + RMS_EPS) * g


def _gelu(x):
    return 0.5 * x * (1.0 + jnp.tanh(math.sqrt(2.0 / math.pi) * (x + 0.044715 * (x * x * x))))


def _sigmoid(x):
    return 1.0 / (1.0 + jnp.exp(-x))


def _softplus(x):
    return jnp.maximum(x, 0.0) + jnp.log(1.0 + jnp.exp(-jnp.abs(x)))


def _bdot(a, b):
    return jnp.dot(a.astype(BF16), b.astype(BF16), preferred_element_type=F32)


def _hdot(a, b):
    return jnp.dot(a, b, preferred_element_type=F32, precision=HI)


def _dot3(a, b):
    m = a.shape[0]
    a_hi = a.astype(BF16)
    b_hi = b.astype(BF16)
    a_lo = (a - a_hi.astype(F32)).astype(BF16)
    b_lo = (b - b_hi.astype(F32)).astype(BF16)
    dot = functools.partial(jnp.dot, preferred_element_type=F32)
    top = dot(jnp.concatenate([a_hi, a_lo], axis=0), b_hi)
    return top[:m] + (top[m:] + dot(a_hi, b_lo))


def _dot3_many(a_list, b_list):
    m = a_list[0].shape[0]
    dot = functools.partial(jnp.dot, preferred_element_type=F32)
    a_hi = [a.astype(BF16) for a in a_list]
    b_hi = [b.astype(BF16) for b in b_list]
    a_lo = [(a - h.astype(F32)).astype(BF16) for a, h in zip(a_list, a_hi)]
    b_lo = [(b - h.astype(F32)).astype(BF16) for b, h in zip(b_list, b_hi)]
    top = [dot(jnp.concatenate([h, l], axis=0), b) for h, l, b in zip(a_hi, a_lo, b_hi)]
    low = [dot(h, b) for h, b in zip(a_hi, b_lo)]
    return [t[:m] + (t[m:] + l) for t, l in zip(top, low)]


def _split_dot_many(xs, m, rhs=False, pieces=3):
    accs = [None] * len(xs)
    xs = list(xs)
    for _ in range(pieces):
        his = [x.astype(BF16) for x in xs]
        parts = [jnp.dot(m, hi, preferred_element_type=F32) if rhs else jnp.dot(hi, m, preferred_element_type=F32)
                 for hi in his]
        accs = [p if a is None else a + p for a, p in zip(accs, parts)]
        xs = [x - hi.astype(F32) for x, hi in zip(xs, his)]
    return accs


def _ab_in_kernel(h_ref, g_ref, w_ref, u_ref, gate_ref, rec_ref, *, s5w, lruw):
    xn = _rms(h_ref[...], g_ref[...])
    z = _bdot(xn, w_ref[...])
    u_ref[...] = z[:, :s5w].astype(u_ref.dtype)
    gate_ref[...] = z[:, s5w:s5w + lruw]
    rec_ref[...] = z[:, s5w + lruw:]


def _ab_in(h, g, w_in, s5w, lruw):
    tp, d = h.shape
    tm = _pick_tile(tp, 512)
    row = lambda n: pl.BlockSpec((tm, n), lambda i: (i, 0))
    return pl.pallas_call(
        functools.partial(_ab_in_kernel, s5w=s5w, lruw=lruw),
        grid=(tp // tm,),
        in_specs=[row(d), _const_spec((1, d)), _const_spec(w_in.shape)],
        out_specs=[row(s5w), row(lruw), row(lruw)],
        out_shape=[jax.ShapeDtypeStruct((tp, s5w), BF16),
                   jax.ShapeDtypeStruct((tp, lruw), F32),
                   jax.ShapeDtypeStruct((tp, lruw), F32)],
        compiler_params=_cparams(("parallel",)),
        name="ab_in",
    )(h, g.reshape(1, d), w_in.astype(BF16))


def _s5_tables(lam_re, lam_im, log_dt, b_re, b_im, c_re, c_im, d_skip):
    g, p = lam_re.shape
    hh = b_re.shape[-1]
    c = S5_CHUNK
    lr, li = lam_re.astype(F32), lam_im.astype(F32)
    dt = jnp.exp(log_dt.astype(F32))[:, None]
    mag = jnp.exp(lr * dt)
    abar_r = mag * jnp.cos(li * dt)
    abar_i = mag * jnp.sin(li * dt)
    den = lr * lr + li * li
    zr = ((abar_r - 1.0) * lr + abar_i * li) / den
    zi = (abar_i * lr - (abar_r - 1.0) * li) / den
    bbar_r = zr[..., None] * b_re - zi[..., None] * b_im
    bbar_i = zr[..., None] * b_im + zi[..., None] * b_re
    bbr_t, bbi_t = jnp.swapaxes(bbar_r, 1, 2), jnp.swapaxes(bbar_i, 1, 2)
    cr_t, ci_t = jnp.swapaxes(c_re, 1, 2), jnp.swapaxes(c_im, 1, 2)

    def powers(steps):
        st = steps.astype(F32)[None, :, None]
        pmag = jnp.exp(st * (lr * dt)[:, None, :])
        ang = st * (li * dt)[:, None, :]
        return pmag * jnp.cos(ang), pmag * jnp.sin(ang)

    down = (c - 1) - jnp.arange(c)
    rev_r, rev_i = powers(down)
    m1_r = rev_r[:, :, None, :] * bbr_t[:, None] - rev_i[:, :, None, :] * bbi_t[:, None]
    m1_i = rev_r[:, :, None, :] * bbi_t[:, None] + rev_i[:, :, None, :] * bbr_t[:, None]
    car = c_re[:, None] * rev_r[:, :, None, :] - c_im[:, None] * rev_i[:, :, None, :]
    cai = c_re[:, None] * rev_i[:, :, None, :] + c_im[:, None] * rev_r[:, :, None, :]
    kern = (jnp.einsum('gqhp,gpj->gqjh', car, bbar_r, precision=HI)
            - jnp.einsum('gqhp,gpj->gqjh', cai, bbar_i, precision=HI))
    is_tau0 = (down == 0).astype(F32)[None, :, None, None]
    kern = kern + is_tau0 * (d_skip[:, None, None, :] * jnp.eye(hh, dtype=F32)[None, None])
    up_r, up_i = powers(jnp.arange(1, c + 1))
    up_r, up_i = jnp.swapaxes(up_r, 1, 2)[..., None], jnp.swapaxes(up_i, 1, 2)[..., None]
    m2_r = cr_t[:, :, None, :] * up_r - ci_t[:, :, None, :] * up_i
    m2_i = -(cr_t[:, :, None, :] * up_i + ci_t[:, :, None, :] * up_r)
    adv_r, adv_i = powers(jnp.full((1,), c))
    gb = S5_LANES // hh
    nj = g // gb

    def rows_sgh(x):
        w = x.shape[-1]
        return jnp.transpose(x.reshape(nj, gb, c, hh, w), (0, 2, 1, 3, 4)).reshape(nj, c * gb * hh, w)

    def place(base, spread, row_group, col_group):
        out = jnp.einsum('jrw,wc->jrc', base.astype(BF16), spread.astype(BF16), preferred_element_type=F32)
        rg = row_group(lax.broadcasted_iota(jnp.int32, out.shape[1:], 0))
        cg = col_group(lax.broadcasted_iota(jnp.int32, out.shape[1:], 1))
        return jnp.where(rg == cg, out, 0.0)

    grp_sgh = lambda r: (r // hh) % gb
    rep = lambda w: jnp.tile(jnp.eye(w, dtype=F32), (1, gb))
    m1 = jnp.concatenate([place(rows_sgh(m), rep(p), grp_sgh, lambda col: col // p) for m in (m1_r, m1_i)], axis=-1)
    krev = place(rows_sgh(kern), rep(hh), grp_sgh, lambda col: col // hh)
    ri = lax.broadcasted_iota(jnp.int32, (c * hh, c * gb * hh), 0)
    ci = lax.broadcasted_iota(jnp.int32, (c * hh, c * gb * hh), 1)
    spread_th = ((ri // hh == ci // (gb * hh)) & (ri % hh == ci % hh)).astype(F32)
    m2 = jnp.concatenate([place(m.reshape(nj, gb * p, c * hh), spread_th, lambda r: r // p,
                                lambda col: (col // hh) % gb) for m in (m2_r, m2_i)], axis=1)
    return (m1.astype(BF16), krev.astype(BF16), m2.astype(BF16),
            adv_r.reshape(nj, 1, gb * p), adv_i.reshape(nj, 1, gb * p))


def _s5_kernel(u_ref, m1_ref, kr_ref, m2_ref, ar_ref, ai_ref, y_ref, xe_ref, xin_ref, st_ref, *, n_chunks, nb):
    nl = S5_LANES
    csz = S5_CHUNK
    sw = ar_ref.shape[-1]

    @pl.when(pl.program_id(1) == 0)
    def _():
        st_ref[...] = jnp.zeros(st_ref.shape, F32)

    u = u_ref[0]
    xe_ref[...] = jnp.dot(u, m1_ref[0], preferred_element_type=F32)
    ar = jnp.broadcast_to(ar_ref[0], (nb, sw))
    ai = jnp.broadcast_to(ai_ref[0], (nb, sw))

    def body(c, carry):
        sr, si = carry
        off = pl.multiple_of(c * nb, nb)
        xin_ref[pl.ds(off, nb), :] = jnp.concatenate([sr, si], axis=1)
        e = xe_ref[pl.ds(off, nb), :]
        return (ar * sr - ai * si + e[:, :sw], ar * si + ai * sr + e[:, sw:])

    sr, si = lax.fori_loop(0, n_chunks, body, (st_ref[:, :sw], st_ref[:, sw:]))
    st_ref[:, :sw] = sr
    st_ref[:, sw:] = si
    y_ref[0] = _bdot(xin_ref[...], m2_ref[0])
    for t in range(csz):
        y_ref[0, :, t * nl:(t + 1) * nl] += jnp.dot(u[:, :(t + 1) * nl], kr_ref[0, (csz - 1 - t) * nl:, :],
                                                    preferred_element_type=F32)


def _s5_scan(u, tables, bsz, lp):
    m1, krev, m2, adv_r, adv_i = tables
    nj, kin, sw2 = m1.shape
    c = S5_CHUNK
    nl = S5_LANES
    nc = lp // c
    cpt = max(d for d in range(1, nc + 1) if nc % d == 0 and d * bsz <= 512)
    rows = cpt * bsz
    ug = jnp.transpose(u.reshape(bsz, nc, c, nj, nl), (3, 1, 0, 2, 4)).reshape(nj, nc * bsz, kin)
    y = pl.pallas_call(
        functools.partial(_s5_kernel, n_chunks=cpt, nb=bsz),
        grid=(nj, nc // cpt),
        in_specs=[pl.BlockSpec((1, rows, kin), lambda j, r: (j, r, 0)),
                  pl.BlockSpec((1, kin, sw2), lambda j, r: (j, 0, 0)),
                  pl.BlockSpec((1, kin, nl), lambda j, r: (j, 0, 0)),
                  pl.BlockSpec((1, sw2, kin), lambda j, r: (j, 0, 0)),
                  pl.BlockSpec((1, 1, sw2 // 2), lambda j, r: (j, 0, 0)),
                  pl.BlockSpec((1, 1, sw2 // 2), lambda j, r: (j, 0, 0))],
        out_specs=pl.BlockSpec((1, rows, kin), lambda j, r: (j, r, 0)),
        out_shape=jax.ShapeDtypeStruct((nj, nc * bsz, kin), F32),
        scratch_shapes=[pltpu.VMEM((rows, sw2), F32), pltpu.VMEM((rows, sw2), F32), pltpu.VMEM((bsz, sw2), F32)],
        compiler_params=_cparams(("parallel", "arbitrary")),
        name="s5_scan",
    )(ug, m1, krev, m2, adv_r, adv_i)
    y = jnp.transpose(y.reshape(nj, nc, bsz, c, nl), (2, 1, 3, 0, 4))
    return y.reshape(bsz * lp, nj * nl)


def _lru_kernel(rec_ref, gate_ref, cw_ref, cb_ref, wa_ref, ba_ref, wx_ref, bx_ref, lam_ref,
                o_ref, ext_ref, a_ref, b_ref, h_ref, *, tl):
    t = pl.program_id(1)

    @pl.when(t == 0)
    def _():
        ext_ref[0:8, :] = jnp.zeros((8, ext_ref.shape[1]), F32)
        h_ref[...] = jnp.zeros(h_ref.shape, F32)

    x = rec_ref[...]
    ext_ref[8:, :] = x
    xc = cb_ref[...] + cw_ref[3:4, :] * x
    for k in range(3):
        xc = xc + cw_ref[k:k + 1, :] * ext_ref[5 + k:5 + k + tl, :]
    ext_ref[0:8, :] = x[tl - 8:, :]
    r = _sigmoid(_bdot(xc, wa_ref[...]) + ba_ref[...])
    i = _sigmoid(_bdot(xc, wx_ref[...]) + bx_ref[...])
    log_a = (-LRU_C) * r * _softplus(-lam_ref[...])
    a = jnp.exp(log_a)
    a_ref[...] = a
    b_ref[...] = jnp.sqrt(1.0 - a * a) * (i * xc)

    sub = 8
    unroll = max(u for u in (4, 2, 1) if (tl // sub) % u == 0)
    row = lax.broadcasted_iota(jnp.int32, (sub, a.shape[1]), 0)

    def body(j, h):
        offs = [pl.multiple_of((j * unroll + q) * sub, sub) for q in range(unroll)]
        ab = [a_ref[pl.ds(o, sub), :] for o in offs]
        bb = [b_ref[pl.ds(o, sub), :] for o in offs]
        for dist in (1, 2, 4):
            keep = row >= dist
            bb = [jnp.where(keep, x * pltpu.roll(y, dist, axis=0) + y, y) for x, y in zip(ab, bb)]
            ab = [jnp.where(keep, x * pltpu.roll(x, dist, axis=0), x) for x in ab]
        for o, x, y in zip(offs, ab, bb):
            hb = x * h + y
            o_ref[pl.ds(o, sub), :] = hb * _gelu(gate_ref[pl.ds(o, sub), :])
            h = hb[sub - 1:sub, :]
        return h

    h_ref[...] = lax.fori_loop(0, tl // (sub * unroll), body, h_ref[...])


def _lru(rec, gate, conv_w, conv_b, w_a, b_a, w_x, b_x, lam, bsz, lp):
    tp, w = rec.shape
    tl = _pick_tile(lp, 1056)
    nt = lp // tl
    heads, hd, _ = w_a.shape

    def dense(wb):
        eye = jnp.eye(heads, dtype=F32)
        return jnp.einsum('hij,hg->higj', wb, eye).reshape(w, w).astype(BF16)

    row = pl.BlockSpec((tl, w), lambda b, t: (b * nt + t, 0))
    vec = _const_spec((1, w))
    return pl.pallas_call(
        functools.partial(_lru_kernel, tl=tl),
        grid=(bsz, nt),
        in_specs=[row, row, _const_spec((4, w)), vec, _const_spec((w, w)), vec, _const_spec((w, w)), vec, vec],
        out_specs=row,
        out_shape=jax.ShapeDtypeStruct((tp, w), F32),
        scratch_shapes=[pltpu.VMEM((tl + 8, w), F32), pltpu.VMEM((tl, w), F32),
                        pltpu.VMEM((tl, w), F32), pltpu.VMEM((1, w), F32)],
        compiler_params=_cparams(("parallel", "arbitrary")),
        name="rglru",
    )(rec, gate, conv_w, conv_b.reshape(1, w), dense(w_a), b_a.reshape(1, w), dense(w_x), b_x.reshape(1, w),
      lam.reshape(1, w))


def _ab_out_kernel(y5_ref, lru_ref, h_ref, wglu_ref, bglu_ref, na_ref, nb_ref, wo_ref, gf_ref, wr_ref, br_ref,
                   o_ref, xn_ref, rt_ref, *, s5w):
    y = _gelu(y5_ref[...])
    ya = y * _sigmoid(_bdot(y, wglu_ref[...]) + bglu_ref[...])
    ya = _rms(ya, na_ref[...])
    yb = _rms(lru_ref[...], nb_ref[...])
    h = h_ref[...] + _bdot(ya, wo_ref[:s5w, :]) + _bdot(yb, wo_ref[s5w:, :])
    o_ref[...] = h
    xn_ref[...], rt_ref[...] = _route_tile(h, gf_ref[...], wr_ref[0], wr_ref[1], br_ref[...])


def _ab_out(y5, lru, h, w_glu, b_glu, norm_a, norm_b, w_out, g_ffn, router):
    tp, d = h.shape
    s5w, lruw = y5.shape[1], lru.shape[1]
    wr, br = router
    tm = _pick_tile(tp, 512)
    row = lambda n: pl.BlockSpec((tm, n), lambda i: (i, 0))
    return pl.pallas_call(
        functools.partial(_ab_out_kernel, s5w=s5w),
        grid=(tp // tm,),
        in_specs=[row(s5w), row(lruw), row(d), _const_spec((s5w, s5w)), _const_spec((1, s5w)),
                  _const_spec((1, s5w)), _const_spec((1, lruw)), _const_spec(w_out.shape),
                  _const_spec((1, d)), _const_spec(wr.shape), _const_spec(br.shape)],
        out_specs=[row(d), row(d), row(ROUTE_LANES)],
        out_shape=[jax.ShapeDtypeStruct((tp, d), F32), jax.ShapeDtypeStruct((tp, d), F32),
                   jax.ShapeDtypeStruct((tp, ROUTE_LANES), F32)],
        compiler_params=_cparams(("parallel",)),
        name="ab_out",
    )(y5, lru, h, w_glu.astype(BF16), b_glu.reshape(1, s5w), norm_a.reshape(1, s5w),
      norm_b.reshape(1, lruw), w_out.astype(BF16), g_ffn.reshape(1, d), wr, br)


def _route_tile(h, g, wr_hi, wr_lo, br):
    xn = _rms(h, g)
    m = xn.shape[0]
    x_hi = xn.astype(BF16)
    x_lo = (xn - x_hi.astype(F32)).astype(BF16)
    top = jnp.dot(jnp.concatenate([x_hi, x_lo], axis=0), wr_hi, preferred_element_type=F32)
    lg = top[:m] + (top[m:] + jnp.dot(x_hi, wr_lo, preferred_element_type=F32)) + br
    lane = lax.broadcasted_iota(jnp.int32, lg.shape, 1).astype(F32)
    big = float(lg.shape[1])
    neg = -jnp.inf
    gl = jnp.where(lane < N_GROUPS, lg, neg)
    mg = jnp.max(gl, axis=-1, keepdims=True)
    gidx = jnp.min(jnp.where(gl == mg, lane, big), axis=-1, keepdims=True)
    pg_sel = 1.0 / jnp.sum(jnp.exp(gl - mg), axis=-1, keepdims=True)
    lo = N_GROUPS + EXPERTS_PER_GROUP * gidx
    le = jnp.where(lane >= lo, jnp.where(lane < lo + EXPERTS_PER_GROUP, lg, neg), neg)
    v1 = jnp.max(le, axis=-1, keepdims=True)
    i1 = jnp.min(jnp.where(le == v1, lane, big), axis=-1, keepdims=True)
    le2 = jnp.where(lane == i1, neg, le)
    v2 = jnp.max(le2, axis=-1, keepdims=True)
    i2 = jnp.min(jnp.where(le2 == v2, lane, big), axis=-1, keepdims=True)
    e2 = jnp.exp(v2 - v1)
    w1 = pg_sel / (1.0 + e2)
    w2 = w1 * e2
    rt = jnp.where(lane == 0.0, w1, jnp.where(lane == 1.0, w2, jnp.where(
        lane == 2.0, i1 - N_GROUPS, jnp.where(lane == 3.0, i2 - N_GROUPS, 0.0))))
    return xn, rt


def _router_table(wr_g, br_g, wr_e, br_e):
    d = wr_g.shape[0]
    wr = jnp.zeros((d, ROUTE_LANES), F32).at[:, :N_GROUPS].set(wr_g).at[:, N_GROUPS:N_GROUPS + N_EXPERTS].set(wr_e)
    br = jnp.zeros((1, ROUTE_LANES), F32).at[0, :N_GROUPS].set(br_g).at[0, N_GROUPS:N_GROUPS + N_EXPERTS].set(br_e)
    wr_hi = wr.astype(BF16)
    wr_lo = (wr - wr_hi.astype(F32)).astype(BF16)
    return jnp.stack([wr_hi, wr_lo]), br


def _gmm_kernel(te_ref, tv_ref, xa_ref, xb_ref, wg_ref, wu_ref, wd_ref, o_ref, wg_bf, wu_bf, wd_bf, *, n_half):
    i = pl.program_id(0)

    @pl.when(jnp.logical_or(i == 0, te_ref[i] != te_ref[jnp.maximum(i - 1, 0)]))
    def _():
        wg_bf[...] = wg_ref[0, 0].astype(BF16)
        wu_bf[...] = wu_ref[0, 0].astype(BF16)
        wd_bf[...] = wd_ref[0, 0].astype(BF16)

    @pl.when(tv_ref[i] != 0)
    def _():
        x = jnp.where(i < n_half, xa_ref[...], xb_ref[...]).astype(BF16)
        hg = jnp.dot(x, wg_bf[...], preferred_element_type=F32)
        hu = jnp.dot(x, wu_bf[...], preferred_element_type=F32)
        hid = hg * _sigmoid(hg) * hu
        o_ref[...] = _bdot(hid, wd_bf[...])

    @pl.when(tv_ref[i] == 0)
    def _():
        o_ref[...] = jnp.zeros(o_ref.shape, o_ref.dtype)


def _moe(h, xn, rt, w_gate, w_up, w_down, layer):
    tp, d = h.shape
    f = w_gate.shape[-1]
    gate = rt[:, 0:2]
    eid = rt[:, 2:4].astype(jnp.int32)
    tmm = MOE_TILE
    na = 2 * tp
    e_flat = eid.reshape(na)
    onehot = (e_flat[:, None] == jnp.arange(N_EXPERTS, dtype=jnp.int32)[None, :]).astype(jnp.int32)
    csum = jnp.cumsum(onehot, axis=0)
    counts = csum[-1]
    padded = ((counts + tmm - 1) // tmm) * tmm
    ends = jnp.cumsum(padded)
    starts = ends - padded
    dest = jnp.sum(onehot * (csum + (starts - 1)[None, :]), axis=1)
    n_half = (-(-na // tmm) + N_EXPERTS + 1) // 2
    n_tiles = 2 * n_half
    nrows = n_tiles * tmm
    tile_start = jnp.arange(n_tiles, dtype=jnp.int32) * tmm
    tile_e = jnp.sum((ends[None, :] <= tile_start[:, None]).astype(jnp.int32), axis=1)
    tile_v = (tile_e < N_EXPERTS).astype(jnp.int32)
    tile_e = jnp.minimum(tile_e, N_EXPERTS - 1)
    src = (jnp.arange(nrows, dtype=jnp.int32) % tp).at[dest].set(
        jnp.arange(na, dtype=jnp.int32) // 2, mode="promise_in_bounds", unique_indices=True)
    take = lambda a, i: a.at[i].get(mode="promise_in_bounds")
    xs_a = take(xn, src[:n_half * tmm])
    xs_b = take(xn, src[n_half * tmm:])
    ys = pl.pallas_call(
        functools.partial(_gmm_kernel, n_half=n_half),
        grid_spec=pltpu.PrefetchScalarGridSpec(
            num_scalar_prefetch=2,
            grid=(n_tiles,),
            in_specs=[pl.BlockSpec((tmm, d), lambda i, te, tv: (jnp.minimum(i, n_half - 1), 0)),
                      pl.BlockSpec((tmm, d), lambda i, te, tv: (jnp.maximum(i - n_half, 0), 0)),
                      pl.BlockSpec((1, 1, d, f), lambda i, te, tv: (layer, te[i], 0, 0)),
                      pl.BlockSpec((1, 1, d, f), lambda i, te, tv: (layer, te[i], 0, 0)),
                      pl.BlockSpec((1, 1, f, d), lambda i, te, tv: (layer, te[i], 0, 0))],
            out_specs=pl.BlockSpec((tmm, d), lambda i, te, tv: (i, 0)),
            scratch_shapes=[pltpu.VMEM((d, f), BF16), pltpu.VMEM((d, f), BF16), pltpu.VMEM((f, d), BF16)],
        ),
        out_shape=jax.ShapeDtypeStruct((nrows, d), F32),
        compiler_params=_cparams(("arbitrary",)),
        name="moe_gmm",
    )(tile_e, tile_v, xs_a, xs_b, w_gate, w_up, w_down)
    d2 = dest.reshape(tp, 2)
    return h + gate[:, 0:1] * take(ys, d2[:, 0]) + gate[:, 1:2] * take(ys, d2[:, 1])


def _rw_pre_kernel(*refs, tm, lp, has_vres):
    if has_vres:
        (h_ref, hp_ref, g_ref, mu_ref, wr_ref, wk_ref, wv_ref, w0_ref, wl1_ref, wl2_ref,
         a0_ref, al1_ref, al2_ref, gl1_ref, gl2_ref, vf_ref, v0_ref, vl1_ref, vl2_ref,
         r_ref, k_ref, v_ref, w_ref, a_ref, gg_ref) = refs
    else:
        (h_ref, hp_ref, g_ref, mu_ref, wr_ref, wk_ref, wv_ref, w0_ref, wl1_ref, wl2_ref,
         a0_ref, al1_ref, al2_ref, gl1_ref, gl2_ref,
         r_ref, k_ref, v_ref, w_ref, a_ref, gg_ref) = refs
    i = pl.program_id(0)
    g = g_ref[...]
    x = _rms(h_ref[...], g)
    xp8 = _rms(hp_ref[...], g)
    row = lax.broadcasted_iota(jnp.int32, x.shape, 0)
    prev = jnp.where(row == 0, jnp.broadcast_to(xp8[7:8, :], x.shape), pltpu.roll(x, 1, axis=0))
    first = lax.rem(lp - lax.rem(i * tm, lp), lp)
    prev = jnp.where(row == first, 0.0, prev)
    xx = prev - x
    xr, xw, xk, xv, xa, xg = [x + xx * mu_ref[j:j + 1, :] for j in range(6)]
    r_ref[...] = _bdot(xr, wr_ref[...]).astype(r_ref.dtype)
    k_ref[...] = _bdot(xk, wk_ref[...]).astype(k_ref.dtype)
    v = _bdot(xv, wv_ref[...])
    if has_vres:
        mix = _sigmoid(v0_ref[...] + _bdot(_bdot(xv, vl1_ref[...]), vl2_ref[...]))
        v = v + (vf_ref[...] - v) * mix
    v_ref[...] = v.astype(v_ref.dtype)
    w_ref[...] = -_softplus(-(w0_ref[...] + _bdot(jnp.tanh(_bdot(xw, wl1_ref[...])), wl2_ref[...]))) - 0.5
    a_ref[...] = _sigmoid(a0_ref[...] + _bdot(_bdot(xa, al1_ref[...]), al2_ref[...]))
    gg_ref[...] = _bdot(_sigmoid(_bdot(xg, gl1_ref[...])), gl2_ref[...]).astype(gg_ref.dtype)


def _rw_pre(h, g, mu, w_r, w_k, w_v, w0, w_l1, w_l2, a0, a_l1, a_l2, g_l1, g_l2, lp, v_first, v_res):
    tp, d = h.shape
    tm = _pick_tile(tp, min(512, lp))
    has_vres = v_res is not None
    row = pl.BlockSpec((tm, d), lambda i: (i, 0))
    prev8 = pl.BlockSpec((8, d), lambda i: (jnp.maximum(i * (tm // 8) - 1, 0), 0))
    vec = _const_spec((1, d))
    mu8 = jnp.zeros((8, d), F32).at[:6].set(mu)
    bf = lambda w: w.astype(BF16)
    ins = [h, h, g.reshape(1, d), mu8, bf(w_r), bf(w_k), bf(w_v), w0.reshape(1, d), bf(w_l1), bf(w_l2),
           a0.reshape(1, d), bf(a_l1), bf(a_l2), bf(g_l1), bf(g_l2)]
    specs = [row, prev8, vec, _const_spec((8, d))] + [_const_spec((d, d))] * 3 + [
        vec, _const_spec(w_l1.shape), _const_spec(w_l2.shape),
        vec, _const_spec(a_l1.shape), _const_spec(a_l2.shape), _const_spec(g_l1.shape), _const_spec(g_l2.shape)]
    if has_vres:
        v0, v_l1, v_l2 = v_res
        ins += [v_first, v0.reshape(1, d), bf(v_l1), bf(v_l2)]
        specs += [row, vec, _const_spec(v_l1.shape), _const_spec(v_l2.shape)]
    return pl.pallas_call(
        functools.partial(_rw_pre_kernel, tm=tm, lp=lp, has_vres=has_vres),
        grid=(tp // tm,),
        in_specs=specs,
        out_specs=[row] * 6,
        out_shape=[jax.ShapeDtypeStruct((tp, d), dt) for dt in (BF16, BF16, BF16, F32, F32, BF16)],
        compiler_params=_cparams(("parallel",)),
        name="rwkv_pre",
    )(*ins)


def _rw_scan_kernel(r_ref, w_ref, k_ref, v_ref, a_ref, kk_ref, ka_ref, rk_ref, lw_ref, lb_ref,
                    o_ref, g_scr, h_scr, q_scr, y0_scr, *, n_chunks, unroll, pairs):
    c = RW_CHUNK
    nl = 2 * RW_HEAD
    c2 = 2 * c
    ri = lax.broadcasted_iota(jnp.int32, (c2, nl), 0)
    ci = lax.broadcasted_iota(jnp.int32, (c2, nl), 1)
    own = ((ri >= c) == (ci >= RW_HEAD)).astype(F32)
    t_in = jnp.bitwise_and(ri, c - 1)
    s_in = jnp.bitwise_and(ci, c - 1)
    strict = jnp.where(s_in < t_in, own, 0.0)
    incl = jnp.where(s_in <= t_in, own, 0.0)
    causal2 = jnp.concatenate([strict, incl], axis=0)
    eye_l = (ri == ci).astype(F32)
    same_head = own.astype(BF16)
    ti = lax.broadcasted_iota(jnp.int32, (c, c), 0)
    si = lax.broadcasted_iota(jnp.int32, (c, c), 1)
    tril_c = (si <= ti).astype(BF16)
    lanes = lambda p: slice(p * nl, (p + 1) * nl)

    def stack(x):
        return jnp.concatenate([x, x], axis=0) * own

    def offset(ch):
        return ch * c if isinstance(ch, int) else pl.multiple_of(ch * c, c)

    def groups(fn):
        for p in range(pairs):
            def body(i, _):
                fn([i * unroll + q for q in range(unroll)], p)
                return 0
            lax.fori_loop(0, n_chunks // unroll, body, 0)
            if n_chunks % unroll:
                fn(list(range(n_chunks - n_chunks % unroll, n_chunks)), p)

    def prep(chs, p):
        ln = lanes(p)
        kk_w, ka_w = kk_ref[:, ln], ka_ref[:, ln]
        offs = [offset(ch) for ch in chs]
        r = [r_ref[pl.ds(o, c), ln] for o in offs]
        k = [k_ref[pl.ds(o, c), ln] for o in offs]
        v = [v_ref[pl.ds(o, c), ln] for o in offs]
        a = [a_ref[pl.ds(o, c), ln] for o in offs]
        logw = [-jnp.exp(w_ref[pl.ds(o, c), ln]) for o in offs]
        kk = [x * kk_w for x in k]
        ss = _split_dot_many([x * x for x in kk], same_head, pieces=2)
        kk = [x / jnp.maximum(jnp.sqrt(q), 1e-12) for x, q in zip(kk, ss)]
        kmod = [x * (1.0 + (y - 1.0) * ka_w) for x, y in zip(k, a)]
        cum = _split_dot_many(logw, tril_c, rhs=True, pieces=2)
        p_incl = [jnp.exp(x) for x in cum]
        p_inv = [jnp.exp(-x) for x in cum]
        p_end = [jnp.exp(x[c - 1:c, :] - x) for x in cum]
        kka = [x * y for x, y in zip(kk, a)]
        a_s = [stack(-x * jnp.exp(y - z)) for x, y, z in zip(kk, cum, logw)]
        r_s = [stack(x * y) for x, y in zip(r, p_incl)]
        v_s = [stack(x).astype(BF16) for x in v]
        lhs = [jnp.concatenate([x, y], axis=0).astype(BF16) for x, y in zip(a_s, r_s)]
        rhs = [jnp.concatenate([stack(x * z), stack(y * z)], axis=0).astype(BF16) for x, y, z in zip(kka, kmod, p_inv)]
        big = [lax.dot_general(x, y, (((1,), (1,)), ((), ())), preferred_element_type=F32) for x, y in zip(lhs, rhs)]
        lpow = [(x[:c2, :c2] * strict).astype(BF16) for x in big]
        a_rb = [x[c2:, :c2] * incl for x in big]
        avk = [_bdot(x[:, c2:] * causal2, w) for x, w in zip(big, v_s)]
        x = [jnp.concatenate([p, q[:c2]], axis=1) for p, q in zip(a_s, avk)]
        x = [p + _bdot(q, p) for p, q in zip(x, lpow)]
        for _ in range(5):
            lpow = [jnp.dot(q, q, preferred_element_type=F32).astype(BF16) for q in lpow]
            x = [p + _bdot(q, p) for p, q in zip(x, lpow)]
        xb = [p.astype(BF16) for p in x]
        bh_t = [stack(p * q).T for p, q in zip(kka, p_end)]
        kh_t = [stack(p * q).T for p, q in zip(kmod, p_end)]
        both = [_bdot(jnp.concatenate([u, p], axis=0), z) for u, p, z in zip(a_rb, bh_t, xb)]
        qy = [jnp.concatenate([p, q[c2:]], axis=1) + z[:c2] for p, q, z in zip(r_s, avk, both)]
        gh = [z[c2:] + jnp.concatenate([eye_l * q[c - 1:c, :], _bdot(u, w)], axis=1)
              for z, q, u, w in zip(both, p_incl, kh_t, v_s)]
        for ch, u, q in zip(chs, qy, gh):
            q_scr[p, ch] = u[:c, :nl] + u[c:, :nl]
            y0_scr[p, ch] = u[:c, nl:] + u[c:, nl:]
            g_scr[p, ch] = q[:, :nl]
            h_scr[p, ch] = q[:, nl:]

    groups(prep)

    def advance(ch, states):
        both = _dot3_many([jnp.concatenate([q_scr[p, ch], g_scr[p, ch]], axis=0) for p in range(pairs)], states)
        return (tuple(both[p][c:] + h_scr[p, ch] for p in range(pairs)),
                tuple(both[p][:c] + y0_scr[p, ch] for p in range(pairs)))

    def finish(ch, ys):
        o = offset(ch)
        rk = []
        for p in range(pairs):
            ln = lanes(p)
            kmod = k_ref[pl.ds(o, c), ln] * (1.0 + (a_ref[pl.ds(o, c), ln] - 1.0) * ka_ref[:, ln])
            rk.append(r_ref[pl.ds(o, c), ln] * kmod * rk_ref[:, ln])
        sums = _split_dot_many([jnp.concatenate([u, u * u, q], axis=0) for u, q in zip(ys, rk)], same_head)
        for p, (u, s) in enumerate(zip(ys, sums)):
            ln = lanes(p)
            mean = s[:c] * (1.0 / RW_HEAD)
            var = s[c:2 * c] * (1.0 / RW_HEAD) - mean * mean
            o_ref[pl.ds(o, c), ln] = ((u - mean) * lax.rsqrt(var + LNX_EPS) * lw_ref[:, ln] + lb_ref[:, ln]
                                      + s[2 * c:] * v_ref[pl.ds(o, c), ln])

    def step(ch, carry):
        states, ys = carry
        finish(ch - 1, ys)
        return advance(ch, states)

    carry = advance(0, tuple(jnp.zeros((nl, nl), F32) for _ in range(pairs)))
    _, ys = lax.fori_loop(1, n_chunks, step, carry)
    finish(n_chunks - 1, ys)


def _rw_scan(r, w, k, v, a, k_k, k_a, r_k, lnx_w, lnx_b, bsz, lp):
    tp, d = r.shape
    nl = 2 * RW_HEAD
    pairs = RW_PAIRS
    bw = pairs * nl
    nhp = d // bw
    nc = lp // RW_CHUNK
    seq = pl.BlockSpec((lp, bw), lambda b, hp: (b, hp))
    vec = pl.BlockSpec((1, bw), lambda b, hp: (0, hp))
    c = RW_CHUNK
    return pl.pallas_call(
        functools.partial(_rw_scan_kernel, n_chunks=nc, unroll=RW_UNROLL, pairs=pairs),
        grid=(bsz, nhp),
        in_specs=[seq] * 5 + [vec] * 5,
        out_specs=seq,
        out_shape=jax.ShapeDtypeStruct((tp, d), F32),
        scratch_shapes=[pltpu.VMEM((pairs, nc, nl, nl), F32), pltpu.VMEM((pairs, nc, nl, nl), F32),
                        pltpu.VMEM((pairs, nc, c, nl), F32), pltpu.VMEM((pairs, nc, c, nl), F32)],
        compiler_params=_cparams(("parallel", "parallel")),
        name="rwkv_scan",
    )(r, w, k, v, a, k_k.reshape(1, d), k_a.reshape(1, d), r_k.reshape(1, d),
      lnx_w.reshape(1, d), lnx_b.reshape(1, d))


def _rw_out_kernel(y_ref, g_ref, h_ref, wo_ref, gf_ref, wr_ref, br_ref, o_ref, xn_ref, rt_ref):
    h = h_ref[...] + _bdot(y_ref[...] * g_ref[...], wo_ref[...])
    o_ref[...] = h
    xn_ref[...], rt_ref[...] = _route_tile(h, gf_ref[...], wr_ref[0], wr_ref[1], br_ref[...])


def _rw_out(y, g, h, w_o, g_ffn, router):
    tp, d = h.shape
    wr, br = router
    tm = _pick_tile(tp, 512)
    row = lambda n: pl.BlockSpec((tm, n), lambda i: (i, 0))
    return pl.pallas_call(
        _rw_out_kernel,
        grid=(tp // tm,),
        in_specs=[row(d), row(d), row(d), _const_spec((d, d)), _const_spec((1, d)), _const_spec(wr.shape),
                  _const_spec(br.shape)],
        out_specs=[row(d), row(d), row(ROUTE_LANES)],
        out_shape=[jax.ShapeDtypeStruct((tp, d), F32), jax.ShapeDtypeStruct((tp, d), F32),
                   jax.ShapeDtypeStruct((tp, ROUTE_LANES), F32)],
        compiler_params=_cparams(("parallel",)),
        name="rwkv_out",
    )(y, g, h, w_o.astype(BF16), g_ffn.reshape(1, d), wr, br)


def _final_norm_kernel(h_ref, g_ref, o_ref, *, first, rows):
    o_ref[0] = _rms(h_ref[first:first + rows, :], g_ref[...]).astype(o_ref.dtype)


def _final_norm(h, g, dtype, bsz, lp, first, rows):
    d = h.shape[1]
    return pl.pallas_call(
        functools.partial(_final_norm_kernel, first=first, rows=rows),
        grid=(bsz,),
        in_specs=[pl.BlockSpec((lp, d), lambda b: (b, 0)), _const_spec((1, d))],
        out_specs=pl.BlockSpec((1, rows, d), lambda b: (b, 0, 0)),
        out_shape=jax.ShapeDtypeStruct((bsz, rows, d), dtype),
        compiler_params=_cparams(("parallel",)),
        name="final_norm",
    )(h, g.reshape(1, d))


def kernel(x, meta_tokens, norm_mix, norm_ffn, norm_final, ab_w_in, ab_w_out, ab_norm_a, ab_norm_b, s5_lam_re, s5_lam_im, s5_log_dt, s5_b_re, s5_b_im, s5_c_re, s5_c_im, s5_d, s5_w_glu, s5_b_glu, lru_conv_w, lru_conv_b, lru_w_a, lru_b_a, lru_w_x, lru_b_x, lru_lam, rw_mu, rw_w_r, rw_w_k, rw_w_v, rw_w_o, rw_w0, rw_w_l1, rw_w_l2, rw_a0, rw_a_l1, rw_a_l2, rw_v0, rw_v_l1, rw_v_l2, rw_g_l1, rw_g_l2, rw_k_k, rw_k_a, rw_r_k, rw_lnx_w, rw_lnx_b, moe_router_g, moe_router_g_b, moe_router_e, moe_router_e_b, moe_w_gate, moe_w_up, moe_w_down):
    bsz, seq, d = x.shape
    n_meta = meta_tokens.shape[0]
    depth = norm_mix.shape[0]
    ltot = n_meta + seq
    lp = -(-ltot // SEQ_ALIGN) * SEQ_ALIGN
    s5w = s5_w_glu.shape[-1]
    lruw = lru_lam.shape[-1]
    meta = jnp.broadcast_to(meta_tokens.astype(F32)[None], (bsz, n_meta, d))
    h = jnp.concatenate([meta, x.astype(F32), jnp.zeros((bsz, lp - ltot, d), F32)], axis=1).reshape(bsz * lp, d)
    v_first = None
    for layer in range(depth):
        j = layer // 2
        router = _router_table(moe_router_g[layer], moe_router_g_b[layer], moe_router_e[layer], moe_router_e_b[layer])
        if layer % 2 == 0:
            u, gate, rec = _ab_in(h, norm_mix[layer], ab_w_in[j], s5w, lruw)
            tables = _s5_tables(s5_lam_re[j], s5_lam_im[j], s5_log_dt[j], s5_b_re[j], s5_b_im[j],
                                s5_c_re[j], s5_c_im[j], s5_d[j])
            y5 = _s5_scan(u, tables, bsz, lp)
            lru = _lru(rec, gate, lru_conv_w[j], lru_conv_b[j], lru_w_a[j], lru_b_a[j], lru_w_x[j], lru_b_x[j],
                       lru_lam[j], bsz, lp)
            h, xn, rt = _ab_out(y5, lru, h, s5_w_glu[j], s5_b_glu[j], ab_norm_a[j], ab_norm_b[j], ab_w_out[j],
                                norm_ffn[layer], router)
        else:
            v_res = (rw_v0[j - 1], rw_v_l1[j - 1], rw_v_l2[j - 1]) if j > 0 else None
            r, k, v, w, a, g = _rw_pre(h, norm_mix[layer], rw_mu[j], rw_w_r[j], rw_w_k[j], rw_w_v[j], rw_w0[j],
                                       rw_w_l1[j], rw_w_l2[j], rw_a0[j], rw_a_l1[j], rw_a_l2[j], rw_g_l1[j],
                                       rw_g_l2[j], lp, v_first, v_res)
            if v_first is None:
                v_first = v
            y = _rw_scan(r, w, k, v, a, rw_k_k[j], rw_k_a[j], rw_r_k[j].reshape(-1), rw_lnx_w[j], rw_lnx_b[j],
                         bsz, lp)
            h, xn, rt = _rw_out(y, g, h, rw_w_o[j], norm_ffn[layer], router)
        h = _moe(h, xn, rt, moe_w_gate, moe_w_up, moe_w_down, layer)
    return _final_norm(h, norm_final, x.dtype, bsz, lp, n_meta, seq)
```

```python
import functools
import math

import jax
import jax.numpy as jnp
from jax import lax
from jax.experimental import pallas as pl
from jax.experimental.pallas import tpu as pltpu

F32 = jnp.float32
BF16 = jnp.bfloat16
HI = lax.Precision.HIGHEST

RMS_EPS = 1e-6
LNX_EPS = 64e-5
N_META = 16
SEQ_ALIGN = 64
S5_CHUNK = 16
S5_LANES = 128
RW_CHUNK = 64
RW_HEAD = 64
RW_PAIRS = 2
RW_UNROLL = 11
LRU_C = 8.0
N_GROUPS = 4
EXPERTS_PER_GROUP = 4
N_EXPERTS = N_GROUPS * EXPERTS_PER_GROUP
MOE_TILE = 512
ROUTE_LANES = 128
MOE_ROW_DTYPE = BF16
VMEM_LIMIT = 56 * 1024 * 1024


def _cparams(sem):
    return pltpu.CompilerParams(dimension_semantics=sem, vmem_limit_bytes=VMEM_LIMIT)


def _pick_tile(n, target):
    best = 8
    for t in range(8, min(n, target) + 1, 8):
        if n % t == 0:
            best = t
    return best


def _const_spec(shape):
    nd = len(shape)
    return pl.BlockSpec(shape, lambda *_: (0,) * nd)


def _rms(x, g):
    return x * lax.rsqrt(jnp.mean(x * x, axis=-1, keepdims=True) + RMS_EPS) * g


def _gelu(x):
    return 0.5 * x * (1.0 + jnp.tanh(math.sqrt(2.0 / math.pi) * (x + 0.044715 * (x * x * x))))


def _sigmoid(x):
    return 1.0 / (1.0 + jnp.exp(-x))


def _softplus(x):
    return jnp.maximum(x, 0.0) + jnp.log(1.0 + jnp.exp(-jnp.abs(x)))


def _bdot(a, b):
    return jnp.dot(a.astype(BF16), b.astype(BF16), preferred_element_type=F32)


def _hdot(a, b):
    return jnp.dot(a, b, preferred_element_type=F32, precision=HI)


def _dot3(a, b):
    m = a.shape[0]
    a_hi = a.astype(BF16)
    b_hi = b.astype(BF16)
    a_lo = (a - a_hi.astype(F32)).astype(BF16)
    b_lo = (b - b_hi.astype(F32)).astype(BF16)
    dot = functools.partial(jnp.dot, preferred_element_type=F32)
    top = dot(jnp.concatenate([a_hi, a_lo], axis=0), b_hi)
    return top[:m] + (top[m:] + dot(a_hi, b_lo))


def _dot3_many(a_list, b_list):
    m = a_list[0].shape[0]
    dot = functools.partial(jnp.dot, preferred_element_type=F32)
    a_hi = [a.astype(BF16) for a in a_list]
    b_hi = [b.astype(BF16) for b in b_list]
    a_lo = [(a - h.astype(F32)).astype(BF16) for a, h in zip(a_list, a_hi)]
    b_lo = [(b - h.astype(F32)).astype(BF16) for b, h in zip(b_list, b_hi)]
    top = [dot(jnp.concatenate([h, l], axis=0), b) for h, l, b in zip(a_hi, a_lo, b_hi)]
    low = [dot(h, b) for h, b in zip(a_hi, b_lo)]
    return [t[:m] + (t[m:] + l) for t, l in zip(top, low)]


def _split_dot_many(xs, m, rhs=False, pieces=3):
    accs = [None] * len(xs)
    xs = list(xs)
    for _ in range(pieces):
        his = [x.astype(BF16) for x in xs]
        parts = [jnp.dot(m, hi, preferred_element_type=F32) if rhs else jnp.dot(hi, m, preferred_element_type=F32)
                 for hi in his]
        accs = [p if a is None else a + p for a, p in zip(accs, parts)]
        xs = [x - hi.astype(F32) for x, hi in zip(xs, his)]
    return accs


def _ab_in_kernel(h_ref, g_ref, w_ref, u_ref, gate_ref, rec_ref, *, s5w, lruw):
    xn = _rms(h_ref[...], g_ref[...])
    z = _bdot(xn, w_ref[...])
    u_ref[...] = z[:, :s5w].astype(u_ref.dtype)
    gate_ref[...] = z[:, s5w:s5w + lruw]
    rec_ref[...] = z[:, s5w + lruw:]


def _ab_in(h, g, w_in, s5w, lruw):
    tp, d = h.shape
    tm = _pick_tile(tp, 512)
    row = lambda n: pl.BlockSpec((tm, n), lambda i: (i, 0))
    return pl.pallas_call(
        functools.partial(_ab_in_kernel, s5w=s5w, lruw=lruw),
        grid=(tp // tm,),
        in_specs=[row(d), _const_spec((1, d)), _const_spec(w_in.shape)],
        out_specs=[row(s5w), row(lruw), row(lruw)],
        out_shape=[jax.ShapeDtypeStruct((tp, s5w), BF16),
                   jax.ShapeDtypeStruct((tp, lruw), F32),
                   jax.ShapeDtypeStruct((tp, lruw), F32)],
        compiler_params=_cparams(("parallel",)),
        name="ab_in",
    )(h, g.reshape(1, d), w_in.astype(BF16))


def _s5_tables(lam_re, lam_im, log_dt, b_re, b_im, c_re, c_im, d_skip):
    g, p = lam_re.shape
    hh = b_re.shape[-1]
    c = S5_CHUNK
    lr, li = lam_re.astype(F32), lam_im.astype(F32)
    dt = jnp.exp(log_dt.astype(F32))[:, None]
    mag = jnp.exp(lr * dt)
    abar_r = mag * jnp.cos(li * dt)
    abar_i = mag * jnp.sin(li * dt)
    den = lr * lr + li * li
    zr = ((abar_r - 1.0) * lr + abar_i * li) / den
    zi = (abar_i * lr - (abar_r - 1.0) * li) / den
    bbar_r = zr[..., None] * b_re - zi[..., None] * b_im
    bbar_i = zr[..., None] * b_im + zi[..., None] * b_re
    bbr_t, bbi_t = jnp.swapaxes(bbar_r, 1, 2), jnp.swapaxes(bbar_i, 1, 2)
    cr_t, ci_t = jnp.swapaxes(c_re, 1, 2), jnp.swapaxes(c_im, 1, 2)

    def powers(steps):
        st = steps.astype(F32)[None, :, None]
        pmag = jnp.exp(st * (lr * dt)[:, None, :])
        ang = st * (li * dt)[:, None, :]
        return pmag * jnp.cos(ang), pmag * jnp.sin(ang)

    down = (c - 1) - jnp.arange(c)
    rev_r, rev_i = powers(down)
    m1_r = rev_r[:, :, None, :] * bbr_t[:, None] - rev_i[:, :, None, :] * bbi_t[:, None]
    m1_i = rev_r[:, :, None, :] * bbi_t[:, None] + rev_i[:, :, None, :] * bbr_t[:, None]
    car = c_re[:, None] * rev_r[:, :, None, :] - c_im[:, None] * rev_i[:, :, None, :]
    cai = c_re[:, None] * rev_i[:, :, None, :] + c_im[:, None] * rev_r[:, :, None, :]
    kern = (jnp.einsum('gqhp,gpj->gqjh', car, bbar_r, precision=HI)
            - jnp.einsum('gqhp,gpj->gqjh', cai, bbar_i, precision=HI))
    is_tau0 = (down == 0).astype(F32)[None, :, None, None]
    kern = kern + is_tau0 * (d_skip[:, None, None, :] * jnp.eye(hh, dtype=F32)[None, None])
    up_r, up_i = powers(jnp.arange(1, c + 1))
    up_r, up_i = jnp.swapaxes(up_r, 1, 2)[..., None], jnp.swapaxes(up_i, 1, 2)[..., None]
    m2_r = cr_t[:, :, None, :] * up_r - ci_t[:, :, None, :] * up_i
    m2_i = -(cr_t[:, :, None, :] * up_i + ci_t[:, :, None, :] * up_r)
    adv_r, adv_i = powers(jnp.full((1,), c))
    gb = S5_LANES // hh
    nj = g // gb

    def rows_sgh(x):
        w = x.shape[-1]
        return jnp.transpose(x.reshape(nj, gb, c, hh, w), (0, 2, 1, 3, 4)).reshape(nj, c * gb * hh, w)

    def place(base, spread, row_group, col_group):
        out = jnp.einsum('jrw,wc->jrc', base.astype(BF16), spread.astype(BF16), preferred_element_type=BF16)
        rg = row_group(lax.broadcasted_iota(jnp.int32, out.shape[1:], 0))
        cg = col_group(lax.broadcasted_iota(jnp.int32, out.shape[1:], 1))
        return jnp.where(rg == cg, out, jnp.zeros((), BF16))

    grp_sgh = lambda r: (r // hh) % gb
    rep = lambda w: jnp.tile(jnp.eye(w, dtype=F32), (1, gb))
    m1 = jnp.concatenate([place(rows_sgh(m), rep(p), grp_sgh, lambda col: col // p) for m in (m1_r, m1_i)], axis=-1)
    krev = place(rows_sgh(kern), rep(hh), grp_sgh, lambda col: col // hh)
    ri = lax.broadcasted_iota(jnp.int32, (c * hh, c * gb * hh), 0)
    ci = lax.broadcasted_iota(jnp.int32, (c * hh, c * gb * hh), 1)
    spread_th = ((ri // hh == ci // (gb * hh)) & (ri % hh == ci % hh)).astype(F32)
    m2 = jnp.concatenate([place(m.reshape(nj, gb * p, c * hh), spread_th, lambda r: r // p,
                                lambda col: (col // hh) % gb) for m in (m2_r, m2_i)], axis=1)
    return (m1.astype(BF16), krev.astype(BF16), m2.astype(BF16),
            adv_r.reshape(nj, 1, gb * p), adv_i.reshape(nj, 1, gb * p))


def _s5_kernel(u_ref, m1_ref, kr_ref, m2_ref, ar_ref, ai_ref, y_ref, xe_ref, xin_ref, st_ref, *, n_chunks, nb):
    nl = S5_LANES
    csz = S5_CHUNK
    sw = ar_ref.shape[-1]

    @pl.when(pl.program_id(1) == 0)
    def _():
        st_ref[...] = jnp.zeros(st_ref.shape, F32)

    u = u_ref[0]
    xe_ref[...] = jnp.dot(u, m1_ref[0], preferred_element_type=F32)
    ar = jnp.broadcast_to(ar_ref[0], (nb, sw))
    ai = jnp.broadcast_to(ai_ref[0], (nb, sw))

    def body(c, carry):
        sr, si = carry
        off = pl.multiple_of(c * nb, nb)
        xin_ref[pl.ds(off, nb), :] = jnp.concatenate([sr, si], axis=1)
        e = xe_ref[pl.ds(off, nb), :]
        return (ar * sr - ai * si + e[:, :sw], ar * si + ai * sr + e[:, sw:])

    sr, si = lax.fori_loop(0, n_chunks, body, (st_ref[:, :sw], st_ref[:, sw:]))
    st_ref[:, :sw] = sr
    st_ref[:, sw:] = si
    y_ref[0] = _bdot(xin_ref[...], m2_ref[0])
    for t in range(csz):
        y_ref[0, :, t * nl:(t + 1) * nl] += jnp.dot(u[:, :(t + 1) * nl], kr_ref[0, (csz - 1 - t) * nl:, :],
                                                    preferred_element_type=F32)


def _s5_scan(u, tables, bsz, lp):
    m1, krev, m2, adv_r, adv_i = tables
    nj, kin, sw2 = m1.shape
    c = S5_CHUNK
    nl = S5_LANES
    nc = lp // c
    cpt = max(d for d in range(1, nc + 1) if nc % d == 0 and d * bsz <= 512)
    rows = cpt * bsz
    ug = jnp.transpose(u.reshape(bsz, nc, c, nj, nl), (3, 1, 0, 2, 4)).reshape(nj, nc * bsz, kin)
    y = pl.pallas_call(
        functools.partial(_s5_kernel, n_chunks=cpt, nb=bsz),
        grid=(nj, nc // cpt),
        in_specs=[pl.BlockSpec((1, rows, kin), lambda j, r: (j, r, 0)),
                  pl.BlockSpec((1, kin, sw2), lambda j, r: (j, 0, 0)),
                  pl.BlockSpec((1, kin, nl), lambda j, r: (j, 0, 0)),
                  pl.BlockSpec((1, sw2, kin), lambda j, r: (j, 0, 0)),
                  pl.BlockSpec((1, 1, sw2 // 2), lambda j, r: (j, 0, 0)),
                  pl.BlockSpec((1, 1, sw2 // 2), lambda j, r: (j, 0, 0))],
        out_specs=pl.BlockSpec((1, rows, kin), lambda j, r: (j, r, 0)),
        out_shape=jax.ShapeDtypeStruct((nj, nc * bsz, kin), F32),
        scratch_shapes=[pltpu.VMEM((rows, sw2), F32), pltpu.VMEM((rows, sw2), F32), pltpu.VMEM((bsz, sw2), F32)],
        compiler_params=_cparams(("parallel", "arbitrary")),
        name="s5_scan",
    )(ug, m1, krev, m2, adv_r, adv_i)
    y = jnp.transpose(y.reshape(nj, nc, bsz, c, nl), (2, 1, 3, 0, 4))
    return y.reshape(bsz * lp, nj * nl)


def _lru_kernel(rec_ref, gate_ref, cw_ref, cb_ref, wa_ref, ba_ref, wx_ref, bx_ref, lam_ref,
                o_ref, ext_ref, a_ref, b_ref, h_ref, *, tl):
    t = pl.program_id(1)

    @pl.when(t == 0)
    def _():
        ext_ref[0:8, :] = jnp.zeros((8, ext_ref.shape[1]), F32)
        h_ref[...] = jnp.zeros(h_ref.shape, F32)

    x = rec_ref[...]
    ext_ref[8:, :] = x
    xc = cb_ref[...] + cw_ref[3:4, :] * x
    for k in range(3):
        xc = xc + cw_ref[k:k + 1, :] * ext_ref[5 + k:5 + k + tl, :]
    ext_ref[0:8, :] = x[tl - 8:, :]
    r = _sigmoid(_bdot(xc, wa_ref[...]) + ba_ref[...])
    i = _sigmoid(_bdot(xc, wx_ref[...]) + bx_ref[...])
    log_a = (-LRU_C) * r * _softplus(-lam_ref[...])
    a = jnp.exp(log_a)
    a_ref[...] = a
    b_ref[...] = jnp.sqrt(1.0 - a * a) * (i * xc)

    sub = 8
    unroll = max(u for u in (4, 2, 1) if (tl // sub) % u == 0)
    row = lax.broadcasted_iota(jnp.int32, (sub, a.shape[1]), 0)

    def body(j, h):
        offs = [pl.multiple_of((j * unroll + q) * sub, sub) for q in range(unroll)]
        ab = [a_ref[pl.ds(o, sub), :] for o in offs]
        bb = [b_ref[pl.ds(o, sub), :] for o in offs]
        for dist in (1, 2, 4):
            keep = row >= dist
            bb = [jnp.where(keep, x * pltpu.roll(y, dist, axis=0) + y, y) for x, y in zip(ab, bb)]
            ab = [jnp.where(keep, x * pltpu.roll(x, dist, axis=0), x) for x in ab]
        for o, x, y in zip(offs, ab, bb):
            hb = x * h + y
            o_ref[pl.ds(o, sub), :] = hb * _gelu(gate_ref[pl.ds(o, sub), :])
            h = hb[sub - 1:sub, :]
        return h

    h_ref[...] = lax.fori_loop(0, tl // (sub * unroll), body, h_ref[...])


def _lru(rec, gate, conv_w, conv_b, w_a, b_a, w_x, b_x, lam, bsz, lp):
    tp, w = rec.shape
    tl = _pick_tile(lp, 1056)
    nt = lp // tl
    heads, hd, _ = w_a.shape

    def dense(wb):
        eye = jnp.eye(heads, dtype=F32)
        return jnp.einsum('hij,hg->higj', wb, eye).reshape(w, w).astype(BF16)

    row = pl.BlockSpec((tl, w), lambda b, t: (b * nt + t, 0))
    vec = _const_spec((1, w))
    return pl.pallas_call(
        functools.partial(_lru_kernel, tl=tl),
        grid=(bsz, nt),
        in_specs=[row, row, _const_spec((4, w)), vec, _const_spec((w, w)), vec, _const_spec((w, w)), vec, vec],
        out_specs=row,
        out_shape=jax.ShapeDtypeStruct((tp, w), F32),
        scratch_shapes=[pltpu.VMEM((tl + 8, w), F32), pltpu.VMEM((tl, w), F32),
                        pltpu.VMEM((tl, w), F32), pltpu.VMEM((1, w), F32)],
        compiler_params=_cparams(("parallel", "arbitrary")),
        name="rglru",
    )(rec, gate, conv_w, conv_b.reshape(1, w), dense(w_a), b_a.reshape(1, w), dense(w_x), b_x.reshape(1, w),
      lam.reshape(1, w))


def _ab_out_kernel(y5_ref, lru_ref, h_ref, wglu_ref, bglu_ref, na_ref, nb_ref, wo_ref, gf_ref, wr_ref, br_ref,
                   o_ref, xn_ref, rt_ref, *, s5w):
    y = _gelu(y5_ref[...])
    ya = y * _sigmoid(_bdot(y, wglu_ref[...]) + bglu_ref[...])
    ya = _rms(ya, na_ref[...])
    yb = _rms(lru_ref[...], nb_ref[...])
    h = h_ref[...] + _bdot(ya, wo_ref[:s5w, :]) + _bdot(yb, wo_ref[s5w:, :])
    o_ref[...] = h
    xn, rt_ref[...] = _route_tile(h, gf_ref[...], wr_ref[0], wr_ref[1], br_ref[...])
    xn_ref[...] = xn.astype(xn_ref.dtype)


def _ab_out(y5, lru, h, w_glu, b_glu, norm_a, norm_b, w_out, g_ffn, router):
    tp, d = h.shape
    s5w, lruw = y5.shape[1], lru.shape[1]
    wr, br = router
    tm = _pick_tile(tp, 512)
    row = lambda n: pl.BlockSpec((tm, n), lambda i: (i, 0))
    return pl.pallas_call(
        functools.partial(_ab_out_kernel, s5w=s5w),
        grid=(tp // tm,),
        in_specs=[row(s5w), row(lruw), row(d), _const_spec((s5w, s5w)), _const_spec((1, s5w)),
                  _const_spec((1, s5w)), _const_spec((1, lruw)), _const_spec(w_out.shape),
                  _const_spec((1, d)), _const_spec(wr.shape), _const_spec(br.shape)],
        out_specs=[row(d), row(d), row(ROUTE_LANES)],
        out_shape=[jax.ShapeDtypeStruct((tp, d), F32), jax.ShapeDtypeStruct((tp, d), F32),
                   jax.ShapeDtypeStruct((tp, ROUTE_LANES), F32)],
        compiler_params=_cparams(("parallel",)),
        name="ab_out",
    )(y5, lru, h, w_glu.astype(BF16), b_glu.reshape(1, s5w), norm_a.reshape(1, s5w),
      norm_b.reshape(1, lruw), w_out.astype(BF16), g_ffn.reshape(1, d), wr, br)


def _route_tile(h, g, wr_hi, wr_lo, br):
    xn = _rms(h, g)
    m = xn.shape[0]
    x_hi = xn.astype(BF16)
    x_lo = (xn - x_hi.astype(F32)).astype(BF16)
    top = jnp.dot(jnp.concatenate([x_hi, x_lo], axis=0), wr_hi, preferred_element_type=F32)
    lg = top[:m] + (top[m:] + jnp.dot(x_hi, wr_lo, preferred_element_type=F32)) + br
    lane = lax.broadcasted_iota(jnp.int32, lg.shape, 1).astype(F32)
    big = float(lg.shape[1])
    neg = -jnp.inf
    gl = jnp.where(lane < N_GROUPS, lg, neg)
    mg = jnp.max(gl, axis=-1, keepdims=True)
    gidx = jnp.min(jnp.where(gl == mg, lane, big), axis=-1, keepdims=True)
    pg_sel = 1.0 / jnp.sum(jnp.exp(gl - mg), axis=-1, keepdims=True)
    lo = N_GROUPS + EXPERTS_PER_GROUP * gidx
    le = jnp.where(lane >= lo, jnp.where(lane < lo + EXPERTS_PER_GROUP, lg, neg), neg)
    v1 = jnp.max(le, axis=-1, keepdims=True)
    i1 = jnp.min(jnp.where(le == v1, lane, big), axis=-1, keepdims=True)
    le2 = jnp.where(lane == i1, neg, le)
    v2 = jnp.max(le2, axis=-1, keepdims=True)
    i2 = jnp.min(jnp.where(le2 == v2, lane, big), axis=-1, keepdims=True)
    e2 = jnp.exp(v2 - v1)
    w1 = pg_sel / (1.0 + e2)
    w2 = w1 * e2
    rt = jnp.where(lane == 0.0, w1, jnp.where(lane == 1.0, w2, jnp.where(
        lane == 2.0, i1 - N_GROUPS, jnp.where(lane == 3.0, i2 - N_GROUPS, 0.0))))
    return xn, rt


def _router_table(wr_g, br_g, wr_e, br_e):
    d = wr_g.shape[0]
    wr = jnp.zeros((d, ROUTE_LANES), F32).at[:, :N_GROUPS].set(wr_g).at[:, N_GROUPS:N_GROUPS + N_EXPERTS].set(wr_e)
    br = jnp.zeros((1, ROUTE_LANES), F32).at[0, :N_GROUPS].set(br_g).at[0, N_GROUPS:N_GROUPS + N_EXPERTS].set(br_e)
    wr_hi = wr.astype(BF16)
    wr_lo = (wr - wr_hi.astype(F32)).astype(BF16)
    return jnp.stack([wr_hi, wr_lo]), br


def _gmm_kernel(te_ref, tv_ref, xa_ref, xb_ref, wg_ref, wu_ref, wd_ref, o_ref, wg_bf, wu_bf, wd_bf, *, n_half):
    i = pl.program_id(0)

    @pl.when(jnp.logical_or(i == 0, te_ref[i] != te_ref[jnp.maximum(i - 1, 0)]))
    def _():
        wg_bf[...] = wg_ref[0, 0].astype(BF16)
        wu_bf[...] = wu_ref[0, 0].astype(BF16)
        wd_bf[...] = wd_ref[0, 0].astype(BF16)

    @pl.when(tv_ref[i] != 0)
    def _():
        x = jnp.where(i < n_half, xa_ref[...], xb_ref[...]).astype(BF16)
        hg = jnp.dot(x, wg_bf[...], preferred_element_type=F32)
        hu = jnp.dot(x, wu_bf[...], preferred_element_type=F32)
        hid = hg * _sigmoid(hg) * hu
        o_ref[...] = _bdot(hid, wd_bf[...]).astype(o_ref.dtype)

    @pl.when(tv_ref[i] == 0)
    def _():
        o_ref[...] = jnp.zeros(o_ref.shape, o_ref.dtype)


def _moe(h, xn, rt, w_gate, w_up, w_down, layer):
    tp, d = h.shape
    f = w_gate.shape[-1]
    gate = rt[:, 0:2]
    eid = rt[:, 2:4].astype(jnp.int32)
    tmm = MOE_TILE
    na = 2 * tp
    e_flat = eid.reshape(na)
    onehot = (e_flat[:, None] == jnp.arange(N_EXPERTS, dtype=jnp.int32)[None, :]).astype(jnp.int32)
    csum = jnp.cumsum(onehot, axis=0)
    counts = csum[-1]
    padded = ((counts + tmm - 1) // tmm) * tmm
    ends = jnp.cumsum(padded)
    starts = ends - padded
    dest = jnp.sum(onehot * (csum + (starts - 1)[None, :]), axis=1)
    n_half = (-(-na // tmm) + N_EXPERTS + 1) // 2
    n_tiles = 2 * n_half
    nrows = n_tiles * tmm
    tile_start = jnp.arange(n_tiles, dtype=jnp.int32) * tmm
    tile_e = jnp.sum((ends[None, :] <= tile_start[:, None]).astype(jnp.int32), axis=1)
    tile_v = (tile_e < N_EXPERTS).astype(jnp.int32)
    tile_e = jnp.minimum(tile_e, N_EXPERTS - 1)
    src = (jnp.arange(nrows, dtype=jnp.int32) % tp).at[dest].set(
        jnp.arange(na, dtype=jnp.int32) // 2, mode="promise_in_bounds", unique_indices=True)
    take = lambda a, i: a.at[i].get(mode="promise_in_bounds")
    xs_a = take(xn, src[:n_half * tmm])
    xs_b = take(xn, src[n_half * tmm:])
    ys = pl.pallas_call(
        functools.partial(_gmm_kernel, n_half=n_half),
        grid_spec=pltpu.PrefetchScalarGridSpec(
            num_scalar_prefetch=2,
            grid=(n_tiles,),
            in_specs=[pl.BlockSpec((tmm, d), lambda i, te, tv: (jnp.minimum(i, n_half - 1), 0)),
                      pl.BlockSpec((tmm, d), lambda i, te, tv: (jnp.maximum(i - n_half, 0), 0)),
                      pl.BlockSpec((1, 1, d, f), lambda i, te, tv: (layer, te[i], 0, 0)),
                      pl.BlockSpec((1, 1, d, f), lambda i, te, tv: (layer, te[i], 0, 0)),
                      pl.BlockSpec((1, 1, f, d), lambda i, te, tv: (layer, te[i], 0, 0))],
            out_specs=pl.BlockSpec((tmm, d), lambda i, te, tv: (i, 0)),
            scratch_shapes=[pltpu.VMEM((d, f), BF16), pltpu.VMEM((d, f), BF16), pltpu.VMEM((f, d), BF16)],
        ),
        out_shape=jax.ShapeDtypeStruct((nrows, d), MOE_ROW_DTYPE),
        compiler_params=_cparams(("arbitrary",)),
        name="moe_gmm",
    )(tile_e, tile_v, xs_a, xs_b, w_gate, w_up, w_down)
    d2 = dest.reshape(tp, 2)
    return h + gate[:, 0:1] * take(ys, d2[:, 0]) + gate[:, 1:2] * take(ys, d2[:, 1])


def _rw_pre_kernel(*refs, tm, lp, has_vres):
    if has_vres:
        (h_ref, hp_ref, g_ref, mu_ref, wr_ref, wk_ref, wv_ref, w0_ref, wl1_ref, wl2_ref,
         a0_ref, al1_ref, al2_ref, gl1_ref, gl2_ref, vf_ref, v0_ref, vl1_ref, vl2_ref,
         r_ref, k_ref, v_ref, w_ref, a_ref, gg_ref) = refs
    else:
        (h_ref, hp_ref, g_ref, mu_ref, wr_ref, wk_ref, wv_ref, w0_ref, wl1_ref, wl2_ref,
         a0_ref, al1_ref, al2_ref, gl1_ref, gl2_ref,
         r_ref, k_ref, v_ref, w_ref, a_ref, gg_ref) = refs
    i = pl.program_id(0)
    g = g_ref[...]
    x = _rms(h_ref[...], g)
    xp8 = _rms(hp_ref[...], g)
    row = lax.broadcasted_iota(jnp.int32, x.shape, 0)
    prev = jnp.where(row == 0, jnp.broadcast_to(xp8[7:8, :], x.shape), pltpu.roll(x, 1, axis=0))
    first = lax.rem(lp - lax.rem(i * tm, lp), lp)
    prev = jnp.where(row == first, 0.0, prev)
    xx = prev - x
    xr, xw, xk, xv, xa, xg = [x + xx * mu_ref[j:j + 1, :] for j in range(6)]
    r_ref[...] = _bdot(xr, wr_ref[...]).astype(r_ref.dtype)
    k_ref[...] = _bdot(xk, wk_ref[...]).astype(k_ref.dtype)
    v = _bdot(xv, wv_ref[...])
    if has_vres:
        mix = _sigmoid(v0_ref[...] + _bdot(_bdot(xv, vl1_ref[...]), vl2_ref[...]))
        v = v + (vf_ref[...] - v) * mix
    v_ref[...] = v.astype(v_ref.dtype)
    w_ref[...] = -_softplus(-(w0_ref[...] + _bdot(jnp.tanh(_bdot(xw, wl1_ref[...])), wl2_ref[...]))) - 0.5
    a_ref[...] = _sigmoid(a0_ref[...] + _bdot(_bdot(xa, al1_ref[...]), al2_ref[...]))
    gg_ref[...] = _bdot(_sigmoid(_bdot(xg, gl1_ref[...])), gl2_ref[...]).astype(gg_ref.dtype)


def _rw_pre(h, g, mu, w_r, w_k, w_v, w0, w_l1, w_l2, a0, a_l1, a_l2, g_l1, g_l2, lp, v_first, v_res):
    tp, d = h.shape
    tm = _pick_tile(tp, min(512, lp))
    has_vres = v_res is not None
    row = pl.BlockSpec((tm, d), lambda i: (i, 0))
    prev8 = pl.BlockSpec((8, d), lambda i: (jnp.maximum(i * (tm // 8) - 1, 0), 0))
    vec = _const_spec((1, d))
    mu8 = jnp.zeros((8, d), F32).at[:6].set(mu)
    bf = lambda w: w.astype(BF16)
    ins = [h, h, g.reshape(1, d), mu8, bf(w_r), bf(w_k), bf(w_v), w0.reshape(1, d), bf(w_l1), bf(w_l2),
           a0.reshape(1, d), bf(a_l1), bf(a_l2), bf(g_l1), bf(g_l2)]
    specs = [row, prev8, vec, _const_spec((8, d))] + [_const_spec((d, d))] * 3 + [
        vec, _const_spec(w_l1.shape), _const_spec(w_l2.shape),
        vec, _const_spec(a_l1.shape), _const_spec(a_l2.shape), _const_spec(g_l1.shape), _const_spec(g_l2.shape)]
    if has_vres:
        v0, v_l1, v_l2 = v_res
        ins += [v_first, v0.reshape(1, d), bf(v_l1), bf(v_l2)]
        specs += [row, vec, _const_spec(v_l1.shape), _const_spec(v_l2.shape)]
    return pl.pallas_call(
        functools.partial(_rw_pre_kernel, tm=tm, lp=lp, has_vres=has_vres),
        grid=(tp // tm,),
        in_specs=specs,
        out_specs=[row] * 6,
        out_shape=[jax.ShapeDtypeStruct((tp, d), dt) for dt in (BF16, BF16, BF16, F32, F32, BF16)],
        compiler_params=_cparams(("parallel",)),
        name="rwkv_pre",
    )(*ins)


def _rw_scan_kernel(r_ref, w_ref, k_ref, v_ref, a_ref, kk_ref, ka_ref, rk_ref, lw_ref, lb_ref,
                    o_ref, g_scr, h_scr, q_scr, y0_scr, *, n_chunks, unroll, pairs):
    c = RW_CHUNK
    nl = 2 * RW_HEAD
    c2 = 2 * c
    ri = lax.broadcasted_iota(jnp.int32, (c2, nl), 0)
    ci = lax.broadcasted_iota(jnp.int32, (c2, nl), 1)
    own = ((ri >= c) == (ci >= RW_HEAD)).astype(F32)
    t_in = jnp.bitwise_and(ri, c - 1)
    s_in = jnp.bitwise_and(ci, c - 1)
    strict = jnp.where(s_in < t_in, own, 0.0)
    incl = jnp.where(s_in <= t_in, own, 0.0)
    causal2 = jnp.concatenate([strict, incl], axis=0)
    eye_l = (ri == ci).astype(F32)
    same_head = own.astype(BF16)
    ti = lax.broadcasted_iota(jnp.int32, (c, c), 0)
    si = lax.broadcasted_iota(jnp.int32, (c, c), 1)
    tril_c = (si <= ti).astype(BF16)
    lanes = lambda p: slice(p * nl, (p + 1) * nl)

    def stack(x):
        return jnp.concatenate([x, x], axis=0) * own

    def offset(ch):
        return ch * c if isinstance(ch, int) else pl.multiple_of(ch * c, c)

    def groups(fn):
        for p in range(pairs):
            def body(i, _):
                fn([i * unroll + q for q in range(unroll)], p)
                return 0
            lax.fori_loop(0, n_chunks // unroll, body, 0)
            if n_chunks % unroll:
                fn(list(range(n_chunks - n_chunks % unroll, n_chunks)), p)

    def prep(chs, p):
        ln = lanes(p)
        kk_w, ka_w = kk_ref[:, ln], ka_ref[:, ln]
        offs = [offset(ch) for ch in chs]
        r = [r_ref[pl.ds(o, c), ln] for o in offs]
        k = [k_ref[pl.ds(o, c), ln] for o in offs]
        v = [v_ref[pl.ds(o, c), ln] for o in offs]
        a = [a_ref[pl.ds(o, c), ln] for o in offs]
        logw = [-jnp.exp(w_ref[pl.ds(o, c), ln]) for o in offs]
        kk = [x * kk_w for x in k]
        ss = _split_dot_many([x * x for x in kk], same_head, pieces=2)
        kk = [x / jnp.maximum(jnp.sqrt(q), 1e-12) for x, q in zip(kk, ss)]
        kmod = [x * (1.0 + (y - 1.0) * ka_w) for x, y in zip(k, a)]
        cum = _split_dot_many(logw, tril_c, rhs=True, pieces=2)
        p_incl = [jnp.exp(x) for x in cum]
        p_inv = [jnp.exp(-x) for x in cum]
        p_end = [jnp.exp(x[c - 1:c, :] - x) for x in cum]
        kka = [x * y for x, y in zip(kk, a)]
        a_s = [stack(-x * jnp.exp(y - z)) for x, y, z in zip(kk, cum, logw)]
        r_s = [stack(x * y) for x, y in zip(r, p_incl)]
        v_s = [stack(x).astype(BF16) for x in v]
        lhs = [jnp.concatenate([x, y], axis=0).astype(BF16) for x, y in zip(a_s, r_s)]
        rhs = [jnp.concatenate([stack(x * z), stack(y * z)], axis=0).astype(BF16) for x, y, z in zip(kka, kmod, p_inv)]
        big = [lax.dot_general(x, y, (((1,), (1,)), ((), ())), preferred_element_type=F32) for x, y in zip(lhs, rhs)]
        lpow = [(x[:c2, :c2] * strict).astype(BF16) for x in big]
        a_rb = [x[c2:, :c2] * incl for x in big]
        avk = [_bdot(x[:, c2:] * causal2, w) for x, w in zip(big, v_s)]
        x = [jnp.concatenate([p, q[:c2]], axis=1) for p, q in zip(a_s, avk)]
        x = [p + _bdot(q, p) for p, q in zip(x, lpow)]
        for _ in range(5):
            lpow = [jnp.dot(q, q, preferred_element_type=F32).astype(BF16) for q in lpow]
            x = [p + _bdot(q, p) for p, q in zip(x, lpow)]
        xb = [p.astype(BF16) for p in x]
        bh_t = [stack(p * q).T for p, q in zip(kka, p_end)]
        kh_t = [stack(p * q).T for p, q in zip(kmod, p_end)]
        both = [_bdot(jnp.concatenate([u, p], axis=0), z) for u, p, z in zip(a_rb, bh_t, xb)]
        qy = [jnp.concatenate([p, q[c2:]], axis=1) + z[:c2] for p, q, z in zip(r_s, avk, both)]
        gh = [z[c2:] + jnp.concatenate([eye_l * q[c - 1:c, :], _bdot(u, w)], axis=1)
              for z, q, u, w in zip(both, p_incl, kh_t, v_s)]
        for ch, u, q in zip(chs, qy, gh):
            q_scr[p, ch] = u[:c, :nl] + u[c:, :nl]
            y0_scr[p, ch] = u[:c, nl:] + u[c:, nl:]
            g_scr[p, ch] = q[:, :nl]
            h_scr[p, ch] = q[:, nl:]

    groups(prep)

    def advance(ch, states):
        both = _dot3_many([jnp.concatenate([q_scr[p, ch], g_scr[p, ch]], axis=0) for p in range(pairs)], states)
        return (tuple(both[p][c:] + h_scr[p, ch] for p in range(pairs)),
                tuple(both[p][:c] + y0_scr[p, ch] for p in range(pairs)))

    def finish(ch, ys):
        o = offset(ch)
        rk = []
        for p in range(pairs):
            ln = lanes(p)
            kmod = k_ref[pl.ds(o, c), ln] * (1.0 + (a_ref[pl.ds(o, c), ln] - 1.0) * ka_ref[:, ln])
            rk.append(r_ref[pl.ds(o, c), ln] * kmod * rk_ref[:, ln])
        sums = _split_dot_many([jnp.concatenate([u, u * u, q], axis=0) for u, q in zip(ys, rk)], same_head)
        for p, (u, s) in enumerate(zip(ys, sums)):
            ln = lanes(p)
            mean = s[:c] * (1.0 / RW_HEAD)
            var = s[c:2 * c] * (1.0 / RW_HEAD) - mean * mean
            o_ref[pl.ds(o, c), ln] = ((u - mean) * lax.rsqrt(var + LNX_EPS) * lw_ref[:, ln] + lb_ref[:, ln]
                                      + s[2 * c:] * v_ref[pl.ds(o, c), ln])

    def step(ch, carry):
        states, ys = carry
        finish(ch - 1, ys)
        return advance(ch, states)

    carry = advance(0, tuple(jnp.zeros((nl, nl), F32) for _ in range(pairs)))
    _, ys = lax.fori_loop(1, n_chunks, step, carry)
    finish(n_chunks - 1, ys)


def _rw_scan(r, w, k, v, a, k_k, k_a, r_k, lnx_w, lnx_b, bsz, lp):
    tp, d = r.shape
    nl = 2 * RW_HEAD
    pairs = RW_PAIRS
    bw = pairs * nl
    nhp = d // bw
    nc = lp // RW_CHUNK
    seq = pl.BlockSpec((lp, bw), lambda b, hp: (b, hp))
    vec = pl.BlockSpec((1, bw), lambda b, hp: (0, hp))
    c = RW_CHUNK
    return pl.pallas_call(
        functools.partial(_rw_scan_kernel, n_chunks=nc, unroll=RW_UNROLL, pairs=pairs),
        grid=(bsz, nhp),
        in_specs=[seq] * 5 + [vec] * 5,
        out_specs=seq,
        out_shape=jax.ShapeDtypeStruct((tp, d), F32),
        scratch_shapes=[pltpu.VMEM((pairs, nc, nl, nl), F32), pltpu.VMEM((pairs, nc, nl, nl), F32),
                        pltpu.VMEM((pairs, nc, c, nl), F32), pltpu.VMEM((pairs, nc, c, nl), F32)],
        compiler_params=_cparams(("parallel", "parallel")),
        name="rwkv_scan",
    )(r, w, k, v, a, k_k.reshape(1, d), k_a.reshape(1, d), r_k.reshape(1, d),
      lnx_w.reshape(1, d), lnx_b.reshape(1, d))


def _rw_out_kernel(y_ref, g_ref, h_ref, wo_ref, gf_ref, wr_ref, br_ref, o_ref, xn_ref, rt_ref):
    h = h_ref[...] + _bdot(y_ref[...] * g_ref[...], wo_ref[...])
    o_ref[...] = h
    xn, rt_ref[...] = _route_tile(h, gf_ref[...], wr_ref[0], wr_ref[1], br_ref[...])
    xn_ref[...] = xn.astype(xn_ref.dtype)


def _rw_out(y, g, h, w_o, g_ffn, router):
    tp, d = h.shape
    wr, br = router
    tm = _pick_tile(tp, 512)
    row = lambda n: pl.BlockSpec((tm, n), lambda i: (i, 0))
    return pl.pallas_call(
        _rw_out_kernel,
        grid=(tp // tm,),
        in_specs=[row(d), row(d), row(d), _const_spec((d, d)), _const_spec((1, d)), _const_spec(wr.shape),
                  _const_spec(br.shape)],
        out_specs=[row(d), row(d), row(ROUTE_LANES)],
        out_shape=[jax.ShapeDtypeStruct((tp, d), F32), jax.ShapeDtypeStruct((tp, d), F32),
                   jax.ShapeDtypeStruct((tp, ROUTE_LANES), F32)],
        compiler_params=_cparams(("parallel",)),
        name="rwkv_out",
    )(y, g, h, w_o.astype(BF16), g_ffn.reshape(1, d), wr, br)


def _final_norm_kernel(h_ref, g_ref, o_ref, *, first, rows):
    o_ref[0] = _rms(h_ref[first:first + rows, :], g_ref[...]).astype(o_ref.dtype)


def _final_norm(h, g, dtype, bsz, lp, first, rows):
    d = h.shape[1]
    return pl.pallas_call(
        functools.partial(_final_norm_kernel, first=first, rows=rows),
        grid=(bsz,),
        in_specs=[pl.BlockSpec((lp, d), lambda b: (b, 0)), _const_spec((1, d))],
        out_specs=pl.BlockSpec((1, rows, d), lambda b: (b, 0, 0)),
        out_shape=jax.ShapeDtypeStruct((bsz, rows, d), dtype),
        compiler_params=_cparams(("parallel",)),
        name="final_norm",
    )(h, g.reshape(1, d))


def kernel(x, meta_tokens, norm_mix, norm_ffn, norm_final, ab_w_in, ab_w_out, ab_norm_a, ab_norm_b, s5_lam_re, s5_lam_im, s5_log_dt, s5_b_re, s5_b_im, s5_c_re, s5_c_im, s5_d, s5_w_glu, s5_b_glu, lru_conv_w, lru_conv_b, lru_w_a, lru_b_a, lru_w_x, lru_b_x, lru_lam, rw_mu, rw_w_r, rw_w_k, rw_w_v, rw_w_o, rw_w0, rw_w_l1, rw_w_l2, rw_a0, rw_a_l1, rw_a_l2, rw_v0, rw_v_l1, rw_v_l2, rw_g_l1, rw_g_l2, rw_k_k, rw_k_a, rw_r_k, rw_lnx_w, rw_lnx_b, moe_router_g, moe_router_g_b, moe_router_e, moe_router_e_b, moe_w_gate, moe_w_up, moe_w_down):
    bsz, seq, d = x.shape
    n_meta = meta_tokens.shape[0]
    depth = norm_mix.shape[0]
    ltot = n_meta + seq
    lp = -(-ltot // SEQ_ALIGN) * SEQ_ALIGN
    s5w = s5_w_glu.shape[-1]
    lruw = lru_lam.shape[-1]
    meta = jnp.broadcast_to(meta_tokens.astype(F32)[None], (bsz, n_meta, d))
    h = jnp.concatenate([meta, x.astype(F32), jnp.zeros((bsz, lp - ltot, d), F32)], axis=1).reshape(bsz * lp, d)
    v_first = None
    for layer in range(depth):
        j = layer // 2
        router = _router_table(moe_router_g[layer], moe_router_g_b[layer], moe_router_e[layer], moe_router_e_b[layer])
        if layer % 2 == 0:
            u, gate, rec = _ab_in(h, norm_mix[layer], ab_w_in[j], s5w, lruw)
            tables = _s5_tables(s5_lam_re[j], s5_lam_im[j], s5_log_dt[j], s5_b_re[j], s5_b_im[j],
                                s5_c_re[j], s5_c_im[j], s5_d[j])
            y5 = _s5_scan(u, tables, bsz, lp)
            lru = _lru(rec, gate, lru_conv_w[j], lru_conv_b[j], lru_w_a[j], lru_b_a[j], lru_w_x[j], lru_b_x[j],
                       lru_lam[j], bsz, lp)
            h, xn, rt = _ab_out(y5, lru, h, s5_w_glu[j], s5_b_glu[j], ab_norm_a[j], ab_norm_b[j], ab_w_out[j],
                                norm_ffn[layer], router)
        else:
            v_res = (rw_v0[j - 1], rw_v_l1[j - 1], rw_v_l2[j - 1]) if j > 0 else None
            r, k, v, w, a, g = _rw_pre(h, norm_mix[layer], rw_mu[j], rw_w_r[j], rw_w_k[j], rw_w_v[j], rw_w0[j],
                                       rw_w_l1[j], rw_w_l2[j], rw_a0[j], rw_a_l1[j], rw_a_l2[j], rw_g_l1[j],
                                       rw_g_l2[j], lp, v_first, v_res)
            if v_first is None:
                v_first = v
            y = _rw_scan(r, w, k, v, a, rw_k_k[j], rw_k_a[j], rw_r_k[j].reshape(-1), rw_lnx_w[j], rw_lnx_b[j],
                         bsz, lp)
            h, xn, rt = _rw_out(y, g, h, rw_w_o[j], norm_ffn[layer], router)
        h = _moe(h, xn, rt, moe_w_gate, moe_w_up, moe_w_down, layer)
    return _final_norm(h, norm_final, x.dtype, bsz, lp, n_meta, seq)
```

```python
import functools
import math

import jax
import jax.numpy as jnp
from jax import lax
from jax.experimental import pallas as pl
from jax.experimental.pallas import tpu as pltpu

F32 = jnp.float32
BF16 = jnp.bfloat16
HI = lax.Precision.HIGHEST

RMS_EPS = 1e-6
LNX_EPS = 64e-5
N_META = 16
SEQ_ALIGN = 64
S5_CHUNK = 16
S5_LANES = 128
RW_CHUNK = 64
RW_HEAD = 64
RW_PAIRS = 2
RW_UNROLL = 11
LRU_C = 8.0
N_GROUPS = 4
EXPERTS_PER_GROUP = 4
N_EXPERTS = N_GROUPS * EXPERTS_PER_GROUP
MOE_TILE = 512
ROUTE_LANES = 128
MOE_ROW_DTYPE = BF16
VMEM_LIMIT = 56 * 1024 * 1024


def _cparams(sem):
    return pltpu.CompilerParams(dimension_semantics=sem, vmem_limit_bytes=VMEM_LIMIT)


def _pick_tile(n, target):
    best = 8
    for t in range(8, min(n, target) + 1, 8):
        if n % t == 0:
            best = t
    return best


def _const_spec(shape):
    nd = len(shape)
    return pl.BlockSpec(shape, lambda *_: (0,) * nd)


def _rms(x, g):
    return x * lax.rsqrt(jnp.mean(x * x, axis=-1, keepdims=True) + RMS_EPS) * g


def _gelu(x):
    return 0.5 * x * (1.0 + jnp.tanh(math.sqrt(2.0 / math.pi) * (x + 0.044715 * (x * x * x))))


def _sigmoid(x):
    return 1.0 / (1.0 + jnp.exp(-x))


def _softplus(x):
    return jnp.maximum(x, 0.0) + jnp.log(1.0 + jnp.exp(-jnp.abs(x)))


def _bdot(a, b):
    return jnp.dot(a.astype(BF16), b.astype(BF16), preferred_element_type=F32)


def _hdot(a, b):
    return jnp.dot(a, b, preferred_element_type=F32, precision=HI)


def _dot3(a, b):
    m = a.shape[0]
    a_hi = a.astype(BF16)
    b_hi = b.astype(BF16)
    a_lo = (a - a_hi.astype(F32)).astype(BF16)
    b_lo = (b - b_hi.astype(F32)).astype(BF16)
    dot = functools.partial(jnp.dot, preferred_element_type=F32)
    top = dot(jnp.concatenate([a_hi, a_lo], axis=0), b_hi)
    return top[:m] + (top[m:] + dot(a_hi, b_lo))


def _dot3_many(a_list, b_list):
    m = a_list[0].shape[0]
    dot = functools.partial(jnp.dot, preferred_element_type=F32)
    a_hi = [a.astype(BF16) for a in a_list]
    b_hi = [b.astype(BF16) for b in b_list]
    a_lo = [(a - h.astype(F32)).astype(BF16) for a, h in zip(a_list, a_hi)]
    b_lo = [(b - h.astype(F32)).astype(BF16) for b, h in zip(b_list, b_hi)]
    top = [dot(jnp.concatenate([h, l], axis=0), b) for h, l, b in zip(a_hi, a_lo, b_hi)]
    low = [dot(h, b) for h, b in zip(a_hi, b_lo)]
    return [t[:m] + (t[m:] + l) for t, l in zip(top, low)]


def _split_dot_many(xs, m, rhs=False, pieces=3):
    accs = [None] * len(xs)
    xs = list(xs)
    for _ in range(pieces):
        his = [x.astype(BF16) for x in xs]
        parts = [jnp.dot(m, hi, preferred_element_type=F32) if rhs else jnp.dot(hi, m, preferred_element_type=F32)
                 for hi in his]
        accs = [p if a is None else a + p for a, p in zip(accs, parts)]
        xs = [x - hi.astype(F32) for x, hi in zip(xs, his)]
    return accs


def _ab_in_kernel(h_ref, g_ref, w_ref, u_ref, gate_ref, rec_ref, *, s5w, lruw):
    xn = _rms(h_ref[...], g_ref[...])
    z = _bdot(xn, w_ref[...])
    u_ref[...] = z[:, :s5w].astype(u_ref.dtype)
    gate_ref[...] = z[:, s5w:s5w + lruw]
    rec_ref[...] = z[:, s5w + lruw:]


def _ab_in(h, g, w_in, s5w, lruw):
    tp, d = h.shape
    tm = _pick_tile(tp, 512)
    row = lambda n: pl.BlockSpec((tm, n), lambda i: (i, 0))
    return pl.pallas_call(
        functools.partial(_ab_in_kernel, s5w=s5w, lruw=lruw),
        grid=(tp // tm,),
        in_specs=[row(d), _const_spec((1, d)), _const_spec(w_in.shape)],
        out_specs=[row(s5w), row(lruw), row(lruw)],
        out_shape=[jax.ShapeDtypeStruct((tp, s5w), BF16),
                   jax.ShapeDtypeStruct((tp, lruw), F32),
                   jax.ShapeDtypeStruct((tp, lruw), F32)],
        compiler_params=_cparams(("parallel",)),
        name="ab_in",
    )(h, g.reshape(1, d), w_in.astype(BF16))


def _s5_tables(lam_re, lam_im, log_dt, b_re, b_im, c_re, c_im, d_skip):
    g, p = lam_re.shape
    hh = b_re.shape[-1]
    c = S5_CHUNK
    lr, li = lam_re.astype(F32), lam_im.astype(F32)
    dt = jnp.exp(log_dt.astype(F32))[:, None]
    mag = jnp.exp(lr * dt)
    abar_r = mag * jnp.cos(li * dt)
    abar_i = mag * jnp.sin(li * dt)
    den = lr * lr + li * li
    zr = ((abar_r - 1.0) * lr + abar_i * li) / den
    zi = (abar_i * lr - (abar_r - 1.0) * li) / den
    bbar_r = zr[..., None] * b_re - zi[..., None] * b_im
    bbar_i = zr[..., None] * b_im + zi[..., None] * b_re
    bbr_t, bbi_t = jnp.swapaxes(bbar_r, 1, 2), jnp.swapaxes(bbar_i, 1, 2)
    cr_t, ci_t = jnp.swapaxes(c_re, 1, 2), jnp.swapaxes(c_im, 1, 2)

    def powers(steps):
        st = steps.astype(F32)[None, :, None]
        pmag = jnp.exp(st * (lr * dt)[:, None, :])
        ang = st * (li * dt)[:, None, :]
        return pmag * jnp.cos(ang), pmag * jnp.sin(ang)

    down = (c - 1) - jnp.arange(c)
    rev_r, rev_i = powers(down)
    m1_r = rev_r[:, :, None, :] * bbr_t[:, None] - rev_i[:, :, None, :] * bbi_t[:, None]
    m1_i = rev_r[:, :, None, :] * bbi_t[:, None] + rev_i[:, :, None, :] * bbr_t[:, None]
    car = c_re[:, None] * rev_r[:, :, None, :] - c_im[:, None] * rev_i[:, :, None, :]
    cai = c_re[:, None] * rev_i[:, :, None, :] + c_im[:, None] * rev_r[:, :, None, :]
    kern = (jnp.einsum('gqhp,gpj->gqjh', car, bbar_r, precision=HI)
            - jnp.einsum('gqhp,gpj->gqjh', cai, bbar_i, precision=HI))
    is_tau0 = (down == 0).astype(F32)[None, :, None, None]
    kern = kern + is_tau0 * (d_skip[:, None, None, :] * jnp.eye(hh, dtype=F32)[None, None])
    up_r, up_i = powers(jnp.arange(1, c + 1))
    up_r, up_i = jnp.swapaxes(up_r, 1, 2)[..., None], jnp.swapaxes(up_i, 1, 2)[..., None]
    m2_r = cr_t[:, :, None, :] * up_r - ci_t[:, :, None, :] * up_i
    m2_i = -(cr_t[:, :, None, :] * up_i + ci_t[:, :, None, :] * up_r)
    adv_r, adv_i = powers(jnp.full((1,), c))
    gb = S5_LANES // hh
    nj = g // gb

    def rows_sgh(x):
        w = x.shape[-1]
        return jnp.transpose(x.reshape(nj, gb, c, hh, w), (0, 2, 1, 3, 4)).reshape(nj, c * gb * hh, w)

    def place(base, spread, row_group, col_group):
        out = jnp.einsum('jrw,wc->jrc', base.astype(BF16), spread.astype(BF16), preferred_element_type=BF16)
        rg = row_group(lax.broadcasted_iota(jnp.int32, out.shape[1:], 0))
        cg = col_group(lax.broadcasted_iota(jnp.int32, out.shape[1:], 1))
        return jnp.where(rg == cg, out, jnp.zeros((), BF16))

    grp_sgh = lambda r: (r // hh) % gb
    rep = lambda w: jnp.tile(jnp.eye(w, dtype=F32), (1, gb))
    m1 = jnp.concatenate([place(rows_sgh(m), rep(p), grp_sgh, lambda col: col // p) for m in (m1_r, m1_i)], axis=-1)
    krev = place(rows_sgh(kern), rep(hh), grp_sgh, lambda col: col // hh)
    ri = lax.broadcasted_iota(jnp.int32, (c * hh, c * gb * hh), 0)
    ci = lax.broadcasted_iota(jnp.int32, (c * hh, c * gb * hh), 1)
    spread_th = ((ri // hh == ci // (gb * hh)) & (ri % hh == ci % hh)).astype(F32)
    m2 = jnp.concatenate([place(m.reshape(nj, gb * p, c * hh), spread_th, lambda r: r // p,
                                lambda col: (col // hh) % gb) for m in (m2_r, m2_i)], axis=1)
    return (m1.astype(BF16), krev.astype(BF16), m2.astype(BF16),
            adv_r.reshape(nj, 1, gb * p), adv_i.reshape(nj, 1, gb * p))


def _s5_kernel(u_ref, m1_ref, kr_ref, m2_ref, ar_ref, ai_ref, y_ref, xe_ref, xin_ref, st_ref, *, n_chunks, nb):
    nl = S5_LANES
    csz = S5_CHUNK
    sw = ar_ref.shape[-1]

    @pl.when(pl.program_id(1) == 0)
    def _():
        st_ref[...] = jnp.zeros(st_ref.shape, F32)

    u = u_ref[0]
    xe_ref[...] = jnp.dot(u, m1_ref[0], preferred_element_type=F32)
    ar = jnp.broadcast_to(ar_ref[0], (nb, sw))
    ai = jnp.broadcast_to(ai_ref[0], (nb, sw))

    def body(c, carry):
        sr, si = carry
        off = pl.multiple_of(c * nb, nb)
        xin_ref[pl.ds(off, nb), :] = jnp.concatenate([sr, si], axis=1)
        e = xe_ref[pl.ds(off, nb), :]
        return (ar * sr - ai * si + e[:, :sw], ar * si + ai * sr + e[:, sw:])

    sr, si = lax.fori_loop(0, n_chunks, body, (st_ref[:, :sw], st_ref[:, sw:]))
    st_ref[:, :sw] = sr
    st_ref[:, sw:] = si
    y_ref[0] = _bdot(xin_ref[...], m2_ref[0])
    for t in range(csz):
        y_ref[0, :, t * nl:(t + 1) * nl] += jnp.dot(u[:, :(t + 1) * nl], kr_ref[0, (csz - 1 - t) * nl:, :],
                                                    preferred_element_type=F32)


def _s5_scan(u, tables, bsz, lp):
    m1, krev, m2, adv_r, adv_i = tables
    nj, kin, sw2 = m1.shape
    c = S5_CHUNK
    nl = S5_LANES
    nc = lp // c
    cpt = max(d for d in range(1, nc + 1) if nc % d == 0 and d * bsz <= 512)
    rows = cpt * bsz
    ug = jnp.transpose(u.reshape(bsz, nc, c, nj, nl), (3, 1, 0, 2, 4)).reshape(nj, nc * bsz, kin)
    y = pl.pallas_call(
        functools.partial(_s5_kernel, n_chunks=cpt, nb=bsz),
        grid=(nj, nc // cpt),
        in_specs=[pl.BlockSpec((1, rows, kin), lambda j, r: (j, r, 0)),
                  pl.BlockSpec((1, kin, sw2), lambda j, r: (j, 0, 0)),
                  pl.BlockSpec((1, kin, nl), lambda j, r: (j, 0, 0)),
                  pl.BlockSpec((1, sw2, kin), lambda j, r: (j, 0, 0)),
                  pl.BlockSpec((1, 1, sw2 // 2), lambda j, r: (j, 0, 0)),
                  pl.BlockSpec((1, 1, sw2 // 2), lambda j, r: (j, 0, 0))],
        out_specs=pl.BlockSpec((1, rows, kin), lambda j, r: (j, r, 0)),
        out_shape=jax.ShapeDtypeStruct((nj, nc * bsz, kin), F32),
        scratch_shapes=[pltpu.VMEM((rows, sw2), F32), pltpu.VMEM((rows, sw2), F32), pltpu.VMEM((bsz, sw2), F32)],
        compiler_params=_cparams(("parallel", "arbitrary")),
        name="s5_scan",
    )(ug, m1, krev, m2, adv_r, adv_i)
    y = jnp.transpose(y.reshape(nj, nc, bsz, c, nl), (2, 1, 3, 0, 4))
    return y.reshape(bsz * lp, nj * nl)


def _lru_kernel(rec_ref, gate_ref, cw_ref, cb_ref, wa_ref, ba_ref, wx_ref, bx_ref, lam_ref,
                o_ref, ext_ref, a_ref, b_ref, h_ref, *, tl):
    t = pl.program_id(1)

    @pl.when(t == 0)
    def _():
        ext_ref[0:8, :] = jnp.zeros((8, ext_ref.shape[1]), F32)
        h_ref[...] = jnp.zeros(h_ref.shape, F32)

    x = rec_ref[...]
    ext_ref[8:, :] = x
    xc = cb_ref[...] + cw_ref[3:4, :] * x
    for k in range(3):
        xc = xc + cw_ref[k:k + 1, :] * ext_ref[5 + k:5 + k + tl, :]
    ext_ref[0:8, :] = x[tl - 8:, :]
    r = _sigmoid(_bdot(xc, wa_ref[...]) + ba_ref[...])
    i = _sigmoid(_bdot(xc, wx_ref[...]) + bx_ref[...])
    log_a = (-LRU_C) * r * _softplus(-lam_ref[...])
    a = jnp.exp(log_a)
    a_ref[...] = a
    b_ref[...] = jnp.sqrt(1.0 - a * a) * (i * xc)

    sub = 8
    unroll = max(u for u in (4, 2, 1) if (tl // sub) % u == 0)
    row = lax.broadcasted_iota(jnp.int32, (sub, a.shape[1]), 0)

    def body(j, h):
        offs = [pl.multiple_of((j * unroll + q) * sub, sub) for q in range(unroll)]
        ab = [a_ref[pl.ds(o, sub), :] for o in offs]
        bb = [b_ref[pl.ds(o, sub), :] for o in offs]
        for dist in (1, 2, 4):
            keep = row >= dist
            bb = [jnp.where(keep, x * pltpu.roll(y, dist, axis=0) + y, y) for x, y in zip(ab, bb)]
            ab = [jnp.where(keep, x * pltpu.roll(x, dist, axis=0), x) for x in ab]
        for o, x, y in zip(offs, ab, bb):
            hb = x * h + y
            o_ref[pl.ds(o, sub), :] = hb * _gelu(gate_ref[pl.ds(o, sub), :])
            h = hb[sub - 1:sub, :]
        return h

    h_ref[...] = lax.fori_loop(0, tl // (sub * unroll), body, h_ref[...])


def _lru(rec, gate, conv_w, conv_b, w_a, b_a, w_x, b_x, lam, bsz, lp):
    tp, w = rec.shape
    tl = _pick_tile(lp, 1056)
    nt = lp // tl
    heads, hd, _ = w_a.shape

    def dense(wb):
        eye = jnp.eye(heads, dtype=F32)
        return jnp.einsum('hij,hg->higj', wb, eye).reshape(w, w).astype(BF16)

    row = pl.BlockSpec((tl, w), lambda b, t: (b * nt + t, 0))
    vec = _const_spec((1, w))
    return pl.pallas_call(
        functools.partial(_lru_kernel, tl=tl),
        grid=(bsz, nt),
        in_specs=[row, row, _const_spec((4, w)), vec, _const_spec((w, w)), vec, _const_spec((w, w)), vec, vec],
        out_specs=row,
        out_shape=jax.ShapeDtypeStruct((tp, w), F32),
        scratch_shapes=[pltpu.VMEM((tl + 8, w), F32), pltpu.VMEM((tl, w), F32),
                        pltpu.VMEM((tl, w), F32), pltpu.VMEM((1, w), F32)],
        compiler_params=_cparams(("parallel", "arbitrary")),
        name="rglru",
    )(rec, gate, conv_w, conv_b.reshape(1, w), dense(w_a), b_a.reshape(1, w), dense(w_x), b_x.reshape(1, w),
      lam.reshape(1, w))


def _ab_out_kernel(y5_ref, lru_ref, h_ref, wglu_ref, bglu_ref, na_ref, nb_ref, wo_ref, gf_ref, wr_ref, br_ref,
                   o_ref, xn_ref, rt_ref, *, s5w):
    y = _gelu(y5_ref[...])
    ya = y * _sigmoid(_bdot(y, wglu_ref[...]) + bglu_ref[...])
    ya = _rms(ya, na_ref[...])
    yb = _rms(lru_ref[...], nb_ref[...])
    h = h_ref[...] + _bdot(ya, wo_ref[:s5w, :]) + _bdot(yb, wo_ref[s5w:, :])
    o_ref[...] = h
    xn, rt_ref[...] = _route_tile(h, gf_ref[...], wr_ref[0], wr_ref[1], br_ref[...])
    xn_ref[...] = xn.astype(xn_ref.dtype)


def _ab_out(y5, lru, h, w_glu, b_glu, norm_a, norm_b, w_out, g_ffn, router):
    tp, d = h.shape
    s5w, lruw = y5.shape[1], lru.shape[1]
    wr, br = router
    tm = _pick_tile(tp, 512)
    row = lambda n: pl.BlockSpec((tm, n), lambda i: (i, 0))
    return pl.pallas_call(
        functools.partial(_ab_out_kernel, s5w=s5w),
        grid=(tp // tm,),
        in_specs=[row(s5w), row(lruw), row(d), _const_spec((s5w, s5w)), _const_spec((1, s5w)),
                  _const_spec((1, s5w)), _const_spec((1, lruw)), _const_spec(w_out.shape),
                  _const_spec((1, d)), _const_spec(wr.shape), _const_spec(br.shape)],
        out_specs=[row(d), row(d), row(ROUTE_LANES)],
        out_shape=[jax.ShapeDtypeStruct((tp, d), F32), jax.ShapeDtypeStruct((tp, d), F32),
                   jax.ShapeDtypeStruct((tp, ROUTE_LANES), F32)],
        compiler_params=_cparams(("parallel",)),
        name="ab_out",
    )(y5, lru, h, w_glu.astype(BF16), b_glu.reshape(1, s5w), norm_a.reshape(1, s5w),
      norm_b.reshape(1, lruw), w_out.astype(BF16), g_ffn.reshape(1, d), wr, br)


def _route_tile(h, g, wr_hi, wr_lo, br):
    xn = _rms(h, g)
    m = xn.shape[0]
    x_hi = xn.astype(BF16)
    x_lo = (xn - x_hi.astype(F32)).astype(BF16)
    top = jnp.dot(jnp.concatenate([x_hi, x_lo], axis=0), wr_hi, preferred_element_type=F32)
    lg = top[:m] + (top[m:] + jnp.dot(x_hi, wr_lo, preferred_element_type=F32)) + br
    lane = lax.broadcasted_iota(jnp.int32, lg.shape, 1).astype(F32)
    big = float(lg.shape[1])
    neg = -jnp.inf
    gl = jnp.where(lane < N_GROUPS, lg, neg)
    mg = jnp.max(gl, axis=-1, keepdims=True)
    gidx = jnp.min(jnp.where(gl == mg, lane, big), axis=-1, keepdims=True)
    pg_sel = 1.0 / jnp.sum(jnp.exp(gl - mg), axis=-1, keepdims=True)
    lo = N_GROUPS + EXPERTS_PER_GROUP * gidx
    le = jnp.where(lane >= lo, jnp.where(lane < lo + EXPERTS_PER_GROUP, lg, neg), neg)
    v1 = jnp.max(le, axis=-1, keepdims=True)
    i1 = jnp.min(jnp.where(le == v1, lane, big), axis=-1, keepdims=True)
    le2 = jnp.where(lane == i1, neg, le)
    v2 = jnp.max(le2, axis=-1, keepdims=True)
    i2 = jnp.min(jnp.where(le2 == v2, lane, big), axis=-1, keepdims=True)
    e2 = jnp.exp(v2 - v1)
    w1 = pg_sel / (1.0 + e2)
    w2 = w1 * e2
    rt = jnp.where(lane == 0.0, w1, jnp.where(lane == 1.0, w2, jnp.where(
        lane == 2.0, i1 - N_GROUPS, jnp.where(lane == 3.0, i2 - N_GROUPS, 0.0))))
    return xn, rt


def _router_table(wr_g, br_g, wr_e, br_e):
    d = wr_g.shape[0]
    wr = jnp.zeros((d, ROUTE_LANES), F32).at[:, :N_GROUPS].set(wr_g).at[:, N_GROUPS:N_GROUPS + N_EXPERTS].set(wr_e)
    br = jnp.zeros((1, ROUTE_LANES), F32).at[0, :N_GROUPS].set(br_g).at[0, N_GROUPS:N_GROUPS + N_EXPERTS].set(br_e)
    wr_hi = wr.astype(BF16)
    wr_lo = (wr - wr_hi.astype(F32)).astype(BF16)
    return jnp.stack([wr_hi, wr_lo]), br


def _gmm_kernel(te_ref, tv_ref, xa_ref, xb_ref, wg_ref, wu_ref, wd_ref, o_ref, wg_bf, wu_bf, wd_bf, *, n_half):
    i = pl.program_id(0)

    @pl.when(jnp.logical_or(i == 0, te_ref[i] != te_ref[jnp.maximum(i - 1, 0)]))
    def _():
        wg_bf[...] = wg_ref[0, 0].astype(BF16)
        wu_bf[...] = wu_ref[0, 0].astype(BF16)
        wd_bf[...] = wd_ref[0, 0].astype(BF16)

    @pl.when(tv_ref[i] != 0)
    def _():
        x = jnp.where(i < n_half, xa_ref[...], xb_ref[...]).astype(BF16)
        hg = jnp.dot(x, wg_bf[...], preferred_element_type=F32)
        hu = jnp.dot(x, wu_bf[...], preferred_element_type=F32)
        hid = hg * _sigmoid(hg) * hu
        o_ref[...] = _bdot(hid, wd_bf[...]).astype(o_ref.dtype)

    @pl.when(tv_ref[i] == 0)
    def _():
        o_ref[...] = jnp.zeros(o_ref.shape, o_ref.dtype)


def _invert_rows_kernel(dest_ref, init_ref, src_ref, *, unroll):
    del init_ref
    n_tok = dest_ref.shape[0] // 2

    def put(t):
        src_ref[dest_ref[2 * t]] = t
        src_ref[dest_ref[2 * t + 1]] = t

    def body(i, _):
        for u in range(unroll):
            put(i * unroll + u)
        return 0

    lax.fori_loop(0, n_tok // unroll, body, 0)
    for t in range(n_tok - n_tok % unroll, n_tok):
        put(t)


def _invert_rows(dest, init):
    smem = pl.BlockSpec(memory_space=pltpu.SMEM)
    return pl.pallas_call(
        functools.partial(_invert_rows_kernel, unroll=8),
        in_specs=[smem, smem],
        out_specs=smem,
        out_shape=jax.ShapeDtypeStruct(init.shape, jnp.int32),
        input_output_aliases={1: 0},
        name="moe_invert",
    )(dest, init)


def _moe(h, xn, rt, w_gate, w_up, w_down, layer):
    tp, d = h.shape
    f = w_gate.shape[-1]
    gate = rt[:, 0:2]
    eid = rt[:, 2:4].astype(jnp.int32)
    tmm = MOE_TILE
    na = 2 * tp
    e_flat = eid.reshape(na)
    onehot = (e_flat[:, None] == jnp.arange(N_EXPERTS, dtype=jnp.int32)[None, :]).astype(jnp.int32)
    csum = jnp.cumsum(onehot, axis=0)
    counts = csum[-1]
    padded = ((counts + tmm - 1) // tmm) * tmm
    ends = jnp.cumsum(padded)
    starts = ends - padded
    dest = jnp.sum(onehot * (csum + (starts - 1)[None, :]), axis=1)
    n_half = (-(-na // tmm) + N_EXPERTS + 1) // 2
    n_tiles = 2 * n_half
    nrows = n_tiles * tmm
    tile_start = jnp.arange(n_tiles, dtype=jnp.int32) * tmm
    tile_e = jnp.sum((ends[None, :] <= tile_start[:, None]).astype(jnp.int32), axis=1)
    tile_v = (tile_e < N_EXPERTS).astype(jnp.int32)
    tile_e = jnp.minimum(tile_e, N_EXPERTS - 1)
    src = _invert_rows(dest, jnp.arange(nrows, dtype=jnp.int32) % tp)
    take = lambda a, i: a.at[i].get(mode="promise_in_bounds")
    xs_a = take(xn, src[:n_half * tmm])
    xs_b = take(xn, src[n_half * tmm:])
    ys = pl.pallas_call(
        functools.partial(_gmm_kernel, n_half=n_half),
        grid_spec=pltpu.PrefetchScalarGridSpec(
            num_scalar_prefetch=2,
            grid=(n_tiles,),
            in_specs=[pl.BlockSpec((tmm, d), lambda i, te, tv: (jnp.minimum(i, n_half - 1), 0)),
                      pl.BlockSpec((tmm, d), lambda i, te, tv: (jnp.maximum(i - n_half, 0), 0)),
                      pl.BlockSpec((1, 1, d, f), lambda i, te, tv: (layer, te[i], 0, 0)),
                      pl.BlockSpec((1, 1, d, f), lambda i, te, tv: (layer, te[i], 0, 0)),
                      pl.BlockSpec((1, 1, f, d), lambda i, te, tv: (layer, te[i], 0, 0))],
            out_specs=pl.BlockSpec((tmm, d), lambda i, te, tv: (i, 0)),
            scratch_shapes=[pltpu.VMEM((d, f), BF16), pltpu.VMEM((d, f), BF16), pltpu.VMEM((f, d), BF16)],
        ),
        out_shape=jax.ShapeDtypeStruct((nrows, d), MOE_ROW_DTYPE),
        compiler_params=_cparams(("arbitrary",)),
        name="moe_gmm",
    )(tile_e, tile_v, xs_a, xs_b, w_gate, w_up, w_down)
    d2 = dest.reshape(tp, 2)
    return h + gate[:, 0:1] * take(ys, d2[:, 0]) + gate[:, 1:2] * take(ys, d2[:, 1])


def _rw_pre_kernel(*refs, tm, lp, has_vres):
    if has_vres:
        (h_ref, hp_ref, g_ref, mu_ref, wr_ref, wk_ref, wv_ref, w0_ref, wl1_ref, wl2_ref,
         a0_ref, al1_ref, al2_ref, gl1_ref, gl2_ref, vf_ref, v0_ref, vl1_ref, vl2_ref,
         r_ref, k_ref, v_ref, w_ref, a_ref, gg_ref) = refs
    else:
        (h_ref, hp_ref, g_ref, mu_ref, wr_ref, wk_ref, wv_ref, w0_ref, wl1_ref, wl2_ref,
         a0_ref, al1_ref, al2_ref, gl1_ref, gl2_ref,
         r_ref, k_ref, v_ref, w_ref, a_ref, gg_ref) = refs
    i = pl.program_id(0)
    g = g_ref[...]
    x = _rms(h_ref[...], g)
    xp8 = _rms(hp_ref[...], g)
    row = lax.broadcasted_iota(jnp.int32, x.shape, 0)
    prev = jnp.where(row == 0, jnp.broadcast_to(xp8[7:8, :], x.shape), pltpu.roll(x, 1, axis=0))
    first = lax.rem(lp - lax.rem(i * tm, lp), lp)
    prev = jnp.where(row == first, 0.0, prev)
    xx = prev - x
    xr, xw, xk, xv, xa, xg = [x + xx * mu_ref[j:j + 1, :] for j in range(6)]
    r_ref[...] = _bdot(xr, wr_ref[...]).astype(r_ref.dtype)
    k_ref[...] = _bdot(xk, wk_ref[...]).astype(k_ref.dtype)
    v = _bdot(xv, wv_ref[...])
    if has_vres:
        mix = _sigmoid(v0_ref[...] + _bdot(_bdot(xv, vl1_ref[...]), vl2_ref[...]))
        v = v + (vf_ref[...] - v) * mix
    v_ref[...] = v.astype(v_ref.dtype)
    w_ref[...] = -_softplus(-(w0_ref[...] + _bdot(jnp.tanh(_bdot(xw, wl1_ref[...])), wl2_ref[...]))) - 0.5
    a_ref[...] = _sigmoid(a0_ref[...] + _bdot(_bdot(xa, al1_ref[...]), al2_ref[...]))
    gg_ref[...] = _bdot(_sigmoid(_bdot(xg, gl1_ref[...])), gl2_ref[...]).astype(gg_ref.dtype)


def _rw_pre(h, g, mu, w_r, w_k, w_v, w0, w_l1, w_l2, a0, a_l1, a_l2, g_l1, g_l2, lp, v_first, v_res):
    tp, d = h.shape
    tm = _pick_tile(tp, min(512, lp))
    has_vres = v_res is not None
    row = pl.BlockSpec((tm, d), lambda i: (i, 0))
    prev8 = pl.BlockSpec((8, d), lambda i: (jnp.maximum(i * (tm // 8) - 1, 0), 0))
    vec = _const_spec((1, d))
    mu8 = jnp.zeros((8, d), F32).at[:6].set(mu)
    bf = lambda w: w.astype(BF16)
    ins = [h, h, g.reshape(1, d), mu8, bf(w_r), bf(w_k), bf(w_v), w0.reshape(1, d), bf(w_l1), bf(w_l2),
           a0.reshape(1, d), bf(a_l1), bf(a_l2), bf(g_l1), bf(g_l2)]
    specs = [row, prev8, vec, _const_spec((8, d))] + [_const_spec((d, d))] * 3 + [
        vec, _const_spec(w_l1.shape), _const_spec(w_l2.shape),
        vec, _const_spec(a_l1.shape), _const_spec(a_l2.shape), _const_spec(g_l1.shape), _const_spec(g_l2.shape)]
    if has_vres:
        v0, v_l1, v_l2 = v_res
        ins += [v_first, v0.reshape(1, d), bf(v_l1), bf(v_l2)]
        specs += [row, vec, _const_spec(v_l1.shape), _const_spec(v_l2.shape)]
    return pl.pallas_call(
        functools.partial(_rw_pre_kernel, tm=tm, lp=lp, has_vres=has_vres),
        grid=(tp // tm,),
        in_specs=specs,
        out_specs=[row] * 6,
        out_shape=[jax.ShapeDtypeStruct((tp, d), dt) for dt in (BF16, BF16, BF16, F32, F32, BF16)],
        compiler_params=_cparams(("parallel",)),
        name="rwkv_pre",
    )(*ins)


def _rw_scan_kernel(r_ref, w_ref, k_ref, v_ref, a_ref, kk_ref, ka_ref, rk_ref, lw_ref, lb_ref,
                    o_ref, g_scr, h_scr, q_scr, y0_scr, *, n_chunks, unroll, pairs):
    c = RW_CHUNK
    nl = 2 * RW_HEAD
    c2 = 2 * c
    ri = lax.broadcasted_iota(jnp.int32, (c2, nl), 0)
    ci = lax.broadcasted_iota(jnp.int32, (c2, nl), 1)
    own = ((ri >= c) == (ci >= RW_HEAD)).astype(F32)
    t_in = jnp.bitwise_and(ri, c - 1)
    s_in = jnp.bitwise_and(ci, c - 1)
    strict = jnp.where(s_in < t_in, own, 0.0)
    incl = jnp.where(s_in <= t_in, own, 0.0)
    causal2 = jnp.concatenate([strict, incl], axis=0)
    eye_l = (ri == ci).astype(F32)
    same_head = own.astype(BF16)
    ti = lax.broadcasted_iota(jnp.int32, (c, c), 0)
    si = lax.broadcasted_iota(jnp.int32, (c, c), 1)
    tril_c = (si <= ti).astype(BF16)
    lanes = lambda p: slice(p * nl, (p + 1) * nl)

    def stack(x):
        return jnp.concatenate([x, x], axis=0) * own

    def offset(ch):
        return ch * c if isinstance(ch, int) else pl.multiple_of(ch * c, c)

    def groups(fn):
        for p in range(pairs):
            def body(i, _):
                fn([i * unroll + q for q in range(unroll)], p)
                return 0
            lax.fori_loop(0, n_chunks // unroll, body, 0)
            if n_chunks % unroll:
                fn(list(range(n_chunks - n_chunks % unroll, n_chunks)), p)

    def prep(chs, p):
        ln = lanes(p)
        kk_w, ka_w = kk_ref[:, ln], ka_ref[:, ln]
        offs = [offset(ch) for ch in chs]
        r = [r_ref[pl.ds(o, c), ln] for o in offs]
        k = [k_ref[pl.ds(o, c), ln] for o in offs]
        v = [v_ref[pl.ds(o, c), ln] for o in offs]
        a = [a_ref[pl.ds(o, c), ln] for o in offs]
        logw = [-jnp.exp(w_ref[pl.ds(o, c), ln]) for o in offs]
        kk = [x * kk_w for x in k]
        ss = _split_dot_many([x * x for x in kk], same_head, pieces=2)
        kk = [x / jnp.maximum(jnp.sqrt(q), 1e-12) for x, q in zip(kk, ss)]
        kmod = [x * (1.0 + (y - 1.0) * ka_w) for x, y in zip(k, a)]
        cum = _split_dot_many(logw, tril_c, rhs=True, pieces=2)
        p_incl = [jnp.exp(x) for x in cum]
        p_inv = [jnp.exp(-x) for x in cum]
        p_end = [jnp.exp(x[c - 1:c, :] - x) for x in cum]
        kka = [x * y for x, y in zip(kk, a)]
        a_s = [stack(-x * jnp.exp(y - z)) for x, y, z in zip(kk, cum, logw)]
        r_s = [stack(x * y) for x, y in zip(r, p_incl)]
        v_s = [stack(x).astype(BF16) for x in v]
        lhs = [jnp.concatenate([x, y], axis=0).astype(BF16) for x, y in zip(a_s, r_s)]
        rhs = [jnp.concatenate([stack(x * z), stack(y * z)], axis=0).astype(BF16) for x, y, z in zip(kka, kmod, p_inv)]
        big = [lax.dot_general(x, y, (((1,), (1,)), ((), ())), preferred_element_type=F32) for x, y in zip(lhs, rhs)]
        lpow = [(x[:c2, :c2] * strict).astype(BF16) for x in big]
        a_rb = [x[c2:, :c2] * incl for x in big]
        avk = [_bdot(x[:, c2:] * causal2, w) for x, w in zip(big, v_s)]
        x = [jnp.concatenate([p, q[:c2]], axis=1) for p, q in zip(a_s, avk)]
        x = [p + _bdot(q, p) for p, q in zip(x, lpow)]
        for _ in range(5):
            lpow = [jnp.dot(q, q, preferred_element_type=F32).astype(BF16) for q in lpow]
            x = [p + _bdot(q, p) for p, q in zip(x, lpow)]
        xb = [p.astype(BF16) for p in x]
        bh_t = [stack(p * q).T for p, q in zip(kka, p_end)]
        kh_t = [stack(p * q).T for p, q in zip(kmod, p_end)]
        both = [_bdot(jnp.concatenate([u, p], axis=0), z) for u, p, z in zip(a_rb, bh_t, xb)]
        qy = [jnp.concatenate([p, q[c2:]], axis=1) + z[:c2] for p, q, z in zip(r_s, avk, both)]
        gh = [z[c2:] + jnp.concatenate([eye_l * q[c - 1:c, :], _bdot(u, w)], axis=1)
              for z, q, u, w in zip(both, p_incl, kh_t, v_s)]
        for ch, u, q in zip(chs, qy, gh):
            q_scr[p, ch] = u[:c, :nl] + u[c:, :nl]
            y0_scr[p, ch] = u[:c, nl:] + u[c:, nl:]
            g_scr[p, ch] = q[:, :nl]
            h_scr[p, ch] = q[:, nl:]

    groups(prep)

    def advance(ch, states):
        both = _dot3_many([jnp.concatenate([q_scr[p, ch], g_scr[p, ch]], axis=0) for p in range(pairs)], states)
        return (tuple(both[p][c:] + h_scr[p, ch] for p in range(pairs)),
                tuple(both[p][:c] + y0_scr[p, ch] for p in range(pairs)))

    def finish(ch, ys):
        o = offset(ch)
        rk = []
        for p in range(pairs):
            ln = lanes(p)
            kmod = k_ref[pl.ds(o, c), ln] * (1.0 + (a_ref[pl.ds(o, c), ln] - 1.0) * ka_ref[:, ln])
            rk.append(r_ref[pl.ds(o, c), ln] * kmod * rk_ref[:, ln])
        sums = _split_dot_many([jnp.concatenate([u, u * u, q], axis=0) for u, q in zip(ys, rk)], same_head)
        for p, (u, s) in enumerate(zip(ys, sums)):
            ln = lanes(p)
            mean = s[:c] * (1.0 / RW_HEAD)
            var = s[c:2 * c] * (1.0 / RW_HEAD) - mean * mean
            o_ref[pl.ds(o, c), ln] = ((u - mean) * lax.rsqrt(var + LNX_EPS) * lw_ref[:, ln] + lb_ref[:, ln]
                                      + s[2 * c:] * v_ref[pl.ds(o, c), ln])

    def step(ch, carry):
        states, ys = carry
        finish(ch - 1, ys)
        return advance(ch, states)

    carry = advance(0, tuple(jnp.zeros((nl, nl), F32) for _ in range(pairs)))
    _, ys = lax.fori_loop(1, n_chunks, step, carry)
    finish(n_chunks - 1, ys)


def _rw_scan(r, w, k, v, a, k_k, k_a, r_k, lnx_w, lnx_b, bsz, lp):
    tp, d = r.shape
    nl = 2 * RW_HEAD
    pairs = RW_PAIRS
    bw = pairs * nl
    nhp = d // bw
    nc = lp // RW_CHUNK
    seq = pl.BlockSpec((lp, bw), lambda b, hp: (b, hp))
    vec = pl.BlockSpec((1, bw), lambda b, hp: (0, hp))
    c = RW_CHUNK
    return pl.pallas_call(
        functools.partial(_rw_scan_kernel, n_chunks=nc, unroll=RW_UNROLL, pairs=pairs),
        grid=(bsz, nhp),
        in_specs=[seq] * 5 + [vec] * 5,
        out_specs=seq,
        out_shape=jax.ShapeDtypeStruct((tp, d), F32),
        scratch_shapes=[pltpu.VMEM((pairs, nc, nl, nl), F32), pltpu.VMEM((pairs, nc, nl, nl), F32),
                        pltpu.VMEM((pairs, nc, c, nl), F32), pltpu.VMEM((pairs, nc, c, nl), F32)],
        compiler_params=_cparams(("parallel", "parallel")),
        name="rwkv_scan",
    )(r, w, k, v, a, k_k.reshape(1, d), k_a.reshape(1, d), r_k.reshape(1, d),
      lnx_w.reshape(1, d), lnx_b.reshape(1, d))


def _rw_out_kernel(y_ref, g_ref, h_ref, wo_ref, gf_ref, wr_ref, br_ref, o_ref, xn_ref, rt_ref):
    h = h_ref[...] + _bdot(y_ref[...] * g_ref[...], wo_ref[...])
    o_ref[...] = h
    xn, rt_ref[...] = _route_tile(h, gf_ref[...], wr_ref[0], wr_ref[1], br_ref[...])
    xn_ref[...] = xn.astype(xn_ref.dtype)


def _rw_out(y, g, h, w_o, g_ffn, router):
    tp, d = h.shape
    wr, br = router
    tm = _pick_tile(tp, 512)
    row = lambda n: pl.BlockSpec((tm, n), lambda i: (i, 0))
    return pl.pallas_call(
        _rw_out_kernel,
        grid=(tp // tm,),
        in_specs=[row(d), row(d), row(d), _const_spec((d, d)), _const_spec((1, d)), _const_spec(wr.shape),
                  _const_spec(br.shape)],
        out_specs=[row(d), row(d), row(ROUTE_LANES)],
        out_shape=[jax.ShapeDtypeStruct((tp, d), F32), jax.ShapeDtypeStruct((tp, d), F32),
                   jax.ShapeDtypeStruct((tp, ROUTE_LANES), F32)],
        compiler_params=_cparams(("parallel",)),
        name="rwkv_out",
    )(y, g, h, w_o.astype(BF16), g_ffn.reshape(1, d), wr, br)


def _final_norm_kernel(h_ref, g_ref, o_ref, *, first, rows):
    o_ref[0] = _rms(h_ref[first:first + rows, :], g_ref[...]).astype(o_ref.dtype)


def _final_norm(h, g, dtype, bsz, lp, first, rows):
    d = h.shape[1]
    return pl.pallas_call(
        functools.partial(_final_norm_kernel, first=first, rows=rows),
        grid=(bsz,),
        in_specs=[pl.BlockSpec((lp, d), lambda b: (b, 0)), _const_spec((1, d))],
        out_specs=pl.BlockSpec((1, rows, d), lambda b: (b, 0, 0)),
        out_shape=jax.ShapeDtypeStruct((bsz, rows, d), dtype),
        compiler_params=_cparams(("parallel",)),
        name="final_norm",
    )(h, g.reshape(1, d))


def kernel(x, meta_tokens, norm_mix, norm_ffn, norm_final, ab_w_in, ab_w_out, ab_norm_a, ab_norm_b, s5_lam_re, s5_lam_im, s5_log_dt, s5_b_re, s5_b_im, s5_c_re, s5_c_im, s5_d, s5_w_glu, s5_b_glu, lru_conv_w, lru_conv_b, lru_w_a, lru_b_a, lru_w_x, lru_b_x, lru_lam, rw_mu, rw_w_r, rw_w_k, rw_w_v, rw_w_o, rw_w0, rw_w_l1, rw_w_l2, rw_a0, rw_a_l1, rw_a_l2, rw_v0, rw_v_l1, rw_v_l2, rw_g_l1, rw_g_l2, rw_k_k, rw_k_a, rw_r_k, rw_lnx_w, rw_lnx_b, moe_router_g, moe_router_g_b, moe_router_e, moe_router_e_b, moe_w_gate, moe_w_up, moe_w_down):
    bsz, seq, d = x.shape
    n_meta = meta_tokens.shape[0]
    depth = norm_mix.shape[0]
    ltot = n_meta + seq
    lp = -(-ltot // SEQ_ALIGN) * SEQ_ALIGN
    s5w = s5_w_glu.shape[-1]
    lruw = lru_lam.shape[-1]
    meta = jnp.broadcast_to(meta_tokens.astype(F32)[None], (bsz, n_meta, d))
    h = jnp.concatenate([meta, x.astype(F32), jnp.zeros((bsz, lp - ltot, d), F32)], axis=1).reshape(bsz * lp, d)
    v_first = None
    for layer in range(depth):
        j = layer // 2
        router = _router_table(moe_router_g[layer], moe_router_g_b[layer], moe_router_e[layer], moe_router_e_b[layer])
        if layer % 2 == 0:
            u, gate, rec = _ab_in(h, norm_mix[layer], ab_w_in[j], s5w, lruw)
            tables = _s5_tables(s5_lam_re[j], s5_lam_im[j], s5_log_dt[j], s5_b_re[j], s5_b_im[j],
                                s5_c_re[j], s5_c_im[j], s5_d[j])
            y5 = _s5_scan(u, tables, bsz, lp)
            lru = _lru(rec, gate, lru_conv_w[j], lru_conv_b[j], lru_w_a[j], lru_b_a[j], lru_w_x[j], lru_b_x[j],
                       lru_lam[j], bsz, lp)
            h, xn, rt = _ab_out(y5, lru, h, s5_w_glu[j], s5_b_glu[j], ab_norm_a[j], ab_norm_b[j], ab_w_out[j],
                                norm_ffn[layer], router)
        else:
            v_res = (rw_v0[j - 1], rw_v_l1[j - 1], rw_v_l2[j - 1]) if j > 0 else None
            r, k, v, w, a, g = _rw_pre(h, norm_mix[layer], rw_mu[j], rw_w_r[j], rw_w_k[j], rw_w_v[j], rw_w0[j],
                                       rw_w_l1[j], rw_w_l2[j], rw_a0[j], rw_a_l1[j], rw_a_l2[j], rw_g_l1[j],
                                       rw_g_l2[j], lp, v_first, v_res)
            if v_first is None:
                v_first = v
            y = _rw_scan(r, w, k, v, a, rw_k_k[j], rw_k_a[j], rw_r_k[j].reshape(-1), rw_lnx_w[j], rw_lnx_b[j],
                         bsz, lp)
            h, xn, rt = _rw_out(y, g, h, rw_w_o[j], norm_ffn[layer], router)
        h = _moe(h, xn, rt, moe_w_gate, moe_w_up, moe_w_down, layer)
    return _final_norm(h, norm_final, x.dtype, bsz, lp, n_meta, seq)
```

```python
import functools
import math

import jax
import jax.numpy as jnp
from jax import lax
from jax.experimental import pallas as pl
from jax.experimental.pallas import tpu as pltpu

F32 = jnp.float32
BF16 = jnp.bfloat16
HI = lax.Precision.HIGHEST

RMS_EPS = 1e-6
LNX_EPS = 64e-5
N_META = 16
SEQ_ALIGN = 64
S5_CHUNK = 16
S5_LANES = 128
RW_CHUNK = 64
RW_HEAD = 64
RW_PAIRS = 2
RW_UNROLL = 11
LRU_C = 8.0
N_GROUPS = 4
EXPERTS_PER_GROUP = 4
N_EXPERTS = N_GROUPS * EXPERTS_PER_GROUP
MOE_TILE = 512
ROUTE_LANES = 128
MOE_ROW_DTYPE = BF16
VMEM_LIMIT = 56 * 1024 * 1024


def _cparams(sem):
    return pltpu.CompilerParams(dimension_semantics=sem, vmem_limit_bytes=VMEM_LIMIT)


def _pick_tile(n, target):
    best = 8
    for t in range(8, min(n, target) + 1, 8):
        if n % t == 0:
            best = t
    return best


def _const_spec(shape):
    nd = len(shape)
    return pl.BlockSpec(shape, lambda *_: (0,) * nd)


def _rms(x, g):
    return x * lax.rsqrt(jnp.mean(x * x, axis=-1, keepdims=True) + RMS_EPS) * g


def _gelu(x):
    return 0.5 * x * (1.0 + jnp.tanh(math.sqrt(2.0 / math.pi) * (x + 0.044715 * (x * x * x))))


def _sigmoid(x):
    return 1.0 / (1.0 + jnp.exp(-x))


def _softplus(x):
    return jnp.maximum(x, 0.0) + jnp.log(1.0 + jnp.exp(-jnp.abs(x)))


def _bdot(a, b):
    return jnp.dot(a.astype(BF16), b.astype(BF16), preferred_element_type=F32)


def _hdot(a, b):
    return jnp.dot(a, b, preferred_element_type=F32, precision=HI)


def _dot3(a, b):
    m = a.shape[0]
    a_hi = a.astype(BF16)
    b_hi = b.astype(BF16)
    a_lo = (a - a_hi.astype(F32)).astype(BF16)
    b_lo = (b - b_hi.astype(F32)).astype(BF16)
    dot = functools.partial(jnp.dot, preferred_element_type=F32)
    top = dot(jnp.concatenate([a_hi, a_lo], axis=0), b_hi)
    return top[:m] + (top[m:] + dot(a_hi, b_lo))


def _dot3_many(a_list, b_list):
    m = a_list[0].shape[0]
    dot = functools.partial(jnp.dot, preferred_element_type=F32)
    a_hi = [a.astype(BF16) for a in a_list]
    b_hi = [b.astype(BF16) for b in b_list]
    a_lo = [(a - h.astype(F32)).astype(BF16) for a, h in zip(a_list, a_hi)]
    b_lo = [(b - h.astype(F32)).astype(BF16) for b, h in zip(b_list, b_hi)]
    top = [dot(jnp.concatenate([h, l], axis=0), b) for h, l, b in zip(a_hi, a_lo, b_hi)]
    low = [dot(h, b) for h, b in zip(a_hi, b_lo)]
    return [t[:m] + (t[m:] + l) for t, l in zip(top, low)]


def _split_dot_many(xs, m, rhs=False, pieces=3):
    accs = [None] * len(xs)
    xs = list(xs)
    for _ in range(pieces):
        his = [x.astype(BF16) for x in xs]
        parts = [jnp.dot(m, hi, preferred_element_type=F32) if rhs else jnp.dot(hi, m, preferred_element_type=F32)
                 for hi in his]
        accs = [p if a is None else a + p for a, p in zip(accs, parts)]
        xs = [x - hi.astype(F32) for x, hi in zip(xs, his)]
    return accs


def _ab_in_kernel(h_ref, g_ref, w_ref, u_ref, gate_ref, rec_ref, *, s5w, lruw):
    xn = _rms(h_ref[...], g_ref[...])
    z = _bdot(xn, w_ref[...])
    u_ref[...] = z[:, :s5w].astype(u_ref.dtype)
    gate_ref[...] = z[:, s5w:s5w + lruw]
    rec_ref[...] = z[:, s5w + lruw:]


def _ab_in(h, g, w_in, s5w, lruw):
    tp, d = h.shape
    tm = _pick_tile(tp, 512)
    row = lambda n: pl.BlockSpec((tm, n), lambda i: (i, 0))
    return pl.pallas_call(
        functools.partial(_ab_in_kernel, s5w=s5w, lruw=lruw),
        grid=(tp // tm,),
        in_specs=[row(d), _const_spec((1, d)), _const_spec(w_in.shape)],
        out_specs=[row(s5w), row(lruw), row(lruw)],
        out_shape=[jax.ShapeDtypeStruct((tp, s5w), BF16),
                   jax.ShapeDtypeStruct((tp, lruw), F32),
                   jax.ShapeDtypeStruct((tp, lruw), F32)],
        compiler_params=_cparams(("parallel",)),
        name="ab_in",
    )(h, g.reshape(1, d), w_in.astype(BF16))


def _s5_tables(lam_re, lam_im, log_dt, b_re, b_im, c_re, c_im, d_skip):
    g, p = lam_re.shape
    hh = b_re.shape[-1]
    c = S5_CHUNK
    lr, li = lam_re.astype(F32), lam_im.astype(F32)
    dt = jnp.exp(log_dt.astype(F32))[:, None]
    mag = jnp.exp(lr * dt)
    abar_r = mag * jnp.cos(li * dt)
    abar_i = mag * jnp.sin(li * dt)
    den = lr * lr + li * li
    zr = ((abar_r - 1.0) * lr + abar_i * li) / den
    zi = (abar_i * lr - (abar_r - 1.0) * li) / den
    bbar_r = zr[..., None] * b_re - zi[..., None] * b_im
    bbar_i = zr[..., None] * b_im + zi[..., None] * b_re
    bbr_t, bbi_t = jnp.swapaxes(bbar_r, 1, 2), jnp.swapaxes(bbar_i, 1, 2)
    cr_t, ci_t = jnp.swapaxes(c_re, 1, 2), jnp.swapaxes(c_im, 1, 2)

    def powers(steps):
        st = steps.astype(F32)[None, :, None]
        pmag = jnp.exp(st * (lr * dt)[:, None, :])
        ang = st * (li * dt)[:, None, :]
        return pmag * jnp.cos(ang), pmag * jnp.sin(ang)

    down = (c - 1) - jnp.arange(c)
    rev_r, rev_i = powers(down)
    m1_r = rev_r[:, :, None, :] * bbr_t[:, None] - rev_i[:, :, None, :] * bbi_t[:, None]
    m1_i = rev_r[:, :, None, :] * bbi_t[:, None] + rev_i[:, :, None, :] * bbr_t[:, None]
    car = c_re[:, None] * rev_r[:, :, None, :] - c_im[:, None] * rev_i[:, :, None, :]
    cai = c_re[:, None] * rev_i[:, :, None, :] + c_im[:, None] * rev_r[:, :, None, :]
    kern = (jnp.einsum('gqhp,gpj->gqjh', car, bbar_r, precision=HI)
            - jnp.einsum('gqhp,gpj->gqjh', cai, bbar_i, precision=HI))
    is_tau0 = (down == 0).astype(F32)[None, :, None, None]
    kern = kern + is_tau0 * (d_skip[:, None, None, :] * jnp.eye(hh, dtype=F32)[None, None])
    up_r, up_i = powers(jnp.arange(1, c + 1))
    up_r, up_i = jnp.swapaxes(up_r, 1, 2)[..., None], jnp.swapaxes(up_i, 1, 2)[..., None]
    m2_r = cr_t[:, :, None, :] * up_r - ci_t[:, :, None, :] * up_i
    m2_i = -(cr_t[:, :, None, :] * up_i + ci_t[:, :, None, :] * up_r)
    adv_r, adv_i = powers(jnp.full((1,), c))
    gb = S5_LANES // hh
    nj = g // gb

    def rows_sgh(x):
        w = x.shape[-1]
        return jnp.transpose(x.reshape(nj, gb, c, hh, w), (0, 2, 1, 3, 4)).reshape(nj, c * gb * hh, w)

    def place(base, spread, row_group, col_group):
        out = jnp.einsum('jrw,wc->jrc', base.astype(BF16), spread.astype(BF16), preferred_element_type=BF16)
        rg = row_group(lax.broadcasted_iota(jnp.int32, out.shape[1:], 0))
        cg = col_group(lax.broadcasted_iota(jnp.int32, out.shape[1:], 1))
        return jnp.where(rg == cg, out, jnp.zeros((), BF16))

    grp_sgh = lambda r: (r // hh) % gb
    rep = lambda w: jnp.tile(jnp.eye(w, dtype=F32), (1, gb))
    m1 = jnp.concatenate([place(rows_sgh(m), rep(p), grp_sgh, lambda col: col // p) for m in (m1_r, m1_i)], axis=-1)
    krev = place(rows_sgh(kern), rep(hh), grp_sgh, lambda col: col // hh)
    ri = lax.broadcasted_iota(jnp.int32, (c * hh, c * gb * hh), 0)
    ci = lax.broadcasted_iota(jnp.int32, (c * hh, c * gb * hh), 1)
    spread_th = ((ri // hh == ci // (gb * hh)) & (ri % hh == ci % hh)).astype(F32)
    m2 = jnp.concatenate([place(m.reshape(nj, gb * p, c * hh), spread_th, lambda r: r // p,
                                lambda col: (col // hh) % gb) for m in (m2_r, m2_i)], axis=1)
    return (m1.astype(BF16), krev.astype(BF16), m2.astype(BF16),
            adv_r.reshape(nj, 1, gb * p), adv_i.reshape(nj, 1, gb * p))


def _s5_kernel(u_ref, m1_ref, kr_ref, m2_ref, ar_ref, ai_ref, y_ref, xe_ref, xin_ref, st_ref, *, n_chunks, nb):
    nl = S5_LANES
    csz = S5_CHUNK
    sw = ar_ref.shape[-1]

    @pl.when(pl.program_id(1) == 0)
    def _():
        st_ref[...] = jnp.zeros(st_ref.shape, F32)

    u = u_ref[0]
    xe_ref[...] = jnp.dot(u, m1_ref[0], preferred_element_type=F32)
    ar = jnp.broadcast_to(ar_ref[0], (nb, sw))
    ai = jnp.broadcast_to(ai_ref[0], (nb, sw))

    def body(c, carry):
        sr, si = carry
        off = pl.multiple_of(c * nb, nb)
        xin_ref[pl.ds(off, nb), :] = jnp.concatenate([sr, si], axis=1)
        e = xe_ref[pl.ds(off, nb), :]
        return (ar * sr - ai * si + e[:, :sw], ar * si + ai * sr + e[:, sw:])

    sr, si = lax.fori_loop(0, n_chunks, body, (st_ref[:, :sw], st_ref[:, sw:]))
    st_ref[:, :sw] = sr
    st_ref[:, sw:] = si
    y_ref[0] = _bdot(xin_ref[...], m2_ref[0])
    for t in range(csz):
        y_ref[0, :, t * nl:(t + 1) * nl] += jnp.dot(u[:, :(t + 1) * nl], kr_ref[0, (csz - 1 - t) * nl:, :],
                                                    preferred_element_type=F32)


def _s5_scan(u, tables, bsz, lp):
    m1, krev, m2, adv_r, adv_i = tables
    nj, kin, sw2 = m1.shape
    c = S5_CHUNK
    nl = S5_LANES
    nc = lp // c
    cpt = max(d for d in range(1, nc + 1) if nc % d == 0 and d * bsz <= 512)
    rows = cpt * bsz
    ug = jnp.transpose(u.reshape(bsz, nc, c, nj, nl), (3, 1, 0, 2, 4)).reshape(nj, nc * bsz, kin)
    y = pl.pallas_call(
        functools.partial(_s5_kernel, n_chunks=cpt, nb=bsz),
        grid=(nj, nc // cpt),
        in_specs=[pl.BlockSpec((1, rows, kin), lambda j, r: (j, r, 0)),
                  pl.BlockSpec((1, kin, sw2), lambda j, r: (j, 0, 0)),
                  pl.BlockSpec((1, kin, nl), lambda j, r: (j, 0, 0)),
                  pl.BlockSpec((1, sw2, kin), lambda j, r: (j, 0, 0)),
                  pl.BlockSpec((1, 1, sw2 // 2), lambda j, r: (j, 0, 0)),
                  pl.BlockSpec((1, 1, sw2 // 2), lambda j, r: (j, 0, 0))],
        out_specs=pl.BlockSpec((1, rows, kin), lambda j, r: (j, r, 0)),
        out_shape=jax.ShapeDtypeStruct((nj, nc * bsz, kin), F32),
        scratch_shapes=[pltpu.VMEM((rows, sw2), F32), pltpu.VMEM((rows, sw2), F32), pltpu.VMEM((bsz, sw2), F32)],
        compiler_params=_cparams(("parallel", "arbitrary")),
        name="s5_scan",
    )(ug, m1, krev, m2, adv_r, adv_i)
    y = jnp.transpose(y.reshape(nj, nc, bsz, c, nl), (2, 1, 3, 0, 4))
    return y.reshape(bsz * lp, nj * nl)


def _lru_kernel(rec_ref, gate_ref, cw_ref, cb_ref, wa_ref, ba_ref, wx_ref, bx_ref, lam_ref,
                o_ref, ext_ref, a_ref, b_ref, h_ref, *, tl):
    t = pl.program_id(1)

    @pl.when(t == 0)
    def _():
        ext_ref[0:8, :] = jnp.zeros((8, ext_ref.shape[1]), F32)
        h_ref[...] = jnp.zeros(h_ref.shape, F32)

    x = rec_ref[...]
    ext_ref[8:, :] = x
    xc = cb_ref[...] + cw_ref[3:4, :] * x
    for k in range(3):
        xc = xc + cw_ref[k:k + 1, :] * ext_ref[5 + k:5 + k + tl, :]
    ext_ref[0:8, :] = x[tl - 8:, :]
    r = _sigmoid(_bdot(xc, wa_ref[...]) + ba_ref[...])
    i = _sigmoid(_bdot(xc, wx_ref[...]) + bx_ref[...])
    log_a = (-LRU_C) * r * _softplus(-lam_ref[...])
    a = jnp.exp(log_a)
    a_ref[...] = a
    b_ref[...] = jnp.sqrt(1.0 - a * a) * (i * xc)

    sub = 8
    unroll = max(u for u in (4, 2, 1) if (tl // sub) % u == 0)
    row = lax.broadcasted_iota(jnp.int32, (sub, a.shape[1]), 0)

    def body(j, h):
        offs = [pl.multiple_of((j * unroll + q) * sub, sub) for q in range(unroll)]
        ab = [a_ref[pl.ds(o, sub), :] for o in offs]
        bb = [b_ref[pl.ds(o, sub), :] for o in offs]
        for dist in (1, 2, 4):
            keep = row >= dist
            bb = [jnp.where(keep, x * pltpu.roll(y, dist, axis=0) + y, y) for x, y in zip(ab, bb)]
            ab = [jnp.where(keep, x * pltpu.roll(x, dist, axis=0), x) for x in ab]
        for o, x, y in zip(offs, ab, bb):
            hb = x * h + y
            o_ref[pl.ds(o, sub), :] = hb * _gelu(gate_ref[pl.ds(o, sub), :])
            h = hb[sub - 1:sub, :]
        return h

    h_ref[...] = lax.fori_loop(0, tl // (sub * unroll), body, h_ref[...])


def _lru(rec, gate, conv_w, conv_b, w_a, b_a, w_x, b_x, lam, bsz, lp):
    tp, w = rec.shape
    tl = _pick_tile(lp, 1056)
    nt = lp // tl
    heads, hd, _ = w_a.shape

    def dense(wb):
        eye = jnp.eye(heads, dtype=F32)
        return jnp.einsum('hij,hg->higj', wb, eye).reshape(w, w).astype(BF16)

    row = pl.BlockSpec((tl, w), lambda b, t: (b * nt + t, 0))
    vec = _const_spec((1, w))
    return pl.pallas_call(
        functools.partial(_lru_kernel, tl=tl),
        grid=(bsz, nt),
        in_specs=[row, row, _const_spec((4, w)), vec, _const_spec((w, w)), vec, _const_spec((w, w)), vec, vec],
        out_specs=row,
        out_shape=jax.ShapeDtypeStruct((tp, w), F32),
        scratch_shapes=[pltpu.VMEM((tl + 8, w), F32), pltpu.VMEM((tl, w), F32),
                        pltpu.VMEM((tl, w), F32), pltpu.VMEM((1, w), F32)],
        compiler_params=_cparams(("parallel", "arbitrary")),
        name="rglru",
    )(rec, gate, conv_w, conv_b.reshape(1, w), dense(w_a), b_a.reshape(1, w), dense(w_x), b_x.reshape(1, w),
      lam.reshape(1, w))


def _ab_out_kernel(y5_ref, lru_ref, h_ref, wglu_ref, bglu_ref, na_ref, nb_ref, wo_ref, gf_ref, wr_ref, br_ref,
                   o_ref, xn_ref, rt_ref, *, s5w):
    y = _gelu(y5_ref[...])
    ya = y * _sigmoid(_bdot(y, wglu_ref[...]) + bglu_ref[...])
    ya = _rms(ya, na_ref[...])
    yb = _rms(lru_ref[...], nb_ref[...])
    h = h_ref[...] + _bdot(ya, wo_ref[:s5w, :]) + _bdot(yb, wo_ref[s5w:, :])
    o_ref[...] = h
    xn, rt_ref[...] = _route_tile(h, gf_ref[...], wr_ref[0], wr_ref[1], br_ref[...])
    xn_ref[...] = xn.astype(xn_ref.dtype)


def _ab_out(y5, lru, h, w_glu, b_glu, norm_a, norm_b, w_out, g_ffn, router):
    tp, d = h.shape
    s5w, lruw = y5.shape[1], lru.shape[1]
    wr, br = router
    tm = _pick_tile(tp, 512)
    row = lambda n: pl.BlockSpec((tm, n), lambda i: (i, 0))
    return pl.pallas_call(
        functools.partial(_ab_out_kernel, s5w=s5w),
        grid=(tp // tm,),
        in_specs=[row(s5w), row(lruw), row(d), _const_spec((s5w, s5w)), _const_spec((1, s5w)),
                  _const_spec((1, s5w)), _const_spec((1, lruw)), _const_spec(w_out.shape),
                  _const_spec((1, d)), _const_spec(wr.shape), _const_spec(br.shape)],
        out_specs=[row(d), row(d), row(ROUTE_LANES)],
        out_shape=[jax.ShapeDtypeStruct((tp, d), F32), jax.ShapeDtypeStruct((tp, d), F32),
                   jax.ShapeDtypeStruct((tp, ROUTE_LANES), F32)],
        compiler_params=_cparams(("parallel",)),
        name="ab_out",
    )(y5, lru, h, w_glu.astype(BF16), b_glu.reshape(1, s5w), norm_a.reshape(1, s5w),
      norm_b.reshape(1, lruw), w_out.astype(BF16), g_ffn.reshape(1, d), wr, br)


def _route_tile(h, g, wr_hi, wr_lo, br):
    xn = _rms(h, g)
    m = xn.shape[0]
    x_hi = xn.astype(BF16)
    x_lo = (xn - x_hi.astype(F32)).astype(BF16)
    top = jnp.dot(jnp.concatenate([x_hi, x_lo], axis=0), wr_hi, preferred_element_type=F32)
    lg = top[:m] + (top[m:] + jnp.dot(x_hi, wr_lo, preferred_element_type=F32)) + br
    lane = lax.broadcasted_iota(jnp.int32, lg.shape, 1).astype(F32)
    big = float(lg.shape[1])
    neg = -jnp.inf
    gl = jnp.where(lane < N_GROUPS, lg, neg)
    mg = jnp.max(gl, axis=-1, keepdims=True)
    gidx = jnp.min(jnp.where(gl == mg, lane, big), axis=-1, keepdims=True)
    pg_sel = 1.0 / jnp.sum(jnp.exp(gl - mg), axis=-1, keepdims=True)
    lo = N_GROUPS + EXPERTS_PER_GROUP * gidx
    le = jnp.where(lane >= lo, jnp.where(lane < lo + EXPERTS_PER_GROUP, lg, neg), neg)
    v1 = jnp.max(le, axis=-1, keepdims=True)
    i1 = jnp.min(jnp.where(le == v1, lane, big), axis=-1, keepdims=True)
    le2 = jnp.where(lane == i1, neg, le)
    v2 = jnp.max(le2, axis=-1, keepdims=True)
    i2 = jnp.min(jnp.where(le2 == v2, lane, big), axis=-1, keepdims=True)
    e2 = jnp.exp(v2 - v1)
    w1 = pg_sel / (1.0 + e2)
    w2 = w1 * e2
    rt = jnp.where(lane == 0.0, w1, jnp.where(lane == 1.0, w2, jnp.where(
        lane == 2.0, i1 - N_GROUPS, jnp.where(lane == 3.0, i2 - N_GROUPS, 0.0))))
    return xn, rt


def _router_table(wr_g, br_g, wr_e, br_e):
    d = wr_g.shape[0]
    wr = jnp.zeros((d, ROUTE_LANES), F32).at[:, :N_GROUPS].set(wr_g).at[:, N_GROUPS:N_GROUPS + N_EXPERTS].set(wr_e)
    br = jnp.zeros((1, ROUTE_LANES), F32).at[0, :N_GROUPS].set(br_g).at[0, N_GROUPS:N_GROUPS + N_EXPERTS].set(br_e)
    wr_hi = wr.astype(BF16)
    wr_lo = (wr - wr_hi.astype(F32)).astype(BF16)
    return jnp.stack([wr_hi, wr_lo]), br


def _gmm_kernel(te_ref, tv_ref, xa_ref, xb_ref, wg_ref, wu_ref, wd_ref, o_ref, wg_bf, wu_bf, wd_bf, *, n_half):
    i = pl.program_id(0)

    @pl.when(jnp.logical_or(i == 0, te_ref[i] != te_ref[jnp.maximum(i - 1, 0)]))
    def _():
        wg_bf[...] = wg_ref[0, 0].astype(BF16)
        wu_bf[...] = wu_ref[0, 0].astype(BF16)
        wd_bf[...] = wd_ref[0, 0].astype(BF16)

    @pl.when(tv_ref[i] != 0)
    def _():
        x = jnp.where(i < n_half, xa_ref[...], xb_ref[...]).astype(BF16)
        hg = jnp.dot(x, wg_bf[...], preferred_element_type=F32)
        hu = jnp.dot(x, wu_bf[...], preferred_element_type=F32)
        hid = hg * _sigmoid(hg) * hu
        o_ref[...] = _bdot(hid, wd_bf[...]).astype(o_ref.dtype)

    @pl.when(tv_ref[i] == 0)
    def _():
        o_ref[...] = jnp.zeros(o_ref.shape, o_ref.dtype)


def _invert_rows_kernel(dest_ref, init_ref, src_ref, *, unroll):
    pltpu.sync_copy(init_ref, src_ref)
    n_tok = dest_ref.shape[0] // 2

    def put(t):
        src_ref[dest_ref[2 * t]] = t
        src_ref[dest_ref[2 * t + 1]] = t

    def body(i, _):
        for u in range(unroll):
            put(i * unroll + u)
        return 0

    lax.fori_loop(0, n_tok // unroll, body, 0)
    for t in range(n_tok - n_tok % unroll, n_tok):
        put(t)


def _invert_rows(dest, init):
    smem = pl.BlockSpec(memory_space=pltpu.SMEM)
    return pl.pallas_call(
        functools.partial(_invert_rows_kernel, unroll=8),
        in_specs=[smem, pl.BlockSpec(memory_space=pl.ANY)],
        out_specs=smem,
        out_shape=jax.ShapeDtypeStruct(init.shape, jnp.int32),
        name="moe_invert",
    )(dest, init)


def _moe(h, xn, rt, w_gate, w_up, w_down, layer):
    tp, d = h.shape
    f = w_gate.shape[-1]
    gate = rt[:, 0:2]
    eid = rt[:, 2:4].astype(jnp.int32)
    tmm = MOE_TILE
    na = 2 * tp
    e_flat = eid.reshape(na)
    onehot = (e_flat[:, None] == jnp.arange(N_EXPERTS, dtype=jnp.int32)[None, :]).astype(jnp.int32)
    csum = jnp.cumsum(onehot, axis=0)
    counts = csum[-1]
    padded = ((counts + tmm - 1) // tmm) * tmm
    ends = jnp.cumsum(padded)
    starts = ends - padded
    dest = jnp.sum(onehot * (csum + (starts - 1)[None, :]), axis=1)
    n_half = (-(-na // tmm) + N_EXPERTS + 1) // 2
    n_tiles = 2 * n_half
    nrows = n_tiles * tmm
    tile_start = jnp.arange(n_tiles, dtype=jnp.int32) * tmm
    tile_e = jnp.sum((ends[None, :] <= tile_start[:, None]).astype(jnp.int32), axis=1)
    tile_v = (tile_e < N_EXPERTS).astype(jnp.int32)
    tile_e = jnp.minimum(tile_e, N_EXPERTS - 1)
    src = _invert_rows(dest, jnp.arange(nrows, dtype=jnp.int32) % tp)
    take = lambda a, i: a.at[i].get(mode="promise_in_bounds")
    xs_a = take(xn, src[:n_half * tmm])
    xs_b = take(xn, src[n_half * tmm:])
    ys = pl.pallas_call(
        functools.partial(_gmm_kernel, n_half=n_half),
        grid_spec=pltpu.PrefetchScalarGridSpec(
            num_scalar_prefetch=2,
            grid=(n_tiles,),
            in_specs=[pl.BlockSpec((tmm, d), lambda i, te, tv: (jnp.minimum(i, n_half - 1), 0)),
                      pl.BlockSpec((tmm, d), lambda i, te, tv: (jnp.maximum(i - n_half, 0), 0)),
                      pl.BlockSpec((1, 1, d, f), lambda i, te, tv: (layer, te[i], 0, 0)),
                      pl.BlockSpec((1, 1, d, f), lambda i, te, tv: (layer, te[i], 0, 0)),
                      pl.BlockSpec((1, 1, f, d), lambda i, te, tv: (layer, te[i], 0, 0))],
            out_specs=pl.BlockSpec((tmm, d), lambda i, te, tv: (i, 0)),
            scratch_shapes=[pltpu.VMEM((d, f), BF16), pltpu.VMEM((d, f), BF16), pltpu.VMEM((f, d), BF16)],
        ),
        out_shape=jax.ShapeDtypeStruct((nrows, d), MOE_ROW_DTYPE),
        compiler_params=_cparams(("arbitrary",)),
        name="moe_gmm",
    )(tile_e, tile_v, xs_a, xs_b, w_gate, w_up, w_down)
    d2 = dest.reshape(tp, 2)
    return h + gate[:, 0:1] * take(ys, d2[:, 0]) + gate[:, 1:2] * take(ys, d2[:, 1])


def _rw_pre_kernel(*refs, tm, lp, has_vres):
    if has_vres:
        (h_ref, hp_ref, g_ref, mu_ref, wr_ref, wk_ref, wv_ref, w0_ref, wl1_ref, wl2_ref,
         a0_ref, al1_ref, al2_ref, gl1_ref, gl2_ref, vf_ref, v0_ref, vl1_ref, vl2_ref,
         r_ref, k_ref, v_ref, w_ref, a_ref, gg_ref) = refs
    else:
        (h_ref, hp_ref, g_ref, mu_ref, wr_ref, wk_ref, wv_ref, w0_ref, wl1_ref, wl2_ref,
         a0_ref, al1_ref, al2_ref, gl1_ref, gl2_ref,
         r_ref, k_ref, v_ref, w_ref, a_ref, gg_ref) = refs
    i = pl.program_id(0)
    g = g_ref[...]
    x = _rms(h_ref[...], g)
    xp8 = _rms(hp_ref[...], g)
    row = lax.broadcasted_iota(jnp.int32, x.shape, 0)
    prev = jnp.where(row == 0, jnp.broadcast_to(xp8[7:8, :], x.shape), pltpu.roll(x, 1, axis=0))
    first = lax.rem(lp - lax.rem(i * tm, lp), lp)
    prev = jnp.where(row == first, 0.0, prev)
    xx = prev - x
    xr, xw, xk, xv, xa, xg = [x + xx * mu_ref[j:j + 1, :] for j in range(6)]
    r_ref[...] = _bdot(xr, wr_ref[...]).astype(r_ref.dtype)
    k_ref[...] = _bdot(xk, wk_ref[...]).astype(k_ref.dtype)
    v = _bdot(xv, wv_ref[...])
    if has_vres:
        mix = _sigmoid(v0_ref[...] + _bdot(_bdot(xv, vl1_ref[...]), vl2_ref[...]))
        v = v + (vf_ref[...] - v) * mix
    v_ref[...] = v.astype(v_ref.dtype)
    w_ref[...] = -_softplus(-(w0_ref[...] + _bdot(jnp.tanh(_bdot(xw, wl1_ref[...])), wl2_ref[...]))) - 0.5
    a_ref[...] = _sigmoid(a0_ref[...] + _bdot(_bdot(xa, al1_ref[...]), al2_ref[...]))
    gg_ref[...] = _bdot(_sigmoid(_bdot(xg, gl1_ref[...])), gl2_ref[...]).astype(gg_ref.dtype)


def _rw_pre(h, g, mu, w_r, w_k, w_v, w0, w_l1, w_l2, a0, a_l1, a_l2, g_l1, g_l2, lp, v_first, v_res):
    tp, d = h.shape
    tm = _pick_tile(tp, min(512, lp))
    has_vres = v_res is not None
    row = pl.BlockSpec((tm, d), lambda i: (i, 0))
    prev8 = pl.BlockSpec((8, d), lambda i: (jnp.maximum(i * (tm // 8) - 1, 0), 0))
    vec = _const_spec((1, d))
    mu8 = jnp.zeros((8, d), F32).at[:6].set(mu)
    bf = lambda w: w.astype(BF16)
    ins = [h, h, g.reshape(1, d), mu8, bf(w_r), bf(w_k), bf(w_v), w0.reshape(1, d), bf(w_l1), bf(w_l2),
           a0.reshape(1, d), bf(a_l1), bf(a_l2), bf(g_l1), bf(g_l2)]
    specs = [row, prev8, vec, _const_spec((8, d))] + [_const_spec((d, d))] * 3 + [
        vec, _const_spec(w_l1.shape), _const_spec(w_l2.shape),
        vec, _const_spec(a_l1.shape), _const_spec(a_l2.shape), _const_spec(g_l1.shape), _const_spec(g_l2.shape)]
    if has_vres:
        v0, v_l1, v_l2 = v_res
        ins += [v_first, v0.reshape(1, d), bf(v_l1), bf(v_l2)]
        specs += [row, vec, _const_spec(v_l1.shape), _const_spec(v_l2.shape)]
    return pl.pallas_call(
        functools.partial(_rw_pre_kernel, tm=tm, lp=lp, has_vres=has_vres),
        grid=(tp // tm,),
        in_specs=specs,
        out_specs=[row] * 6,
        out_shape=[jax.ShapeDtypeStruct((tp, d), dt) for dt in (BF16, BF16, BF16, F32, F32, BF16)],
        compiler_params=_cparams(("parallel",)),
        name="rwkv_pre",
    )(*ins)


def _rw_scan_kernel(r_ref, w_ref, k_ref, v_ref, a_ref, kk_ref, ka_ref, rk_ref, lw_ref, lb_ref,
                    o_ref, g_scr, h_scr, q_scr, y0_scr, *, n_chunks, unroll, pairs):
    c = RW_CHUNK
    nl = 2 * RW_HEAD
    c2 = 2 * c
    ri = lax.broadcasted_iota(jnp.int32, (c2, nl), 0)
    ci = lax.broadcasted_iota(jnp.int32, (c2, nl), 1)
    own = ((ri >= c) == (ci >= RW_HEAD)).astype(F32)
    t_in = jnp.bitwise_and(ri, c - 1)
    s_in = jnp.bitwise_and(ci, c - 1)
    strict = jnp.where(s_in < t_in, own, 0.0)
    incl = jnp.where(s_in <= t_in, own, 0.0)
    causal2 = jnp.concatenate([strict, incl], axis=0)
    eye_l = (ri == ci).astype(F32)
    same_head = own.astype(BF16)
    ti = lax.broadcasted_iota(jnp.int32, (c, c), 0)
    si = lax.broadcasted_iota(jnp.int32, (c, c), 1)
    tril_c = (si <= ti).astype(BF16)
    lanes = lambda p: slice(p * nl, (p + 1) * nl)

    def stack(x):
        return jnp.concatenate([x, x], axis=0) * own

    def offset(ch):
        return ch * c if isinstance(ch, int) else pl.multiple_of(ch * c, c)

    def groups(fn):
        for p in range(pairs):
            def body(i, _):
                fn([i * unroll + q for q in range(unroll)], p)
                return 0
            lax.fori_loop(0, n_chunks // unroll, body, 0)
            if n_chunks % unroll:
                fn(list(range(n_chunks - n_chunks % unroll, n_chunks)), p)

    def prep(chs, p):
        ln = lanes(p)
        kk_w, ka_w = kk_ref[:, ln], ka_ref[:, ln]
        offs = [offset(ch) for ch in chs]
        r = [r_ref[pl.ds(o, c), ln] for o in offs]
        k = [k_ref[pl.ds(o, c), ln] for o in offs]
        v = [v_ref[pl.ds(o, c), ln] for o in offs]
        a = [a_ref[pl.ds(o, c), ln] for o in offs]
        logw = [-jnp.exp(w_ref[pl.ds(o, c), ln]) for o in offs]
        kk = [x * kk_w for x in k]
        ss = _split_dot_many([x * x for x in kk], same_head, pieces=2)
        kk = [x / jnp.maximum(jnp.sqrt(q), 1e-12) for x, q in zip(kk, ss)]
        kmod = [x * (1.0 + (y - 1.0) * ka_w) for x, y in zip(k, a)]
        cum = _split_dot_many(logw, tril_c, rhs=True, pieces=2)
        p_incl = [jnp.exp(x) for x in cum]
        p_inv = [jnp.exp(-x) for x in cum]
        p_end = [jnp.exp(x[c - 1:c, :] - x) for x in cum]
        kka = [x * y for x, y in zip(kk, a)]
        a_s = [stack(-x * jnp.exp(y - z)) for x, y, z in zip(kk, cum, logw)]
        r_s = [stack(x * y) for x, y in zip(r, p_incl)]
        v_s = [stack(x).astype(BF16) for x in v]
        lhs = [jnp.concatenate([x, y], axis=0).astype(BF16) for x, y in zip(a_s, r_s)]
        rhs = [jnp.concatenate([stack(x * z), stack(y * z)], axis=0).astype(BF16) for x, y, z in zip(kka, kmod, p_inv)]
        big = [lax.dot_general(x, y, (((1,), (1,)), ((), ())), preferred_element_type=F32) for x, y in zip(lhs, rhs)]
        lpow = [(x[:c2, :c2] * strict).astype(BF16) for x in big]
        a_rb = [x[c2:, :c2] * incl for x in big]
        avk = [_bdot(x[:, c2:] * causal2, w) for x, w in zip(big, v_s)]
        x = [jnp.concatenate([p, q[:c2]], axis=1) for p, q in zip(a_s, avk)]
        x = [p + _bdot(q, p) for p, q in zip(x, lpow)]
        for _ in range(5):
            lpow = [jnp.dot(q, q, preferred_element_type=F32).astype(BF16) for q in lpow]
            x = [p + _bdot(q, p) for p, q in zip(x, lpow)]
        xb = [p.astype(BF16) for p in x]
        bh_t = [stack(p * q).T for p, q in zip(kka, p_end)]
        kh_t = [stack(p * q).T for p, q in zip(kmod, p_end)]
        both = [_bdot(jnp.concatenate([u, p], axis=0), z) for u, p, z in zip(a_rb, bh_t, xb)]
        qy = [jnp.concatenate([p, q[c2:]], axis=1) + z[:c2] for p, q, z in zip(r_s, avk, both)]
        gh = [z[c2:] + jnp.concatenate([eye_l * q[c - 1:c, :], _bdot(u, w)], axis=1)
              for z, q, u, w in zip(both, p_incl, kh_t, v_s)]
        for ch, u, q in zip(chs, qy, gh):
            q_scr[p, ch] = u[:c, :nl] + u[c:, :nl]
            y0_scr[p, ch] = u[:c, nl:] + u[c:, nl:]
            g_scr[p, ch] = q[:, :nl]
            h_scr[p, ch] = q[:, nl:]

    groups(prep)

    def advance(ch, states):
        both = _dot3_many([jnp.concatenate([q_scr[p, ch], g_scr[p, ch]], axis=0) for p in range(pairs)], states)
        return (tuple(both[p][c:] + h_scr[p, ch] for p in range(pairs)),
                tuple(both[p][:c] + y0_scr[p, ch] for p in range(pairs)))

    def finish(ch, ys):
        o = offset(ch)
        rk = []
        for p in range(pairs):
            ln = lanes(p)
            kmod = k_ref[pl.ds(o, c), ln] * (1.0 + (a_ref[pl.ds(o, c), ln] - 1.0) * ka_ref[:, ln])
            rk.append(r_ref[pl.ds(o, c), ln] * kmod * rk_ref[:, ln])
        sums = _split_dot_many([jnp.concatenate([u, u * u, q], axis=0) for u, q in zip(ys, rk)], same_head)
        for p, (u, s) in enumerate(zip(ys, sums)):
            ln = lanes(p)
            mean = s[:c] * (1.0 / RW_HEAD)
            var = s[c:2 * c] * (1.0 / RW_HEAD) - mean * mean
            o_ref[pl.ds(o, c), ln] = ((u - mean) * lax.rsqrt(var + LNX_EPS) * lw_ref[:, ln] + lb_ref[:, ln]
                                      + s[2 * c:] * v_ref[pl.ds(o, c), ln])

    def step(ch, carry):
        states, ys = carry
        finish(ch - 1, ys)
        return advance(ch, states)

    carry = advance(0, tuple(jnp.zeros((nl, nl), F32) for _ in range(pairs)))
    _, ys = lax.fori_loop(1, n_chunks, step, carry)
    finish(n_chunks - 1, ys)


def _rw_scan(r, w, k, v, a, k_k, k_a, r_k, lnx_w, lnx_b, bsz, lp):
    tp, d = r.shape
    nl = 2 * RW_HEAD
    pairs = RW_PAIRS
    bw = pairs * nl
    nhp = d // bw
    nc = lp // RW_CHUNK
    seq = pl.BlockSpec((lp, bw), lambda b, hp: (b, hp))
    vec = pl.BlockSpec((1, bw), lambda b, hp: (0, hp))
    c = RW_CHUNK
    return pl.pallas_call(
        functools.partial(_rw_scan_kernel, n_chunks=nc, unroll=RW_UNROLL, pairs=pairs),
        grid=(bsz, nhp),
        in_specs=[seq] * 5 + [vec] * 5,
        out_specs=seq,
        out_shape=jax.ShapeDtypeStruct((tp, d), F32),
        scratch_shapes=[pltpu.VMEM((pairs, nc, nl, nl), F32), pltpu.VMEM((pairs, nc, nl, nl), F32),
                        pltpu.VMEM((pairs, nc, c, nl), F32), pltpu.VMEM((pairs, nc, c, nl), F32)],
        compiler_params=_cparams(("parallel", "parallel")),
        name="rwkv_scan",
    )(r, w, k, v, a, k_k.reshape(1, d), k_a.reshape(1, d), r_k.reshape(1, d),
      lnx_w.reshape(1, d), lnx_b.reshape(1, d))


def _rw_out_kernel(y_ref, g_ref, h_ref, wo_ref, gf_ref, wr_ref, br_ref, o_ref, xn_ref, rt_ref):
    h = h_ref[...] + _bdot(y_ref[...] * g_ref[...], wo_ref[...])
    o_ref[...] = h
    xn, rt_ref[...] = _route_tile(h, gf_ref[...], wr_ref[0], wr_ref[1], br_ref[...])
    xn_ref[...] = xn.astype(xn_ref.dtype)


def _rw_out(y, g, h, w_o, g_ffn, router):
    tp, d = h.shape
    wr, br = router
    tm = _pick_tile(tp, 512)
    row = lambda n: pl.BlockSpec((tm, n), lambda i: (i, 0))
    return pl.pallas_call(
        _rw_out_kernel,
        grid=(tp // tm,),
        in_specs=[row(d), row(d), row(d), _const_spec((d, d)), _const_spec((1, d)), _const_spec(wr.shape),
                  _const_spec(br.shape)],
        out_specs=[row(d), row(d), row(ROUTE_LANES)],
        out_shape=[jax.ShapeDtypeStruct((tp, d), F32), jax.ShapeDtypeStruct((tp, d), F32),
                   jax.ShapeDtypeStruct((tp, ROUTE_LANES), F32)],
        compiler_params=_cparams(("parallel",)),
        name="rwkv_out",
    )(y, g, h, w_o.astype(BF16), g_ffn.reshape(1, d), wr, br)


def _final_norm_kernel(h_ref, g_ref, o_ref, *, first, rows):
    o_ref[0] = _rms(h_ref[first:first + rows, :], g_ref[...]).astype(o_ref.dtype)


def _final_norm(h, g, dtype, bsz, lp, first, rows):
    d = h.shape[1]
    return pl.pallas_call(
        functools.partial(_final_norm_kernel, first=first, rows=rows),
        grid=(bsz,),
        in_specs=[pl.BlockSpec((lp, d), lambda b: (b, 0)), _const_spec((1, d))],
        out_specs=pl.BlockSpec((1, rows, d), lambda b: (b, 0, 0)),
        out_shape=jax.ShapeDtypeStruct((bsz, rows, d), dtype),
        compiler_params=_cparams(("parallel",)),
        name="final_norm",
    )(h, g.reshape(1, d))


def kernel(x, meta_tokens, norm_mix, norm_ffn, norm_final, ab_w_in, ab_w_out, ab_norm_a, ab_norm_b, s5_lam_re, s5_lam_im, s5_log_dt, s5_b_re, s5_b_im, s5_c_re, s5_c_im, s5_d, s5_w_glu, s5_b_glu, lru_conv_w, lru_conv_b, lru_w_a, lru_b_a, lru_w_x, lru_b_x, lru_lam, rw_mu, rw_w_r, rw_w_k, rw_w_v, rw_w_o, rw_w0, rw_w_l1, rw_w_l2, rw_a0, rw_a_l1, rw_a_l2, rw_v0, rw_v_l1, rw_v_l2, rw_g_l1, rw_g_l2, rw_k_k, rw_k_a, rw_r_k, rw_lnx_w, rw_lnx_b, moe_router_g, moe_router_g_b, moe_router_e, moe_router_e_b, moe_w_gate, moe_w_up, moe_w_down):
    bsz, seq, d = x.shape
    n_meta = meta_tokens.shape[0]
    depth = norm_mix.shape[0]
    ltot = n_meta + seq
    lp = -(-ltot // SEQ_ALIGN) * SEQ_ALIGN
    s5w = s5_w_glu.shape[-1]
    lruw = lru_lam.shape[-1]
    meta = jnp.broadcast_to(meta_tokens.astype(F32)[None], (bsz, n_meta, d))
    h = jnp.concatenate([meta, x.astype(F32), jnp.zeros((bsz, lp - ltot, d), F32)], axis=1).reshape(bsz * lp, d)
    v_first = None
    for layer in range(depth):
        j = layer // 2
        router = _router_table(moe_router_g[layer], moe_router_g_b[layer], moe_router_e[layer], moe_router_e_b[layer])
        if layer % 2 == 0:
            u, gate, rec = _ab_in(h, norm_mix[layer], ab_w_in[j], s5w, lruw)
            tables = _s5_tables(s5_lam_re[j], s5_lam_im[j], s5_log_dt[j], s5_b_re[j], s5_b_im[j],
                                s5_c_re[j], s5_c_im[j], s5_d[j])
            y5 = _s5_scan(u, tables, bsz, lp)
            lru = _lru(rec, gate, lru_conv_w[j], lru_conv_b[j], lru_w_a[j], lru_b_a[j], lru_w_x[j], lru_b_x[j],
                       lru_lam[j], bsz, lp)
            h, xn, rt = _ab_out(y5, lru, h, s5_w_glu[j], s5_b_glu[j], ab_norm_a[j], ab_norm_b[j], ab_w_out[j],
                                norm_ffn[layer], router)
        else:
            v_res = (rw_v0[j - 1], rw_v_l1[j - 1], rw_v_l2[j - 1]) if j > 0 else None
            r, k, v, w, a, g = _rw_pre(h, norm_mix[layer], rw_mu[j], rw_w_r[j], rw_w_k[j], rw_w_v[j], rw_w0[j],
                                       rw_w_l1[j], rw_w_l2[j], rw_a0[j], rw_a_l1[j], rw_a_l2[j], rw_g_l1[j],
                                       rw_g_l2[j], lp, v_first, v_res)
            if v_first is None:
                v_first = v
            y = _rw_scan(r, w, k, v, a, rw_k_k[j], rw_k_a[j], rw_r_k[j].reshape(-1), rw_lnx_w[j], rw_lnx_b[j],
                         bsz, lp)
            h, xn, rt = _rw_out(y, g, h, rw_w_o[j], norm_ffn[layer], router)
        h = _moe(h, xn, rt, moe_w_gate, moe_w_up, moe_w_down, layer)
    return _final_norm(h, norm_final, x.dtype, bsz, lp, n_meta, seq)
```

```python
import functools
import math

import jax
import jax.numpy as jnp
from jax import lax
from jax.experimental import pallas as pl
from jax.experimental.pallas import tpu as pltpu

F32 = jnp.float32
BF16 = jnp.bfloat16
HI = lax.Precision.HIGHEST

RMS_EPS = 1e-6
LNX_EPS = 64e-5
N_META = 16
SEQ_ALIGN = 64
S5_CHUNK = 16
S5_LANES = 128
RW_CHUNK = 64
RW_HEAD = 64
RW_PAIRS = 2
RW_UNROLL = 11
LRU_C = 8.0
N_GROUPS = 4
EXPERTS_PER_GROUP = 4
N_EXPERTS = N_GROUPS * EXPERTS_PER_GROUP
MOE_TILE = 512
ROUTE_LANES = 128
MOE_ROW_DTYPE = BF16
VMEM_LIMIT = 56 * 1024 * 1024


def _cparams(sem):
    return pltpu.CompilerParams(dimension_semantics=sem, vmem_limit_bytes=VMEM_LIMIT)


def _pick_tile(n, target):
    best = 8
    for t in range(8, min(n, target) + 1, 8):
        if n % t == 0:
            best = t
    return best


def _const_spec(shape):
    nd = len(shape)
    return pl.BlockSpec(shape, lambda *_: (0,) * nd)


def _rms(x, g):
    return x * lax.rsqrt(jnp.mean(x * x, axis=-1, keepdims=True) + RMS_EPS) * g


def _gelu(x):
    return 0.5 * x * (1.0 + jnp.tanh(math.sqrt(2.0 / math.pi) * (x + 0.044715 * (x * x * x))))


def _sigmoid(x):
    return 1.0 / (1.0 + jnp.exp(-x))


def _softplus(x):
    return jnp.maximum(x, 0.0) + jnp.log(1.0 + jnp.exp(-jnp.abs(x)))


def _bdot(a, b):
    return jnp.dot(a.astype(BF16), b.astype(BF16), preferred_element_type=F32)


def _hdot(a, b):
    return jnp.dot(a, b, preferred_element_type=F32, precision=HI)


def _dot3(a, b):
    m = a.shape[0]
    a_hi = a.astype(BF16)
    b_hi = b.astype(BF16)
    a_lo = (a - a_hi.astype(F32)).astype(BF16)
    b_lo = (b - b_hi.astype(F32)).astype(BF16)
    dot = functools.partial(jnp.dot, preferred_element_type=F32)
    top = dot(jnp.concatenate([a_hi, a_lo], axis=0), b_hi)
    return top[:m] + (top[m:] + dot(a_hi, b_lo))


def _dot3_many(a_list, b_list):
    m = a_list[0].shape[0]
    dot = functools.partial(jnp.dot, preferred_element_type=F32)
    a_hi = [a.astype(BF16) for a in a_list]
    b_hi = [b.astype(BF16) for b in b_list]
    a_lo = [(a - h.astype(F32)).astype(BF16) for a, h in zip(a_list, a_hi)]
    b_lo = [(b - h.astype(F32)).astype(BF16) for b, h in zip(b_list, b_hi)]
    top = [dot(jnp.concatenate([h, l], axis=0), b) for h, l, b in zip(a_hi, a_lo, b_hi)]
    low = [dot(h, b) for h, b in zip(a_hi, b_lo)]
    return [t[:m] + (t[m:] + l) for t, l in zip(top, low)]


def _split_dot_many(xs, m, rhs=False, pieces=3):
    accs = [None] * len(xs)
    xs = list(xs)
    for _ in range(pieces):
        his = [x.astype(BF16) for x in xs]
        parts = [jnp.dot(m, hi, preferred_element_type=F32) if rhs else jnp.dot(hi, m, preferred_element_type=F32)
                 for hi in his]
        accs = [p if a is None else a + p for a, p in zip(accs, parts)]
        xs = [x - hi.astype(F32) for x, hi in zip(xs, his)]
    return accs


def _ab_in_kernel(h_ref, g_ref, w_ref, u_ref, gate_ref, rec_ref, *, s5w, lruw):
    xn = _rms(h_ref[...], g_ref[...])
    z = _bdot(xn, w_ref[...])
    u_ref[...] = z[:, :s5w].astype(u_ref.dtype)
    gate_ref[...] = z[:, s5w:s5w + lruw]
    rec_ref[...] = z[:, s5w + lruw:]


def _ab_in(h, g, w_in, s5w, lruw):
    tp, d = h.shape
    tm = _pick_tile(tp, 512)
    row = lambda n: pl.BlockSpec((tm, n), lambda i: (i, 0))
    return pl.pallas_call(
        functools.partial(_ab_in_kernel, s5w=s5w, lruw=lruw),
        grid=(tp // tm,),
        in_specs=[row(d), _const_spec((1, d)), _const_spec(w_in.shape)],
        out_specs=[row(s5w), row(lruw), row(lruw)],
        out_shape=[jax.ShapeDtypeStruct((tp, s5w), BF16),
                   jax.ShapeDtypeStruct((tp, lruw), F32),
                   jax.ShapeDtypeStruct((tp, lruw), F32)],
        compiler_params=_cparams(("parallel",)),
        name="ab_in",
    )(h, g.reshape(1, d), w_in.astype(BF16))


def _s5_tables(lam_re, lam_im, log_dt, b_re, b_im, c_re, c_im, d_skip):
    g, p = lam_re.shape
    hh = b_re.shape[-1]
    c = S5_CHUNK
    lr, li = lam_re.astype(F32), lam_im.astype(F32)
    dt = jnp.exp(log_dt.astype(F32))[:, None]
    mag = jnp.exp(lr * dt)
    abar_r = mag * jnp.cos(li * dt)
    abar_i = mag * jnp.sin(li * dt)
    den = lr * lr + li * li
    zr = ((abar_r - 1.0) * lr + abar_i * li) / den
    zi = (abar_i * lr - (abar_r - 1.0) * li) / den
    bbar_r = zr[..., None] * b_re - zi[..., None] * b_im
    bbar_i = zr[..., None] * b_im + zi[..., None] * b_re
    bbr_t, bbi_t = jnp.swapaxes(bbar_r, 1, 2), jnp.swapaxes(bbar_i, 1, 2)
    cr_t, ci_t = jnp.swapaxes(c_re, 1, 2), jnp.swapaxes(c_im, 1, 2)

    def powers(steps):
        st = steps.astype(F32)[None, :, None]
        pmag = jnp.exp(st * (lr * dt)[:, None, :])
        ang = st * (li * dt)[:, None, :]
        return pmag * jnp.cos(ang), pmag * jnp.sin(ang)

    down = (c - 1) - jnp.arange(c)
    rev_r, rev_i = powers(down)
    m1_r = rev_r[:, :, None, :] * bbr_t[:, None] - rev_i[:, :, None, :] * bbi_t[:, None]
    m1_i = rev_r[:, :, None, :] * bbi_t[:, None] + rev_i[:, :, None, :] * bbr_t[:, None]
    car = c_re[:, None] * rev_r[:, :, None, :] - c_im[:, None] * rev_i[:, :, None, :]
    cai = c_re[:, None] * rev_i[:, :, None, :] + c_im[:, None] * rev_r[:, :, None, :]
    kern = (jnp.einsum('gqhp,gpj->gqjh', car, bbar_r, precision=HI)
            - jnp.einsum('gqhp,gpj->gqjh', cai, bbar_i, precision=HI))
    is_tau0 = (down == 0).astype(F32)[None, :, None, None]
    kern = kern + is_tau0 * (d_skip[:, None, None, :] * jnp.eye(hh, dtype=F32)[None, None])
    up_r, up_i = powers(jnp.arange(1, c + 1))
    up_r, up_i = jnp.swapaxes(up_r, 1, 2)[..., None], jnp.swapaxes(up_i, 1, 2)[..., None]
    m2_r = cr_t[:, :, None, :] * up_r - ci_t[:, :, None, :] * up_i
    m2_i = -(cr_t[:, :, None, :] * up_i + ci_t[:, :, None, :] * up_r)
    adv_r, adv_i = powers(jnp.full((1,), c))
    gb = S5_LANES // hh
    nj = g // gb

    def rows_sgh(x):
        w = x.shape[-1]
        return jnp.transpose(x.reshape(nj, gb, c, hh, w), (0, 2, 1, 3, 4)).reshape(nj, c * gb * hh, w)

    def place(base, spread, row_group, col_group):
        out = jnp.einsum('jrw,wc->jrc', base.astype(BF16), spread.astype(BF16), preferred_element_type=BF16)
        rg = row_group(lax.broadcasted_iota(jnp.int32, out.shape[1:], 0))
        cg = col_group(lax.broadcasted_iota(jnp.int32, out.shape[1:], 1))
        return jnp.where(rg == cg, out, jnp.zeros((), BF16))

    grp_sgh = lambda r: (r // hh) % gb
    rep = lambda w: jnp.tile(jnp.eye(w, dtype=F32), (1, gb))
    m1 = jnp.concatenate([place(rows_sgh(m), rep(p), grp_sgh, lambda col: col // p) for m in (m1_r, m1_i)], axis=-1)
    krev = place(rows_sgh(kern), rep(hh), grp_sgh, lambda col: col // hh)
    ri = lax.broadcasted_iota(jnp.int32, (c * hh, c * gb * hh), 0)
    ci = lax.broadcasted_iota(jnp.int32, (c * hh, c * gb * hh), 1)
    spread_th = ((ri // hh == ci // (gb * hh)) & (ri % hh == ci % hh)).astype(F32)
    m2 = jnp.concatenate([place(m.reshape(nj, gb * p, c * hh), spread_th, lambda r: r // p,
                                lambda col: (col // hh) % gb) for m in (m2_r, m2_i)], axis=1)
    return (m1.astype(BF16), krev.astype(BF16), m2.astype(BF16),
            adv_r.reshape(nj, 1, gb * p), adv_i.reshape(nj, 1, gb * p))


def _s5_kernel(u_ref, m1_ref, kr_ref, m2_ref, ar_ref, ai_ref, y_ref, xe_ref, xin_ref, st_ref, *, n_chunks, nb):
    nl = S5_LANES
    csz = S5_CHUNK
    sw = ar_ref.shape[-1]

    @pl.when(pl.program_id(1) == 0)
    def _():
        st_ref[...] = jnp.zeros(st_ref.shape, F32)

    u = u_ref[0]
    xe_ref[...] = jnp.dot(u, m1_ref[0], preferred_element_type=F32)
    ar = jnp.broadcast_to(ar_ref[0], (nb, sw))
    ai = jnp.broadcast_to(ai_ref[0], (nb, sw))

    def body(c, carry):
        sr, si = carry
        off = pl.multiple_of(c * nb, nb)
        xin_ref[pl.ds(off, nb), :] = jnp.concatenate([sr, si], axis=1)
        e = xe_ref[pl.ds(off, nb), :]
        return (ar * sr - ai * si + e[:, :sw], ar * si + ai * sr + e[:, sw:])

    sr, si = lax.fori_loop(0, n_chunks, body, (st_ref[:, :sw], st_ref[:, sw:]))
    st_ref[:, :sw] = sr
    st_ref[:, sw:] = si
    y_ref[0] = _bdot(xin_ref[...], m2_ref[0])
    for t in range(csz):
        y_ref[0, :, t * nl:(t + 1) * nl] += jnp.dot(u[:, :(t + 1) * nl], kr_ref[0, (csz - 1 - t) * nl:, :],
                                                    preferred_element_type=F32)


def _s5_scan(u, tables, bsz, lp, blk0=0):
    m1, krev, m2, adv_r, adv_i = tables
    _, kin, sw2 = m1.shape
    c = S5_CHUNK
    nl = S5_LANES
    nj = u.shape[1] // nl
    nc = lp // c
    cpt = max(d for d in range(1, nc + 1) if nc % d == 0 and d * bsz <= 512)
    rows = cpt * bsz
    ug = jnp.transpose(u.reshape(bsz, nc, c, nj, nl), (3, 1, 0, 2, 4)).reshape(nj, nc * bsz, kin)
    y = pl.pallas_call(
        functools.partial(_s5_kernel, n_chunks=cpt, nb=bsz),
        grid=(nj, nc // cpt),
        in_specs=[pl.BlockSpec((1, rows, kin), lambda j, r: (j, r, 0)),
                  pl.BlockSpec((1, kin, sw2), lambda j, r: (j + blk0, 0, 0)),
                  pl.BlockSpec((1, kin, nl), lambda j, r: (j + blk0, 0, 0)),
                  pl.BlockSpec((1, sw2, kin), lambda j, r: (j + blk0, 0, 0)),
                  pl.BlockSpec((1, 1, sw2 // 2), lambda j, r: (j + blk0, 0, 0)),
                  pl.BlockSpec((1, 1, sw2 // 2), lambda j, r: (j + blk0, 0, 0))],
        out_specs=pl.BlockSpec((1, rows, kin), lambda j, r: (j, r, 0)),
        out_shape=jax.ShapeDtypeStruct((nj, nc * bsz, kin), F32),
        scratch_shapes=[pltpu.VMEM((rows, sw2), F32), pltpu.VMEM((rows, sw2), F32), pltpu.VMEM((bsz, sw2), F32)],
        compiler_params=_cparams(("parallel", "arbitrary")),
        name="s5_scan",
    )(ug, m1, krev, m2, adv_r, adv_i)
    y = jnp.transpose(y.reshape(nj, nc, bsz, c, nl), (2, 1, 3, 0, 4))
    return y.reshape(bsz * lp, nj * nl)


def _lru_kernel(rec_ref, gate_ref, cw_ref, cb_ref, wa_ref, ba_ref, wx_ref, bx_ref, lam_ref,
                o_ref, ext_ref, a_ref, b_ref, h_ref, *, tl):
    t = pl.program_id(1)

    @pl.when(t == 0)
    def _():
        ext_ref[0:8, :] = jnp.zeros((8, ext_ref.shape[1]), F32)
        h_ref[...] = jnp.zeros(h_ref.shape, F32)

    x = rec_ref[...]
    ext_ref[8:, :] = x
    xc = cb_ref[...] + cw_ref[3:4, :] * x
    for k in range(3):
        xc = xc + cw_ref[k:k + 1, :] * ext_ref[5 + k:5 + k + tl, :]
    ext_ref[0:8, :] = x[tl - 8:, :]
    r = _sigmoid(_bdot(xc, wa_ref[...]) + ba_ref[...])
    i = _sigmoid(_bdot(xc, wx_ref[...]) + bx_ref[...])
    log_a = (-LRU_C) * r * _softplus(-lam_ref[...])
    a = jnp.exp(log_a)
    a_ref[...] = a
    b_ref[...] = jnp.sqrt(1.0 - a * a) * (i * xc)

    sub = 8
    unroll = max(u for u in (4, 2, 1) if (tl // sub) % u == 0)
    row = lax.broadcasted_iota(jnp.int32, (sub, a.shape[1]), 0)

    def body(j, h):
        offs = [pl.multiple_of((j * unroll + q) * sub, sub) for q in range(unroll)]
        ab = [a_ref[pl.ds(o, sub), :] for o in offs]
        bb = [b_ref[pl.ds(o, sub), :] for o in offs]
        for dist in (1, 2, 4):
            keep = row >= dist
            bb = [jnp.where(keep, x * pltpu.roll(y, dist, axis=0) + y, y) for x, y in zip(ab, bb)]
            ab = [jnp.where(keep, x * pltpu.roll(x, dist, axis=0), x) for x in ab]
        for o, x, y in zip(offs, ab, bb):
            hb = x * h + y
            o_ref[pl.ds(o, sub), :] = hb * _gelu(gate_ref[pl.ds(o, sub), :])
            h = hb[sub - 1:sub, :]
        return h

    h_ref[...] = lax.fori_loop(0, tl // (sub * unroll), body, h_ref[...])


def _lru(rec, gate, conv_w, conv_b, w_a, b_a, w_x, b_x, lam, bsz, lp):
    tp, w = rec.shape
    tl = _pick_tile(lp, 1056)
    nt = lp // tl
    heads, hd, _ = w_a.shape

    def dense(wb):
        eye = jnp.eye(heads, dtype=F32)
        return jnp.einsum('hij,hg->higj', wb, eye).reshape(w, w).astype(BF16)

    row = pl.BlockSpec((tl, w), lambda b, t: (b * nt + t, 0))
    vec = _const_spec((1, w))
    return pl.pallas_call(
        functools.partial(_lru_kernel, tl=tl),
        grid=(bsz, nt),
        in_specs=[row, row, _const_spec((4, w)), vec, _const_spec((w, w)), vec, _const_spec((w, w)), vec, vec],
        out_specs=row,
        out_shape=jax.ShapeDtypeStruct((tp, w), F32),
        scratch_shapes=[pltpu.VMEM((tl + 8, w), F32), pltpu.VMEM((tl, w), F32),
                        pltpu.VMEM((tl, w), F32), pltpu.VMEM((1, w), F32)],
        compiler_params=_cparams(("parallel", "arbitrary")),
        name="rglru",
    )(rec, gate, conv_w, conv_b.reshape(1, w), dense(w_a), b_a.reshape(1, w), dense(w_x), b_x.reshape(1, w),
      lam.reshape(1, w))


def _ab_out_kernel(y5_ref, lru_ref, h_ref, wglu_ref, bglu_ref, na_ref, nb_ref, wo_ref, gf_ref, wr_ref, br_ref,
                   o_ref, xn_ref, rt_ref, cnt_ref, *, s5w):
    y = _gelu(y5_ref[...])
    ya = y * _sigmoid(_bdot(y, wglu_ref[...]) + bglu_ref[...])
    ya = _rms(ya, na_ref[...])
    yb = _rms(lru_ref[...], nb_ref[...])
    h = h_ref[...] + _bdot(ya, wo_ref[:s5w, :]) + _bdot(yb, wo_ref[s5w:, :])
    o_ref[...] = h
    _route_and_count(h, gf_ref, wr_ref, br_ref, xn_ref, rt_ref, cnt_ref)


def _ab_out(y5, lru, h, w_glu, b_glu, norm_a, norm_b, w_out, g_ffn, router):
    tp, d = h.shape
    s5w, lruw = y5.shape[1], lru.shape[1]
    wr, br = router
    tm = _pick_tile(tp, 512)
    row = lambda n: pl.BlockSpec((tm, n), lambda i: (i, 0))
    return pl.pallas_call(
        functools.partial(_ab_out_kernel, s5w=s5w),
        grid=(tp // tm,),
        in_specs=[row(s5w), row(lruw), row(d), _const_spec((s5w, s5w)), _const_spec((1, s5w)),
                  _const_spec((1, s5w)), _const_spec((1, lruw)), _const_spec(w_out.shape),
                  _const_spec((1, d)), _const_spec(wr.shape), _const_spec(br.shape)],
        out_specs=[row(d), row(d), row(ROUTE_LANES), _const_spec((1, ROUTE_LANES))],
        out_shape=[jax.ShapeDtypeStruct((tp, d), F32), jax.ShapeDtypeStruct((tp, d), F32),
                   jax.ShapeDtypeStruct((tp, ROUTE_LANES), F32), jax.ShapeDtypeStruct((1, ROUTE_LANES), F32)],
        compiler_params=_cparams(("arbitrary",)),
        name="ab_out",
    )(y5, lru, h, w_glu.astype(BF16), b_glu.reshape(1, s5w), norm_a.reshape(1, s5w),
      norm_b.reshape(1, lruw), w_out.astype(BF16), g_ffn.reshape(1, d), wr, br)


def _route_tile(h, g, wr_hi, wr_lo, br, seen):
    xn = _rms(h, g)
    m = xn.shape[0]
    x_hi = xn.astype(BF16)
    x_lo = (xn - x_hi.astype(F32)).astype(BF16)
    top = jnp.dot(jnp.concatenate([x_hi, x_lo], axis=0), wr_hi, preferred_element_type=F32)
    lg = top[:m] + (top[m:] + jnp.dot(x_hi, wr_lo, preferred_element_type=F32)) + br
    lane = lax.broadcasted_iota(jnp.int32, lg.shape, 1).astype(F32)
    big = float(lg.shape[1])
    neg = -jnp.inf
    gl = jnp.where(lane < N_GROUPS, lg, neg)
    mg = jnp.max(gl, axis=-1, keepdims=True)
    gidx = jnp.min(jnp.where(gl == mg, lane, big), axis=-1, keepdims=True)
    pg_sel = 1.0 / jnp.sum(jnp.exp(gl - mg), axis=-1, keepdims=True)
    lo = N_GROUPS + EXPERTS_PER_GROUP * gidx
    le = jnp.where(lane >= lo, jnp.where(lane < lo + EXPERTS_PER_GROUP, lg, neg), neg)
    v1 = jnp.max(le, axis=-1, keepdims=True)
    i1 = jnp.min(jnp.where(le == v1, lane, big), axis=-1, keepdims=True)
    le2 = jnp.where(lane == i1, neg, le)
    v2 = jnp.max(le2, axis=-1, keepdims=True)
    i2 = jnp.min(jnp.where(le2 == v2, lane, big), axis=-1, keepdims=True)
    e2 = jnp.exp(v2 - v1)
    w1 = pg_sel / (1.0 + e2)
    w2 = w1 * e2
    hit1, hit2 = lane == i1, lane == i2
    chosen = jnp.where(hit1, 1.0, jnp.where(hit2, 1.0, 0.0))
    earlier = (lax.broadcasted_iota(jnp.int32, (m, m), 1) <= lax.broadcasted_iota(jnp.int32, (m, m), 0)).astype(BF16)
    upto = jnp.dot(earlier, chosen.astype(BF16), preferred_element_type=F32)
    before = seen + upto - chosen
    rank1 = jnp.sum(jnp.where(hit1, before, 0.0), axis=-1, keepdims=True)
    rank2 = jnp.sum(jnp.where(hit2, before, 0.0), axis=-1, keepdims=True)
    rt = jnp.where(lane == 0.0, w1, jnp.where(lane == 1.0, w2, jnp.where(
        lane == 2.0, i1 - N_GROUPS, jnp.where(lane == 3.0, i2 - N_GROUPS, jnp.where(
            lane == 4.0, rank1, jnp.where(lane == 5.0, rank2, 0.0))))))
    return xn, rt, seen + upto[m - 1:m, :]


def _route_and_count(h, gf_ref, wr_ref, br_ref, xn_ref, rt_ref, cnt_ref):
    @pl.when(pl.program_id(0) == 0)
    def _():
        cnt_ref[...] = jnp.zeros(cnt_ref.shape, F32)

    xn, rt_ref[...], cnt_ref[...] = _route_tile(h, gf_ref[...], wr_ref[0], wr_ref[1], br_ref[...], cnt_ref[...])
    xn_ref[...] = xn.astype(xn_ref.dtype)


def _router_table(wr_g, br_g, wr_e, br_e):
    d = wr_g.shape[0]
    wr = jnp.zeros((d, ROUTE_LANES), F32).at[:, :N_GROUPS].set(wr_g).at[:, N_GROUPS:N_GROUPS + N_EXPERTS].set(wr_e)
    br = jnp.zeros((1, ROUTE_LANES), F32).at[0, :N_GROUPS].set(br_g).at[0, N_GROUPS:N_GROUPS + N_EXPERTS].set(br_e)
    wr_hi = wr.astype(BF16)
    wr_lo = (wr - wr_hi.astype(F32)).astype(BF16)
    return jnp.stack([wr_hi, wr_lo]), br


def _gmm_kernel(te_ref, tv_ref, xa_ref, xb_ref, wg_ref, wu_ref, wd_ref, o_ref, wg_bf, wu_bf, wd_bf, *, n_half):
    i = pl.program_id(0)

    @pl.when(jnp.logical_or(i == 0, te_ref[i] != te_ref[jnp.maximum(i - 1, 0)]))
    def _():
        wg_bf[...] = wg_ref[0, 0].astype(BF16)
        wu_bf[...] = wu_ref[0, 0].astype(BF16)
        wd_bf[...] = wd_ref[0, 0].astype(BF16)

    @pl.when(tv_ref[i] != 0)
    def _():
        x = jnp.where(i < n_half, xa_ref[...], xb_ref[...]).astype(BF16)
        hg = jnp.dot(x, wg_bf[...], preferred_element_type=F32)
        hu = jnp.dot(x, wu_bf[...], preferred_element_type=F32)
        hid = hg * _sigmoid(hg) * hu
        o_ref[...] = _bdot(hid, wd_bf[...]).astype(o_ref.dtype)

    @pl.when(tv_ref[i] == 0)
    def _():
        o_ref[...] = jnp.zeros(o_ref.shape, o_ref.dtype)


def _invert_rows_kernel(dest_ref, init_ref, src_ref, *, unroll):
    pltpu.sync_copy(init_ref, src_ref)
    n_tok = dest_ref.shape[0] // 2

    def put(t):
        src_ref[dest_ref[2 * t]] = t
        src_ref[dest_ref[2 * t + 1]] = t

    def body(i, _):
        for u in range(unroll):
            put(i * unroll + u)
        return 0

    lax.fori_loop(0, n_tok // unroll, body, 0)
    for t in range(n_tok - n_tok % unroll, n_tok):
        put(t)


def _invert_rows(dest, init):
    smem = pl.BlockSpec(memory_space=pltpu.SMEM)
    return pl.pallas_call(
        functools.partial(_invert_rows_kernel, unroll=8),
        in_specs=[smem, pl.BlockSpec(memory_space=pl.ANY)],
        out_specs=smem,
        out_shape=jax.ShapeDtypeStruct(init.shape, jnp.int32),
        name="moe_invert",
    )(dest, init)


def _moe(h, xn, rt, cnt, w_gate, w_up, w_down, layer):
    tp, d = h.shape
    f = w_gate.shape[-1]
    gate = rt[:, 0:2]
    eid = rt[:, 2:4].astype(jnp.int32)
    rank = rt[:, 4:6].astype(jnp.int32)
    tmm = MOE_TILE
    na = 2 * tp
    counts = cnt[0, N_GROUPS:N_GROUPS + N_EXPERTS].astype(jnp.int32)
    padded = ((counts + tmm - 1) // tmm) * tmm
    ends = jnp.cumsum(padded)
    starts = ends - padded
    dest = (starts.at[eid].get(mode="promise_in_bounds") + rank).reshape(na)
    n_half = (-(-na // tmm) + N_EXPERTS + 1) // 2
    n_tiles = 2 * n_half
    nrows = n_tiles * tmm
    tile_start = jnp.arange(n_tiles, dtype=jnp.int32) * tmm
    tile_e = jnp.sum((ends[None, :] <= tile_start[:, None]).astype(jnp.int32), axis=1)
    tile_v = (tile_e < N_EXPERTS).astype(jnp.int32)
    tile_e = jnp.minimum(tile_e, N_EXPERTS - 1)
    src = _invert_rows(dest, jnp.arange(nrows, dtype=jnp.int32) % tp)
    take = lambda a, i: a.at[i].get(mode="promise_in_bounds")
    xs_a = take(xn, src[:n_half * tmm])
    xs_b = take(xn, src[n_half * tmm:])
    ys = pl.pallas_call(
        functools.partial(_gmm_kernel, n_half=n_half),
        grid_spec=pltpu.PrefetchScalarGridSpec(
            num_scalar_prefetch=2,
            grid=(n_tiles,),
            in_specs=[pl.BlockSpec((tmm, d), lambda i, te, tv: (jnp.minimum(i, n_half - 1), 0)),
                      pl.BlockSpec((tmm, d), lambda i, te, tv: (jnp.maximum(i - n_half, 0), 0)),
                      pl.BlockSpec((1, 1, d, f), lambda i, te, tv: (layer, te[i], 0, 0)),
                      pl.BlockSpec((1, 1, d, f), lambda i, te, tv: (layer, te[i], 0, 0)),
                      pl.BlockSpec((1, 1, f, d), lambda i, te, tv: (layer, te[i], 0, 0))],
            out_specs=pl.BlockSpec((tmm, d), lambda i, te, tv: (i, 0)),
            scratch_shapes=[pltpu.VMEM((d, f), BF16), pltpu.VMEM((d, f), BF16), pltpu.VMEM((f, d), BF16)],
        ),
        out_shape=jax.ShapeDtypeStruct((nrows, d), MOE_ROW_DTYPE),
        compiler_params=_cparams(("arbitrary",)),
        name="moe_gmm",
    )(tile_e, tile_v, xs_a, xs_b, w_gate, w_up, w_down)
    d2 = dest.reshape(tp, 2)
    return h + gate[:, 0:1] * take(ys, d2[:, 0]) + gate[:, 1:2] * take(ys, d2[:, 1])


def _rw_pre_kernel(*refs, tm, lp, has_vres):
    if has_vres:
        (h_ref, hp_ref, g_ref, mu_ref, wr_ref, wk_ref, wv_ref, w0_ref, wl1_ref, wl2_ref,
         a0_ref, al1_ref, al2_ref, gl1_ref, gl2_ref, vf_ref, v0_ref, vl1_ref, vl2_ref,
         r_ref, k_ref, v_ref, w_ref, a_ref, gg_ref) = refs
    else:
        (h_ref, hp_ref, g_ref, mu_ref, wr_ref, wk_ref, wv_ref, w0_ref, wl1_ref, wl2_ref,
         a0_ref, al1_ref, al2_ref, gl1_ref, gl2_ref,
         r_ref, k_ref, v_ref, w_ref, a_ref, gg_ref) = refs
    i = pl.program_id(0)
    g = g_ref[...]
    x = _rms(h_ref[...], g)
    xp8 = _rms(hp_ref[...], g)
    row = lax.broadcasted_iota(jnp.int32, x.shape, 0)
    prev = jnp.where(row == 0, jnp.broadcast_to(xp8[7:8, :], x.shape), pltpu.roll(x, 1, axis=0))
    first = lax.rem(lp - lax.rem(i * tm, lp), lp)
    prev = jnp.where(row == first, 0.0, prev)
    xx = prev - x
    xr, xw, xk, xv, xa, xg = [x + xx * mu_ref[j:j + 1, :] for j in range(6)]
    r_ref[...] = _bdot(xr, wr_ref[...]).astype(r_ref.dtype)
    k_ref[...] = _bdot(xk, wk_ref[...]).astype(k_ref.dtype)
    v = _bdot(xv, wv_ref[...])
    if has_vres:
        mix = _sigmoid(v0_ref[...] + _bdot(_bdot(xv, vl1_ref[...]), vl2_ref[...]))
        v = v + (vf_ref[...] - v) * mix
    v_ref[...] = v.astype(v_ref.dtype)
    w_ref[...] = -_softplus(-(w0_ref[...] + _bdot(jnp.tanh(_bdot(xw, wl1_ref[...])), wl2_ref[...]))) - 0.5
    a_ref[...] = _sigmoid(a0_ref[...] + _bdot(_bdot(xa, al1_ref[...]), al2_ref[...]))
    gg_ref[...] = _bdot(_sigmoid(_bdot(xg, gl1_ref[...])), gl2_ref[...]).astype(gg_ref.dtype)


def _rw_pre(h, g, mu, w_r, w_k, w_v, w0, w_l1, w_l2, a0, a_l1, a_l2, g_l1, g_l2, lp, v_first, v_res):
    tp, d = h.shape
    tm = _pick_tile(tp, min(512, lp))
    has_vres = v_res is not None
    row = pl.BlockSpec((tm, d), lambda i: (i, 0))
    prev8 = pl.BlockSpec((8, d), lambda i: (jnp.maximum(i * (tm // 8) - 1, 0), 0))
    vec = _const_spec((1, d))
    mu8 = jnp.zeros((8, d), F32).at[:6].set(mu)
    bf = lambda w: w.astype(BF16)
    ins = [h, h, g.reshape(1, d), mu8, bf(w_r), bf(w_k), bf(w_v), w0.reshape(1, d), bf(w_l1), bf(w_l2),
           a0.reshape(1, d), bf(a_l1), bf(a_l2), bf(g_l1), bf(g_l2)]
    specs = [row, prev8, vec, _const_spec((8, d))] + [_const_spec((d, d))] * 3 + [
        vec, _const_spec(w_l1.shape), _const_spec(w_l2.shape),
        vec, _const_spec(a_l1.shape), _const_spec(a_l2.shape), _const_spec(g_l1.shape), _const_spec(g_l2.shape)]
    if has_vres:
        v0, v_l1, v_l2 = v_res
        ins += [v_first, v0.reshape(1, d), bf(v_l1), bf(v_l2)]
        specs += [row, vec, _const_spec(v_l1.shape), _const_spec(v_l2.shape)]
    return pl.pallas_call(
        functools.partial(_rw_pre_kernel, tm=tm, lp=lp, has_vres=has_vres),
        grid=(tp // tm,),
        in_specs=specs,
        out_specs=[row] * 6,
        out_shape=[jax.ShapeDtypeStruct((tp, d), dt) for dt in (BF16, BF16, BF16, F32, F32, BF16)],
        compiler_params=_cparams(("parallel",)),
        name="rwkv_pre",
    )(*ins)


def _rw_scan_kernel(r_ref, w_ref, k_ref, v_ref, a_ref, kk_ref, ka_ref, rk_ref, lw_ref, lb_ref,
                    o_ref, g_scr, h_scr, q_scr, y0_scr, *, n_chunks, unroll, pairs):
    c = RW_CHUNK
    nl = 2 * RW_HEAD
    c2 = 2 * c
    ri = lax.broadcasted_iota(jnp.int32, (c2, nl), 0)
    ci = lax.broadcasted_iota(jnp.int32, (c2, nl), 1)
    own = ((ri >= c) == (ci >= RW_HEAD)).astype(F32)
    t_in = jnp.bitwise_and(ri, c - 1)
    s_in = jnp.bitwise_and(ci, c - 1)
    strict = jnp.where(s_in < t_in, own, 0.0)
    incl = jnp.where(s_in <= t_in, own, 0.0)
    causal2 = jnp.concatenate([strict, incl], axis=0)
    eye_l = (ri == ci).astype(F32)
    same_head = own.astype(BF16)
    ti = lax.broadcasted_iota(jnp.int32, (c, c), 0)
    si = lax.broadcasted_iota(jnp.int32, (c, c), 1)
    tril_c = (si <= ti).astype(BF16)
    lanes = lambda p: slice(p * nl, (p + 1) * nl)

    def stack(x):
        return jnp.concatenate([x, x], axis=0) * own

    def offset(ch):
        return ch * c if isinstance(ch, int) else pl.multiple_of(ch * c, c)

    def groups(fn):
        for p in range(pairs):
            def body(i, _):
                fn([i * unroll + q for q in range(unroll)], p)
                return 0
            lax.fori_loop(0, n_chunks // unroll, body, 0)
            if n_chunks % unroll:
                fn(list(range(n_chunks - n_chunks % unroll, n_chunks)), p)

    def prep(chs, p):
        ln = lanes(p)
        kk_w, ka_w = kk_ref[:, ln], ka_ref[:, ln]
        offs = [offset(ch) for ch in chs]
        r = [r_ref[pl.ds(o, c), ln] for o in offs]
        k = [k_ref[pl.ds(o, c), ln] for o in offs]
        v = [v_ref[pl.ds(o, c), ln] for o in offs]
        a = [a_ref[pl.ds(o, c), ln] for o in offs]
        logw = [-jnp.exp(w_ref[pl.ds(o, c), ln]) for o in offs]
        kk = [x * kk_w for x in k]
        ss = _split_dot_many([x * x for x in kk], same_head, pieces=2)
        kk = [x / jnp.maximum(jnp.sqrt(q), 1e-12) for x, q in zip(kk, ss)]
        kmod = [x * (1.0 + (y - 1.0) * ka_w) for x, y in zip(k, a)]
        cum = _split_dot_many(logw, tril_c, rhs=True, pieces=2)
        p_incl = [jnp.exp(x) for x in cum]
        p_inv = [jnp.exp(-x) for x in cum]
        p_end = [jnp.exp(x[c - 1:c, :] - x) for x in cum]
        kka = [x * y for x, y in zip(kk, a)]
        a_s = [stack(-x * jnp.exp(y - z)) for x, y, z in zip(kk, cum, logw)]
        r_s = [stack(x * y) for x, y in zip(r, p_incl)]
        v_s = [stack(x).astype(BF16) for x in v]
        lhs = [jnp.concatenate([x, y], axis=0).astype(BF16) for x, y in zip(a_s, r_s)]
        rhs = [jnp.concatenate([stack(x * z), stack(y * z)], axis=0).astype(BF16) for x, y, z in zip(kka, kmod, p_inv)]
        big = [lax.dot_general(x, y, (((1,), (1,)), ((), ())), preferred_element_type=F32) for x, y in zip(lhs, rhs)]
        lpow = [(x[:c2, :c2] * strict).astype(BF16) for x in big]
        a_rb = [x[c2:, :c2] * incl for x in big]
        avk = [_bdot(x[:, c2:] * causal2, w) for x, w in zip(big, v_s)]
        x = [jnp.concatenate([p, q[:c2]], axis=1) for p, q in zip(a_s, avk)]
        x = [p + _bdot(q, p) for p, q in zip(x, lpow)]
        for _ in range(5):
            lpow = [jnp.dot(q, q, preferred_element_type=F32).astype(BF16) for q in lpow]
            x = [p + _bdot(q, p) for p, q in zip(x, lpow)]
        xb = [p.astype(BF16) for p in x]
        bh_t = [stack(p * q).T for p, q in zip(kka, p_end)]
        kh_t = [stack(p * q).T for p, q in zip(kmod, p_end)]
        both = [_bdot(jnp.concatenate([u, p], axis=0), z) for u, p, z in zip(a_rb, bh_t, xb)]
        qy = [jnp.concatenate([p, q[c2:]], axis=1) + z[:c2] for p, q, z in zip(r_s, avk, both)]
        gh = [z[c2:] + jnp.concatenate([eye_l * q[c - 1:c, :], _bdot(u, w)], axis=1)
              for z, q, u, w in zip(both, p_incl, kh_t, v_s)]
        for ch, u, q in zip(chs, qy, gh):
            q_scr[p, ch] = u[:c, :nl] + u[c:, :nl]
            y0_scr[p, ch] = u[:c, nl:] + u[c:, nl:]
            g_scr[p, ch] = q[:, :nl]
            h_scr[p, ch] = q[:, nl:]

    groups(prep)

    def advance(ch, states):
        both = _dot3_many([jnp.concatenate([q_scr[p, ch], g_scr[p, ch]], axis=0) for p in range(pairs)], states)
        return (tuple(both[p][c:] + h_scr[p, ch] for p in range(pairs)),
                tuple(both[p][:c] + y0_scr[p, ch] for p in range(pairs)))

    def finish(ch, ys):
        o = offset(ch)
        rk = []
        for p in range(pairs):
            ln = lanes(p)
            kmod = k_ref[pl.ds(o, c), ln] * (1.0 + (a_ref[pl.ds(o, c), ln] - 1.0) * ka_ref[:, ln])
            rk.append(r_ref[pl.ds(o, c), ln] * kmod * rk_ref[:, ln])
        sums = _split_dot_many([jnp.concatenate([u, u * u, q], axis=0) for u, q in zip(ys, rk)], same_head)
        for p, (u, s) in enumerate(zip(ys, sums)):
            ln = lanes(p)
            mean = s[:c] * (1.0 / RW_HEAD)
            var = s[c:2 * c] * (1.0 / RW_HEAD) - mean * mean
            o_ref[pl.ds(o, c), ln] = ((u - mean) * lax.rsqrt(var + LNX_EPS) * lw_ref[:, ln] + lb_ref[:, ln]
                                      + s[2 * c:] * v_ref[pl.ds(o, c), ln])

    def step(ch, carry):
        states, ys = carry
        finish(ch - 1, ys)
        return advance(ch, states)

    carry = advance(0, tuple(jnp.zeros((nl, nl), F32) for _ in range(pairs)))
    _, ys = lax.fori_loop(1, n_chunks, step, carry)
    finish(n_chunks - 1, ys)


def _rw_scan(r, w, k, v, a, k_k, k_a, r_k, lnx_w, lnx_b, bsz, lp):
    tp, d = r.shape
    nl = 2 * RW_HEAD
    pairs = RW_PAIRS
    bw = pairs * nl
    nhp = d // bw
    nc = lp // RW_CHUNK
    seq = pl.BlockSpec((lp, bw), lambda b, hp: (b, hp))
    vec = pl.BlockSpec((1, bw), lambda b, hp: (0, hp))
    c = RW_CHUNK
    return pl.pallas_call(
        functools.partial(_rw_scan_kernel, n_chunks=nc, unroll=RW_UNROLL, pairs=pairs),
        grid=(bsz, nhp),
        in_specs=[seq] * 5 + [vec] * 5,
        out_specs=seq,
        out_shape=jax.ShapeDtypeStruct((tp, d), F32),
        scratch_shapes=[pltpu.VMEM((pairs, nc, nl, nl), F32), pltpu.VMEM((pairs, nc, nl, nl), F32),
                        pltpu.VMEM((pairs, nc, c, nl), F32), pltpu.VMEM((pairs, nc, c, nl), F32)],
        compiler_params=_cparams(("parallel", "parallel")),
        name="rwkv_scan",
    )(r, w, k, v, a, k_k.reshape(1, d), k_a.reshape(1, d), r_k.reshape(1, d),
      lnx_w.reshape(1, d), lnx_b.reshape(1, d))


def _rw_out_kernel(y_ref, g_ref, h_ref, wo_ref, gf_ref, wr_ref, br_ref, o_ref, xn_ref, rt_ref, cnt_ref):
    h = h_ref[...] + _bdot(y_ref[...] * g_ref[...], wo_ref[...])
    o_ref[...] = h
    _route_and_count(h, gf_ref, wr_ref, br_ref, xn_ref, rt_ref, cnt_ref)


def _rw_out(y, g, h, w_o, g_ffn, router):
    tp, d = h.shape
    wr, br = router
    tm = _pick_tile(tp, 512)
    row = lambda n: pl.BlockSpec((tm, n), lambda i: (i, 0))
    return pl.pallas_call(
        _rw_out_kernel,
        grid=(tp // tm,),
        in_specs=[row(d), row(d), row(d), _const_spec((d, d)), _const_spec((1, d)), _const_spec(wr.shape),
                  _const_spec(br.shape)],
        out_specs=[row(d), row(d), row(ROUTE_LANES), _const_spec((1, ROUTE_LANES))],
        out_shape=[jax.ShapeDtypeStruct((tp, d), F32), jax.ShapeDtypeStruct((tp, d), F32),
                   jax.ShapeDtypeStruct((tp, ROUTE_LANES), F32), jax.ShapeDtypeStruct((1, ROUTE_LANES), F32)],
        compiler_params=_cparams(("arbitrary",)),
        name="rwkv_out",
    )(y, g, h, w_o.astype(BF16), g_ffn.reshape(1, d), wr, br)


def _final_norm_kernel(h_ref, g_ref, o_ref, *, first, rows):
    o_ref[0] = _rms(h_ref[first:first + rows, :], g_ref[...]).astype(o_ref.dtype)


def _final_norm(h, g, dtype, bsz, lp, first, rows):
    d = h.shape[1]
    return pl.pallas_call(
        functools.partial(_final_norm_kernel, first=first, rows=rows),
        grid=(bsz,),
        in_specs=[pl.BlockSpec((lp, d), lambda b: (b, 0)), _const_spec((1, d))],
        out_specs=pl.BlockSpec((1, rows, d), lambda b: (b, 0, 0)),
        out_shape=jax.ShapeDtypeStruct((bsz, rows, d), dtype),
        compiler_params=_cparams(("parallel",)),
        name="final_norm",
    )(h, g.reshape(1, d))


def kernel(x, meta_tokens, norm_mix, norm_ffn, norm_final, ab_w_in, ab_w_out, ab_norm_a, ab_norm_b, s5_lam_re, s5_lam_im, s5_log_dt, s5_b_re, s5_b_im, s5_c_re, s5_c_im, s5_d, s5_w_glu, s5_b_glu, lru_conv_w, lru_conv_b, lru_w_a, lru_b_a, lru_w_x, lru_b_x, lru_lam, rw_mu, rw_w_r, rw_w_k, rw_w_v, rw_w_o, rw_w0, rw_w_l1, rw_w_l2, rw_a0, rw_a_l1, rw_a_l2, rw_v0, rw_v_l1, rw_v_l2, rw_g_l1, rw_g_l2, rw_k_k, rw_k_a, rw_r_k, rw_lnx_w, rw_lnx_b, moe_router_g, moe_router_g_b, moe_router_e, moe_router_e_b, moe_w_gate, moe_w_up, moe_w_down):
    bsz, seq, d = x.shape
    n_meta = meta_tokens.shape[0]
    depth = norm_mix.shape[0]
    ltot = n_meta + seq
    lp = -(-ltot // SEQ_ALIGN) * SEQ_ALIGN
    s5w = s5_w_glu.shape[-1]
    lruw = lru_lam.shape[-1]
    meta = jnp.broadcast_to(meta_tokens.astype(F32)[None], (bsz, n_meta, d))
    h = jnp.concatenate([meta, x.astype(F32), jnp.zeros((bsz, lp - ltot, d), F32)], axis=1).reshape(bsz * lp, d)
    v_first = None
    flat = lambda a: a.reshape((-1,) + a.shape[2:])
    s5_tables = _s5_tables(flat(s5_lam_re), flat(s5_lam_im), flat(s5_log_dt), flat(s5_b_re), flat(s5_b_im),
                           flat(s5_c_re), flat(s5_c_im), flat(s5_d))
    for layer in range(depth):
        j = layer // 2
        router = _router_table(moe_router_g[layer], moe_router_g_b[layer], moe_router_e[layer], moe_router_e_b[layer])
        if layer % 2 == 0:
            u, gate, rec = _ab_in(h, norm_mix[layer], ab_w_in[j], s5w, lruw)
            y5 = _s5_scan(u, s5_tables, bsz, lp, j * (s5w // S5_LANES))
            lru = _lru(rec, gate, lru_conv_w[j], lru_conv_b[j], lru_w_a[j], lru_b_a[j], lru_w_x[j], lru_b_x[j],
                       lru_lam[j], bsz, lp)
            h, xn, rt, cnt = _ab_out(y5, lru, h, s5_w_glu[j], s5_b_glu[j], ab_norm_a[j], ab_norm_b[j], ab_w_out[j],
                                norm_ffn[layer], router)
        else:
            v_res = (rw_v0[j - 1], rw_v_l1[j - 1], rw_v_l2[j - 1]) if j > 0 else None
            r, k, v, w, a, g = _rw_pre(h, norm_mix[layer], rw_mu[j], rw_w_r[j], rw_w_k[j], rw_w_v[j], rw_w0[j],
                                       rw_w_l1[j], rw_w_l2[j], rw_a0[j], rw_a_l1[j], rw_a_l2[j], rw_g_l1[j],
                                       rw_g_l2[j], lp, v_first, v_res)
            if v_first is None:
                v_first = v
            y = _rw_scan(r, w, k, v, a, rw_k_k[j], rw_k_a[j], rw_r_k[j].reshape(-1), rw_lnx_w[j], rw_lnx_b[j],
                         bsz, lp)
            h, xn, rt, cnt = _rw_out(y, g, h, rw_w_o[j], norm_ffn[layer], router)
        h = _moe(h, xn, rt, cnt, moe_w_gate, moe_w_up, moe_w_down, layer)
    return _final_norm(h, norm_final, x.dtype, bsz, lp, n_meta, seq)
```

```python
import functools
import math

import jax
import jax.numpy as jnp
from jax import lax
from jax.experimental import pallas as pl
from jax.experimental.pallas import tpu as pltpu

F32 = jnp.float32
BF16 = jnp.bfloat16
HI = lax.Precision.HIGHEST

RMS_EPS = 1e-6
LNX_EPS = 64e-5
N_META = 16
SEQ_ALIGN = 64
S5_CHUNK = 16
S5_LANES = 128
RW_CHUNK = 64
RW_HEAD = 64
RW_PAIRS = 2
RW_UNROLL = 11
LRU_C = 8.0
N_GROUPS = 4
EXPERTS_PER_GROUP = 4
N_EXPERTS = N_GROUPS * EXPERTS_PER_GROUP
MOE_TILE = 512
ROUTE_LANES = 128
MOE_ROW_DTYPE = BF16
VMEM_LIMIT = 56 * 1024 * 1024


def _cparams(sem):
    return pltpu.CompilerParams(dimension_semantics=sem, vmem_limit_bytes=VMEM_LIMIT)


def _pick_tile(n, target):
    best = 8
    for t in range(8, min(n, target) + 1, 8):
        if n % t == 0:
            best = t
    return best


def _const_spec(shape):
    nd = len(shape)
    return pl.BlockSpec(shape, lambda *_: (0,) * nd)


def _rms(x, g):
    return x * lax.rsqrt(jnp.mean(x * x, axis=-1, keepdims=True) + RMS_EPS) * g


def _gelu(x):
    return 0.5 * x * (1.0 + jnp.tanh(math.sqrt(2.0 / math.pi) * (x + 0.044715 * (x * x * x))))


def _sigmoid(x):
    return 1.0 / (1.0 + jnp.exp(-x))


def _softplus(x):
    return jnp.maximum(x, 0.0) + jnp.log(1.0 + jnp.exp(-jnp.abs(x)))


def _bdot(a, b):
    return jnp.dot(a.astype(BF16), b.astype(BF16), preferred_element_type=F32)


def _hdot(a, b):
    return jnp.dot(a, b, preferred_element_type=F32, precision=HI)


def _dot3(a, b):
    m = a.shape[0]
    a_hi = a.astype(BF16)
    b_hi = b.astype(BF16)
    a_lo = (a - a_hi.astype(F32)).astype(BF16)
    b_lo = (b - b_hi.astype(F32)).astype(BF16)
    dot = functools.partial(jnp.dot, preferred_element_type=F32)
    top = dot(jnp.concatenate([a_hi, a_lo], axis=0), b_hi)
    return top[:m] + (top[m:] + dot(a_hi, b_lo))


def _dot3_many(a_list, b_list):
    m = a_list[0].shape[0]
    dot = functools.partial(jnp.dot, preferred_element_type=F32)
    a_hi = [a.astype(BF16) for a in a_list]
    b_hi = [b.astype(BF16) for b in b_list]
    a_lo = [(a - h.astype(F32)).astype(BF16) for a, h in zip(a_list, a_hi)]
    b_lo = [(b - h.astype(F32)).astype(BF16) for b, h in zip(b_list, b_hi)]
    top = [dot(jnp.concatenate([h, l], axis=0), b) for h, l, b in zip(a_hi, a_lo, b_hi)]
    low = [dot(h, b) for h, b in zip(a_hi, b_lo)]
    return [t[:m] + (t[m:] + l) for t, l in zip(top, low)]


def _split_dot_many(xs, m, rhs=False, pieces=3):
    accs = [None] * len(xs)
    xs = list(xs)
    for _ in range(pieces):
        his = [x.astype(BF16) for x in xs]
        parts = [jnp.dot(m, hi, preferred_element_type=F32) if rhs else jnp.dot(hi, m, preferred_element_type=F32)
                 for hi in his]
        accs = [p if a is None else a + p for a, p in zip(accs, parts)]
        xs = [x - hi.astype(F32) for x, hi in zip(xs, his)]
    return accs


def _ab_in_kernel(h_ref, g_ref, w_ref, u_ref, gate_ref, rec_ref, *, s5w, lruw):
    xn = _rms(h_ref[...], g_ref[...])
    z = _bdot(xn, w_ref[...])
    u_ref[...] = z[:, :s5w].astype(u_ref.dtype)
    gate_ref[...] = z[:, s5w:s5w + lruw]
    rec_ref[...] = z[:, s5w + lruw:]


def _ab_in(h, g, w_in, s5w, lruw):
    tp, d = h.shape
    tm = _pick_tile(tp, 512)
    row = lambda n: pl.BlockSpec((tm, n), lambda i: (i, 0))
    return pl.pallas_call(
        functools.partial(_ab_in_kernel, s5w=s5w, lruw=lruw),
        grid=(tp // tm,),
        in_specs=[row(d), _const_spec((1, d)), _const_spec(w_in.shape)],
        out_specs=[row(s5w), row(lruw), row(lruw)],
        out_shape=[jax.ShapeDtypeStruct((tp, s5w), BF16),
                   jax.ShapeDtypeStruct((tp, lruw), F32),
                   jax.ShapeDtypeStruct((tp, lruw), F32)],
        compiler_params=_cparams(("parallel",)),
        name="ab_in",
    )(h, g.reshape(1, d), w_in.astype(BF16))


def _s5_tables(lam_re, lam_im, log_dt, b_re, b_im, c_re, c_im, d_skip):
    g, p = lam_re.shape
    hh = b_re.shape[-1]
    c = S5_CHUNK
    lr, li = lam_re.astype(F32), lam_im.astype(F32)
    dt = jnp.exp(log_dt.astype(F32))[:, None]
    mag = jnp.exp(lr * dt)
    abar_r = mag * jnp.cos(li * dt)
    abar_i = mag * jnp.sin(li * dt)
    den = lr * lr + li * li
    zr = ((abar_r - 1.0) * lr + abar_i * li) / den
    zi = (abar_i * lr - (abar_r - 1.0) * li) / den
    bbar_r = zr[..., None] * b_re - zi[..., None] * b_im
    bbar_i = zr[..., None] * b_im + zi[..., None] * b_re
    bbr_t, bbi_t = jnp.swapaxes(bbar_r, 1, 2), jnp.swapaxes(bbar_i, 1, 2)
    cr_t, ci_t = jnp.swapaxes(c_re, 1, 2), jnp.swapaxes(c_im, 1, 2)

    def powers(steps):
        st = steps.astype(F32)[None, :, None]
        pmag = jnp.exp(st * (lr * dt)[:, None, :])
        ang = st * (li * dt)[:, None, :]
        return pmag * jnp.cos(ang), pmag * jnp.sin(ang)

    down = (c - 1) - jnp.arange(c)
    rev_r, rev_i = powers(down)
    m1_r = rev_r[:, :, None, :] * bbr_t[:, None] - rev_i[:, :, None, :] * bbi_t[:, None]
    m1_i = rev_r[:, :, None, :] * bbi_t[:, None] + rev_i[:, :, None, :] * bbr_t[:, None]
    car = c_re[:, None] * rev_r[:, :, None, :] - c_im[:, None] * rev_i[:, :, None, :]
    cai = c_re[:, None] * rev_i[:, :, None, :] + c_im[:, None] * rev_r[:, :, None, :]
    kern = (jnp.einsum('gqhp,gpj->gqjh', car, bbar_r, precision=HI)
            - jnp.einsum('gqhp,gpj->gqjh', cai, bbar_i, precision=HI))
    is_tau0 = (down == 0).astype(F32)[None, :, None, None]
    kern = kern + is_tau0 * (d_skip[:, None, None, :] * jnp.eye(hh, dtype=F32)[None, None])
    up_r, up_i = powers(jnp.arange(1, c + 1))
    up_r, up_i = jnp.swapaxes(up_r, 1, 2)[..., None], jnp.swapaxes(up_i, 1, 2)[..., None]
    m2_r = cr_t[:, :, None, :] * up_r - ci_t[:, :, None, :] * up_i
    m2_i = -(cr_t[:, :, None, :] * up_i + ci_t[:, :, None, :] * up_r)
    adv_r, adv_i = powers(jnp.full((1,), c))
    gb = S5_LANES // hh
    nj = g // gb

    def rows_sgh(x):
        w = x.shape[-1]
        return jnp.transpose(x.reshape(nj, gb, c, hh, w), (0, 2, 1, 3, 4)).reshape(nj, c * gb * hh, w)

    def place(base, spread, row_group, col_group):
        out = jnp.einsum('jrw,wc->jrc', base.astype(BF16), spread.astype(BF16), preferred_element_type=BF16)
        rg = row_group(lax.broadcasted_iota(jnp.int32, out.shape[1:], 0))
        cg = col_group(lax.broadcasted_iota(jnp.int32, out.shape[1:], 1))
        return jnp.where(rg == cg, out, jnp.zeros((), BF16))

    grp_sgh = lambda r: (r // hh) % gb
    rep = lambda w: jnp.tile(jnp.eye(w, dtype=F32), (1, gb))
    m1 = jnp.concatenate([place(rows_sgh(m), rep(p), grp_sgh, lambda col: col // p) for m in (m1_r, m1_i)], axis=-1)
    krev = place(rows_sgh(kern), rep(hh), grp_sgh, lambda col: col // hh)
    ri = lax.broadcasted_iota(jnp.int32, (c * hh, c * gb * hh), 0)
    ci = lax.broadcasted_iota(jnp.int32, (c * hh, c * gb * hh), 1)
    spread_th = ((ri // hh == ci // (gb * hh)) & (ri % hh == ci % hh)).astype(F32)
    m2 = jnp.concatenate([place(m.reshape(nj, gb * p, c * hh), spread_th, lambda r: r // p,
                                lambda col: (col // hh) % gb) for m in (m2_r, m2_i)], axis=1)
    return (m1.astype(BF16), krev.astype(BF16), m2.astype(BF16),
            adv_r.reshape(nj, 1, gb * p), adv_i.reshape(nj, 1, gb * p))


def _s5_kernel(u_ref, m1_ref, kr_ref, m2_ref, ar_ref, ai_ref, y_ref, xe_ref, xin_ref, st_ref, *, n_chunks, nb):
    nl = S5_LANES
    csz = S5_CHUNK
    sw = ar_ref.shape[-1]

    @pl.when(pl.program_id(1) == 0)
    def _():
        st_ref[...] = jnp.zeros(st_ref.shape, F32)

    u = u_ref[0]
    xe_ref[...] = jnp.dot(u, m1_ref[0], preferred_element_type=F32)
    ar = jnp.broadcast_to(ar_ref[0], (nb, sw))
    ai = jnp.broadcast_to(ai_ref[0], (nb, sw))

    def body(c, carry):
        sr, si = carry
        off = pl.multiple_of(c * nb, nb)
        xin_ref[pl.ds(off, nb), :] = jnp.concatenate([sr, si], axis=1)
        e = xe_ref[pl.ds(off, nb), :]
        return (ar * sr - ai * si + e[:, :sw], ar * si + ai * sr + e[:, sw:])

    sr, si = lax.fori_loop(0, n_chunks, body, (st_ref[:, :sw], st_ref[:, sw:]))
    st_ref[:, :sw] = sr
    st_ref[:, sw:] = si
    y_ref[0] = _bdot(xin_ref[...], m2_ref[0])
    for t in range(csz):
        y_ref[0, :, t * nl:(t + 1) * nl] += jnp.dot(u[:, :(t + 1) * nl], kr_ref[0, (csz - 1 - t) * nl:, :],
                                                    preferred_element_type=F32)


def _s5_scan(u, tables, bsz, lp, blk0=0):
    m1, krev, m2, adv_r, adv_i = tables
    _, kin, sw2 = m1.shape
    c = S5_CHUNK
    nl = S5_LANES
    nj = u.shape[1] // nl
    nc = lp // c
    cpt = max(d for d in range(1, nc + 1) if nc % d == 0 and d * bsz <= 512)
    rows = cpt * bsz
    ug = jnp.transpose(u.reshape(bsz, nc, c, nj, nl), (3, 1, 0, 2, 4)).reshape(nj, nc * bsz, kin)
    y = pl.pallas_call(
        functools.partial(_s5_kernel, n_chunks=cpt, nb=bsz),
        grid=(nj, nc // cpt),
        in_specs=[pl.BlockSpec((1, rows, kin), lambda j, r: (j, r, 0)),
                  pl.BlockSpec((1, kin, sw2), lambda j, r: (j + blk0, 0, 0)),
                  pl.BlockSpec((1, kin, nl), lambda j, r: (j + blk0, 0, 0)),
                  pl.BlockSpec((1, sw2, kin), lambda j, r: (j + blk0, 0, 0)),
                  pl.BlockSpec((1, 1, sw2 // 2), lambda j, r: (j + blk0, 0, 0)),
                  pl.BlockSpec((1, 1, sw2 // 2), lambda j, r: (j + blk0, 0, 0))],
        out_specs=pl.BlockSpec((1, rows, kin), lambda j, r: (j, r, 0)),
        out_shape=jax.ShapeDtypeStruct((nj, nc * bsz, kin), F32),
        scratch_shapes=[pltpu.VMEM((rows, sw2), F32), pltpu.VMEM((rows, sw2), F32), pltpu.VMEM((bsz, sw2), F32)],
        compiler_params=_cparams(("parallel", "arbitrary")),
        name="s5_scan",
    )(ug, m1, krev, m2, adv_r, adv_i)
    y = jnp.transpose(y.reshape(nj, nc, bsz, c, nl), (2, 1, 3, 0, 4))
    return y.reshape(bsz * lp, nj * nl)


def _lru_kernel(rec_ref, gate_ref, cw_ref, cb_ref, wa_ref, ba_ref, wx_ref, bx_ref, lam_ref,
                o_ref, ext_ref, a_ref, b_ref, h_ref, *, tl):
    t = pl.program_id(1)

    @pl.when(t == 0)
    def _():
        ext_ref[0:8, :] = jnp.zeros((8, ext_ref.shape[1]), F32)
        h_ref[...] = jnp.zeros(h_ref.shape, F32)

    x = rec_ref[...]
    ext_ref[8:, :] = x
    xc = cb_ref[...] + cw_ref[3:4, :] * x
    for k in range(3):
        xc = xc + cw_ref[k:k + 1, :] * ext_ref[5 + k:5 + k + tl, :]
    ext_ref[0:8, :] = x[tl - 8:, :]
    r = _sigmoid(_bdot(xc, wa_ref[...]) + ba_ref[...])
    i = _sigmoid(_bdot(xc, wx_ref[...]) + bx_ref[...])
    log_a = (-LRU_C) * r * _softplus(-lam_ref[...])
    a = jnp.exp(log_a)
    a_ref[...] = a
    b_ref[...] = jnp.sqrt(1.0 - a * a) * (i * xc)

    sub = 8
    unroll = max(u for u in (4, 2, 1) if (tl // sub) % u == 0)
    row = lax.broadcasted_iota(jnp.int32, (sub, a.shape[1]), 0)

    def body(j, h):
        offs = [pl.multiple_of((j * unroll + q) * sub, sub) for q in range(unroll)]
        ab = [a_ref[pl.ds(o, sub), :] for o in offs]
        bb = [b_ref[pl.ds(o, sub), :] for o in offs]
        for dist in (1, 2, 4):
            keep = row >= dist
            bb = [jnp.where(keep, x * pltpu.roll(y, dist, axis=0) + y, y) for x, y in zip(ab, bb)]
            ab = [jnp.where(keep, x * pltpu.roll(x, dist, axis=0), x) for x in ab]
        for o, x, y in zip(offs, ab, bb):
            hb = x * h + y
            o_ref[pl.ds(o, sub), :] = hb * _gelu(gate_ref[pl.ds(o, sub), :])
            h = hb[sub - 1:sub, :]
        return h

    h_ref[...] = lax.fori_loop(0, tl // (sub * unroll), body, h_ref[...])


def _lru(rec, gate, conv_w, conv_b, w_a, b_a, w_x, b_x, lam, bsz, lp):
    tp, w = rec.shape
    tl = _pick_tile(lp, 1056)
    nt = lp // tl
    heads, hd, _ = w_a.shape

    def dense(wb):
        eye = jnp.eye(heads, dtype=F32)
        return jnp.einsum('hij,hg->higj', wb, eye).reshape(w, w).astype(BF16)

    row = pl.BlockSpec((tl, w), lambda b, t: (b * nt + t, 0))
    vec = _const_spec((1, w))
    return pl.pallas_call(
        functools.partial(_lru_kernel, tl=tl),
        grid=(bsz, nt),
        in_specs=[row, row, _const_spec((4, w)), vec, _const_spec((w, w)), vec, _const_spec((w, w)), vec, vec],
        out_specs=row,
        out_shape=jax.ShapeDtypeStruct((tp, w), F32),
        scratch_shapes=[pltpu.VMEM((tl + 8, w), F32), pltpu.VMEM((tl, w), F32),
                        pltpu.VMEM((tl, w), F32), pltpu.VMEM((1, w), F32)],
        compiler_params=_cparams(("parallel", "arbitrary")),
        name="rglru",
    )(rec, gate, conv_w, conv_b.reshape(1, w), dense(w_a), b_a.reshape(1, w), dense(w_x), b_x.reshape(1, w),
      lam.reshape(1, w))


def _ab_out_kernel(y5_ref, lru_ref, h_ref, wglu_ref, bglu_ref, na_ref, nb_ref, wo_ref, gf_ref, wr_ref, br_ref,
                   o_ref, xn_ref, rt_ref, *, s5w):
    y = _gelu(y5_ref[...])
    ya = y * _sigmoid(_bdot(y, wglu_ref[...]) + bglu_ref[...])
    ya = _rms(ya, na_ref[...])
    yb = _rms(lru_ref[...], nb_ref[...])
    h = h_ref[...] + _bdot(ya, wo_ref[:s5w, :]) + _bdot(yb, wo_ref[s5w:, :])
    o_ref[...] = h
    xn, rt_ref[...] = _route_tile(h, gf_ref[...], wr_ref[0], wr_ref[1], br_ref[...])
    xn_ref[...] = xn.astype(xn_ref.dtype)


def _ab_out(y5, lru, h, w_glu, b_glu, norm_a, norm_b, w_out, g_ffn, router):
    tp, d = h.shape
    s5w, lruw = y5.shape[1], lru.shape[1]
    wr, br = router
    tm = _pick_tile(tp, 512)
    row = lambda n: pl.BlockSpec((tm, n), lambda i: (i, 0))
    return pl.pallas_call(
        functools.partial(_ab_out_kernel, s5w=s5w),
        grid=(tp // tm,),
        in_specs=[row(s5w), row(lruw), row(d), _const_spec((s5w, s5w)), _const_spec((1, s5w)),
                  _const_spec((1, s5w)), _const_spec((1, lruw)), _const_spec(w_out.shape),
                  _const_spec((1, d)), _const_spec(wr.shape), _const_spec(br.shape)],
        out_specs=[row(d), row(d), row(ROUTE_LANES)],
        out_shape=[jax.ShapeDtypeStruct((tp, d), F32), jax.ShapeDtypeStruct((tp, d), F32),
                   jax.ShapeDtypeStruct((tp, ROUTE_LANES), F32)],
        compiler_params=_cparams(("parallel",)),
        name="ab_out",
    )(y5, lru, h, w_glu.astype(BF16), b_glu.reshape(1, s5w), norm_a.reshape(1, s5w),
      norm_b.reshape(1, lruw), w_out.astype(BF16), g_ffn.reshape(1, d), wr, br)


def _route_tile(h, g, wr_hi, wr_lo, br):
    xn = _rms(h, g)
    m = xn.shape[0]
    x_hi = xn.astype(BF16)
    x_lo = (xn - x_hi.astype(F32)).astype(BF16)
    top = jnp.dot(jnp.concatenate([x_hi, x_lo], axis=0), wr_hi, preferred_element_type=F32)
    lg = top[:m] + (top[m:] + jnp.dot(x_hi, wr_lo, preferred_element_type=F32)) + br
    lane = lax.broadcasted_iota(jnp.int32, lg.shape, 1).astype(F32)
    big = float(lg.shape[1])
    neg = -jnp.inf
    gl = jnp.where(lane < N_GROUPS, lg, neg)
    mg = jnp.max(gl, axis=-1, keepdims=True)
    gidx = jnp.min(jnp.where(gl == mg, lane, big), axis=-1, keepdims=True)
    pg_sel = 1.0 / jnp.sum(jnp.exp(gl - mg), axis=-1, keepdims=True)
    lo = N_GROUPS + EXPERTS_PER_GROUP * gidx
    le = jnp.where(lane >= lo, jnp.where(lane < lo + EXPERTS_PER_GROUP, lg, neg), neg)
    v1 = jnp.max(le, axis=-1, keepdims=True)
    i1 = jnp.min(jnp.where(le == v1, lane, big), axis=-1, keepdims=True)
    le2 = jnp.where(lane == i1, neg, le)
    v2 = jnp.max(le2, axis=-1, keepdims=True)
    i2 = jnp.min(jnp.where(le2 == v2, lane, big), axis=-1, keepdims=True)
    e2 = jnp.exp(v2 - v1)
    w1 = pg_sel / (1.0 + e2)
    w2 = w1 * e2
    rt = jnp.where(lane == 0.0, w1, jnp.where(lane == 1.0, w2, jnp.where(
        lane == 2.0, i1 - N_GROUPS, jnp.where(lane == 3.0, i2 - N_GROUPS, 0.0))))
    return xn, rt


def _router_table(wr_g, br_g, wr_e, br_e):
    d = wr_g.shape[0]
    wr = jnp.zeros((d, ROUTE_LANES), F32).at[:, :N_GROUPS].set(wr_g).at[:, N_GROUPS:N_GROUPS + N_EXPERTS].set(wr_e)
    br = jnp.zeros((1, ROUTE_LANES), F32).at[0, :N_GROUPS].set(br_g).at[0, N_GROUPS:N_GROUPS + N_EXPERTS].set(br_e)
    wr_hi = wr.astype(BF16)
    wr_lo = (wr - wr_hi.astype(F32)).astype(BF16)
    return jnp.stack([wr_hi, wr_lo]), br


def _gmm_kernel(te_ref, tv_ref, x_ref, wg_ref, wu_ref, wd_ref, prev_ref, o_ref, wg_bf, wu_bf, wd_bf, *, first):
    del prev_ref
    step = pl.program_id(0)
    i = step + first

    @pl.when(jnp.logical_or(step == 0, te_ref[i] != te_ref[jnp.maximum(i - 1, 0)]))
    def _():
        wg_bf[...] = wg_ref[0, 0].astype(BF16)
        wu_bf[...] = wu_ref[0, 0].astype(BF16)
        wd_bf[...] = wd_ref[0, 0].astype(BF16)

    @pl.when(tv_ref[i] != 0)
    def _():
        x = x_ref[...].astype(BF16)
        hg = jnp.dot(x, wg_bf[...], preferred_element_type=F32)
        hu = jnp.dot(x, wu_bf[...], preferred_element_type=F32)
        hid = hg * _sigmoid(hg) * hu
        o_ref[...] = _bdot(hid, wd_bf[...]).astype(o_ref.dtype)

    @pl.when(tv_ref[i] == 0)
    def _():
        o_ref[...] = jnp.zeros(o_ref.shape, o_ref.dtype)


def _invert_rows_kernel(dest_ref, init_ref, src_ref, *, unroll):
    pltpu.sync_copy(init_ref, src_ref)
    n_tok = dest_ref.shape[0] // 2

    def put(t):
        src_ref[dest_ref[2 * t]] = t
        src_ref[dest_ref[2 * t + 1]] = t

    def body(i, _):
        for u in range(unroll):
            put(i * unroll + u)
        return 0

    lax.fori_loop(0, n_tok // unroll, body, 0)
    for t in range(n_tok - n_tok % unroll, n_tok):
        put(t)


def _invert_rows(dest, init):
    smem = pl.BlockSpec(memory_space=pltpu.SMEM)
    return pl.pallas_call(
        functools.partial(_invert_rows_kernel, unroll=8),
        in_specs=[smem, pl.BlockSpec(memory_space=pl.ANY)],
        out_specs=smem,
        out_shape=jax.ShapeDtypeStruct(init.shape, jnp.int32),
        name="moe_invert",
    )(dest, init)


def _moe(h, xn, rt, w_gate, w_up, w_down, layer):
    tp, d = h.shape
    f = w_gate.shape[-1]
    gate = rt[:, 0:2]
    eid = rt[:, 2:4].astype(jnp.int32)
    tmm = MOE_TILE
    na = 2 * tp
    e_flat = eid.reshape(na)
    onehot = (e_flat[:, None] == jnp.arange(N_EXPERTS, dtype=jnp.int32)[None, :]).astype(jnp.int32)
    csum = jnp.cumsum(onehot, axis=0)
    counts = csum[-1]
    padded = ((counts + tmm - 1) // tmm) * tmm
    ends = jnp.cumsum(padded)
    starts = ends - padded
    dest = jnp.sum(onehot * (csum + (starts - 1)[None, :]), axis=1)
    n_half = (-(-na // tmm) + N_EXPERTS + 1) // 2
    n_tiles = 2 * n_half
    nrows = n_tiles * tmm
    tile_start = jnp.arange(n_tiles, dtype=jnp.int32) * tmm
    tile_e = jnp.sum((ends[None, :] <= tile_start[:, None]).astype(jnp.int32), axis=1)
    tile_v = (tile_e < N_EXPERTS).astype(jnp.int32)
    tile_e = jnp.minimum(tile_e, N_EXPERTS - 1)
    src = _invert_rows(dest, jnp.arange(nrows, dtype=jnp.int32) % tp)
    take = lambda a, i: a.at[i].get(mode="promise_in_bounds")
    ys = jnp.zeros((nrows, d), MOE_ROW_DTYPE)
    for first in (0, n_half):
        xs = take(xn, src[first * tmm:(first + n_half) * tmm])
        wspec = lambda shape: pl.BlockSpec((1, 1) + shape, lambda i, te, tv: (layer, te[i + first], 0, 0))
        ys = pl.pallas_call(
            functools.partial(_gmm_kernel, first=first),
            grid_spec=pltpu.PrefetchScalarGridSpec(
                num_scalar_prefetch=2,
                grid=(n_half,),
                in_specs=[pl.BlockSpec((tmm, d), lambda i, te, tv: (i, 0)), wspec((d, f)), wspec((d, f)),
                          wspec((f, d)), pl.BlockSpec(memory_space=pl.ANY)],
                out_specs=pl.BlockSpec((tmm, d), lambda i, te, tv: (i + first, 0)),
                scratch_shapes=[pltpu.VMEM((d, f), BF16), pltpu.VMEM((d, f), BF16), pltpu.VMEM((f, d), BF16)],
            ),
            out_shape=jax.ShapeDtypeStruct((nrows, d), MOE_ROW_DTYPE),
            input_output_aliases={6: 0},
            compiler_params=_cparams(("arbitrary",)),
            name="moe_gmm",
        )(tile_e, tile_v, xs, w_gate, w_up, w_down, ys)
    d2 = dest.reshape(tp, 2)
    return h + gate[:, 0:1] * take(ys, d2[:, 0]) + gate[:, 1:2] * take(ys, d2[:, 1])


def _rw_pre_kernel(*refs, tm, lp, has_vres):
    if has_vres:
        (h_ref, hp_ref, g_ref, mu_ref, wr_ref, wk_ref, wv_ref, w0_ref, wl1_ref, wl2_ref,
         a0_ref, al1_ref, al2_ref, gl1_ref, gl2_ref, vf_ref, v0_ref, vl1_ref, vl2_ref,
         r_ref, k_ref, v_ref, w_ref, a_ref, gg_ref) = refs
    else:
        (h_ref, hp_ref, g_ref, mu_ref, wr_ref, wk_ref, wv_ref, w0_ref, wl1_ref, wl2_ref,
         a0_ref, al1_ref, al2_ref, gl1_ref, gl2_ref,
         r_ref, k_ref, v_ref, w_ref, a_ref, gg_ref) = refs
    i = pl.program_id(0)
    g = g_ref[...]
    x = _rms(h_ref[...], g)
    xp8 = _rms(hp_ref[...], g)
    row = lax.broadcasted_iota(jnp.int32, x.shape, 0)
    prev = jnp.where(row == 0, jnp.broadcast_to(xp8[7:8, :], x.shape), pltpu.roll(x, 1, axis=0))
    first = lax.rem(lp - lax.rem(i * tm, lp), lp)
    prev = jnp.where(row == first, 0.0, prev)
    xx = prev - x
    xr, xw, xk, xv, xa, xg = [x + xx * mu_ref[j:j + 1, :] for j in range(6)]
    r_ref[...] = _bdot(xr, wr_ref[...]).astype(r_ref.dtype)
    k_ref[...] = _bdot(xk, wk_ref[...]).astype(k_ref.dtype)
    v = _bdot(xv, wv_ref[...])
    if has_vres:
        mix = _sigmoid(v0_ref[...] + _bdot(_bdot(xv, vl1_ref[...]), vl2_ref[...]))
        v = v + (vf_ref[...] - v) * mix
    v_ref[...] = v.astype(v_ref.dtype)
    w_ref[...] = -_softplus(-(w0_ref[...] + _bdot(jnp.tanh(_bdot(xw, wl1_ref[...])), wl2_ref[...]))) - 0.5
    a_ref[...] = _sigmoid(a0_ref[...] + _bdot(_bdot(xa, al1_ref[...]), al2_ref[...]))
    gg_ref[...] = _bdot(_sigmoid(_bdot(xg, gl1_ref[...])), gl2_ref[...]).astype(gg_ref.dtype)


def _rw_pre(h, g, mu, w_r, w_k, w_v, w0, w_l1, w_l2, a0, a_l1, a_l2, g_l1, g_l2, lp, v_first, v_res):
    tp, d = h.shape
    tm = _pick_tile(tp, min(512, lp))
    has_vres = v_res is not None
    row = pl.BlockSpec((tm, d), lambda i: (i, 0))
    prev8 = pl.BlockSpec((8, d), lambda i: (jnp.maximum(i * (tm // 8) - 1, 0), 0))
    vec = _const_spec((1, d))
    mu8 = jnp.zeros((8, d), F32).at[:6].set(mu)
    bf = lambda w: w.astype(BF16)
    ins = [h, h, g.reshape(1, d), mu8, bf(w_r), bf(w_k), bf(w_v), w0.reshape(1, d), bf(w_l1), bf(w_l2),
           a0.reshape(1, d), bf(a_l1), bf(a_l2), bf(g_l1), bf(g_l2)]
    specs = [row, prev8, vec, _const_spec((8, d))] + [_const_spec((d, d))] * 3 + [
        vec, _const_spec(w_l1.shape), _const_spec(w_l2.shape),
        vec, _const_spec(a_l1.shape), _const_spec(a_l2.shape), _const_spec(g_l1.shape), _const_spec(g_l2.shape)]
    if has_vres:
        v0, v_l1, v_l2 = v_res
        ins += [v_first, v0.reshape(1, d), bf(v_l1), bf(v_l2)]
        specs += [row, vec, _const_spec(v_l1.shape), _const_spec(v_l2.shape)]
    return pl.pallas_call(
        functools.partial(_rw_pre_kernel, tm=tm, lp=lp, has_vres=has_vres),
        grid=(tp // tm,),
        in_specs=specs,
        out_specs=[row] * 6,
        out_shape=[jax.ShapeDtypeStruct((tp, d), dt) for dt in (BF16, BF16, BF16, F32, F32, BF16)],
        compiler_params=_cparams(("parallel",)),
        name="rwkv_pre",
    )(*ins)


def _rw_scan_kernel(r_ref, w_ref, k_ref, v_ref, a_ref, kk_ref, ka_ref, rk_ref, lw_ref, lb_ref,
                    o_ref, g_scr, h_scr, q_scr, y0_scr, *, n_chunks, unroll, pairs):
    c = RW_CHUNK
    nl = 2 * RW_HEAD
    c2 = 2 * c
    ri = lax.broadcasted_iota(jnp.int32, (c2, nl), 0)
    ci = lax.broadcasted_iota(jnp.int32, (c2, nl), 1)
    own = ((ri >= c) == (ci >= RW_HEAD)).astype(F32)
    t_in = jnp.bitwise_and(ri, c - 1)
    s_in = jnp.bitwise_and(ci, c - 1)
    strict = jnp.where(s_in < t_in, own, 0.0)
    incl = jnp.where(s_in <= t_in, own, 0.0)
    causal2 = jnp.concatenate([strict, incl], axis=0)
    eye_l = (ri == ci).astype(F32)
    same_head = own.astype(BF16)
    ti = lax.broadcasted_iota(jnp.int32, (c, c), 0)
    si = lax.broadcasted_iota(jnp.int32, (c, c), 1)
    tril_c = (si <= ti).astype(BF16)
    lanes = lambda p: slice(p * nl, (p + 1) * nl)

    def stack(x):
        return jnp.concatenate([x, x], axis=0) * own

    def offset(ch):
        return ch * c if isinstance(ch, int) else pl.multiple_of(ch * c, c)

    def groups(fn):
        for p in range(pairs):
            def body(i, _):
                fn([i * unroll + q for q in range(unroll)], p)
                return 0
            lax.fori_loop(0, n_chunks // unroll, body, 0)
            if n_chunks % unroll:
                fn(list(range(n_chunks - n_chunks % unroll, n_chunks)), p)

    def prep(chs, p):
        ln = lanes(p)
        kk_w, ka_w = kk_ref[:, ln], ka_ref[:, ln]
        offs = [offset(ch) for ch in chs]
        r = [r_ref[pl.ds(o, c), ln] for o in offs]
        k = [k_ref[pl.ds(o, c), ln] for o in offs]
        v = [v_ref[pl.ds(o, c), ln] for o in offs]
        a = [a_ref[pl.ds(o, c), ln] for o in offs]
        logw = [-jnp.exp(w_ref[pl.ds(o, c), ln]) for o in offs]
        kk = [x * kk_w for x in k]
        ss = _split_dot_many([x * x for x in kk], same_head, pieces=2)
        kk = [x / jnp.maximum(jnp.sqrt(q), 1e-12) for x, q in zip(kk, ss)]
        kmod = [x * (1.0 + (y - 1.0) * ka_w) for x, y in zip(k, a)]
        cum = _split_dot_many(logw, tril_c, rhs=True, pieces=2)
        p_incl = [jnp.exp(x) for x in cum]
        p_inv = [jnp.exp(-x) for x in cum]
        p_end = [jnp.exp(x[c - 1:c, :] - x) for x in cum]
        kka = [x * y for x, y in zip(kk, a)]
        a_s = [stack(-x * jnp.exp(y - z)) for x, y, z in zip(kk, cum, logw)]
        r_s = [stack(x * y) for x, y in zip(r, p_incl)]
        v_s = [stack(x).astype(BF16) for x in v]
        lhs = [jnp.concatenate([x, y], axis=0).astype(BF16) for x, y in zip(a_s, r_s)]
        rhs = [jnp.concatenate([stack(x * z), stack(y * z)], axis=0).astype(BF16) for x, y, z in zip(kka, kmod, p_inv)]
        big = [lax.dot_general(x, y, (((1,), (1,)), ((), ())), preferred_element_type=F32) for x, y in zip(lhs, rhs)]
        lpow = [(x[:c2, :c2] * strict).astype(BF16) for x in big]
        a_rb = [x[c2:, :c2] * incl for x in big]
        avk = [_bdot(x[:, c2:] * causal2, w) for x, w in zip(big, v_s)]
        x = [jnp.concatenate([p, q[:c2]], axis=1) for p, q in zip(a_s, avk)]
        x = [p + _bdot(q, p) for p, q in zip(x, lpow)]
        for _ in range(5):
            lpow = [jnp.dot(q, q, preferred_element_type=F32).astype(BF16) for q in lpow]
            x = [p + _bdot(q, p) for p, q in zip(x, lpow)]
        xb = [p.astype(BF16) for p in x]
        bh_t = [stack(p * q).T for p, q in zip(kka, p_end)]
        kh_t = [stack(p * q).T for p, q in zip(kmod, p_end)]
        both = [_bdot(jnp.concatenate([u, p], axis=0), z) for u, p, z in zip(a_rb, bh_t, xb)]
        qy = [jnp.concatenate([p, q[c2:]], axis=1) + z[:c2] for p, q, z in zip(r_s, avk, both)]
        gh = [z[c2:] + jnp.concatenate([eye_l * q[c - 1:c, :], _bdot(u, w)], axis=1)
              for z, q, u, w in zip(both, p_incl, kh_t, v_s)]
        for ch, u, q in zip(chs, qy, gh):
            q_scr[p, ch] = u[:c, :nl] + u[c:, :nl]
            y0_scr[p, ch] = u[:c, nl:] + u[c:, nl:]
            g_scr[p, ch] = q[:, :nl]
            h_scr[p, ch] = q[:, nl:]

    groups(prep)

    def advance(ch, states):
        both = _dot3_many([jnp.concatenate([q_scr[p, ch], g_scr[p, ch]], axis=0) for p in range(pairs)], states)
        return (tuple(both[p][c:] + h_scr[p, ch] for p in range(pairs)),
                tuple(both[p][:c] + y0_scr[p, ch] for p in range(pairs)))

    def finish(ch, ys):
        o = offset(ch)
        rk = []
        for p in range(pairs):
            ln = lanes(p)
            kmod = k_ref[pl.ds(o, c), ln] * (1.0 + (a_ref[pl.ds(o, c), ln] - 1.0) * ka_ref[:, ln])
            rk.append(r_ref[pl.ds(o, c), ln] * kmod * rk_ref[:, ln])
        sums = _split_dot_many([jnp.concatenate([u, u * u, q], axis=0) for u, q in zip(ys, rk)], same_head)
        for p, (u, s) in enumerate(zip(ys, sums)):
            ln = lanes(p)
            mean = s[:c] * (1.0 / RW_HEAD)
            var = s[c:2 * c] * (1.0 / RW_HEAD) - mean * mean
            o_ref[pl.ds(o, c), ln] = ((u - mean) * lax.rsqrt(var + LNX_EPS) * lw_ref[:, ln] + lb_ref[:, ln]
                                      + s[2 * c:] * v_ref[pl.ds(o, c), ln])

    def step(ch, carry):
        states, ys = carry
        finish(ch - 1, ys)
        return advance(ch, states)

    carry = advance(0, tuple(jnp.zeros((nl, nl), F32) for _ in range(pairs)))
    _, ys = lax.fori_loop(1, n_chunks, step, carry)
    finish(n_chunks - 1, ys)


def _rw_scan(r, w, k, v, a, k_k, k_a, r_k, lnx_w, lnx_b, bsz, lp):
    tp, d = r.shape
    nl = 2 * RW_HEAD
    pairs = RW_PAIRS
    bw = pairs * nl
    nhp = d // bw
    nc = lp // RW_CHUNK
    seq = pl.BlockSpec((lp, bw), lambda b, hp: (b, hp))
    vec = pl.BlockSpec((1, bw), lambda b, hp: (0, hp))
    c = RW_CHUNK
    return pl.pallas_call(
        functools.partial(_rw_scan_kernel, n_chunks=nc, unroll=RW_UNROLL, pairs=pairs),
        grid=(bsz, nhp),
        in_specs=[seq] * 5 + [vec] * 5,
        out_specs=seq,
        out_shape=jax.ShapeDtypeStruct((tp, d), F32),
        scratch_shapes=[pltpu.VMEM((pairs, nc, nl, nl), F32), pltpu.VMEM((pairs, nc, nl, nl), F32),
                        pltpu.VMEM((pairs, nc, c, nl), F32), pltpu.VMEM((pairs, nc, c, nl), F32)],
        compiler_params=_cparams(("parallel", "parallel")),
        name="rwkv_scan",
    )(r, w, k, v, a, k_k.reshape(1, d), k_a.reshape(1, d), r_k.reshape(1, d),
      lnx_w.reshape(1, d), lnx_b.reshape(1, d))


def _rw_out_kernel(y_ref, g_ref, h_ref, wo_ref, gf_ref, wr_ref, br_ref, o_ref, xn_ref, rt_ref):
    h = h_ref[...] + _bdot(y_ref[...] * g_ref[...], wo_ref[...])
    o_ref[...] = h
    xn, rt_ref[...] = _route_tile(h, gf_ref[...], wr_ref[0], wr_ref[1], br_ref[...])
    xn_ref[...] = xn.astype(xn_ref.dtype)


def _rw_out(y, g, h, w_o, g_ffn, router):
    tp, d = h.shape
    wr, br = router
    tm = _pick_tile(tp, 512)
    row = lambda n: pl.BlockSpec((tm, n), lambda i: (i, 0))
    return pl.pallas_call(
        _rw_out_kernel,
        grid=(tp // tm,),
        in_specs=[row(d), row(d), row(d), _const_spec((d, d)), _const_spec((1, d)), _const_spec(wr.shape),
                  _const_spec(br.shape)],
        out_specs=[row(d), row(d), row(ROUTE_LANES)],
        out_shape=[jax.ShapeDtypeStruct((tp, d), F32), jax.ShapeDtypeStruct((tp, d), F32),
                   jax.ShapeDtypeStruct((tp, ROUTE_LANES), F32)],
        compiler_params=_cparams(("parallel",)),
        name="rwkv_out",
    )(y, g, h, w_o.astype(BF16), g_ffn.reshape(1, d), wr, br)


def _final_norm_kernel(h_ref, g_ref, o_ref, *, first, rows):
    o_ref[0] = _rms(h_ref[first:first + rows, :], g_ref[...]).astype(o_ref.dtype)


def _final_norm(h, g, dtype, bsz, lp, first, rows):
    d = h.shape[1]
    return pl.pallas_call(
        functools.partial(_final_norm_kernel, first=first, rows=rows),
        grid=(bsz,),
        in_specs=[pl.BlockSpec((lp, d), lambda b: (b, 0)), _const_spec((1, d))],
        out_specs=pl.BlockSpec((1, rows, d), lambda b: (b, 0, 0)),
        out_shape=jax.ShapeDtypeStruct((bsz, rows, d), dtype),
        compiler_params=_cparams(("parallel",)),
        name="final_norm",
    )(h, g.reshape(1, d))


def kernel(x, meta_tokens, norm_mix, norm_ffn, norm_final, ab_w_in, ab_w_out, ab_norm_a, ab_norm_b, s5_lam_re, s5_lam_im, s5_log_dt, s5_b_re, s5_b_im, s5_c_re, s5_c_im, s5_d, s5_w_glu, s5_b_glu, lru_conv_w, lru_conv_b, lru_w_a, lru_b_a, lru_w_x, lru_b_x, lru_lam, rw_mu, rw_w_r, rw_w_k, rw_w_v, rw_w_o, rw_w0, rw_w_l1, rw_w_l2, rw_a0, rw_a_l1, rw_a_l2, rw_v0, rw_v_l1, rw_v_l2, rw_g_l1, rw_g_l2, rw_k_k, rw_k_a, rw_r_k, rw_lnx_w, rw_lnx_b, moe_router_g, moe_router_g_b, moe_router_e, moe_router_e_b, moe_w_gate, moe_w_up, moe_w_down):
    bsz, seq, d = x.shape
    n_meta = meta_tokens.shape[0]
    depth = norm_mix.shape[0]
    ltot = n_meta + seq
    lp = -(-ltot // SEQ_ALIGN) * SEQ_ALIGN
    s5w = s5_w_glu.shape[-1]
    lruw = lru_lam.shape[-1]
    meta = jnp.broadcast_to(meta_tokens.astype(F32)[None], (bsz, n_meta, d))
    h = jnp.concatenate([meta, x.astype(F32), jnp.zeros((bsz, lp - ltot, d), F32)], axis=1).reshape(bsz * lp, d)
    v_first = None
    flat = lambda a: a.reshape((-1,) + a.shape[2:])
    s5_tables = _s5_tables(flat(s5_lam_re), flat(s5_lam_im), flat(s5_log_dt), flat(s5_b_re), flat(s5_b_im),
                           flat(s5_c_re), flat(s5_c_im), flat(s5_d))
    for layer in range(depth):
        j = layer // 2
        router = _router_table(moe_router_g[layer], moe_router_g_b[layer], moe_router_e[layer], moe_router_e_b[layer])
        if layer % 2 == 0:
            u, gate, rec = _ab_in(h, norm_mix[layer], ab_w_in[j], s5w, lruw)
            y5 = _s5_scan(u, s5_tables, bsz, lp, j * (s5w // S5_LANES))
            lru = _lru(rec, gate, lru_conv_w[j], lru_conv_b[j], lru_w_a[j], lru_b_a[j], lru_w_x[j], lru_b_x[j],
                       lru_lam[j], bsz, lp)
            h, xn, rt = _ab_out(y5, lru, h, s5_w_glu[j], s5_b_glu[j], ab_norm_a[j], ab_norm_b[j], ab_w_out[j],
                                norm_ffn[layer], router)
        else:
            v_res = (rw_v0[j - 1], rw_v_l1[j - 1], rw_v_l2[j - 1]) if j > 0 else None
            r, k, v, w, a, g = _rw_pre(h, norm_mix[layer], rw_mu[j], rw_w_r[j], rw_w_k[j], rw_w_v[j], rw_w0[j],
                                       rw_w_l1[j], rw_w_l2[j], rw_a0[j], rw_a_l1[j], rw_a_l2[j], rw_g_l1[j],
                                       rw_g_l2[j], lp, v_first, v_res)
            if v_first is None:
                v_first = v
            y = _rw_scan(r, w, k, v, a, rw_k_k[j], rw_k_a[j], rw_r_k[j].reshape(-1), rw_lnx_w[j], rw_lnx_b[j],
                         bsz, lp)
            h, xn, rt = _rw_out(y, g, h, rw_w_o[j], norm_ffn[layer], router)
        h = _moe(h, xn, rt, moe_w_gate, moe_w_up, moe_w_down, layer)
    return _final_norm(h, norm_final, x.dtype, bsz, lp, n_meta, seq)
```

```python
import functools
import math

import jax
import jax.numpy as jnp
from jax import lax
from jax.experimental import pallas as pl
from jax.experimental.pallas import tpu as pltpu

F32 = jnp.float32
BF16 = jnp.bfloat16
HI = lax.Precision.HIGHEST

RMS_EPS = 1e-6
LNX_EPS = 64e-5
N_META = 16
SEQ_ALIGN = 64
S5_CHUNK = 16
S5_LANES = 128
RW_CHUNK = 64
RW_HEAD = 64
RW_PAIRS = 2
RW_UNROLL = 11
LRU_C = 8.0
N_GROUPS = 4
EXPERTS_PER_GROUP = 4
N_EXPERTS = N_GROUPS * EXPERTS_PER_GROUP
MOE_TILE = 512
ROUTE_LANES = 128
MOE_ROW_DTYPE = BF16
VMEM_LIMIT = 56 * 1024 * 1024


def _cparams(sem):
    return pltpu.CompilerParams(dimension_semantics=sem, vmem_limit_bytes=VMEM_LIMIT)


def _pick_tile(n, target):
    best = 8
    for t in range(8, min(n, target) + 1, 8):
        if n % t == 0:
            best = t
    return best


def _const_spec(shape):
    nd = len(shape)
    return pl.BlockSpec(shape, lambda *_: (0,) * nd)


def _rms(x, g):
    return x * lax.rsqrt(jnp.mean(x * x, axis=-1, keepdims=True) + RMS_EPS) * g


def _gelu(x):
    return 0.5 * x * (1.0 + jnp.tanh(math.sqrt(2.0 / math.pi) * (x + 0.044715 * (x * x * x))))


def _sigmoid(x):
    return 1.0 / (1.0 + jnp.exp(-x))


def _softplus(x):
    return jnp.maximum(x, 0.0) + jnp.log(1.0 + jnp.exp(-jnp.abs(x)))


def _bdot(a, b):
    return jnp.dot(a.astype(BF16), b.astype(BF16), preferred_element_type=F32)


def _hdot(a, b):
    return jnp.dot(a, b, preferred_element_type=F32, precision=HI)


def _dot3(a, b):
    m = a.shape[0]
    a_hi = a.astype(BF16)
    b_hi = b.astype(BF16)
    a_lo = (a - a_hi.astype(F32)).astype(BF16)
    b_lo = (b - b_hi.astype(F32)).astype(BF16)
    dot = functools.partial(jnp.dot, preferred_element_type=F32)
    top = dot(jnp.concatenate([a_hi, a_lo], axis=0), b_hi)
    return top[:m] + (top[m:] + dot(a_hi, b_lo))


def _dot3_many(a_list, b_list):
    m = a_list[0].shape[0]
    dot = functools.partial(jnp.dot, preferred_element_type=F32)
    a_hi = [a.astype(BF16) for a in a_list]
    b_hi = [b.astype(BF16) for b in b_list]
    a_lo = [(a - h.astype(F32)).astype(BF16) for a, h in zip(a_list, a_hi)]
    b_lo = [(b - h.astype(F32)).astype(BF16) for b, h in zip(b_list, b_hi)]
    top = [dot(jnp.concatenate([h, l], axis=0), b) for h, l, b in zip(a_hi, a_lo, b_hi)]
    low = [dot(h, b) for h, b in zip(a_hi, b_lo)]
    return [t[:m] + (t[m:] + l) for t, l in zip(top, low)]


def _split_dot_many(xs, m, rhs=False, pieces=3):
    accs = [None] * len(xs)
    xs = list(xs)
    for _ in range(pieces):
        his = [x.astype(BF16) for x in xs]
        parts = [jnp.dot(m, hi, preferred_element_type=F32) if rhs else jnp.dot(hi, m, preferred_element_type=F32)
                 for hi in his]
        accs = [p if a is None else a + p for a, p in zip(accs, parts)]
        xs = [x - hi.astype(F32) for x, hi in zip(xs, his)]
    return accs


def _ab_in_kernel(h_ref, g_ref, w_ref, u_ref, gate_ref, rec_ref, *, s5w, lruw):
    xn = _rms(h_ref[...], g_ref[...])
    z = _bdot(xn, w_ref[...])
    u_ref[...] = z[:, :s5w].astype(u_ref.dtype)
    gate_ref[...] = z[:, s5w:s5w + lruw]
    rec_ref[...] = z[:, s5w + lruw:]


def _ab_in(h, g, w_in, s5w, lruw):
    tp, d = h.shape
    tm = _pick_tile(tp, 512)
    row = lambda n: pl.BlockSpec((tm, n), lambda i: (i, 0))
    return pl.pallas_call(
        functools.partial(_ab_in_kernel, s5w=s5w, lruw=lruw),
        grid=(tp // tm,),
        in_specs=[row(d), _const_spec((1, d)), _const_spec(w_in.shape)],
        out_specs=[row(s5w), row(lruw), row(lruw)],
        out_shape=[jax.ShapeDtypeStruct((tp, s5w), BF16),
                   jax.ShapeDtypeStruct((tp, lruw), F32),
                   jax.ShapeDtypeStruct((tp, lruw), F32)],
        compiler_params=_cparams(("parallel",)),
        name="ab_in",
    )(h, g.reshape(1, d), w_in.astype(BF16))


def _s5_tables(lam_re, lam_im, log_dt, b_re, b_im, c_re, c_im, d_skip):
    g, p = lam_re.shape
    hh = b_re.shape[-1]
    c = S5_CHUNK
    lr, li = lam_re.astype(F32), lam_im.astype(F32)
    dt = jnp.exp(log_dt.astype(F32))[:, None]
    mag = jnp.exp(lr * dt)
    abar_r = mag * jnp.cos(li * dt)
    abar_i = mag * jnp.sin(li * dt)
    den = lr * lr + li * li
    zr = ((abar_r - 1.0) * lr + abar_i * li) / den
    zi = (abar_i * lr - (abar_r - 1.0) * li) / den
    bbar_r = zr[..., None] * b_re - zi[..., None] * b_im
    bbar_i = zr[..., None] * b_im + zi[..., None] * b_re
    bbr_t, bbi_t = jnp.swapaxes(bbar_r, 1, 2), jnp.swapaxes(bbar_i, 1, 2)
    cr_t, ci_t = jnp.swapaxes(c_re, 1, 2), jnp.swapaxes(c_im, 1, 2)

    def powers(steps):
        st = steps.astype(F32)[None, :, None]
        pmag = jnp.exp(st * (lr * dt)[:, None, :])
        ang = st * (li * dt)[:, None, :]
        return pmag * jnp.cos(ang), pmag * jnp.sin(ang)

    down = (c - 1) - jnp.arange(c)
    rev_r, rev_i = powers(down)
    m1_r = rev_r[:, :, None, :] * bbr_t[:, None] - rev_i[:, :, None, :] * bbi_t[:, None]
    m1_i = rev_r[:, :, None, :] * bbi_t[:, None] + rev_i[:, :, None, :] * bbr_t[:, None]
    car = c_re[:, None] * rev_r[:, :, None, :] - c_im[:, None] * rev_i[:, :, None, :]
    cai = c_re[:, None] * rev_i[:, :, None, :] + c_im[:, None] * rev_r[:, :, None, :]
    kern = (jnp.einsum('gqhp,gpj->gqjh', car, bbar_r, precision=HI)
            - jnp.einsum('gqhp,gpj->gqjh', cai, bbar_i, precision=HI))
    is_tau0 = (down == 0).astype(F32)[None, :, None, None]
    kern = kern + is_tau0 * (d_skip[:, None, None, :] * jnp.eye(hh, dtype=F32)[None, None])
    up_r, up_i = powers(jnp.arange(1, c + 1))
    up_r, up_i = jnp.swapaxes(up_r, 1, 2)[..., None], jnp.swapaxes(up_i, 1, 2)[..., None]
    m2_r = cr_t[:, :, None, :] * up_r - ci_t[:, :, None, :] * up_i
    m2_i = -(cr_t[:, :, None, :] * up_i + ci_t[:, :, None, :] * up_r)
    adv_r, adv_i = powers(jnp.full((1,), c))
    gb = S5_LANES // hh
    nj = g // gb

    def rows_sgh(x):
        w = x.shape[-1]
        return jnp.transpose(x.reshape(nj, gb, c, hh, w), (0, 2, 1, 3, 4)).reshape(nj, c * gb * hh, w)

    def place(base, spread, row_group, col_group):
        out = jnp.einsum('jrw,wc->jrc', base.astype(BF16), spread.astype(BF16), preferred_element_type=BF16)
        rg = row_group(lax.broadcasted_iota(jnp.int32, out.shape[1:], 0))
        cg = col_group(lax.broadcasted_iota(jnp.int32, out.shape[1:], 1))
        return jnp.where(rg == cg, out, jnp.zeros((), BF16))

    grp_sgh = lambda r: (r // hh) % gb
    rep = lambda w: jnp.tile(jnp.eye(w, dtype=F32), (1, gb))
    m1 = jnp.concatenate([place(rows_sgh(m), rep(p), grp_sgh, lambda col: col // p) for m in (m1_r, m1_i)], axis=-1)
    krev = place(rows_sgh(kern), rep(hh), grp_sgh, lambda col: col // hh)
    ri = lax.broadcasted_iota(jnp.int32, (c * hh, c * gb * hh), 0)
    ci = lax.broadcasted_iota(jnp.int32, (c * hh, c * gb * hh), 1)
    spread_th = ((ri // hh == ci // (gb * hh)) & (ri % hh == ci % hh)).astype(F32)
    m2 = jnp.concatenate([place(m.reshape(nj, gb * p, c * hh), spread_th, lambda r: r // p,
                                lambda col: (col // hh) % gb) for m in (m2_r, m2_i)], axis=1)
    return (m1.astype(BF16), krev.astype(BF16), m2.astype(BF16),
            adv_r.reshape(nj, 1, gb * p), adv_i.reshape(nj, 1, gb * p))


def _s5_kernel(u_ref, m1_ref, kr_ref, m2_ref, ar_ref, ai_ref, y_ref, xe_ref, xin_ref, st_ref, *, n_chunks, nb):
    nl = S5_LANES
    csz = S5_CHUNK
    sw = ar_ref.shape[-1]

    @pl.when(pl.program_id(1) == 0)
    def _():
        st_ref[...] = jnp.zeros(st_ref.shape, F32)

    u = u_ref[0]
    xe_ref[...] = jnp.dot(u, m1_ref[0], preferred_element_type=F32)
    ar = jnp.broadcast_to(ar_ref[0], (nb, sw))
    ai = jnp.broadcast_to(ai_ref[0], (nb, sw))

    def body(c, carry):
        sr, si = carry
        off = pl.multiple_of(c * nb, nb)
        xin_ref[pl.ds(off, nb), :] = jnp.concatenate([sr, si], axis=1)
        e = xe_ref[pl.ds(off, nb), :]
        return (ar * sr - ai * si + e[:, :sw], ar * si + ai * sr + e[:, sw:])

    sr, si = lax.fori_loop(0, n_chunks, body, (st_ref[:, :sw], st_ref[:, sw:]))
    st_ref[:, :sw] = sr
    st_ref[:, sw:] = si
    y_ref[0] = _bdot(xin_ref[...], m2_ref[0])
    for t in range(csz):
        y_ref[0, :, t * nl:(t + 1) * nl] += jnp.dot(u[:, :(t + 1) * nl], kr_ref[0, (csz - 1 - t) * nl:, :],
                                                    preferred_element_type=F32)


def _s5_scan(u, tables, bsz, lp, blk0=0):
    m1, krev, m2, adv_r, adv_i = tables
    _, kin, sw2 = m1.shape
    c = S5_CHUNK
    nl = S5_LANES
    nj = u.shape[1] // nl
    nc = lp // c
    cpt = max(d for d in range(1, nc + 1) if nc % d == 0 and d * bsz <= 512)
    rows = cpt * bsz
    ug = jnp.transpose(u.reshape(bsz, nc, c, nj, nl), (3, 1, 0, 2, 4)).reshape(nj, nc * bsz, kin)
    y = pl.pallas_call(
        functools.partial(_s5_kernel, n_chunks=cpt, nb=bsz),
        grid=(nj, nc // cpt),
        in_specs=[pl.BlockSpec((1, rows, kin), lambda j, r: (j, r, 0)),
                  pl.BlockSpec((1, kin, sw2), lambda j, r: (j + blk0, 0, 0)),
                  pl.BlockSpec((1, kin, nl), lambda j, r: (j + blk0, 0, 0)),
                  pl.BlockSpec((1, sw2, kin), lambda j, r: (j + blk0, 0, 0)),
                  pl.BlockSpec((1, 1, sw2 // 2), lambda j, r: (j + blk0, 0, 0)),
                  pl.BlockSpec((1, 1, sw2 // 2), lambda j, r: (j + blk0, 0, 0))],
        out_specs=pl.BlockSpec((1, rows, kin), lambda j, r: (j, r, 0)),
        out_shape=jax.ShapeDtypeStruct((nj, nc * bsz, kin), F32),
        scratch_shapes=[pltpu.VMEM((rows, sw2), F32), pltpu.VMEM((rows, sw2), F32), pltpu.VMEM((bsz, sw2), F32)],
        compiler_params=_cparams(("parallel", "arbitrary")),
        name="s5_scan",
    )(ug, m1, krev, m2, adv_r, adv_i)
    y = jnp.transpose(y.reshape(nj, nc, bsz, c, nl), (2, 1, 3, 0, 4))
    return y.reshape(bsz * lp, nj * nl)


def _lru_kernel(rec_ref, gate_ref, cw_ref, cb_ref, wa_ref, ba_ref, wx_ref, bx_ref, lam_ref,
                o_ref, ext_ref, a_ref, b_ref, h_ref, *, tl):
    t = pl.program_id(1)

    @pl.when(t == 0)
    def _():
        ext_ref[0:8, :] = jnp.zeros((8, ext_ref.shape[1]), F32)
        h_ref[...] = jnp.zeros(h_ref.shape, F32)

    x = rec_ref[...]
    ext_ref[8:, :] = x
    xc = cb_ref[...] + cw_ref[3:4, :] * x
    for k in range(3):
        xc = xc + cw_ref[k:k + 1, :] * ext_ref[5 + k:5 + k + tl, :]
    ext_ref[0:8, :] = x[tl - 8:, :]
    r = _sigmoid(_bdot(xc, wa_ref[...]) + ba_ref[...])
    i = _sigmoid(_bdot(xc, wx_ref[...]) + bx_ref[...])
    log_a = (-LRU_C) * r * _softplus(-lam_ref[...])
    a = jnp.exp(log_a)
    a_ref[...] = a
    b_ref[...] = jnp.sqrt(1.0 - a * a) * (i * xc)

    sub = 8
    unroll = max(u for u in (4, 2, 1) if (tl // sub) % u == 0)
    row = lax.broadcasted_iota(jnp.int32, (sub, a.shape[1]), 0)

    def body(j, h):
        offs = [pl.multiple_of((j * unroll + q) * sub, sub) for q in range(unroll)]
        ab = [a_ref[pl.ds(o, sub), :] for o in offs]
        bb = [b_ref[pl.ds(o, sub), :] for o in offs]
        for dist in (1, 2, 4):
            keep = row >= dist
            bb = [jnp.where(keep, x * pltpu.roll(y, dist, axis=0) + y, y) for x, y in zip(ab, bb)]
            ab = [jnp.where(keep, x * pltpu.roll(x, dist, axis=0), x) for x in ab]
        for o, x, y in zip(offs, ab, bb):
            hb = x * h + y
            o_ref[pl.ds(o, sub), :] = hb * _gelu(gate_ref[pl.ds(o, sub), :])
            h = hb[sub - 1:sub, :]
        return h

    h_ref[...] = lax.fori_loop(0, tl // (sub * unroll), body, h_ref[...])


def _lru(rec, gate, conv_w, conv_b, w_a, b_a, w_x, b_x, lam, bsz, lp):
    tp, w = rec.shape
    tl = _pick_tile(lp, 1056)
    nt = lp // tl
    heads, hd, _ = w_a.shape

    def dense(wb):
        eye = jnp.eye(heads, dtype=F32)
        return jnp.einsum('hij,hg->higj', wb, eye).reshape(w, w).astype(BF16)

    row = pl.BlockSpec((tl, w), lambda b, t: (b * nt + t, 0))
    vec = _const_spec((1, w))
    return pl.pallas_call(
        functools.partial(_lru_kernel, tl=tl),
        grid=(bsz, nt),
        in_specs=[row, row, _const_spec((4, w)), vec, _const_spec((w, w)), vec, _const_spec((w, w)), vec, vec],
        out_specs=row,
        out_shape=jax.ShapeDtypeStruct((tp, w), F32),
        scratch_shapes=[pltpu.VMEM((tl + 8, w), F32), pltpu.VMEM((tl, w), F32),
                        pltpu.VMEM((tl, w), F32), pltpu.VMEM((1, w), F32)],
        compiler_params=_cparams(("parallel", "arbitrary")),
        name="rglru",
    )(rec, gate, conv_w, conv_b.reshape(1, w), dense(w_a), b_a.reshape(1, w), dense(w_x), b_x.reshape(1, w),
      lam.reshape(1, w))


def _ab_out_kernel(y5_ref, lru_ref, h_ref, wglu_ref, bglu_ref, na_ref, nb_ref, wo_ref, gf_ref, wr_ref, br_ref,
                   o_ref, xn_ref, rt_ref, *, s5w):
    y = _gelu(y5_ref[...])
    ya = y * _sigmoid(_bdot(y, wglu_ref[...]) + bglu_ref[...])
    ya = _rms(ya, na_ref[...])
    yb = _rms(lru_ref[...], nb_ref[...])
    h = h_ref[...] + _bdot(ya, wo_ref[:s5w, :]) + _bdot(yb, wo_ref[s5w:, :])
    o_ref[...] = h
    xn, rt_ref[...] = _route_tile(h, gf_ref[...], wr_ref[0], wr_ref[1], br_ref[...])
    xn_ref[...] = xn.astype(xn_ref.dtype)


def _ab_out(y5, lru, h, w_glu, b_glu, norm_a, norm_b, w_out, g_ffn, router):
    tp, d = h.shape
    s5w, lruw = y5.shape[1], lru.shape[1]
    wr, br = router
    tm = _pick_tile(tp, 512)
    row = lambda n: pl.BlockSpec((tm, n), lambda i: (i, 0))
    return pl.pallas_call(
        functools.partial(_ab_out_kernel, s5w=s5w),
        grid=(tp // tm,),
        in_specs=[row(s5w), row(lruw), row(d), _const_spec((s5w, s5w)), _const_spec((1, s5w)),
                  _const_spec((1, s5w)), _const_spec((1, lruw)), _const_spec(w_out.shape),
                  _const_spec((1, d)), _const_spec(wr.shape), _const_spec(br.shape)],
        out_specs=[row(d), row(d), row(ROUTE_LANES)],
        out_shape=[jax.ShapeDtypeStruct((tp, d), F32), jax.ShapeDtypeStruct((tp, d), F32),
                   jax.ShapeDtypeStruct((tp, ROUTE_LANES), F32)],
        compiler_params=_cparams(("parallel",)),
        name="ab_out",
    )(y5, lru, h, w_glu.astype(BF16), b_glu.reshape(1, s5w), norm_a.reshape(1, s5w),
      norm_b.reshape(1, lruw), w_out.astype(BF16), g_ffn.reshape(1, d), wr, br)


def _route_tile(h, g, wr_hi, wr_lo, br):
    xn = _rms(h, g)
    m = xn.shape[0]
    x_hi = xn.astype(BF16)
    x_lo = (xn - x_hi.astype(F32)).astype(BF16)
    top = jnp.dot(jnp.concatenate([x_hi, x_lo], axis=0), wr_hi, preferred_element_type=F32)
    lg = top[:m] + (top[m:] + jnp.dot(x_hi, wr_lo, preferred_element_type=F32)) + br
    lane = lax.broadcasted_iota(jnp.int32, lg.shape, 1).astype(F32)
    big = float(lg.shape[1])
    neg = -jnp.inf
    gl = jnp.where(lane < N_GROUPS, lg, neg)
    mg = jnp.max(gl, axis=-1, keepdims=True)
    gidx = jnp.min(jnp.where(gl == mg, lane, big), axis=-1, keepdims=True)
    pg_sel = 1.0 / jnp.sum(jnp.exp(gl - mg), axis=-1, keepdims=True)
    lo = N_GROUPS + EXPERTS_PER_GROUP * gidx
    le = jnp.where(lane >= lo, jnp.where(lane < lo + EXPERTS_PER_GROUP, lg, neg), neg)
    v1 = jnp.max(le, axis=-1, keepdims=True)
    i1 = jnp.min(jnp.where(le == v1, lane, big), axis=-1, keepdims=True)
    le2 = jnp.where(lane == i1, neg, le)
    v2 = jnp.max(le2, axis=-1, keepdims=True)
    i2 = jnp.min(jnp.where(le2 == v2, lane, big), axis=-1, keepdims=True)
    e2 = jnp.exp(v2 - v1)
    w1 = pg_sel / (1.0 + e2)
    w2 = w1 * e2
    rt = jnp.where(lane == 0.0, w1, jnp.where(lane == 1.0, w2, jnp.where(
        lane == 2.0, i1 - N_GROUPS, jnp.where(lane == 3.0, i2 - N_GROUPS, 0.0))))
    return xn, rt


def _router_table(wr_g, br_g, wr_e, br_e):
    d = wr_g.shape[0]
    wr = jnp.zeros((d, ROUTE_LANES), F32).at[:, :N_GROUPS].set(wr_g).at[:, N_GROUPS:N_GROUPS + N_EXPERTS].set(wr_e)
    br = jnp.zeros((1, ROUTE_LANES), F32).at[0, :N_GROUPS].set(br_g).at[0, N_GROUPS:N_GROUPS + N_EXPERTS].set(br_e)
    wr_hi = wr.astype(BF16)
    wr_lo = (wr - wr_hi.astype(F32)).astype(BF16)
    return jnp.stack([wr_hi, wr_lo]), br


def _gmm_kernel(te_ref, tv_ref, x_ref, wg_ref, wu_ref, wd_ref, *rest, first, n_active):
    o_ref, wg_bf, wu_bf, wd_bf = rest[-4:]
    step = pl.program_id(0)
    i = jnp.minimum(step, n_active - 1) + first
    live = jnp.logical_and(step < n_active, tv_ref[i] != 0)

    @pl.when(jnp.logical_and(live, jnp.logical_or(step == 0, te_ref[i] != te_ref[jnp.maximum(i - 1, 0)])))
    def _():
        wg_bf[...] = wg_ref[0, 0].astype(BF16)
        wu_bf[...] = wu_ref[0, 0].astype(BF16)
        wd_bf[...] = wd_ref[0, 0].astype(BF16)

    @pl.when(live)
    def _():
        x = x_ref[...].astype(BF16)
        hg = jnp.dot(x, wg_bf[...], preferred_element_type=F32)
        hu = jnp.dot(x, wu_bf[...], preferred_element_type=F32)
        hid = hg * _sigmoid(hg) * hu
        o_ref[...] = _bdot(hid, wd_bf[...]).astype(o_ref.dtype)

    @pl.when(jnp.logical_not(live))
    def _():
        o_ref[...] = jnp.zeros(o_ref.shape, o_ref.dtype)


def _invert_rows_kernel(dest_ref, init_ref, src_ref, *, unroll):
    pltpu.sync_copy(init_ref, src_ref)
    n_tok = dest_ref.shape[0] // 2

    def put(t):
        src_ref[dest_ref[2 * t]] = t
        src_ref[dest_ref[2 * t + 1]] = t

    def body(i, _):
        for u in range(unroll):
            put(i * unroll + u)
        return 0

    lax.fori_loop(0, n_tok // unroll, body, 0)
    for t in range(n_tok - n_tok % unroll, n_tok):
        put(t)


def _invert_rows(dest, init):
    smem = pl.BlockSpec(memory_space=pltpu.SMEM)
    return pl.pallas_call(
        functools.partial(_invert_rows_kernel, unroll=8),
        in_specs=[smem, pl.BlockSpec(memory_space=pl.ANY)],
        out_specs=smem,
        out_shape=jax.ShapeDtypeStruct(init.shape, jnp.int32),
        name="moe_invert",
    )(dest, init)


def _moe(h, xn, rt, w_gate, w_up, w_down, layer):
    tp, d = h.shape
    f = w_gate.shape[-1]
    gate = rt[:, 0:2]
    eid = rt[:, 2:4].astype(jnp.int32)
    tmm = MOE_TILE
    na = 2 * tp
    e_flat = eid.reshape(na)
    onehot = (e_flat[:, None] == jnp.arange(N_EXPERTS, dtype=jnp.int32)[None, :]).astype(jnp.int32)
    csum = jnp.cumsum(onehot, axis=0)
    counts = csum[-1]
    padded = ((counts + tmm - 1) // tmm) * tmm
    ends = jnp.cumsum(padded)
    starts = ends - padded
    dest = jnp.sum(onehot * (csum + (starts - 1)[None, :]), axis=1)
    n_half = (-(-na // tmm) + N_EXPERTS + 1) // 2
    n_tiles = 2 * n_half
    nrows = n_tiles * tmm
    tile_start = jnp.arange(n_tiles, dtype=jnp.int32) * tmm
    tile_e = jnp.sum((ends[None, :] <= tile_start[:, None]).astype(jnp.int32), axis=1)
    tile_v = (tile_e < N_EXPERTS).astype(jnp.int32)
    tile_e = jnp.minimum(tile_e, N_EXPERTS - 1)
    src = _invert_rows(dest, jnp.arange(nrows, dtype=jnp.int32) % tp)
    take = lambda a, i: a.at[i].get(mode="promise_in_bounds")
    ys = None
    for first in (0, n_half):
        xs = take(xn, src[first * tmm:(first + n_half) * tmm])
        local = lambda i: jnp.minimum(i, n_half - 1)
        wspec = lambda shape: pl.BlockSpec((1, 1) + shape, lambda i, te, tv: (layer, te[local(i) + first], 0, 0))
        prev = [] if ys is None else [ys]
        ys = pl.pallas_call(
            functools.partial(_gmm_kernel, first=first, n_active=n_half),
            grid_spec=pltpu.PrefetchScalarGridSpec(
                num_scalar_prefetch=2,
                grid=(n_tiles - first,),
                in_specs=[pl.BlockSpec((tmm, d), lambda i, te, tv: (local(i), 0)), wspec((d, f)), wspec((d, f)),
                          wspec((f, d))] + [pl.BlockSpec(memory_space=pl.ANY)] * len(prev),
                out_specs=pl.BlockSpec((tmm, d), lambda i, te, tv: (i + first, 0)),
                scratch_shapes=[pltpu.VMEM((d, f), BF16), pltpu.VMEM((d, f), BF16), pltpu.VMEM((f, d), BF16)],
            ),
            out_shape=jax.ShapeDtypeStruct((nrows, d), MOE_ROW_DTYPE),
            input_output_aliases={6: 0} if prev else {},
            compiler_params=_cparams(("arbitrary",)),
            name="moe_gmm",
        )(tile_e, tile_v, xs, w_gate, w_up, w_down, *prev)
    d2 = dest.reshape(tp, 2)
    return h + gate[:, 0:1] * take(ys, d2[:, 0]) + gate[:, 1:2] * take(ys, d2[:, 1])


def _rw_pre_kernel(*refs, tm, lp, has_vres):
    if has_vres:
        (h_ref, hp_ref, g_ref, mu_ref, wr_ref, wk_ref, wv_ref, w0_ref, wl1_ref, wl2_ref,
         a0_ref, al1_ref, al2_ref, gl1_ref, gl2_ref, vf_ref, v0_ref, vl1_ref, vl2_ref,
         r_ref, k_ref, v_ref, w_ref, a_ref, gg_ref) = refs
    else:
        (h_ref, hp_ref, g_ref, mu_ref, wr_ref, wk_ref, wv_ref, w0_ref, wl1_ref, wl2_ref,
         a0_ref, al1_ref, al2_ref, gl1_ref, gl2_ref,
         r_ref, k_ref, v_ref, w_ref, a_ref, gg_ref) = refs
    i = pl.program_id(0)
    g = g_ref[...]
    x = _rms(h_ref[...], g)
    xp8 = _rms(hp_ref[...], g)
    row = lax.broadcasted_iota(jnp.int32, x.shape, 0)
    prev = jnp.where(row == 0, jnp.broadcast_to(xp8[7:8, :], x.shape), pltpu.roll(x, 1, axis=0))
    first = lax.rem(lp - lax.rem(i * tm, lp), lp)
    prev = jnp.where(row == first, 0.0, prev)
    xx = prev - x
    xr, xw, xk, xv, xa, xg = [x + xx * mu_ref[j:j + 1, :] for j in range(6)]
    r_ref[...] = _bdot(xr, wr_ref[...]).astype(r_ref.dtype)
    k_ref[...] = _bdot(xk, wk_ref[...]).astype(k_ref.dtype)
    v = _bdot(xv, wv_ref[...])
    if has_vres:
        mix = _sigmoid(v0_ref[...] + _bdot(_bdot(xv, vl1_ref[...]), vl2_ref[...]))
        v = v + (vf_ref[...] - v) * mix
    v_ref[...] = v.astype(v_ref.dtype)
    w_ref[...] = -_softplus(-(w0_ref[...] + _bdot(jnp.tanh(_bdot(xw, wl1_ref[...])), wl2_ref[...]))) - 0.5
    a_ref[...] = _sigmoid(a0_ref[...] + _bdot(_bdot(xa, al1_ref[...]), al2_ref[...]))
    gg_ref[...] = _bdot(_sigmoid(_bdot(xg, gl1_ref[...])), gl2_ref[...]).astype(gg_ref.dtype)


def _rw_pre(h, g, mu, w_r, w_k, w_v, w0, w_l1, w_l2, a0, a_l1, a_l2, g_l1, g_l2, lp, v_first, v_res):
    tp, d = h.shape
    tm = _pick_tile(tp, min(512, lp))
    has_vres = v_res is not None
    row = pl.BlockSpec((tm, d), lambda i: (i, 0))
    prev8 = pl.BlockSpec((8, d), lambda i: (jnp.maximum(i * (tm // 8) - 1, 0), 0))
    vec = _const_spec((1, d))
    mu8 = jnp.zeros((8, d), F32).at[:6].set(mu)
    bf = lambda w: w.astype(BF16)
    ins = [h, h, g.reshape(1, d), mu8, bf(w_r), bf(w_k), bf(w_v), w0.reshape(1, d), bf(w_l1), bf(w_l2),
           a0.reshape(1, d), bf(a_l1), bf(a_l2), bf(g_l1), bf(g_l2)]
    specs = [row, prev8, vec, _const_spec((8, d))] + [_const_spec((d, d))] * 3 + [
        vec, _const_spec(w_l1.shape), _const_spec(w_l2.shape),
        vec, _const_spec(a_l1.shape), _const_spec(a_l2.shape), _const_spec(g_l1.shape), _const_spec(g_l2.shape)]
    if has_vres:
        v0, v_l1, v_l2 = v_res
        ins += [v_first, v0.reshape(1, d), bf(v_l1), bf(v_l2)]
        specs += [row, vec, _const_spec(v_l1.shape), _const_spec(v_l2.shape)]
    return pl.pallas_call(
        functools.partial(_rw_pre_kernel, tm=tm, lp=lp, has_vres=has_vres),
        grid=(tp // tm,),
        in_specs=specs,
        out_specs=[row] * 6,
        out_shape=[jax.ShapeDtypeStruct((tp, d), dt) for dt in (BF16, BF16, BF16, F32, F32, BF16)],
        compiler_params=_cparams(("parallel",)),
        name="rwkv_pre",
    )(*ins)


def _rw_scan_kernel(r_ref, w_ref, k_ref, v_ref, a_ref, kk_ref, ka_ref, rk_ref, lw_ref, lb_ref,
                    o_ref, g_scr, h_scr, q_scr, y0_scr, *, n_chunks, unroll, pairs):
    c = RW_CHUNK
    nl = 2 * RW_HEAD
    c2 = 2 * c
    ri = lax.broadcasted_iota(jnp.int32, (c2, nl), 0)
    ci = lax.broadcasted_iota(jnp.int32, (c2, nl), 1)
    own = ((ri >= c) == (ci >= RW_HEAD)).astype(F32)
    t_in = jnp.bitwise_and(ri, c - 1)
    s_in = jnp.bitwise_and(ci, c - 1)
    strict = jnp.where(s_in < t_in, own, 0.0)
    incl = jnp.where(s_in <= t_in, own, 0.0)
    causal2 = jnp.concatenate([strict, incl], axis=0)
    eye_l = (ri == ci).astype(F32)
    same_head = own.astype(BF16)
    ti = lax.broadcasted_iota(jnp.int32, (c, c), 0)
    si = lax.broadcasted_iota(jnp.int32, (c, c), 1)
    tril_c = (si <= ti).astype(BF16)
    lanes = lambda p: slice(p * nl, (p + 1) * nl)

    def stack(x):
        return jnp.concatenate([x, x], axis=0) * own

    def offset(ch):
        return ch * c if isinstance(ch, int) else pl.multiple_of(ch * c, c)

    def groups(fn):
        for p in range(pairs):
            def body(i, _):
                fn([i * unroll + q for q in range(unroll)], p)
                return 0
            lax.fori_loop(0, n_chunks // unroll, body, 0)
            if n_chunks % unroll:
                fn(list(range(n_chunks - n_chunks % unroll, n_chunks)), p)

    def prep(chs, p):
        ln = lanes(p)
        kk_w, ka_w = kk_ref[:, ln], ka_ref[:, ln]
        offs = [offset(ch) for ch in chs]
        r = [r_ref[pl.ds(o, c), ln] for o in offs]
        k = [k_ref[pl.ds(o, c), ln] for o in offs]
        v = [v_ref[pl.ds(o, c), ln] for o in offs]
        a = [a_ref[pl.ds(o, c), ln] for o in offs]
        logw = [-jnp.exp(w_ref[pl.ds(o, c), ln]) for o in offs]
        kk = [x * kk_w for x in k]
        ss = _split_dot_many([x * x for x in kk], same_head, pieces=2)
        kk = [x / jnp.maximum(jnp.sqrt(q), 1e-12) for x, q in zip(kk, ss)]
        kmod = [x * (1.0 + (y - 1.0) * ka_w) for x, y in zip(k, a)]
        cum = _split_dot_many(logw, tril_c, rhs=True, pieces=2)
        p_incl = [jnp.exp(x) for x in cum]
        p_inv = [jnp.exp(-x) for x in cum]
        p_end = [jnp.exp(x[c - 1:c, :] - x) for x in cum]
        kka = [x * y for x, y in zip(kk, a)]
        a_s = [stack(-x * jnp.exp(y - z)) for x, y, z in zip(kk, cum, logw)]
        r_s = [stack(x * y) for x, y in zip(r, p_incl)]
        v_s = [stack(x).astype(BF16) for x in v]
        lhs = [jnp.concatenate([x, y], axis=0).astype(BF16) for x, y in zip(a_s, r_s)]
        rhs = [jnp.concatenate([stack(x * z), stack(y * z)], axis=0).astype(BF16) for x, y, z in zip(kka, kmod, p_inv)]
        big = [lax.dot_general(x, y, (((1,), (1,)), ((), ())), preferred_element_type=F32) for x, y in zip(lhs, rhs)]
        lpow = [(x[:c2, :c2] * strict).astype(BF16) for x in big]
        a_rb = [x[c2:, :c2] * incl for x in big]
        avk = [_bdot(x[:, c2:] * causal2, w) for x, w in zip(big, v_s)]
        x = [jnp.concatenate([p, q[:c2]], axis=1) for p, q in zip(a_s, avk)]
        x = [p + _bdot(q, p) for p, q in zip(x, lpow)]
        for _ in range(5):
            lpow = [jnp.dot(q, q, preferred_element_type=F32).astype(BF16) for q in lpow]
            x = [p + _bdot(q, p) for p, q in zip(x, lpow)]
        xb = [p.astype(BF16) for p in x]
        bh_t = [stack(p * q).T for p, q in zip(kka, p_end)]
        kh_t = [stack(p * q).T for p, q in zip(kmod, p_end)]
        both = [_bdot(jnp.concatenate([u, p], axis=0), z) for u, p, z in zip(a_rb, bh_t, xb)]
        qy = [jnp.concatenate([p, q[c2:]], axis=1) + z[:c2] for p, q, z in zip(r_s, avk, both)]
        gh = [z[c2:] + jnp.concatenate([eye_l * q[c - 1:c, :], _bdot(u, w)], axis=1)
              for z, q, u, w in zip(both, p_incl, kh_t, v_s)]
        for ch, u, q in zip(chs, qy, gh):
            q_scr[p, ch] = u[:c, :nl] + u[c:, :nl]
            y0_scr[p, ch] = u[:c, nl:] + u[c:, nl:]
            g_scr[p, ch] = q[:, :nl]
            h_scr[p, ch] = q[:, nl:]

    groups(prep)

    def advance(ch, states):
        both = _dot3_many([jnp.concatenate([q_scr[p, ch], g_scr[p, ch]], axis=0) for p in range(pairs)], states)
        return (tuple(both[p][c:] + h_scr[p, ch] for p in range(pairs)),
                tuple(both[p][:c] + y0_scr[p, ch] for p in range(pairs)))

    def finish(ch, ys):
        o = offset(ch)
        rk = []
        for p in range(pairs):
            ln = lanes(p)
            kmod = k_ref[pl.ds(o, c), ln] * (1.0 + (a_ref[pl.ds(o, c), ln] - 1.0) * ka_ref[:, ln])
            rk.append(r_ref[pl.ds(o, c), ln] * kmod * rk_ref[:, ln])
        sums = _split_dot_many([jnp.concatenate([u, u * u, q], axis=0) for u, q in zip(ys, rk)], same_head)
        for p, (u, s) in enumerate(zip(ys, sums)):
            ln = lanes(p)
            mean = s[:c] * (1.0 / RW_HEAD)
            var = s[c:2 * c] * (1.0 / RW_HEAD) - mean * mean
            o_ref[pl.ds(o, c), ln] = ((u - mean) * lax.rsqrt(var + LNX_EPS) * lw_ref[:, ln] + lb_ref[:, ln]
                                      + s[2 * c:] * v_ref[pl.ds(o, c), ln])

    def step(ch, carry):
        states, ys = carry
        finish(ch - 1, ys)
        return advance(ch, states)

    carry = advance(0, tuple(jnp.zeros((nl, nl), F32) for _ in range(pairs)))
    _, ys = lax.fori_loop(1, n_chunks, step, carry)
    finish(n_chunks - 1, ys)


def _rw_scan(r, w, k, v, a, k_k, k_a, r_k, lnx_w, lnx_b, bsz, lp):
    tp, d = r.shape
    nl = 2 * RW_HEAD
    pairs = RW_PAIRS
    bw = pairs * nl
    nhp = d // bw
    nc = lp // RW_CHUNK
    seq = pl.BlockSpec((lp, bw), lambda b, hp: (b, hp))
    vec = pl.BlockSpec((1, bw), lambda b, hp: (0, hp))
    c = RW_CHUNK
    return pl.pallas_call(
        functools.partial(_rw_scan_kernel, n_chunks=nc, unroll=RW_UNROLL, pairs=pairs),
        grid=(bsz, nhp),
        in_specs=[seq] * 5 + [vec] * 5,
        out_specs=seq,
        out_shape=jax.ShapeDtypeStruct((tp, d), F32),
        scratch_shapes=[pltpu.VMEM((pairs, nc, nl, nl), F32), pltpu.VMEM((pairs, nc, nl, nl), F32),
                        pltpu.VMEM((pairs, nc, c, nl), F32), pltpu.VMEM((pairs, nc, c, nl), F32)],
        compiler_params=_cparams(("parallel", "parallel")),
        name="rwkv_scan",
    )(r, w, k, v, a, k_k.reshape(1, d), k_a.reshape(1, d), r_k.reshape(1, d),
      lnx_w.reshape(1, d), lnx_b.reshape(1, d))


def _rw_out_kernel(y_ref, g_ref, h_ref, wo_ref, gf_ref, wr_ref, br_ref, o_ref, xn_ref, rt_ref):
    h = h_ref[...] + _bdot(y_ref[...] * g_ref[...], wo_ref[...])
    o_ref[...] = h
    xn, rt_ref[...] = _route_tile(h, gf_ref[...], wr_ref[0], wr_ref[1], br_ref[...])
    xn_ref[...] = xn.astype(xn_ref.dtype)


def _rw_out(y, g, h, w_o, g_ffn, router):
    tp, d = h.shape
    wr, br = router
    tm = _pick_tile(tp, 512)
    row = lambda n: pl.BlockSpec((tm, n), lambda i: (i, 0))
    return pl.pallas_call(
        _rw_out_kernel,
        grid=(tp // tm,),
        in_specs=[row(d), row(d), row(d), _const_spec((d, d)), _const_spec((1, d)), _const_spec(wr.shape),
                  _const_spec(br.shape)],
        out_specs=[row(d), row(d), row(ROUTE_LANES)],
        out_shape=[jax.ShapeDtypeStruct((tp, d), F32), jax.ShapeDtypeStruct((tp, d), F32),
                   jax.ShapeDtypeStruct((tp, ROUTE_LANES), F32)],
        compiler_params=_cparams(("parallel",)),
        name="rwkv_out",
    )(y, g, h, w_o.astype(BF16), g_ffn.reshape(1, d), wr, br)


def _final_norm_kernel(h_ref, g_ref, o_ref, *, first, rows):
    o_ref[0] = _rms(h_ref[first:first + rows, :], g_ref[...]).astype(o_ref.dtype)


def _final_norm(h, g, dtype, bsz, lp, first, rows):
    d = h.shape[1]
    return pl.pallas_call(
        functools.partial(_final_norm_kernel, first=first, rows=rows),
        grid=(bsz,),
        in_specs=[pl.BlockSpec((lp, d), lambda b: (b, 0)), _const_spec((1, d))],
        out_specs=pl.BlockSpec((1, rows, d), lambda b: (b, 0, 0)),
        out_shape=jax.ShapeDtypeStruct((bsz, rows, d), dtype),
        compiler_params=_cparams(("parallel",)),
        name="final_norm",
    )(h, g.reshape(1, d))


def kernel(x, meta_tokens, norm_mix, norm_ffn, norm_final, ab_w_in, ab_w_out, ab_norm_a, ab_norm_b, s5_lam_re, s5_lam_im, s5_log_dt, s5_b_re, s5_b_im, s5_c_re, s5_c_im, s5_d, s5_w_glu, s5_b_glu, lru_conv_w, lru_conv_b, lru_w_a, lru_b_a, lru_w_x, lru_b_x, lru_lam, rw_mu, rw_w_r, rw_w_k, rw_w_v, rw_w_o, rw_w0, rw_w_l1, rw_w_l2, rw_a0, rw_a_l1, rw_a_l2, rw_v0, rw_v_l1, rw_v_l2, rw_g_l1, rw_g_l2, rw_k_k, rw_k_a, rw_r_k, rw_lnx_w, rw_lnx_b, moe_router_g, moe_router_g_b, moe_router_e, moe_router_e_b, moe_w_gate, moe_w_up, moe_w_down):
    bsz, seq, d = x.shape
    n_meta = meta_tokens.shape[0]
    depth = norm_mix.shape[0]
    ltot = n_meta + seq
    lp = -(-ltot // SEQ_ALIGN) * SEQ_ALIGN
    s5w = s5_w_glu.shape[-1]
    lruw = lru_lam.shape[-1]
    meta = jnp.broadcast_to(meta_tokens.astype(F32)[None], (bsz, n_meta, d))
    h = jnp.concatenate([meta, x.astype(F32), jnp.zeros((bsz, lp - ltot, d), F32)], axis=1).reshape(bsz * lp, d)
    v_first = None
    flat = lambda a: a.reshape((-1,) + a.shape[2:])
    s5_tables = _s5_tables(flat(s5_lam_re), flat(s5_lam_im), flat(s5_log_dt), flat(s5_b_re), flat(s5_b_im),
                           flat(s5_c_re), flat(s5_c_im), flat(s5_d))
    for layer in range(depth):
        j = layer // 2
        router = _router_table(moe_router_g[layer], moe_router_g_b[layer], moe_router_e[layer], moe_router_e_b[layer])
        if layer % 2 == 0:
            u, gate, rec = _ab_in(h, norm_mix[layer], ab_w_in[j], s5w, lruw)
            y5 = _s5_scan(u, s5_tables, bsz, lp, j * (s5w // S5_LANES))
            lru = _lru(rec, gate, lru_conv_w[j], lru_conv_b[j], lru_w_a[j], lru_b_a[j], lru_w_x[j], lru_b_x[j],
                       lru_lam[j], bsz, lp)
            h, xn, rt = _ab_out(y5, lru, h, s5_w_glu[j], s5_b_glu[j], ab_norm_a[j], ab_norm_b[j], ab_w_out[j],
                                norm_ffn[layer], router)
        else:
            v_res = (rw_v0[j - 1], rw_v_l1[j - 1], rw_v_l2[j - 1]) if j > 0 else None
            r, k, v, w, a, g = _rw_pre(h, norm_mix[layer], rw_mu[j], rw_w_r[j], rw_w_k[j], rw_w_v[j], rw_w0[j],
                                       rw_w_l1[j], rw_w_l2[j], rw_a0[j], rw_a_l1[j], rw_a_l2[j], rw_g_l1[j],
                                       rw_g_l2[j], lp, v_first, v_res)
            if v_first is None:
                v_first = v
            y = _rw_scan(r, w, k, v, a, rw_k_k[j], rw_k_a[j], rw_r_k[j].reshape(-1), rw_lnx_w[j], rw_lnx_b[j],
                         bsz, lp)
            h, xn, rt = _rw_out(y, g, h, rw_w_o[j], norm_ffn[layer], router)
        h = _moe(h, xn, rt, moe_w_gate, moe_w_up, moe_w_down, layer)
    return _final_norm(h, norm_final, x.dtype, bsz, lp, n_meta, seq)
```

```python
import functools
import math

import jax
import jax.numpy as jnp
from jax import lax
from jax.experimental import pallas as pl
from jax.experimental.pallas import tpu as pltpu

F32 = jnp.float32
BF16 = jnp.bfloat16
HI = lax.Precision.HIGHEST

RMS_EPS = 1e-6
LNX_EPS = 64e-5
SEQ_ALIGN = 64
S5_CHUNK = 16
S5_LANES = 128
RW_CHUNK = 64
RW_HEAD = 64
RW_PAIRS = 2
RW_UNROLL = 11
LRU_C = 8.0
N_GROUPS = 4
EXPERTS_PER_GROUP = 4
N_EXPERTS = N_GROUPS * EXPERTS_PER_GROUP
MOE_TILE = 512
ROUTE_LANES = 128
MOE_ROW_DTYPE = BF16
VMEM_LIMIT = 56 * 1024 * 1024


def _cparams(sem):
    return pltpu.CompilerParams(dimension_semantics=sem, vmem_limit_bytes=VMEM_LIMIT)


def _pick_tile(n, target):
    best = 8
    for t in range(8, min(n, target) + 1, 8):
        if n % t == 0:
            best = t
    return best


def _const_spec(shape):
    nd = len(shape)
    return pl.BlockSpec(shape, lambda *_: (0,) * nd)


def _rms(x, g):
    return x * lax.rsqrt(jnp.mean(x * x, axis=-1, keepdims=True) + RMS_EPS) * g


def _gelu(x):
    return 0.5 * x * (1.0 + jnp.tanh(math.sqrt(2.0 / math.pi) * (x + 0.044715 * (x * x * x))))


def _sigmoid(x):
    return 1.0 / (1.0 + jnp.exp(-x))


def _softplus(x):
    return jnp.maximum(x, 0.0) + jnp.log(1.0 + jnp.exp(-jnp.abs(x)))


def _bdot(a, b):
    return jnp.dot(a.astype(BF16), b.astype(BF16), preferred_element_type=F32)


def _dot3_many(a_list, b_list):
    m = a_list[0].shape[0]
    dot = functools.partial(jnp.dot, preferred_element_type=F32)
    a_hi = [a.astype(BF16) for a in a_list]
    b_hi = [b.astype(BF16) for b in b_list]
    a_lo = [(a - h.astype(F32)).astype(BF16) for a, h in zip(a_list, a_hi)]
    b_lo = [(b - h.astype(F32)).astype(BF16) for b, h in zip(b_list, b_hi)]
    top = [dot(jnp.concatenate([h, l], axis=0), b) for h, l, b in zip(a_hi, a_lo, b_hi)]
    low = [dot(h, b) for h, b in zip(a_hi, b_lo)]
    return [t[:m] + (t[m:] + l) for t, l in zip(top, low)]


def _split_dot_many(xs, m, rhs=False, pieces=3):
    accs = [None] * len(xs)
    xs = list(xs)
    for _ in range(pieces):
        his = [x.astype(BF16) for x in xs]
        parts = [jnp.dot(m, hi, preferred_element_type=F32) if rhs else jnp.dot(hi, m, preferred_element_type=F32)
                 for hi in his]
        accs = [p if a is None else a + p for a, p in zip(accs, parts)]
        xs = [x - hi.astype(F32) for x, hi in zip(xs, his)]
    return accs


def _ab_in_kernel(h_ref, g_ref, w_ref, u_ref, gate_ref, rec_ref, *, s5w, lruw):
    xn = _rms(h_ref[...], g_ref[...])
    z = _bdot(xn, w_ref[...])
    u_ref[...] = z[:, :s5w].astype(u_ref.dtype)
    gate_ref[...] = z[:, s5w:s5w + lruw]
    rec_ref[...] = z[:, s5w + lruw:]


def _ab_in(h, g, w_in, s5w, lruw):
    tp, d = h.shape
    tm = _pick_tile(tp, 512)
    row = lambda n: pl.BlockSpec((tm, n), lambda i: (i, 0))
    return pl.pallas_call(
        functools.partial(_ab_in_kernel, s5w=s5w, lruw=lruw),
        grid=(tp // tm,),
        in_specs=[row(d), _const_spec((1, d)), _const_spec(w_in.shape)],
        out_specs=[row(s5w), row(lruw), row(lruw)],
        out_shape=[jax.ShapeDtypeStruct((tp, s5w), BF16),
                   jax.ShapeDtypeStruct((tp, lruw), F32),
                   jax.ShapeDtypeStruct((tp, lruw), F32)],
        compiler_params=_cparams(("parallel",)),
        name="ab_in",
    )(h, g.reshape(1, d), w_in.astype(BF16))


def _s5_tables(lam_re, lam_im, log_dt, b_re, b_im, c_re, c_im, d_skip):
    g, p = lam_re.shape
    hh = b_re.shape[-1]
    c = S5_CHUNK
    lr, li = lam_re.astype(F32), lam_im.astype(F32)
    dt = jnp.exp(log_dt.astype(F32))[:, None]
    mag = jnp.exp(lr * dt)
    abar_r = mag * jnp.cos(li * dt)
    abar_i = mag * jnp.sin(li * dt)
    den = lr * lr + li * li
    zr = ((abar_r - 1.0) * lr + abar_i * li) / den
    zi = (abar_i * lr - (abar_r - 1.0) * li) / den
    bbar_r = zr[..., None] * b_re - zi[..., None] * b_im
    bbar_i = zr[..., None] * b_im + zi[..., None] * b_re
    bbr_t, bbi_t = jnp.swapaxes(bbar_r, 1, 2), jnp.swapaxes(bbar_i, 1, 2)
    cr_t, ci_t = jnp.swapaxes(c_re, 1, 2), jnp.swapaxes(c_im, 1, 2)

    def powers(steps):
        st = steps.astype(F32)[None, :, None]
        pmag = jnp.exp(st * (lr * dt)[:, None, :])
        ang = st * (li * dt)[:, None, :]
        return pmag * jnp.cos(ang), pmag * jnp.sin(ang)

    down = (c - 1) - jnp.arange(c)
    rev_r, rev_i = powers(down)
    m1_r = rev_r[:, :, None, :] * bbr_t[:, None] - rev_i[:, :, None, :] * bbi_t[:, None]
    m1_i = rev_r[:, :, None, :] * bbi_t[:, None] + rev_i[:, :, None, :] * bbr_t[:, None]
    car = c_re[:, None] * rev_r[:, :, None, :] - c_im[:, None] * rev_i[:, :, None, :]
    cai = c_re[:, None] * rev_i[:, :, None, :] + c_im[:, None] * rev_r[:, :, None, :]
    kern = (jnp.einsum('gqhp,gpj->gqjh', car, bbar_r, precision=HI)
            - jnp.einsum('gqhp,gpj->gqjh', cai, bbar_i, precision=HI))
    is_tau0 = (down == 0).astype(F32)[None, :, None, None]
    kern = kern + is_tau0 * (d_skip[:, None, None, :] * jnp.eye(hh, dtype=F32)[None, None])
    up_r, up_i = powers(jnp.arange(1, c + 1))
    up_r, up_i = jnp.swapaxes(up_r, 1, 2)[..., None], jnp.swapaxes(up_i, 1, 2)[..., None]
    m2_r = cr_t[:, :, None, :] * up_r - ci_t[:, :, None, :] * up_i
    m2_i = -(cr_t[:, :, None, :] * up_i + ci_t[:, :, None, :] * up_r)
    adv_r, adv_i = powers(jnp.full((1,), c))
    gb = S5_LANES // hh
    nj = g // gb

    def rows_sgh(x):
        w = x.shape[-1]
        return jnp.transpose(x.reshape(nj, gb, c, hh, w), (0, 2, 1, 3, 4)).reshape(nj, c * gb * hh, w)

    def place(base, spread, row_group, col_group):
        out = jnp.einsum('jrw,wc->jrc', base.astype(BF16), spread.astype(BF16), preferred_element_type=BF16)
        rg = row_group(lax.broadcasted_iota(jnp.int32, out.shape[1:], 0))
        cg = col_group(lax.broadcasted_iota(jnp.int32, out.shape[1:], 1))
        return jnp.where(rg == cg, out, jnp.zeros((), BF16))

    grp_sgh = lambda r: (r // hh) % gb
    rep = lambda w: jnp.tile(jnp.eye(w, dtype=F32), (1, gb))
    m1 = jnp.concatenate([place(rows_sgh(m), rep(p), grp_sgh, lambda col: col // p) for m in (m1_r, m1_i)], axis=-1)
    krev = place(rows_sgh(kern), rep(hh), grp_sgh, lambda col: col // hh)
    ri = lax.broadcasted_iota(jnp.int32, (c * hh, c * gb * hh), 0)
    ci = lax.broadcasted_iota(jnp.int32, (c * hh, c * gb * hh), 1)
    spread_th = ((ri // hh == ci // (gb * hh)) & (ri % hh == ci % hh)).astype(F32)
    m2 = jnp.concatenate([place(m.reshape(nj, gb * p, c * hh), spread_th, lambda r: r // p,
                                lambda col: (col // hh) % gb) for m in (m2_r, m2_i)], axis=1)
    return (m1.astype(BF16), krev.astype(BF16), m2.astype(BF16),
            adv_r.reshape(nj, 1, gb * p), adv_i.reshape(nj, 1, gb * p))


def _s5_kernel(u_ref, m1_ref, kr_ref, m2_ref, ar_ref, ai_ref, y_ref, xe_ref, xin_ref, st_ref, *, n_chunks, nb):
    nl = S5_LANES
    csz = S5_CHUNK
    sw = ar_ref.shape[-1]

    @pl.when(pl.program_id(1) == 0)
    def _():
        st_ref[...] = jnp.zeros(st_ref.shape, F32)

    u = u_ref[0]
    xe_ref[...] = jnp.dot(u, m1_ref[0], preferred_element_type=F32)
    ar = jnp.broadcast_to(ar_ref[0], (nb, sw))
    ai = jnp.broadcast_to(ai_ref[0], (nb, sw))

    def body(c, carry):
        sr, si = carry
        off = pl.multiple_of(c * nb, nb)
        xin_ref[pl.ds(off, nb), :] = jnp.concatenate([sr, si], axis=1)
        e = xe_ref[pl.ds(off, nb), :]
        return (ar * sr - ai * si + e[:, :sw], ar * si + ai * sr + e[:, sw:])

    sr, si = lax.fori_loop(0, n_chunks, body, (st_ref[:, :sw], st_ref[:, sw:]))
    st_ref[:, :sw] = sr
    st_ref[:, sw:] = si
    y_ref[0] = _bdot(xin_ref[...], m2_ref[0])
    for t in range(csz):
        y_ref[0, :, t * nl:(t + 1) * nl] += jnp.dot(u[:, :(t + 1) * nl], kr_ref[0, (csz - 1 - t) * nl:, :],
                                                    preferred_element_type=F32)


def _s5_scan(u, tables, bsz, lp, blk0=0):
    m1, krev, m2, adv_r, adv_i = tables
    _, kin, sw2 = m1.shape
    c = S5_CHUNK
    nl = S5_LANES
    nj = u.shape[1] // nl
    nc = lp // c
    cpt = max(d for d in range(1, nc + 1) if nc % d == 0 and d * bsz <= 512)
    rows = cpt * bsz
    ug = jnp.transpose(u.reshape(bsz, nc, c, nj, nl), (3, 1, 0, 2, 4)).reshape(nj, nc * bsz, kin)
    y = pl.pallas_call(
        functools.partial(_s5_kernel, n_chunks=cpt, nb=bsz),
        grid=(nj, nc // cpt),
        in_specs=[pl.BlockSpec((1, rows, kin), lambda j, r: (j, r, 0)),
                  pl.BlockSpec((1, kin, sw2), lambda j, r: (j + blk0, 0, 0)),
                  pl.BlockSpec((1, kin, nl), lambda j, r: (j + blk0, 0, 0)),
                  pl.BlockSpec((1, sw2, kin), lambda j, r: (j + blk0, 0, 0)),
                  pl.BlockSpec((1, 1, sw2 // 2), lambda j, r: (j + blk0, 0, 0)),
                  pl.BlockSpec((1, 1, sw2 // 2), lambda j, r: (j + blk0, 0, 0))],
        out_specs=pl.BlockSpec((1, rows, kin), lambda j, r: (j, r, 0)),
        out_shape=jax.ShapeDtypeStruct((nj, nc * bsz, kin), F32),
        scratch_shapes=[pltpu.VMEM((rows, sw2), F32), pltpu.VMEM((rows, sw2), F32), pltpu.VMEM((bsz, sw2), F32)],
        compiler_params=_cparams(("parallel", "arbitrary")),
        name="s5_scan",
    )(ug, m1, krev, m2, adv_r, adv_i)
    y = jnp.transpose(y.reshape(nj, nc, bsz, c, nl), (2, 1, 3, 0, 4))
    return y.reshape(bsz * lp, nj * nl)


def _lru_kernel(rec_ref, gate_ref, cw_ref, cb_ref, wa_ref, ba_ref, wx_ref, bx_ref, lam_ref,
                o_ref, ext_ref, a_ref, b_ref, h_ref, *, tl):
    t = pl.program_id(1)

    @pl.when(t == 0)
    def _():
        ext_ref[0:8, :] = jnp.zeros((8, ext_ref.shape[1]), F32)
        h_ref[...] = jnp.zeros(h_ref.shape, F32)

    x = rec_ref[...]
    ext_ref[8:, :] = x
    xc = cb_ref[...] + cw_ref[3:4, :] * x
    for k in range(3):
        xc = xc + cw_ref[k:k + 1, :] * ext_ref[5 + k:5 + k + tl, :]
    ext_ref[0:8, :] = x[tl - 8:, :]
    r = _sigmoid(_bdot(xc, wa_ref[...]) + ba_ref[...])
    i = _sigmoid(_bdot(xc, wx_ref[...]) + bx_ref[...])
    log_a = (-LRU_C) * r * _softplus(-lam_ref[...])
    a = jnp.exp(log_a)
    a_ref[...] = a
    b_ref[...] = jnp.sqrt(1.0 - a * a) * (i * xc)

    sub = 8
    unroll = max(u for u in (4, 2, 1) if (tl // sub) % u == 0)
    row = lax.broadcasted_iota(jnp.int32, (sub, a.shape[1]), 0)

    def body(j, h):
        offs = [pl.multiple_of((j * unroll + q) * sub, sub) for q in range(unroll)]
        ab = [a_ref[pl.ds(o, sub), :] for o in offs]
        bb = [b_ref[pl.ds(o, sub), :] for o in offs]
        for dist in (1, 2, 4):
            keep = row >= dist
            bb = [jnp.where(keep, x * pltpu.roll(y, dist, axis=0) + y, y) for x, y in zip(ab, bb)]
            ab = [jnp.where(keep, x * pltpu.roll(x, dist, axis=0), x) for x in ab]
        for o, x, y in zip(offs, ab, bb):
            hb = x * h + y
            o_ref[pl.ds(o, sub), :] = hb * _gelu(gate_ref[pl.ds(o, sub), :])
            h = hb[sub - 1:sub, :]
        return h

    h_ref[...] = lax.fori_loop(0, tl // (sub * unroll), body, h_ref[...])


def _lru(rec, gate, conv_w, conv_b, w_a, b_a, w_x, b_x, lam, bsz, lp):
    tp, w = rec.shape
    tl = _pick_tile(lp, 1056)
    nt = lp // tl
    heads, hd, _ = w_a.shape

    def dense(wb):
        eye = jnp.eye(heads, dtype=F32)
        return jnp.einsum('hij,hg->higj', wb, eye).reshape(w, w).astype(BF16)

    row = pl.BlockSpec((tl, w), lambda b, t: (b * nt + t, 0))
    vec = _const_spec((1, w))
    return pl.pallas_call(
        functools.partial(_lru_kernel, tl=tl),
        grid=(bsz, nt),
        in_specs=[row, row, _const_spec((4, w)), vec, _const_spec((w, w)), vec, _const_spec((w, w)), vec, vec],
        out_specs=row,
        out_shape=jax.ShapeDtypeStruct((tp, w), F32),
        scratch_shapes=[pltpu.VMEM((tl + 8, w), F32), pltpu.VMEM((tl, w), F32),
                        pltpu.VMEM((tl, w), F32), pltpu.VMEM((1, w), F32)],
        compiler_params=_cparams(("parallel", "arbitrary")),
        name="rglru",
    )(rec, gate, conv_w, conv_b.reshape(1, w), dense(w_a), b_a.reshape(1, w), dense(w_x), b_x.reshape(1, w),
      lam.reshape(1, w))


def _ab_out_kernel(y5_ref, lru_ref, h_ref, wglu_ref, bglu_ref, na_ref, nb_ref, wo_ref, gf_ref, wr_ref, br_ref,
                   o_ref, xn_ref, rt_ref, *, s5w):
    y = _gelu(y5_ref[...])
    ya = y * _sigmoid(_bdot(y, wglu_ref[...]) + bglu_ref[...])
    ya = _rms(ya, na_ref[...])
    yb = _rms(lru_ref[...], nb_ref[...])
    h = h_ref[...] + _bdot(ya, wo_ref[:s5w, :]) + _bdot(yb, wo_ref[s5w:, :])
    o_ref[...] = h
    xn, rt_ref[...] = _route_tile(h, gf_ref[...], wr_ref[0], wr_ref[1], br_ref[...])
    xn_ref[...] = xn.astype(xn_ref.dtype)


def _ab_out(y5, lru, h, w_glu, b_glu, norm_a, norm_b, w_out, g_ffn, router):
    tp, d = h.shape
    s5w, lruw = y5.shape[1], lru.shape[1]
    wr, br = router
    tm = _pick_tile(tp, 512)
    row = lambda n: pl.BlockSpec((tm, n), lambda i: (i, 0))
    return pl.pallas_call(
        functools.partial(_ab_out_kernel, s5w=s5w),
        grid=(tp // tm,),
        in_specs=[row(s5w), row(lruw), row(d), _const_spec((s5w, s5w)), _const_spec((1, s5w)),
                  _const_spec((1, s5w)), _const_spec((1, lruw)), _const_spec(w_out.shape),
                  _const_spec((1, d)), _const_spec(wr.shape), _const_spec(br.shape)],
        out_specs=[row(d), row(d), row(ROUTE_LANES)],
        out_shape=[jax.ShapeDtypeStruct((tp, d), F32), jax.ShapeDtypeStruct((tp, d), F32),
                   jax.ShapeDtypeStruct((tp, ROUTE_LANES), F32)],
        compiler_params=_cparams(("parallel",)),
        name="ab_out",
    )(y5, lru, h, w_glu.astype(BF16), b_glu.reshape(1, s5w), norm_a.reshape(1, s5w),
      norm_b.reshape(1, lruw), w_out.astype(BF16), g_ffn.reshape(1, d), wr, br)


def _route_tile(h, g, wr_hi, wr_lo, br):
    xn = _rms(h, g)
    m = xn.shape[0]
    x_hi = xn.astype(BF16)
    x_lo = (xn - x_hi.astype(F32)).astype(BF16)
    top = jnp.dot(jnp.concatenate([x_hi, x_lo], axis=0), wr_hi, preferred_element_type=F32)
    lg = top[:m] + (top[m:] + jnp.dot(x_hi, wr_lo, preferred_element_type=F32)) + br
    lane = lax.broadcasted_iota(jnp.int32, lg.shape, 1).astype(F32)
    big = float(lg.shape[1])
    neg = -jnp.inf
    gl = jnp.where(lane < N_GROUPS, lg, neg)
    mg = jnp.max(gl, axis=-1, keepdims=True)
    gidx = jnp.min(jnp.where(gl == mg, lane, big), axis=-1, keepdims=True)
    pg_sel = 1.0 / jnp.sum(jnp.exp(gl - mg), axis=-1, keepdims=True)
    lo = N_GROUPS + EXPERTS_PER_GROUP * gidx
    le = jnp.where(lane >= lo, jnp.where(lane < lo + EXPERTS_PER_GROUP, lg, neg), neg)
    v1 = jnp.max(le, axis=-1, keepdims=True)
    i1 = jnp.min(jnp.where(le == v1, lane, big), axis=-1, keepdims=True)
    le2 = jnp.where(lane == i1, neg, le)
    v2 = jnp.max(le2, axis=-1, keepdims=True)
    i2 = jnp.min(jnp.where(le2 == v2, lane, big), axis=-1, keepdims=True)
    e2 = jnp.exp(v2 - v1)
    w1 = pg_sel / (1.0 + e2)
    w2 = w1 * e2
    rt = jnp.where(lane == 0.0, w1, jnp.where(lane == 1.0, w2, jnp.where(
        lane == 2.0, i1 - N_GROUPS, jnp.where(lane == 3.0, i2 - N_GROUPS, 0.0))))
    return xn, rt


def _router_table(wr_g, br_g, wr_e, br_e):
    d = wr_g.shape[0]
    wr = jnp.zeros((d, ROUTE_LANES), F32).at[:, :N_GROUPS].set(wr_g).at[:, N_GROUPS:N_GROUPS + N_EXPERTS].set(wr_e)
    br = jnp.zeros((1, ROUTE_LANES), F32).at[0, :N_GROUPS].set(br_g).at[0, N_GROUPS:N_GROUPS + N_EXPERTS].set(br_e)
    wr_hi = wr.astype(BF16)
    wr_lo = (wr - wr_hi.astype(F32)).astype(BF16)
    return jnp.stack([wr_hi, wr_lo]), br


def _gmm_kernel(te_ref, tv_ref, x_ref, wg_ref, wu_ref, wd_ref, keep_ref, o_ref, wg_bf, wu_bf, wd_bf, *, first):
    del keep_ref
    step = pl.program_id(0)
    i = step + first

    @pl.when(jnp.logical_or(step == 0, te_ref[i] != te_ref[jnp.maximum(i - 1, 0)]))
    def _():
        wg_bf[...] = wg_ref[0, 0].astype(BF16)
        wu_bf[...] = wu_ref[0, 0].astype(BF16)
        wd_bf[...] = wd_ref[0, 0].astype(BF16)

    @pl.when(tv_ref[i] != 0)
    def _():
        x = x_ref[...].astype(BF16)
        hg = jnp.dot(x, wg_bf[...], preferred_element_type=F32)
        hu = jnp.dot(x, wu_bf[...], preferred_element_type=F32)
        hid = hg * _sigmoid(hg) * hu
        o_ref[...] = _bdot(hid, wd_bf[...]).astype(o_ref.dtype)

    @pl.when(tv_ref[i] == 0)
    def _():
        o_ref[...] = jnp.zeros(o_ref.shape, o_ref.dtype)


def _invert_rows_kernel(dest_ref, init_ref, src_ref, *, unroll):
    pltpu.sync_copy(init_ref, src_ref)
    n_tok = dest_ref.shape[0] // 2

    def put(t):
        src_ref[dest_ref[2 * t]] = t
        src_ref[dest_ref[2 * t + 1]] = t

    def body(i, _):
        for u in range(unroll):
            put(i * unroll + u)
        return 0

    lax.fori_loop(0, n_tok // unroll, body, 0)
    for t in range(n_tok - n_tok % unroll, n_tok):
        put(t)


def _invert_rows(dest, init):
    smem = pl.BlockSpec(memory_space=pltpu.SMEM)
    return pl.pallas_call(
        functools.partial(_invert_rows_kernel, unroll=8),
        in_specs=[smem, pl.BlockSpec(memory_space=pl.ANY)],
        out_specs=smem,
        out_shape=jax.ShapeDtypeStruct(init.shape, jnp.int32),
        name="moe_invert",
    )(dest, init)


def _moe_rows(tp):
    n_half = (-(-2 * tp // MOE_TILE) + N_EXPERTS + 1) // 2
    return n_half, 2 * n_half * MOE_TILE


def _moe(h, xn, rt, w_gate, w_up, w_down, layer, ys_buf):
    tp, d = h.shape
    f = w_gate.shape[-1]
    gate = rt[:, 0:2]
    eid = rt[:, 2:4].astype(jnp.int32)
    tmm = MOE_TILE
    na = 2 * tp
    e_flat = eid.reshape(na)
    onehot = (e_flat[:, None] == jnp.arange(N_EXPERTS, dtype=jnp.int32)[None, :]).astype(jnp.int32)
    csum = jnp.cumsum(onehot, axis=0)
    counts = csum[-1]
    padded = ((counts + tmm - 1) // tmm) * tmm
    ends = jnp.cumsum(padded)
    starts = ends - padded
    dest = jnp.sum(onehot * (csum + (starts - 1)[None, :]), axis=1)
    n_half, nrows = _moe_rows(tp)
    n_tiles = 2 * n_half
    tile_start = jnp.arange(n_tiles, dtype=jnp.int32) * tmm
    tile_e = jnp.sum((ends[None, :] <= tile_start[:, None]).astype(jnp.int32), axis=1)
    tile_v = (tile_e < N_EXPERTS).astype(jnp.int32)
    tile_e = jnp.minimum(tile_e, N_EXPERTS - 1)
    src = _invert_rows(dest, jnp.arange(nrows, dtype=jnp.int32) % tp)
    take = lambda a, i: a.at[i].get(mode="promise_in_bounds")
    ys = ys_buf
    for first in (0, n_half):
        xs = take(xn, src[first * tmm:(first + n_half) * tmm])
        wspec = lambda shape: pl.BlockSpec((1, 1) + shape, lambda i, te, tv: (layer, te[i + first], 0, 0))
        ys = pl.pallas_call(
            functools.partial(_gmm_kernel, first=first),
            grid_spec=pltpu.PrefetchScalarGridSpec(
                num_scalar_prefetch=2,
                grid=(n_half,),
                in_specs=[pl.BlockSpec((tmm, d), lambda i, te, tv: (i, 0)), wspec((d, f)), wspec((d, f)),
                          wspec((f, d)), pl.BlockSpec(memory_space=pl.ANY)],
                out_specs=pl.BlockSpec((tmm, d), lambda i, te, tv: (i + first, 0)),
                scratch_shapes=[pltpu.VMEM((d, f), BF16), pltpu.VMEM((d, f), BF16), pltpu.VMEM((f, d), BF16)],
            ),
            out_shape=jax.ShapeDtypeStruct((nrows, d), MOE_ROW_DTYPE),
            input_output_aliases={6: 0},
            compiler_params=_cparams(("arbitrary",)),
            name="moe_gmm",
        )(tile_e, tile_v, xs, w_gate, w_up, w_down, ys)
    d2 = dest.reshape(tp, 2)
    return h + gate[:, 0:1] * take(ys, d2[:, 0]) + gate[:, 1:2] * take(ys, d2[:, 1]), ys


def _rw_pre_kernel(*refs, tm, lp, has_vres):
    if has_vres:
        (h_ref, hp_ref, g_ref, mu_ref, wr_ref, wk_ref, wv_ref, w0_ref, wl1_ref, wl2_ref,
         a0_ref, al1_ref, al2_ref, gl1_ref, gl2_ref, vf_ref, v0_ref, vl1_ref, vl2_ref,
         r_ref, k_ref, v_ref, w_ref, a_ref, gg_ref) = refs
    else:
        (h_ref, hp_ref, g_ref, mu_ref, wr_ref, wk_ref, wv_ref, w0_ref, wl1_ref, wl2_ref,
         a0_ref, al1_ref, al2_ref, gl1_ref, gl2_ref,
         r_ref, k_ref, v_ref, w_ref, a_ref, gg_ref) = refs
    i = pl.program_id(0)
    g = g_ref[...]
    x = _rms(h_ref[...], g)
    xp8 = _rms(hp_ref[...], g)
    row = lax.broadcasted_iota(jnp.int32, x.shape, 0)
    prev = jnp.where(row == 0, jnp.broadcast_to(xp8[7:8, :], x.shape), pltpu.roll(x, 1, axis=0))
    first = lax.rem(lp - lax.rem(i * tm, lp), lp)
    prev = jnp.where(row == first, 0.0, prev)
    xx = prev - x
    xr, xw, xk, xv, xa, xg = [x + xx * mu_ref[j:j + 1, :] for j in range(6)]
    r_ref[...] = _bdot(xr, wr_ref[...]).astype(r_ref.dtype)
    k_ref[...] = _bdot(xk, wk_ref[...]).astype(k_ref.dtype)
    v = _bdot(xv, wv_ref[...])
    if has_vres:
        mix = _sigmoid(v0_ref[...] + _bdot(_bdot(xv, vl1_ref[...]), vl2_ref[...]))
        v = v + (vf_ref[...] - v) * mix
    v_ref[...] = v.astype(v_ref.dtype)
    w_ref[...] = -_softplus(-(w0_ref[...] + _bdot(jnp.tanh(_bdot(xw, wl1_ref[...])), wl2_ref[...]))) - 0.5
    a_ref[...] = _sigmoid(a0_ref[...] + _bdot(_bdot(xa, al1_ref[...]), al2_ref[...]))
    gg_ref[...] = _bdot(_sigmoid(_bdot(xg, gl1_ref[...])), gl2_ref[...]).astype(gg_ref.dtype)


def _rw_pre(h, g, mu, w_r, w_k, w_v, w0, w_l1, w_l2, a0, a_l1, a_l2, g_l1, g_l2, lp, v_first, v_res):
    tp, d = h.shape
    tm = _pick_tile(tp, min(512, lp))
    has_vres = v_res is not None
    row = pl.BlockSpec((tm, d), lambda i: (i, 0))
    prev8 = pl.BlockSpec((8, d), lambda i: (jnp.maximum(i * (tm // 8) - 1, 0), 0))
    vec = _const_spec((1, d))
    mu8 = jnp.zeros((8, d), F32).at[:6].set(mu)
    bf = lambda w: w.astype(BF16)
    ins = [h, h, g.reshape(1, d), mu8, bf(w_r), bf(w_k), bf(w_v), w0.reshape(1, d), bf(w_l1), bf(w_l2),
           a0.reshape(1, d), bf(a_l1), bf(a_l2), bf(g_l1), bf(g_l2)]
    specs = [row, prev8, vec, _const_spec((8, d))] + [_const_spec((d, d))] * 3 + [
        vec, _const_spec(w_l1.shape), _const_spec(w_l2.shape),
        vec, _const_spec(a_l1.shape), _const_spec(a_l2.shape), _const_spec(g_l1.shape), _const_spec(g_l2.shape)]
    if has_vres:
        v0, v_l1, v_l2 = v_res
        ins += [v_first, v0.reshape(1, d), bf(v_l1), bf(v_l2)]
        specs += [row, vec, _const_spec(v_l1.shape), _const_spec(v_l2.shape)]
    return pl.pallas_call(
        functools.partial(_rw_pre_kernel, tm=tm, lp=lp, has_vres=has_vres),
        grid=(tp // tm,),
        in_specs=specs,
        out_specs=[row] * 6,
        out_shape=[jax.ShapeDtypeStruct((tp, d), dt) for dt in (BF16, BF16, BF16, F32, F32, BF16)],
        compiler_params=_cparams(("parallel",)),
        name="rwkv_pre",
    )(*ins)


def _rw_scan_kernel(r_ref, w_ref, k_ref, v_ref, a_ref, kk_ref, ka_ref, rk_ref, lw_ref, lb_ref,
                    o_ref, g_scr, h_scr, q_scr, y0_scr, *, n_chunks, unroll, pairs):
    c = RW_CHUNK
    nl = 2 * RW_HEAD
    c2 = 2 * c
    ri = lax.broadcasted_iota(jnp.int32, (c2, nl), 0)
    ci = lax.broadcasted_iota(jnp.int32, (c2, nl), 1)
    own = ((ri >= c) == (ci >= RW_HEAD)).astype(F32)
    t_in = jnp.bitwise_and(ri, c - 1)
    s_in = jnp.bitwise_and(ci, c - 1)
    strict = jnp.where(s_in < t_in, own, 0.0)
    incl = jnp.where(s_in <= t_in, own, 0.0)
    causal2 = jnp.concatenate([strict, incl], axis=0)
    eye_l = (ri == ci).astype(F32)
    same_head = own.astype(BF16)
    ti = lax.broadcasted_iota(jnp.int32, (c, c), 0)
    si = lax.broadcasted_iota(jnp.int32, (c, c), 1)
    tril_c = (si <= ti).astype(BF16)
    lanes = lambda p: slice(p * nl, (p + 1) * nl)

    def stack(x):
        return jnp.concatenate([x, x], axis=0) * own

    def offset(ch):
        return ch * c if isinstance(ch, int) else pl.multiple_of(ch * c, c)

    def groups(fn):
        for p in range(pairs):
            def body(i, _):
                fn([i * unroll + q for q in range(unroll)], p)
                return 0
            lax.fori_loop(0, n_chunks // unroll, body, 0)
            if n_chunks % unroll:
                fn(list(range(n_chunks - n_chunks % unroll, n_chunks)), p)

    def prep(chs, p):
        ln = lanes(p)
        kk_w, ka_w = kk_ref[:, ln], ka_ref[:, ln]
        offs = [offset(ch) for ch in chs]
        r = [r_ref[pl.ds(o, c), ln] for o in offs]
        k = [k_ref[pl.ds(o, c), ln] for o in offs]
        v = [v_ref[pl.ds(o, c), ln] for o in offs]
        a = [a_ref[pl.ds(o, c), ln] for o in offs]
        logw = [-jnp.exp(w_ref[pl.ds(o, c), ln]) for o in offs]
        kk = [x * kk_w for x in k]
        ss = _split_dot_many([x * x for x in kk], same_head, pieces=2)
        kk = [x / jnp.maximum(jnp.sqrt(q), 1e-12) for x, q in zip(kk, ss)]
        kmod = [x * (1.0 + (y - 1.0) * ka_w) for x, y in zip(k, a)]
        cum = _split_dot_many(logw, tril_c, rhs=True, pieces=2)
        p_incl = [jnp.exp(x) for x in cum]
        p_inv = [jnp.exp(-x) for x in cum]
        p_end = [jnp.exp(x[c - 1:c, :] - x) for x in cum]
        kka = [x * y for x, y in zip(kk, a)]
        a_s = [stack(-x * jnp.exp(y - z)) for x, y, z in zip(kk, cum, logw)]
        r_s = [stack(x * y) for x, y in zip(r, p_incl)]
        v_s = [stack(x).astype(BF16) for x in v]
        lhs = [jnp.concatenate([x, y], axis=0).astype(BF16) for x, y in zip(a_s, r_s)]
        rhs = [jnp.concatenate([stack(x * z), stack(y * z)], axis=0).astype(BF16) for x, y, z in zip(kka, kmod, p_inv)]
        big = [lax.dot_general(x, y, (((1,), (1,)), ((), ())), preferred_element_type=F32) for x, y in zip(lhs, rhs)]
        lpow = [(x[:c2, :c2] * strict).astype(BF16) for x in big]
        a_rb = [x[c2:, :c2] * incl for x in big]
        avk = [_bdot(x[:, c2:] * causal2, w) for x, w in zip(big, v_s)]
        x = [jnp.concatenate([p, q[:c2]], axis=1) for p, q in zip(a_s, avk)]
        x = [p + _bdot(q, p) for p, q in zip(x, lpow)]
        for _ in range(5):
            lpow = [jnp.dot(q, q, preferred_element_type=F32).astype(BF16) for q in lpow]
            x = [p + _bdot(q, p) for p, q in zip(x, lpow)]
        xb = [p.astype(BF16) for p in x]
        bh_t = [stack(p * q).T for p, q in zip(kka, p_end)]
        kh_t = [stack(p * q).T for p, q in zip(kmod, p_end)]
        both = [_bdot(jnp.concatenate([u, p], axis=0), z) for u, p, z in zip(a_rb, bh_t, xb)]
        qy = [jnp.concatenate([p, q[c2:]], axis=1) + z[:c2] for p, q, z in zip(r_s, avk, both)]
        gh = [z[c2:] + jnp.concatenate([eye_l * q[c - 1:c, :], _bdot(u, w)], axis=1)
              for z, q, u, w in zip(both, p_incl, kh_t, v_s)]
        for ch, u, q in zip(chs, qy, gh):
            q_scr[p, ch] = u[:c, :nl] + u[c:, :nl]
            y0_scr[p, ch] = u[:c, nl:] + u[c:, nl:]
            g_scr[p, ch] = q[:, :nl]
            h_scr[p, ch] = q[:, nl:]

    groups(prep)

    def advance(ch, states):
        both = _dot3_many([jnp.concatenate([q_scr[p, ch], g_scr[p, ch]], axis=0) for p in range(pairs)], states)
        return (tuple(both[p][c:] + h_scr[p, ch] for p in range(pairs)),
                tuple(both[p][:c] + y0_scr[p, ch] for p in range(pairs)))

    def finish(ch, ys):
        o = offset(ch)
        rk = []
        for p in range(pairs):
            ln = lanes(p)
            kmod = k_ref[pl.ds(o, c), ln] * (1.0 + (a_ref[pl.ds(o, c), ln] - 1.0) * ka_ref[:, ln])
            rk.append(r_ref[pl.ds(o, c), ln] * kmod * rk_ref[:, ln])
        sums = _split_dot_many([jnp.concatenate([u, u * u, q], axis=0) for u, q in zip(ys, rk)], same_head)
        for p, (u, s) in enumerate(zip(ys, sums)):
            ln = lanes(p)
            mean = s[:c] * (1.0 / RW_HEAD)
            var = s[c:2 * c] * (1.0 / RW_HEAD) - mean * mean
            o_ref[pl.ds(o, c), ln] = ((u - mean) * lax.rsqrt(var + LNX_EPS) * lw_ref[:, ln] + lb_ref[:, ln]
                                      + s[2 * c:] * v_ref[pl.ds(o, c), ln])

    def step(ch, carry):
        states, ys = carry
        finish(ch - 1, ys)
        return advance(ch, states)

    carry = advance(0, tuple(jnp.zeros((nl, nl), F32) for _ in range(pairs)))
    _, ys = lax.fori_loop(1, n_chunks, step, carry)
    finish(n_chunks - 1, ys)


def _rw_scan(r, w, k, v, a, k_k, k_a, r_k, lnx_w, lnx_b, bsz, lp):
    tp, d = r.shape
    nl = 2 * RW_HEAD
    pairs = RW_PAIRS
    bw = pairs * nl
    nhp = d // bw
    nc = lp // RW_CHUNK
    seq = pl.BlockSpec((lp, bw), lambda b, hp: (b, hp))
    vec = pl.BlockSpec((1, bw), lambda b, hp: (0, hp))
    c = RW_CHUNK
    return pl.pallas_call(
        functools.partial(_rw_scan_kernel, n_chunks=nc, unroll=RW_UNROLL, pairs=pairs),
        grid=(bsz, nhp),
        in_specs=[seq] * 5 + [vec] * 5,
        out_specs=seq,
        out_shape=jax.ShapeDtypeStruct((tp, d), F32),
        scratch_shapes=[pltpu.VMEM((pairs, nc, nl, nl), F32), pltpu.VMEM((pairs, nc, nl, nl), F32),
                        pltpu.VMEM((pairs, nc, c, nl), F32), pltpu.VMEM((pairs, nc, c, nl), F32)],
        compiler_params=_cparams(("parallel", "parallel")),
        name="rwkv_scan",
    )(r, w, k, v, a, k_k.reshape(1, d), k_a.reshape(1, d), r_k.reshape(1, d),
      lnx_w.reshape(1, d), lnx_b.reshape(1, d))


def _rw_out_kernel(y_ref, g_ref, h_ref, wo_ref, gf_ref, wr_ref, br_ref, o_ref, xn_ref, rt_ref):
    h = h_ref[...] + _bdot(y_ref[...] * g_ref[...], wo_ref[...])
    o_ref[...] = h
    xn, rt_ref[...] = _route_tile(h, gf_ref[...], wr_ref[0], wr_ref[1], br_ref[...])
    xn_ref[...] = xn.astype(xn_ref.dtype)


def _rw_out(y, g, h, w_o, g_ffn, router):
    tp, d = h.shape
    wr, br = router
    tm = _pick_tile(tp, 512)
    row = lambda n: pl.BlockSpec((tm, n), lambda i: (i, 0))
    return pl.pallas_call(
        _rw_out_kernel,
        grid=(tp // tm,),
        in_specs=[row(d), row(d), row(d), _const_spec((d, d)), _const_spec((1, d)), _const_spec(wr.shape),
                  _const_spec(br.shape)],
        out_specs=[row(d), row(d), row(ROUTE_LANES)],
        out_shape=[jax.ShapeDtypeStruct((tp, d), F32), jax.ShapeDtypeStruct((tp, d), F32),
                   jax.ShapeDtypeStruct((tp, ROUTE_LANES), F32)],
        compiler_params=_cparams(("parallel",)),
        name="rwkv_out",
    )(y, g, h, w_o.astype(BF16), g_ffn.reshape(1, d), wr, br)


def _final_norm_kernel(h_ref, g_ref, o_ref, *, first, rows):
    o_ref[0] = _rms(h_ref[first:first + rows, :], g_ref[...]).astype(o_ref.dtype)


def _final_norm(h, g, dtype, bsz, lp, first, rows):
    d = h.shape[1]
    return pl.pallas_call(
        functools.partial(_final_norm_kernel, first=first, rows=rows),
        grid=(bsz,),
        in_specs=[pl.BlockSpec((lp, d), lambda b: (b, 0)), _const_spec((1, d))],
        out_specs=pl.BlockSpec((1, rows, d), lambda b: (b, 0, 0)),
        out_shape=jax.ShapeDtypeStruct((bsz, rows, d), dtype),
        compiler_params=_cparams(("parallel",)),
        name="final_norm",
    )(h, g.reshape(1, d))


def kernel(x, meta_tokens, norm_mix, norm_ffn, norm_final, ab_w_in, ab_w_out, ab_norm_a, ab_norm_b, s5_lam_re, s5_lam_im, s5_log_dt, s5_b_re, s5_b_im, s5_c_re, s5_c_im, s5_d, s5_w_glu, s5_b_glu, lru_conv_w, lru_conv_b, lru_w_a, lru_b_a, lru_w_x, lru_b_x, lru_lam, rw_mu, rw_w_r, rw_w_k, rw_w_v, rw_w_o, rw_w0, rw_w_l1, rw_w_l2, rw_a0, rw_a_l1, rw_a_l2, rw_v0, rw_v_l1, rw_v_l2, rw_g_l1, rw_g_l2, rw_k_k, rw_k_a, rw_r_k, rw_lnx_w, rw_lnx_b, moe_router_g, moe_router_g_b, moe_router_e, moe_router_e_b, moe_w_gate, moe_w_up, moe_w_down):
    bsz, seq, d = x.shape
    n_meta = meta_tokens.shape[0]
    depth = norm_mix.shape[0]
    ltot = n_meta + seq
    lp = -(-ltot // SEQ_ALIGN) * SEQ_ALIGN
    s5w = s5_w_glu.shape[-1]
    lruw = lru_lam.shape[-1]
    meta = jnp.broadcast_to(meta_tokens.astype(F32)[None], (bsz, n_meta, d))
    h = jnp.concatenate([meta, x.astype(F32), jnp.zeros((bsz, lp - ltot, d), F32)], axis=1).reshape(bsz * lp, d)
    v_first = None
    ys_buf = jnp.zeros((_moe_rows(bsz * lp)[1], d), MOE_ROW_DTYPE)
    flat = lambda a: a.reshape((-1,) + a.shape[2:])
    s5_tables = _s5_tables(flat(s5_lam_re), flat(s5_lam_im), flat(s5_log_dt), flat(s5_b_re), flat(s5_b_im),
                           flat(s5_c_re), flat(s5_c_im), flat(s5_d))
    for layer in range(depth):
        j = layer // 2
        router = _router_table(moe_router_g[layer], moe_router_g_b[layer], moe_router_e[layer], moe_router_e_b[layer])
        if layer % 2 == 0:
            u, gate, rec = _ab_in(h, norm_mix[layer], ab_w_in[j], s5w, lruw)
            y5 = _s5_scan(u, s5_tables, bsz, lp, j * (s5w // S5_LANES))
            lru = _lru(rec, gate, lru_conv_w[j], lru_conv_b[j], lru_w_a[j], lru_b_a[j], lru_w_x[j], lru_b_x[j],
                       lru_lam[j], bsz, lp)
            h, xn, rt = _ab_out(y5, lru, h, s5_w_glu[j], s5_b_glu[j], ab_norm_a[j], ab_norm_b[j], ab_w_out[j],
                                norm_ffn[layer], router)
        else:
            v_res = (rw_v0[j - 1], rw_v_l1[j - 1], rw_v_l2[j - 1]) if j > 0 else None
            r, k, v, w, a, g = _rw_pre(h, norm_mix[layer], rw_mu[j], rw_w_r[j], rw_w_k[j], rw_w_v[j], rw_w0[j],
                                       rw_w_l1[j], rw_w_l2[j], rw_a0[j], rw_a_l1[j], rw_a_l2[j], rw_g_l1[j],
                                       rw_g_l2[j], lp, v_first, v_res)
            if v_first is None:
                v_first = v
            y = _rw_scan(r, w, k, v, a, rw_k_k[j], rw_k_a[j], rw_r_k[j].reshape(-1), rw_lnx_w[j], rw_lnx_b[j],
                         bsz, lp)
            h, xn, rt = _rw_out(y, g, h, rw_w_o[j], norm_ffn[layer], router)
        h, ys_buf = _moe(h, xn, rt, moe_w_gate, moe_w_up, moe_w_down, layer, ys_buf)
    return _final_norm(h, norm_final, x.dtype, bsz, lp, n_meta, seq)
```

```python
import functools
import math

import jax
import jax.numpy as jnp
from jax import lax
from jax.experimental import pallas as pl
from jax.experimental.pallas import tpu as pltpu

F32 = jnp.float32
BF16 = jnp.bfloat16
HI = lax.Precision.HIGHEST

RMS_EPS = 1e-6
LNX_EPS = 64e-5
SEQ_ALIGN = 64
S5_CHUNK = 16
S5_LANES = 128
RW_CHUNK = 64
RW_HEAD = 64
RW_PAIRS = 2
RW_UNROLL = 11
LRU_C = 8.0
N_GROUPS = 4
EXPERTS_PER_GROUP = 4
N_EXPERTS = N_GROUPS * EXPERTS_PER_GROUP
MOE_TILE = 512
ROUTE_LANES = 128
MOE_ROW_DTYPE = BF16
VMEM_LIMIT = 56 * 1024 * 1024


def _cparams(sem):
    return pltpu.CompilerParams(dimension_semantics=sem, vmem_limit_bytes=VMEM_LIMIT)


def _pick_tile(n, target):
    best = 8
    for t in range(8, min(n, target) + 1, 8):
        if n % t == 0:
            best = t
    return best


def _const_spec(shape):
    nd = len(shape)
    return pl.BlockSpec(shape, lambda *_: (0,) * nd)


def _rms(x, g):
    return x * lax.rsqrt(jnp.mean(x * x, axis=-1, keepdims=True) + RMS_EPS) * g


def _gelu(x):
    return 0.5 * x * (1.0 + jnp.tanh(math.sqrt(2.0 / math.pi) * (x + 0.044715 * (x * x * x))))


def _sigmoid(x):
    return 1.0 / (1.0 + jnp.exp(-x))


def _softplus(x):
    return jnp.maximum(x, 0.0) + jnp.log(1.0 + jnp.exp(-jnp.abs(x)))


def _bdot(a, b):
    return jnp.dot(a.astype(BF16), b.astype(BF16), preferred_element_type=F32)


def _dot3_many(a_list, b_list):
    m = a_list[0].shape[0]
    dot = functools.partial(jnp.dot, preferred_element_type=F32)
    a_hi = [a.astype(BF16) for a in a_list]
    b_hi = [b.astype(BF16) for b in b_list]
    a_lo = [(a - h.astype(F32)).astype(BF16) for a, h in zip(a_list, a_hi)]
    b_lo = [(b - h.astype(F32)).astype(BF16) for b, h in zip(b_list, b_hi)]
    top = [dot(jnp.concatenate([h, l], axis=0), b) for h, l, b in zip(a_hi, a_lo, b_hi)]
    low = [dot(h, b) for h, b in zip(a_hi, b_lo)]
    return [t[:m] + (t[m:] + l) for t, l in zip(top, low)]


def _split_dot_many(xs, m, rhs=False, pieces=3):
    accs = [None] * len(xs)
    xs = list(xs)
    for _ in range(pieces):
        his = [x.astype(BF16) for x in xs]
        parts = [jnp.dot(m, hi, preferred_element_type=F32) if rhs else jnp.dot(hi, m, preferred_element_type=F32)
                 for hi in his]
        accs = [p if a is None else a + p for a, p in zip(accs, parts)]
        xs = [x - hi.astype(F32) for x, hi in zip(xs, his)]
    return accs


def _ab_in_kernel(h_ref, g_ref, w_ref, u_ref, gate_ref, rec_ref, *, s5w, lruw):
    xn = _rms(h_ref[...], g_ref[...])
    z = _bdot(xn, w_ref[...])
    u_ref[...] = z[:, :s5w].astype(u_ref.dtype)
    gate_ref[...] = z[:, s5w:s5w + lruw]
    rec_ref[...] = z[:, s5w + lruw:]


def _ab_in(h, g, w_in, s5w, lruw):
    tp, d = h.shape
    tm = _pick_tile(tp, 512)
    row = lambda n: pl.BlockSpec((tm, n), lambda i: (i, 0))
    return pl.pallas_call(
        functools.partial(_ab_in_kernel, s5w=s5w, lruw=lruw),
        grid=(tp // tm,),
        in_specs=[row(d), _const_spec((1, d)), _const_spec(w_in.shape)],
        out_specs=[row(s5w), row(lruw), row(lruw)],
        out_shape=[jax.ShapeDtypeStruct((tp, s5w), BF16),
                   jax.ShapeDtypeStruct((tp, lruw), F32),
                   jax.ShapeDtypeStruct((tp, lruw), F32)],
        compiler_params=_cparams(("parallel",)),
        name="ab_in",
    )(h, g.reshape(1, d), w_in.astype(BF16))


def _s5_tables(lam_re, lam_im, log_dt, b_re, b_im, c_re, c_im, d_skip):
    g, p = lam_re.shape
    hh = b_re.shape[-1]
    c = S5_CHUNK
    lr, li = lam_re.astype(F32), lam_im.astype(F32)
    dt = jnp.exp(log_dt.astype(F32))[:, None]
    mag = jnp.exp(lr * dt)
    abar_r = mag * jnp.cos(li * dt)
    abar_i = mag * jnp.sin(li * dt)
    den = lr * lr + li * li
    zr = ((abar_r - 1.0) * lr + abar_i * li) / den
    zi = (abar_i * lr - (abar_r - 1.0) * li) / den
    bbar_r = zr[..., None] * b_re - zi[..., None] * b_im
    bbar_i = zr[..., None] * b_im + zi[..., None] * b_re
    bbr_t, bbi_t = jnp.swapaxes(bbar_r, 1, 2), jnp.swapaxes(bbar_i, 1, 2)
    cr_t, ci_t = jnp.swapaxes(c_re, 1, 2), jnp.swapaxes(c_im, 1, 2)

    def powers(steps):
        st = steps.astype(F32)[None, :, None]
        pmag = jnp.exp(st * (lr * dt)[:, None, :])
        ang = st * (li * dt)[:, None, :]
        return pmag * jnp.cos(ang), pmag * jnp.sin(ang)

    down = (c - 1) - jnp.arange(c)
    rev_r, rev_i = powers(down)
    m1_r = rev_r[:, :, None, :] * bbr_t[:, None] - rev_i[:, :, None, :] * bbi_t[:, None]
    m1_i = rev_r[:, :, None, :] * bbi_t[:, None] + rev_i[:, :, None, :] * bbr_t[:, None]
    car = c_re[:, None] * rev_r[:, :, None, :] - c_im[:, None] * rev_i[:, :, None, :]
    cai = c_re[:, None] * rev_i[:, :, None, :] + c_im[:, None] * rev_r[:, :, None, :]
    kern = (jnp.einsum('gqhp,gpj->gqjh', car, bbar_r, precision=HI)
            - jnp.einsum('gqhp,gpj->gqjh', cai, bbar_i, precision=HI))
    is_tau0 = (down == 0).astype(F32)[None, :, None, None]
    kern = kern + is_tau0 * (d_skip[:, None, None, :] * jnp.eye(hh, dtype=F32)[None, None])
    up_r, up_i = powers(jnp.arange(1, c + 1))
    up_r, up_i = jnp.swapaxes(up_r, 1, 2)[..., None], jnp.swapaxes(up_i, 1, 2)[..., None]
    m2_r = cr_t[:, :, None, :] * up_r - ci_t[:, :, None, :] * up_i
    m2_i = -(cr_t[:, :, None, :] * up_i + ci_t[:, :, None, :] * up_r)
    adv_r, adv_i = powers(jnp.full((1,), c))
    gb = S5_LANES // hh
    nj = g // gb

    def rows_sgh(x):
        w = x.shape[-1]
        return jnp.transpose(x.reshape(nj, gb, c, hh, w), (0, 2, 1, 3, 4)).reshape(nj, c * gb * hh, w)

    def place(base, spread, row_group, col_group):
        out = jnp.einsum('jrw,wc->jrc', base.astype(BF16), spread.astype(BF16), preferred_element_type=BF16)
        rg = row_group(lax.broadcasted_iota(jnp.int32, out.shape[1:], 0))
        cg = col_group(lax.broadcasted_iota(jnp.int32, out.shape[1:], 1))
        return jnp.where(rg == cg, out, jnp.zeros((), BF16))

    grp_sgh = lambda r: (r // hh) % gb
    rep = lambda w: jnp.tile(jnp.eye(w, dtype=F32), (1, gb))
    m1 = jnp.concatenate([place(rows_sgh(m), rep(p), grp_sgh, lambda col: col // p) for m in (m1_r, m1_i)], axis=-1)
    krev = place(rows_sgh(kern), rep(hh), grp_sgh, lambda col: col // hh)
    ri = lax.broadcasted_iota(jnp.int32, (c * hh, c * gb * hh), 0)
    ci = lax.broadcasted_iota(jnp.int32, (c * hh, c * gb * hh), 1)
    spread_th = ((ri // hh == ci // (gb * hh)) & (ri % hh == ci % hh)).astype(F32)
    m2 = jnp.concatenate([place(m.reshape(nj, gb * p, c * hh), spread_th, lambda r: r // p,
                                lambda col: (col // hh) % gb) for m in (m2_r, m2_i)], axis=1)
    return (m1.astype(BF16), krev.astype(BF16), m2.astype(BF16),
            adv_r.reshape(nj, 1, gb * p), adv_i.reshape(nj, 1, gb * p))


def _s5_kernel(u_ref, m1_ref, kr_ref, m2_ref, ar_ref, ai_ref, y_ref, xe_ref, xin_ref, st_ref, *, n_chunks, nb):
    nl = S5_LANES
    csz = S5_CHUNK
    sw = ar_ref.shape[-1]

    @pl.when(pl.program_id(1) == 0)
    def _():
        st_ref[...] = jnp.zeros(st_ref.shape, F32)

    u = u_ref[0]
    xe_ref[...] = jnp.dot(u, m1_ref[0], preferred_element_type=F32)
    ar = jnp.broadcast_to(ar_ref[0], (nb, sw))
    ai = jnp.broadcast_to(ai_ref[0], (nb, sw))

    def body(c, carry):
        sr, si = carry
        off = pl.multiple_of(c * nb, nb)
        xin_ref[pl.ds(off, nb), :] = jnp.concatenate([sr, si], axis=1)
        e = xe_ref[pl.ds(off, nb), :]
        return (ar * sr - ai * si + e[:, :sw], ar * si + ai * sr + e[:, sw:])

    sr, si = lax.fori_loop(0, n_chunks, body, (st_ref[:, :sw], st_ref[:, sw:]))
    st_ref[:, :sw] = sr
    st_ref[:, sw:] = si
    y_ref[0] = _bdot(xin_ref[...], m2_ref[0])
    for t in range(csz):
        y_ref[0, :, t * nl:(t + 1) * nl] += jnp.dot(u[:, :(t + 1) * nl], kr_ref[0, (csz - 1 - t) * nl:, :],
                                                    preferred_element_type=F32)


def _s5_scan(u, tables, bsz, lp, blk0=0):
    m1, krev, m2, adv_r, adv_i = tables
    _, kin, sw2 = m1.shape
    c = S5_CHUNK
    nl = S5_LANES
    nj = u.shape[1] // nl
    nc = lp // c
    cpt = max(d for d in range(1, nc + 1) if nc % d == 0 and d * bsz <= 512)
    rows = cpt * bsz
    ug = jnp.transpose(u.reshape(bsz, nc, c, nj, nl), (3, 1, 0, 2, 4)).reshape(nj, nc * bsz, kin)
    y = pl.pallas_call(
        functools.partial(_s5_kernel, n_chunks=cpt, nb=bsz),
        grid=(nj, nc // cpt),
        in_specs=[pl.BlockSpec((1, rows, kin), lambda j, r: (j, r, 0)),
                  pl.BlockSpec((1, kin, sw2), lambda j, r: (j + blk0, 0, 0)),
                  pl.BlockSpec((1, kin, nl), lambda j, r: (j + blk0, 0, 0)),
                  pl.BlockSpec((1, sw2, kin), lambda j, r: (j + blk0, 0, 0)),
                  pl.BlockSpec((1, 1, sw2 // 2), lambda j, r: (j + blk0, 0, 0)),
                  pl.BlockSpec((1, 1, sw2 // 2), lambda j, r: (j + blk0, 0, 0))],
        out_specs=pl.BlockSpec((1, rows, kin), lambda j, r: (j, r, 0)),
        out_shape=jax.ShapeDtypeStruct((nj, nc * bsz, kin), F32),
        scratch_shapes=[pltpu.VMEM((rows, sw2), F32), pltpu.VMEM((rows, sw2), F32), pltpu.VMEM((bsz, sw2), F32)],
        compiler_params=_cparams(("parallel", "arbitrary")),
        name="s5_scan",
    )(ug, m1, krev, m2, adv_r, adv_i)
    y = jnp.transpose(y.reshape(nj, nc, bsz, c, nl), (2, 1, 3, 0, 4))
    return y.reshape(bsz * lp, nj * nl)


def _lru_kernel(rec_ref, gate_ref, cw_ref, cb_ref, wa_ref, ba_ref, wx_ref, bx_ref, lam_ref,
                o_ref, ext_ref, a_ref, b_ref, h_ref, *, tl):
    t = pl.program_id(1)

    @pl.when(t == 0)
    def _():
        ext_ref[0:8, :] = jnp.zeros((8, ext_ref.shape[1]), F32)
        h_ref[...] = jnp.zeros(h_ref.shape, F32)

    x = rec_ref[...]
    ext_ref[8:, :] = x
    xc = cb_ref[...] + cw_ref[3:4, :] * x
    for k in range(3):
        xc = xc + cw_ref[k:k + 1, :] * ext_ref[5 + k:5 + k + tl, :]
    ext_ref[0:8, :] = x[tl - 8:, :]
    r = _sigmoid(_bdot(xc, wa_ref[...]) + ba_ref[...])
    i = _sigmoid(_bdot(xc, wx_ref[...]) + bx_ref[...])
    log_a = (-LRU_C) * r * _softplus(-lam_ref[...])
    a = jnp.exp(log_a)
    a_ref[...] = a
    b_ref[...] = jnp.sqrt(1.0 - a * a) * (i * xc)

    sub = 8
    unroll = max(u for u in (4, 2, 1) if (tl // sub) % u == 0)
    row = lax.broadcasted_iota(jnp.int32, (sub, a.shape[1]), 0)

    def body(j, h):
        offs = [pl.multiple_of((j * unroll + q) * sub, sub) for q in range(unroll)]
        ab = [a_ref[pl.ds(o, sub), :] for o in offs]
        bb = [b_ref[pl.ds(o, sub), :] for o in offs]
        for dist in (1, 2, 4):
            keep = row >= dist
            bb = [jnp.where(keep, x * pltpu.roll(y, dist, axis=0) + y, y) for x, y in zip(ab, bb)]
            ab = [jnp.where(keep, x * pltpu.roll(x, dist, axis=0), x) for x in ab]
        for o, x, y in zip(offs, ab, bb):
            hb = x * h + y
            o_ref[pl.ds(o, sub), :] = hb * _gelu(gate_ref[pl.ds(o, sub), :])
            h = hb[sub - 1:sub, :]
        return h

    h_ref[...] = lax.fori_loop(0, tl // (sub * unroll), body, h_ref[...])


def _lru(rec, gate, conv_w, conv_b, w_a, b_a, w_x, b_x, lam, bsz, lp):
    tp, w = rec.shape
    tl = _pick_tile(lp, 1056)
    nt = lp // tl
    heads, hd, _ = w_a.shape

    def dense(wb):
        eye = jnp.eye(heads, dtype=F32)
        return jnp.einsum('hij,hg->higj', wb, eye).reshape(w, w).astype(BF16)

    row = pl.BlockSpec((tl, w), lambda b, t: (b * nt + t, 0))
    vec = _const_spec((1, w))
    return pl.pallas_call(
        functools.partial(_lru_kernel, tl=tl),
        grid=(bsz, nt),
        in_specs=[row, row, _const_spec((4, w)), vec, _const_spec((w, w)), vec, _const_spec((w, w)), vec, vec],
        out_specs=row,
        out_shape=jax.ShapeDtypeStruct((tp, w), F32),
        scratch_shapes=[pltpu.VMEM((tl + 8, w), F32), pltpu.VMEM((tl, w), F32),
                        pltpu.VMEM((tl, w), F32), pltpu.VMEM((1, w), F32)],
        compiler_params=_cparams(("parallel", "arbitrary")),
        name="rglru",
    )(rec, gate, conv_w, conv_b.reshape(1, w), dense(w_a), b_a.reshape(1, w), dense(w_x), b_x.reshape(1, w),
      lam.reshape(1, w))


def _ab_out_kernel(y5_ref, lru_ref, h_ref, wglu_ref, bglu_ref, na_ref, nb_ref, wo_ref, gf_ref, wr_ref, br_ref,
                   o_ref, xn_ref, rt_ref, *, s5w):
    y = _gelu(y5_ref[...])
    ya = y * _sigmoid(_bdot(y, wglu_ref[...]) + bglu_ref[...])
    ya = _rms(ya, na_ref[...])
    yb = _rms(lru_ref[...], nb_ref[...])
    h = h_ref[...] + _bdot(ya, wo_ref[:s5w, :]) + _bdot(yb, wo_ref[s5w:, :])
    o_ref[...] = h
    xn, rt_ref[...] = _route_tile(h, gf_ref[...], wr_ref[0], wr_ref[1], br_ref[...])
    xn_ref[...] = xn.astype(xn_ref.dtype)


def _ab_out(y5, lru, h, w_glu, b_glu, norm_a, norm_b, w_out, g_ffn, router):
    tp, d = h.shape
    s5w, lruw = y5.shape[1], lru.shape[1]
    wr, br = router
    tm = _pick_tile(tp, 512)
    row = lambda n: pl.BlockSpec((tm, n), lambda i: (i, 0))
    return pl.pallas_call(
        functools.partial(_ab_out_kernel, s5w=s5w),
        grid=(tp // tm,),
        in_specs=[row(s5w), row(lruw), row(d), _const_spec((s5w, s5w)), _const_spec((1, s5w)),
                  _const_spec((1, s5w)), _const_spec((1, lruw)), _const_spec(w_out.shape),
                  _const_spec((1, d)), _const_spec(wr.shape), _const_spec(br.shape)],
        out_specs=[row(d), row(d), row(ROUTE_LANES)],
        out_shape=[jax.ShapeDtypeStruct((tp, d), F32), jax.ShapeDtypeStruct((tp, d), F32),
                   jax.ShapeDtypeStruct((tp, ROUTE_LANES), F32)],
        compiler_params=_cparams(("parallel",)),
        name="ab_out",
    )(y5, lru, h, w_glu.astype(BF16), b_glu.reshape(1, s5w), norm_a.reshape(1, s5w),
      norm_b.reshape(1, lruw), w_out.astype(BF16), g_ffn.reshape(1, d), wr, br)


def _route_tile(h, g, wr_hi, wr_lo, br):
    xn = _rms(h, g)
    m = xn.shape[0]
    x_hi = xn.astype(BF16)
    x_lo = (xn - x_hi.astype(F32)).astype(BF16)
    top = jnp.dot(jnp.concatenate([x_hi, x_lo], axis=0), wr_hi, preferred_element_type=F32)
    lg = top[:m] + (top[m:] + jnp.dot(x_hi, wr_lo, preferred_element_type=F32)) + br
    lane = lax.broadcasted_iota(jnp.int32, lg.shape, 1).astype(F32)
    big = float(lg.shape[1])
    neg = -jnp.inf
    gl = jnp.where(lane < N_GROUPS, lg, neg)
    mg = jnp.max(gl, axis=-1, keepdims=True)
    gidx = jnp.min(jnp.where(gl == mg, lane, big), axis=-1, keepdims=True)
    pg_sel = 1.0 / jnp.sum(jnp.exp(gl - mg), axis=-1, keepdims=True)
    lo = N_GROUPS + EXPERTS_PER_GROUP * gidx
    le = jnp.where(lane >= lo, jnp.where(lane < lo + EXPERTS_PER_GROUP, lg, neg), neg)
    v1 = jnp.max(le, axis=-1, keepdims=True)
    i1 = jnp.min(jnp.where(le == v1, lane, big), axis=-1, keepdims=True)
    le2 = jnp.where(lane == i1, neg, le)
    v2 = jnp.max(le2, axis=-1, keepdims=True)
    i2 = jnp.min(jnp.where(le2 == v2, lane, big), axis=-1, keepdims=True)
    e2 = jnp.exp(v2 - v1)
    w1 = pg_sel / (1.0 + e2)
    w2 = w1 * e2
    rt = jnp.where(lane == 0.0, w1, jnp.where(lane == 1.0, w2, jnp.where(
        lane == 2.0, i1 - N_GROUPS, jnp.where(lane == 3.0, i2 - N_GROUPS, 0.0))))
    return xn, rt


def _router_table(wr_g, br_g, wr_e, br_e):
    d = wr_g.shape[0]
    wr = jnp.zeros((d, ROUTE_LANES), F32).at[:, :N_GROUPS].set(wr_g).at[:, N_GROUPS:N_GROUPS + N_EXPERTS].set(wr_e)
    br = jnp.zeros((1, ROUTE_LANES), F32).at[0, :N_GROUPS].set(br_g).at[0, N_GROUPS:N_GROUPS + N_EXPERTS].set(br_e)
    wr_hi = wr.astype(BF16)
    wr_lo = (wr - wr_hi.astype(F32)).astype(BF16)
    return jnp.stack([wr_hi, wr_lo]), br


def _gmm_kernel(te_ref, tv_ref, x_ref, wg_ref, wu_ref, wd_ref, *rest, first, n_active):
    o_ref, wg_bf, wu_bf, wd_bf = rest[-4:]
    step = pl.program_id(0)
    i = jnp.minimum(step, n_active - 1) + first
    live = jnp.logical_and(step < n_active, tv_ref[i] != 0)

    @pl.when(jnp.logical_and(live, jnp.logical_or(step == 0, te_ref[i] != te_ref[jnp.maximum(i - 1, 0)])))
    def _():
        wg_bf[...] = wg_ref[0, 0].astype(BF16)
        wu_bf[...] = wu_ref[0, 0].astype(BF16)
        wd_bf[...] = wd_ref[0, 0].astype(BF16)

    @pl.when(live)
    def _():
        x = x_ref[...].astype(BF16)
        hg = jnp.dot(x, wg_bf[...], preferred_element_type=F32)
        hu = jnp.dot(x, wu_bf[...], preferred_element_type=F32)
        hid = hg * _sigmoid(hg) * hu
        o_ref[...] = _bdot(hid, wd_bf[...]).astype(o_ref.dtype)

    @pl.when(jnp.logical_not(live))
    def _():
        o_ref[...] = jnp.zeros(o_ref.shape, o_ref.dtype)


def _invert_rows_kernel(dest_ref, init_ref, src_ref, *, unroll):
    pltpu.sync_copy(init_ref, src_ref)
    n_tok = dest_ref.shape[0] // 2

    def put(t):
        src_ref[dest_ref[2 * t]] = t
        src_ref[dest_ref[2 * t + 1]] = t

    def body(i, _):
        for u in range(unroll):
            put(i * unroll + u)
        return 0

    lax.fori_loop(0, n_tok // unroll, body, 0)
    for t in range(n_tok - n_tok % unroll, n_tok):
        put(t)


def _invert_rows(dest, init):
    smem = pl.BlockSpec(memory_space=pltpu.SMEM)
    return pl.pallas_call(
        functools.partial(_invert_rows_kernel, unroll=8),
        in_specs=[smem, pl.BlockSpec(memory_space=pl.ANY)],
        out_specs=smem,
        out_shape=jax.ShapeDtypeStruct(init.shape, jnp.int32),
        name="moe_invert",
    )(dest, init)


def _moe(h, xn, rt, w_gate, w_up, w_down, layer):
    tp, d = h.shape
    f = w_gate.shape[-1]
    gate = rt[:, 0:2]
    eid = rt[:, 2:4].astype(jnp.int32)
    tmm = MOE_TILE
    na = 2 * tp
    e_flat = eid.reshape(na)
    onehot = (e_flat[:, None] == jnp.arange(N_EXPERTS, dtype=jnp.int32)[None, :]).astype(jnp.int32)
    csum = jnp.cumsum(onehot, axis=0)
    counts = csum[-1]
    padded = ((counts + tmm - 1) // tmm) * tmm
    ends = jnp.cumsum(padded)
    starts = ends - padded
    dest = jnp.sum(onehot * (csum + (starts - 1)[None, :]), axis=1)
    n_half = (-(-na // tmm) + N_EXPERTS + 1) // 2
    n_tiles = 2 * n_half
    nrows = n_tiles * tmm
    tile_start = jnp.arange(n_tiles, dtype=jnp.int32) * tmm
    tile_e = jnp.sum((ends[None, :] <= tile_start[:, None]).astype(jnp.int32), axis=1)
    tile_v = (tile_e < N_EXPERTS).astype(jnp.int32)
    tile_e = jnp.minimum(tile_e, N_EXPERTS - 1)
    src = _invert_rows(dest, jnp.arange(nrows, dtype=jnp.int32) % tp)
    take = lambda a, i: a.at[i].get(mode="promise_in_bounds")
    ys = None
    for first in (0, n_half):
        xs = take(xn, src[first * tmm:(first + n_half) * tmm])
        local = lambda i: jnp.minimum(i, n_half - 1)
        wspec = lambda shape: pl.BlockSpec((1, 1) + shape, lambda i, te, tv: (layer, te[local(i) + first], 0, 0))
        prev = [] if ys is None else [ys]
        ys = pl.pallas_call(
            functools.partial(_gmm_kernel, first=first, n_active=n_half),
            grid_spec=pltpu.PrefetchScalarGridSpec(
                num_scalar_prefetch=2,
                grid=(n_tiles - first,),
                in_specs=[pl.BlockSpec((tmm, d), lambda i, te, tv: (local(i), 0)), wspec((d, f)), wspec((d, f)),
                          wspec((f, d))] + [pl.BlockSpec(memory_space=pl.ANY)] * len(prev),
                out_specs=pl.BlockSpec((tmm, d), lambda i, te, tv: (i + first, 0)),
                scratch_shapes=[pltpu.VMEM((d, f), BF16), pltpu.VMEM((d, f), BF16), pltpu.VMEM((f, d), BF16)],
            ),
            out_shape=jax.ShapeDtypeStruct((nrows, d), MOE_ROW_DTYPE),
            input_output_aliases={6: 0} if prev else {},
            compiler_params=_cparams(("arbitrary",)),
            name="moe_gmm",
        )(tile_e, tile_v, xs, w_gate, w_up, w_down, *prev)
    d2 = dest.reshape(tp, 2)
    return h + gate[:, 0:1] * take(ys, d2[:, 0]) + gate[:, 1:2] * take(ys, d2[:, 1])


def _rw_pre_kernel(*refs, tm, lp, has_vres):
    if has_vres:
        (h_ref, hp_ref, g_ref, mu_ref, wr_ref, wk_ref, wv_ref, w0_ref, wl1_ref, wl2_ref,
         a0_ref, al1_ref, al2_ref, gl1_ref, gl2_ref, vf_ref, v0_ref, vl1_ref, vl2_ref,
         r_ref, k_ref, v_ref, w_ref, a_ref, gg_ref) = refs
    else:
        (h_ref, hp_ref, g_ref, mu_ref, wr_ref, wk_ref, wv_ref, w0_ref, wl1_ref, wl2_ref,
         a0_ref, al1_ref, al2_ref, gl1_ref, gl2_ref,
         r_ref, k_ref, v_ref, w_ref, a_ref, gg_ref) = refs
    i = pl.program_id(0)
    g = g_ref[...]
    x = _rms(h_ref[...], g)
    xp8 = _rms(hp_ref[...], g)
    row = lax.broadcasted_iota(jnp.int32, x.shape, 0)
    prev = jnp.where(row == 0, jnp.broadcast_to(xp8[7:8, :], x.shape), pltpu.roll(x, 1, axis=0))
    first = lax.rem(lp - lax.rem(i * tm, lp), lp)
    prev = jnp.where(row == first, 0.0, prev)
    xx = prev - x
    xr, xw, xk, xv, xa, xg = [x + xx * mu_ref[j:j + 1, :] for j in range(6)]
    r_ref[...] = _bdot(xr, wr_ref[...]).astype(r_ref.dtype)
    k_ref[...] = _bdot(xk, wk_ref[...]).astype(k_ref.dtype)
    v = _bdot(xv, wv_ref[...])
    if has_vres:
        mix = _sigmoid(v0_ref[...] + _bdot(_bdot(xv, vl1_ref[...]), vl2_ref[...]))
        v = v + (vf_ref[...] - v) * mix
    v_ref[...] = v.astype(v_ref.dtype)
    w_ref[...] = -_softplus(-(w0_ref[...] + _bdot(jnp.tanh(_bdot(xw, wl1_ref[...])), wl2_ref[...]))) - 0.5
    a_ref[...] = _sigmoid(a0_ref[...] + _bdot(_bdot(xa, al1_ref[...]), al2_ref[...]))
    gg_ref[...] = _bdot(_sigmoid(_bdot(xg, gl1_ref[...])), gl2_ref[...]).astype(gg_ref.dtype)


def _rw_pre(h, g, mu, w_r, w_k, w_v, w0, w_l1, w_l2, a0, a_l1, a_l2, g_l1, g_l2, lp, v_first, v_res):
    tp, d = h.shape
    tm = _pick_tile(tp, min(512, lp))
    has_vres = v_res is not None
    row = pl.BlockSpec((tm, d), lambda i: (i, 0))
    prev8 = pl.BlockSpec((8, d), lambda i: (jnp.maximum(i * (tm // 8) - 1, 0), 0))
    vec = _const_spec((1, d))
    mu8 = jnp.zeros((8, d), F32).at[:6].set(mu)
    bf = lambda w: w.astype(BF16)
    ins = [h, h, g.reshape(1, d), mu8, bf(w_r), bf(w_k), bf(w_v), w0.reshape(1, d), bf(w_l1), bf(w_l2),
           a0.reshape(1, d), bf(a_l1), bf(a_l2), bf(g_l1), bf(g_l2)]
    specs = [row, prev8, vec, _const_spec((8, d))] + [_const_spec((d, d))] * 3 + [
        vec, _const_spec(w_l1.shape), _const_spec(w_l2.shape),
        vec, _const_spec(a_l1.shape), _const_spec(a_l2.shape), _const_spec(g_l1.shape), _const_spec(g_l2.shape)]
    if has_vres:
        v0, v_l1, v_l2 = v_res
        ins += [v_first, v0.reshape(1, d), bf(v_l1), bf(v_l2)]
        specs += [row, vec, _const_spec(v_l1.shape), _const_spec(v_l2.shape)]
    return pl.pallas_call(
        functools.partial(_rw_pre_kernel, tm=tm, lp=lp, has_vres=has_vres),
        grid=(tp // tm,),
        in_specs=specs,
        out_specs=[row] * 6,
        out_shape=[jax.ShapeDtypeStruct((tp, d), dt) for dt in (BF16, BF16, BF16, F32, F32, BF16)],
        compiler_params=_cparams(("parallel",)),
        name="rwkv_pre",
    )(*ins)


def _rw_scan_kernel(r_ref, w_ref, k_ref, v_ref, a_ref, kk_ref, ka_ref, rk_ref, lw_ref, lb_ref,
                    o_ref, g_scr, h_scr, q_scr, y0_scr, *, n_chunks, unroll, pairs):
    c = RW_CHUNK
    nl = 2 * RW_HEAD
    c2 = 2 * c
    ri = lax.broadcasted_iota(jnp.int32, (c2, nl), 0)
    ci = lax.broadcasted_iota(jnp.int32, (c2, nl), 1)
    own = ((ri >= c) == (ci >= RW_HEAD)).astype(F32)
    t_in = jnp.bitwise_and(ri, c - 1)
    s_in = jnp.bitwise_and(ci, c - 1)
    strict = jnp.where(s_in < t_in, own, 0.0)
    incl = jnp.where(s_in <= t_in, own, 0.0)
    causal2 = jnp.concatenate([strict, incl], axis=0)
    eye_l = (ri == ci).astype(F32)
    same_head = own.astype(BF16)
    ti = lax.broadcasted_iota(jnp.int32, (c, c), 0)
    si = lax.broadcasted_iota(jnp.int32, (c, c), 1)
    tril_c = (si <= ti).astype(BF16)
    lanes = lambda p: slice(p * nl, (p + 1) * nl)

    def stack(x):
        return jnp.concatenate([x, x], axis=0) * own

    def offset(ch):
        return ch * c if isinstance(ch, int) else pl.multiple_of(ch * c, c)

    def groups(fn):
        for p in range(pairs):
            def body(i, _):
                fn([i * unroll + q for q in range(unroll)], p)
                return 0
            lax.fori_loop(0, n_chunks // unroll, body, 0)
            if n_chunks % unroll:
                fn(list(range(n_chunks - n_chunks % unroll, n_chunks)), p)

    def prep(chs, p):
        ln = lanes(p)
        kk_w, ka_w = kk_ref[:, ln], ka_ref[:, ln]
        offs = [offset(ch) for ch in chs]
        r = [r_ref[pl.ds(o, c), ln] for o in offs]
        k = [k_ref[pl.ds(o, c), ln] for o in offs]
        v = [v_ref[pl.ds(o, c), ln] for o in offs]
        a = [a_ref[pl.ds(o, c), ln] for o in offs]
        logw = [-jnp.exp(w_ref[pl.ds(o, c), ln]) for o in offs]
        kk = [x * kk_w for x in k]
        ss = _split_dot_many([x * x for x in kk], same_head, pieces=2)
        kk = [x / jnp.maximum(jnp.sqrt(q), 1e-12) for x, q in zip(kk, ss)]
        kmod = [x * (1.0 + (y - 1.0) * ka_w) for x, y in zip(k, a)]
        cum = _split_dot_many(logw, tril_c, rhs=True, pieces=2)
        p_incl = [jnp.exp(x) for x in cum]
        p_inv = [jnp.exp(-x) for x in cum]
        p_end = [jnp.exp(x[c - 1:c, :] - x) for x in cum]
        kka = [x * y for x, y in zip(kk, a)]
        a_s = [stack(-x * jnp.exp(y - z)) for x, y, z in zip(kk, cum, logw)]
        r_s = [stack(x * y) for x, y in zip(r, p_incl)]
        v_s = [stack(x).astype(BF16) for x in v]
        lhs = [jnp.concatenate([x, y], axis=0).astype(BF16) for x, y in zip(a_s, r_s)]
        rhs = [jnp.concatenate([stack(x * z), stack(y * z)], axis=0).astype(BF16) for x, y, z in zip(kka, kmod, p_inv)]
        big = [lax.dot_general(x, y, (((1,), (1,)), ((), ())), preferred_element_type=F32) for x, y in zip(lhs, rhs)]
        lpow = [(x[:c2, :c2] * strict).astype(BF16) for x in big]
        a_rb = [x[c2:, :c2] * incl for x in big]
        avk = [_bdot(x[:, c2:] * causal2, w) for x, w in zip(big, v_s)]
        x = [jnp.concatenate([p, q[:c2]], axis=1) for p, q in zip(a_s, avk)]
        x = [p + _bdot(q, p) for p, q in zip(x, lpow)]
        for _ in range(5):
            lpow = [jnp.dot(q, q, preferred_element_type=F32).astype(BF16) for q in lpow]
            x = [p + _bdot(q, p) for p, q in zip(x, lpow)]
        xb = [p.astype(BF16) for p in x]
        bh_t = [stack(p * q).T for p, q in zip(kka, p_end)]
        kh_t = [stack(p * q).T for p, q in zip(kmod, p_end)]
        both = [_bdot(jnp.concatenate([u, p], axis=0), z) for u, p, z in zip(a_rb, bh_t, xb)]
        qy = [jnp.concatenate([p, q[c2:]], axis=1) + z[:c2] for p, q, z in zip(r_s, avk, both)]
        gh = [z[c2:] + jnp.concatenate([eye_l * q[c - 1:c, :], _bdot(u, w)], axis=1)
              for z, q, u, w in zip(both, p_incl, kh_t, v_s)]
        for ch, u, q in zip(chs, qy, gh):
            q_scr[p, ch] = u[:c, :nl] + u[c:, :nl]
            y0_scr[p, ch] = u[:c, nl:] + u[c:, nl:]
            g_scr[p, ch] = q[:, :nl]
            h_scr[p, ch] = q[:, nl:]

    groups(prep)

    def advance(ch, states):
        both = _dot3_many([jnp.concatenate([q_scr[p, ch], g_scr[p, ch]], axis=0) for p in range(pairs)], states)
        return (tuple(both[p][c:] + h_scr[p, ch] for p in range(pairs)),
                tuple(both[p][:c] + y0_scr[p, ch] for p in range(pairs)))

    def finish(ch, ys):
        o = offset(ch)
        rk = []
        for p in range(pairs):
            ln = lanes(p)
            kmod = k_ref[pl.ds(o, c), ln] * (1.0 + (a_ref[pl.ds(o, c), ln] - 1.0) * ka_ref[:, ln])
            rk.append(r_ref[pl.ds(o, c), ln] * kmod * rk_ref[:, ln])
        sums = _split_dot_many([jnp.concatenate([u, u * u, q], axis=0) for u, q in zip(ys, rk)], same_head)
        for p, (u, s) in enumerate(zip(ys, sums)):
            ln = lanes(p)
            mean = s[:c] * (1.0 / RW_HEAD)
            var = s[c:2 * c] * (1.0 / RW_HEAD) - mean * mean
            o_ref[pl.ds(o, c), ln] = ((u - mean) * lax.rsqrt(var + LNX_EPS) * lw_ref[:, ln] + lb_ref[:, ln]
                                      + s[2 * c:] * v_ref[pl.ds(o, c), ln])

    def step(ch, carry):
        states, ys = carry
        finish(ch - 1, ys)
        return advance(ch, states)

    carry = advance(0, tuple(jnp.zeros((nl, nl), F32) for _ in range(pairs)))
    _, ys = lax.fori_loop(1, n_chunks, step, carry)
    finish(n_chunks - 1, ys)


def _rw_scan(r, w, k, v, a, k_k, k_a, r_k, lnx_w, lnx_b, bsz, lp):
    tp, d = r.shape
    nl = 2 * RW_HEAD
    pairs = RW_PAIRS
    bw = pairs * nl
    nhp = d // bw
    nc = lp // RW_CHUNK
    seq = pl.BlockSpec((lp, bw), lambda b, hp: (b, hp))
    vec = pl.BlockSpec((1, bw), lambda b, hp: (0, hp))
    c = RW_CHUNK
    return pl.pallas_call(
        functools.partial(_rw_scan_kernel, n_chunks=nc, unroll=RW_UNROLL, pairs=pairs),
        grid=(bsz, nhp),
        in_specs=[seq] * 5 + [vec] * 5,
        out_specs=seq,
        out_shape=jax.ShapeDtypeStruct((tp, d), F32),
        scratch_shapes=[pltpu.VMEM((pairs, nc, nl, nl), F32), pltpu.VMEM((pairs, nc, nl, nl), F32),
                        pltpu.VMEM((pairs, nc, c, nl), F32), pltpu.VMEM((pairs, nc, c, nl), F32)],
        compiler_params=_cparams(("parallel", "parallel")),
        name="rwkv_scan",
    )(r, w, k, v, a, k_k.reshape(1, d), k_a.reshape(1, d), r_k.reshape(1, d),
      lnx_w.reshape(1, d), lnx_b.reshape(1, d))


def _rw_out_kernel(y_ref, g_ref, h_ref, wo_ref, gf_ref, wr_ref, br_ref, o_ref, xn_ref, rt_ref):
    h = h_ref[...] + _bdot(y_ref[...] * g_ref[...], wo_ref[...])
    o_ref[...] = h
    xn, rt_ref[...] = _route_tile(h, gf_ref[...], wr_ref[0], wr_ref[1], br_ref[...])
    xn_ref[...] = xn.astype(xn_ref.dtype)


def _rw_out(y, g, h, w_o, g_ffn, router):
    tp, d = h.shape
    wr, br = router
    tm = _pick_tile(tp, 512)
    row = lambda n: pl.BlockSpec((tm, n), lambda i: (i, 0))
    return pl.pallas_call(
        _rw_out_kernel,
        grid=(tp // tm,),
        in_specs=[row(d), row(d), row(d), _const_spec((d, d)), _const_spec((1, d)), _const_spec(wr.shape),
                  _const_spec(br.shape)],
        out_specs=[row(d), row(d), row(ROUTE_LANES)],
        out_shape=[jax.ShapeDtypeStruct((tp, d), F32), jax.ShapeDtypeStruct((tp, d), F32),
                   jax.ShapeDtypeStruct((tp, ROUTE_LANES), F32)],
        compiler_params=_cparams(("parallel",)),
        name="rwkv_out",
    )(y, g, h, w_o.astype(BF16), g_ffn.reshape(1, d), wr, br)


def _final_norm_kernel(h_ref, g_ref, o_ref, *, first, rows):
    o_ref[0] = _rms(h_ref[first:first + rows, :], g_ref[...]).astype(o_ref.dtype)


def _final_norm(h, g, dtype, bsz, lp, first, rows):
    d = h.shape[1]
    return pl.pallas_call(
        functools.partial(_final_norm_kernel, first=first, rows=rows),
        grid=(bsz,),
        in_specs=[pl.BlockSpec((lp, d), lambda b: (b, 0)), _const_spec((1, d))],
        out_specs=pl.BlockSpec((1, rows, d), lambda b: (b, 0, 0)),
        out_shape=jax.ShapeDtypeStruct((bsz, rows, d), dtype),
        compiler_params=_cparams(("parallel",)),
        name="final_norm",
    )(h, g.reshape(1, d))


def kernel(x, meta_tokens, norm_mix, norm_ffn, norm_final, ab_w_in, ab_w_out, ab_norm_a, ab_norm_b, s5_lam_re, s5_lam_im, s5_log_dt, s5_b_re, s5_b_im, s5_c_re, s5_c_im, s5_d, s5_w_glu, s5_b_glu, lru_conv_w, lru_conv_b, lru_w_a, lru_b_a, lru_w_x, lru_b_x, lru_lam, rw_mu, rw_w_r, rw_w_k, rw_w_v, rw_w_o, rw_w0, rw_w_l1, rw_w_l2, rw_a0, rw_a_l1, rw_a_l2, rw_v0, rw_v_l1, rw_v_l2, rw_g_l1, rw_g_l2, rw_k_k, rw_k_a, rw_r_k, rw_lnx_w, rw_lnx_b, moe_router_g, moe_router_g_b, moe_router_e, moe_router_e_b, moe_w_gate, moe_w_up, moe_w_down):
    bsz, seq, d = x.shape
    n_meta = meta_tokens.shape[0]
    depth = norm_mix.shape[0]
    ltot = n_meta + seq
    lp = -(-ltot // SEQ_ALIGN) * SEQ_ALIGN
    s5w = s5_w_glu.shape[-1]
    lruw = lru_lam.shape[-1]
    meta = jnp.broadcast_to(meta_tokens.astype(F32)[None], (bsz, n_meta, d))
    h = jnp.concatenate([meta, x.astype(F32), jnp.zeros((bsz, lp - ltot, d), F32)], axis=1).reshape(bsz * lp, d)
    v_first = None
    flat = lambda a: a.reshape((-1,) + a.shape[2:])
    s5_tables = _s5_tables(flat(s5_lam_re), flat(s5_lam_im), flat(s5_log_dt), flat(s5_b_re), flat(s5_b_im),
                           flat(s5_c_re), flat(s5_c_im), flat(s5_d))
    for layer in range(depth):
        j = layer // 2
        router = _router_table(moe_router_g[layer], moe_router_g_b[layer], moe_router_e[layer], moe_router_e_b[layer])
        if layer % 2 == 0:
            u, gate, rec = _ab_in(h, norm_mix[layer], ab_w_in[j], s5w, lruw)
            y5 = _s5_scan(u, s5_tables, bsz, lp, j * (s5w // S5_LANES))
            lru = _lru(rec, gate, lru_conv_w[j], lru_conv_b[j], lru_w_a[j], lru_b_a[j], lru_w_x[j], lru_b_x[j],
                       lru_lam[j], bsz, lp)
            h, xn, rt = _ab_out(y5, lru, h, s5_w_glu[j], s5_b_glu[j], ab_norm_a[j], ab_norm_b[j], ab_w_out[j],
                                norm_ffn[layer], router)
        else:
            v_res = (rw_v0[j - 1], rw_v_l1[j - 1], rw_v_l2[j - 1]) if j > 0 else None
            r, k, v, w, a, g = _rw_pre(h, norm_mix[layer], rw_mu[j], rw_w_r[j], rw_w_k[j], rw_w_v[j], rw_w0[j],
                                       rw_w_l1[j], rw_w_l2[j], rw_a0[j], rw_a_l1[j], rw_a_l2[j], rw_g_l1[j],
                                       rw_g_l2[j], lp, v_first, v_res)
            if v_first is None:
                v_first = v
            y = _rw_scan(r, w, k, v, a, rw_k_k[j], rw_k_a[j], rw_r_k[j].reshape(-1), rw_lnx_w[j], rw_lnx_b[j],
                         bsz, lp)
            h, xn, rt = _rw_out(y, g, h, rw_w_o[j], norm_ffn[layer], router)
        h = _moe(h, xn, rt, moe_w_gate, moe_w_up, moe_w_down, layer)
    return _final_norm(h, norm_final, x.dtype, bsz, lp, n_meta, seq)
```

```python
import functools
import math

import jax
import jax.numpy as jnp
from jax import lax
from jax.experimental import pallas as pl
from jax.experimental.pallas import tpu as pltpu

F32 = jnp.float32
BF16 = jnp.bfloat16
HI = lax.Precision.HIGHEST

RMS_EPS = 1e-6
LNX_EPS = 64e-5
SEQ_ALIGN = 64
S5_CHUNK = 16
S5_LANES = 128
RW_CHUNK = 64
RW_HEAD = 64
RW_PAIRS = 2
RW_UNROLL = 11
LRU_C = 8.0
N_GROUPS = 4
EXPERTS_PER_GROUP = 4
N_EXPERTS = N_GROUPS * EXPERTS_PER_GROUP
MOE_TILE = 512
ROUTE_LANES = 128
MOE_ROW_DTYPE = BF16
VMEM_LIMIT = 56 * 1024 * 1024


def _cparams(sem):
    return pltpu.CompilerParams(dimension_semantics=sem, vmem_limit_bytes=VMEM_LIMIT)


def _pick_tile(n, target):
    best = 8
    for t in range(8, min(n, target) + 1, 8):
        if n % t == 0:
            best = t
    return best


def _const_spec(shape):
    nd = len(shape)
    return pl.BlockSpec(shape, lambda *_: (0,) * nd)


def _rms(x, g):
    return x * lax.rsqrt(jnp.mean(x * x, axis=-1, keepdims=True) + RMS_EPS) * g


def _gelu(x):
    return 0.5 * x * (1.0 + jnp.tanh(math.sqrt(2.0 / math.pi) * (x + 0.044715 * (x * x * x))))


def _sigmoid(x):
    return 1.0 / (1.0 + jnp.exp(-x))


def _softplus(x):
    return jnp.maximum(x, 0.0) + jnp.log(1.0 + jnp.exp(-jnp.abs(x)))


def _bdot(a, b):
    return jnp.dot(a.astype(BF16), b.astype(BF16), preferred_element_type=F32)


def _dot3_many(a_list, b_list):
    m = a_list[0].shape[0]
    dot = functools.partial(jnp.dot, preferred_element_type=F32)
    a_hi = [a.astype(BF16) for a in a_list]
    b_hi = [b.astype(BF16) for b in b_list]
    a_lo = [(a - h.astype(F32)).astype(BF16) for a, h in zip(a_list, a_hi)]
    b_lo = [(b - h.astype(F32)).astype(BF16) for b, h in zip(b_list, b_hi)]
    top = [dot(jnp.concatenate([h, l], axis=0), b) for h, l, b in zip(a_hi, a_lo, b_hi)]
    low = [dot(h, b) for h, b in zip(a_hi, b_lo)]
    return [t[:m] + (t[m:] + l) for t, l in zip(top, low)]


def _split_dot_many(xs, m, rhs=False, pieces=3):
    accs = [None] * len(xs)
    xs = list(xs)
    for _ in range(pieces):
        his = [x.astype(BF16) for x in xs]
        parts = [jnp.dot(m, hi, preferred_element_type=F32) if rhs else jnp.dot(hi, m, preferred_element_type=F32)
                 for hi in his]
        accs = [p if a is None else a + p for a, p in zip(accs, parts)]
        xs = [x - hi.astype(F32) for x, hi in zip(xs, his)]
    return accs


def _ab_in_kernel(h_ref, g_ref, w_ref, u_ref, gate_ref, rec_ref, *, s5w, lruw):
    xn = _rms(h_ref[...], g_ref[...])
    z = _bdot(xn, w_ref[...])
    u_ref[...] = z[:, :s5w].astype(u_ref.dtype)
    gate_ref[...] = z[:, s5w:s5w + lruw]
    rec_ref[...] = z[:, s5w + lruw:]


def _ab_in(h, g, w_in, s5w, lruw):
    tp, d = h.shape
    tm = _pick_tile(tp, 512)
    row = lambda n: pl.BlockSpec((tm, n), lambda i: (i, 0))
    return pl.pallas_call(
        functools.partial(_ab_in_kernel, s5w=s5w, lruw=lruw),
        grid=(tp // tm,),
        in_specs=[row(d), _const_spec((1, d)), _const_spec(w_in.shape)],
        out_specs=[row(s5w), row(lruw), row(lruw)],
        out_shape=[jax.ShapeDtypeStruct((tp, s5w), BF16),
                   jax.ShapeDtypeStruct((tp, lruw), F32),
                   jax.ShapeDtypeStruct((tp, lruw), F32)],
        compiler_params=_cparams(("parallel",)),
        name="ab_in",
    )(h, g.reshape(1, d), w_in.astype(BF16))


def _s5_tables(lam_re, lam_im, log_dt, b_re, b_im, c_re, c_im, d_skip):
    g, p = lam_re.shape
    hh = b_re.shape[-1]
    c = S5_CHUNK
    lr, li = lam_re.astype(F32), lam_im.astype(F32)
    dt = jnp.exp(log_dt.astype(F32))[:, None]
    mag = jnp.exp(lr * dt)
    abar_r = mag * jnp.cos(li * dt)
    abar_i = mag * jnp.sin(li * dt)
    den = lr * lr + li * li
    zr = ((abar_r - 1.0) * lr + abar_i * li) / den
    zi = (abar_i * lr - (abar_r - 1.0) * li) / den
    bbar_r = zr[..., None] * b_re - zi[..., None] * b_im
    bbar_i = zr[..., None] * b_im + zi[..., None] * b_re
    bbr_t, bbi_t = jnp.swapaxes(bbar_r, 1, 2), jnp.swapaxes(bbar_i, 1, 2)
    cr_t, ci_t = jnp.swapaxes(c_re, 1, 2), jnp.swapaxes(c_im, 1, 2)

    def powers(steps):
        st = steps.astype(F32)[None, :, None]
        pmag = jnp.exp(st * (lr * dt)[:, None, :])
        ang = st * (li * dt)[:, None, :]
        return pmag * jnp.cos(ang), pmag * jnp.sin(ang)

    down = (c - 1) - jnp.arange(c)
    rev_r, rev_i = powers(down)
    m1_r = rev_r[:, :, None, :] * bbr_t[:, None] - rev_i[:, :, None, :] * bbi_t[:, None]
    m1_i = rev_r[:, :, None, :] * bbi_t[:, None] + rev_i[:, :, None, :] * bbr_t[:, None]
    car = c_re[:, None] * rev_r[:, :, None, :] - c_im[:, None] * rev_i[:, :, None, :]
    cai = c_re[:, None] * rev_i[:, :, None, :] + c_im[:, None] * rev_r[:, :, None, :]
    kern = (jnp.einsum('gqhp,gpj->gqjh', car, bbar_r, precision=HI)
            - jnp.einsum('gqhp,gpj->gqjh', cai, bbar_i, precision=HI))
    is_tau0 = (down == 0).astype(F32)[None, :, None, None]
    kern = kern + is_tau0 * (d_skip[:, None, None, :] * jnp.eye(hh, dtype=F32)[None, None])
    up_r, up_i = powers(jnp.arange(1, c + 1))
    up_r, up_i = jnp.swapaxes(up_r, 1, 2)[..., None], jnp.swapaxes(up_i, 1, 2)[..., None]
    m2_r = cr_t[:, :, None, :] * up_r - ci_t[:, :, None, :] * up_i
    m2_i = -(cr_t[:, :, None, :] * up_i + ci_t[:, :, None, :] * up_r)
    adv_r, adv_i = powers(jnp.full((1,), c))
    gb = S5_LANES // hh
    nj = g // gb

    def rows_sgh(x):
        w = x.shape[-1]
        return jnp.transpose(x.reshape(nj, gb, c, hh, w), (0, 2, 1, 3, 4)).reshape(nj, c * gb * hh, w)

    def place(base, spread, row_group, col_group):
        out = jnp.einsum('jrw,wc->jrc', base.astype(BF16), spread.astype(BF16), preferred_element_type=BF16)
        rg = row_group(lax.broadcasted_iota(jnp.int32, out.shape[1:], 0))
        cg = col_group(lax.broadcasted_iota(jnp.int32, out.shape[1:], 1))
        return jnp.where(rg == cg, out, jnp.zeros((), BF16))

    grp_sgh = lambda r: (r // hh) % gb
    rep = lambda w: jnp.tile(jnp.eye(w, dtype=F32), (1, gb))
    m1 = jnp.concatenate([place(rows_sgh(m), rep(p), grp_sgh, lambda col: col // p) for m in (m1_r, m1_i)], axis=-1)
    krev = place(rows_sgh(kern), rep(hh), grp_sgh, lambda col: col // hh)
    ri = lax.broadcasted_iota(jnp.int32, (c * hh, c * gb * hh), 0)
    ci = lax.broadcasted_iota(jnp.int32, (c * hh, c * gb * hh), 1)
    spread_th = ((ri // hh == ci // (gb * hh)) & (ri % hh == ci % hh)).astype(F32)
    m2 = jnp.concatenate([place(m.reshape(nj, gb * p, c * hh), spread_th, lambda r: r // p,
                                lambda col: (col // hh) % gb) for m in (m2_r, m2_i)], axis=1)
    return (m1.astype(BF16), krev.astype(BF16), m2.astype(BF16),
            adv_r.reshape(nj, 1, gb * p), adv_i.reshape(nj, 1, gb * p))


def _s5_kernel(u_ref, m1_ref, kr_ref, m2_ref, ar_ref, ai_ref, y_ref, xe_ref, xin_ref, st_ref, *, n_chunks, nb):
    nl = S5_LANES
    csz = S5_CHUNK
    sw = ar_ref.shape[-1]

    @pl.when(pl.program_id(1) == 0)
    def _():
        st_ref[...] = jnp.zeros(st_ref.shape, F32)

    u = u_ref[0]
    xe_ref[...] = jnp.dot(u, m1_ref[0], preferred_element_type=F32)
    ar = jnp.broadcast_to(ar_ref[0], (nb, sw))
    ai = jnp.broadcast_to(ai_ref[0], (nb, sw))

    def body(c, carry):
        sr, si = carry
        off = pl.multiple_of(c * nb, nb)
        xin_ref[pl.ds(off, nb), :] = jnp.concatenate([sr, si], axis=1)
        e = xe_ref[pl.ds(off, nb), :]
        return (ar * sr - ai * si + e[:, :sw], ar * si + ai * sr + e[:, sw:])

    sr, si = lax.fori_loop(0, n_chunks, body, (st_ref[:, :sw], st_ref[:, sw:]))
    st_ref[:, :sw] = sr
    st_ref[:, sw:] = si
    y_ref[0] = _bdot(xin_ref[...], m2_ref[0])
    for t in range(csz):
        y_ref[0, :, t * nl:(t + 1) * nl] += jnp.dot(u[:, :(t + 1) * nl], kr_ref[0, (csz - 1 - t) * nl:, :],
                                                    preferred_element_type=F32)


def _s5_scan(u, tables, bsz, lp, blk0=0):
    m1, krev, m2, adv_r, adv_i = tables
    _, kin, sw2 = m1.shape
    c = S5_CHUNK
    nl = S5_LANES
    nj = u.shape[1] // nl
    nc = lp // c
    cpt = max(d for d in range(1, nc + 1) if nc % d == 0 and d * bsz <= 512)
    rows = cpt * bsz
    ug = jnp.transpose(u.reshape(bsz, nc, c, nj, nl), (3, 1, 0, 2, 4)).reshape(nj, nc * bsz, kin)
    y = pl.pallas_call(
        functools.partial(_s5_kernel, n_chunks=cpt, nb=bsz),
        grid=(nj, nc // cpt),
        in_specs=[pl.BlockSpec((1, rows, kin), lambda j, r: (j, r, 0)),
                  pl.BlockSpec((1, kin, sw2), lambda j, r: (j + blk0, 0, 0)),
                  pl.BlockSpec((1, kin, nl), lambda j, r: (j + blk0, 0, 0)),
                  pl.BlockSpec((1, sw2, kin), lambda j, r: (j + blk0, 0, 0)),
                  pl.BlockSpec((1, 1, sw2 // 2), lambda j, r: (j + blk0, 0, 0)),
                  pl.BlockSpec((1, 1, sw2 // 2), lambda j, r: (j + blk0, 0, 0))],
        out_specs=pl.BlockSpec((1, rows, kin), lambda j, r: (j, r, 0)),
        out_shape=jax.ShapeDtypeStruct((nj, nc * bsz, kin), F32),
        scratch_shapes=[pltpu.VMEM((rows, sw2), F32), pltpu.VMEM((rows, sw2), F32), pltpu.VMEM((bsz, sw2), F32)],
        compiler_params=_cparams(("parallel", "arbitrary")),
        name="s5_scan",
    )(ug, m1, krev, m2, adv_r, adv_i)
    y = jnp.transpose(y.reshape(nj, nc, bsz, c, nl), (2, 1, 3, 0, 4))
    return y.reshape(bsz * lp, nj * nl)


def _lru_kernel(rec_ref, gate_ref, cw_ref, cb_ref, wa_ref, ba_ref, wx_ref, bx_ref, lam_ref,
                o_ref, ext_ref, a_ref, b_ref, h_ref, *, tl):
    t = pl.program_id(1)

    @pl.when(t == 0)
    def _():
        ext_ref[0:8, :] = jnp.zeros((8, ext_ref.shape[1]), F32)
        h_ref[...] = jnp.zeros(h_ref.shape, F32)

    x = rec_ref[...]
    ext_ref[8:, :] = x
    xc = cb_ref[...] + cw_ref[3:4, :] * x
    for k in range(3):
        xc = xc + cw_ref[k:k + 1, :] * ext_ref[5 + k:5 + k + tl, :]
    ext_ref[0:8, :] = x[tl - 8:, :]
    r = _sigmoid(_bdot(xc, wa_ref[...]) + ba_ref[...])
    i = _sigmoid(_bdot(xc, wx_ref[...]) + bx_ref[...])
    log_a = (-LRU_C) * r * _softplus(-lam_ref[...])
    a = jnp.exp(log_a)
    a_ref[...] = a
    b_ref[...] = jnp.sqrt(1.0 - a * a) * (i * xc)

    sub = 8
    unroll = max(u for u in (4, 2, 1) if (tl // sub) % u == 0)
    row = lax.broadcasted_iota(jnp.int32, (sub, a.shape[1]), 0)

    def body(j, h):
        offs = [pl.multiple_of((j * unroll + q) * sub, sub) for q in range(unroll)]
        ab = [a_ref[pl.ds(o, sub), :] for o in offs]
        bb = [b_ref[pl.ds(o, sub), :] for o in offs]
        for dist in (1, 2, 4):
            keep = row >= dist
            bb = [jnp.where(keep, x * pltpu.roll(y, dist, axis=0) + y, y) for x, y in zip(ab, bb)]
            ab = [jnp.where(keep, x * pltpu.roll(x, dist, axis=0), x) for x in ab]
        for o, x, y in zip(offs, ab, bb):
            hb = x * h + y
            o_ref[pl.ds(o, sub), :] = hb * _gelu(gate_ref[pl.ds(o, sub), :])
            h = hb[sub - 1:sub, :]
        return h

    h_ref[...] = lax.fori_loop(0, tl // (sub * unroll), body, h_ref[...])


def _lru(rec, gate, conv_w, conv_b, w_a, b_a, w_x, b_x, lam, bsz, lp):
    tp, w = rec.shape
    tl = _pick_tile(lp, 1056)
    nt = lp // tl
    heads, hd, _ = w_a.shape

    def dense(wb):
        eye = jnp.eye(heads, dtype=F32)
        return jnp.einsum('hij,hg->higj', wb, eye).reshape(w, w).astype(BF16)

    row = pl.BlockSpec((tl, w), lambda b, t: (b * nt + t, 0))
    vec = _const_spec((1, w))
    return pl.pallas_call(
        functools.partial(_lru_kernel, tl=tl),
        grid=(bsz, nt),
        in_specs=[row, row, _const_spec((4, w)), vec, _const_spec((w, w)), vec, _const_spec((w, w)), vec, vec],
        out_specs=row,
        out_shape=jax.ShapeDtypeStruct((tp, w), F32),
        scratch_shapes=[pltpu.VMEM((tl + 8, w), F32), pltpu.VMEM((tl, w), F32),
                        pltpu.VMEM((tl, w), F32), pltpu.VMEM((1, w), F32)],
        compiler_params=_cparams(("parallel", "arbitrary")),
        name="rglru",
    )(rec, gate, conv_w, conv_b.reshape(1, w), dense(w_a), b_a.reshape(1, w), dense(w_x), b_x.reshape(1, w),
      lam.reshape(1, w))


def _ab_out_kernel(y5_ref, lru_ref, h_ref, wglu_ref, bglu_ref, na_ref, nb_ref, wo_ref, gf_ref, wr_ref, br_ref,
                   o_ref, xn_ref, rt_ref, *, s5w):
    y = _gelu(y5_ref[...])
    ya = y * _sigmoid(_bdot(y, wglu_ref[...]) + bglu_ref[...])
    ya = _rms(ya, na_ref[...])
    yb = _rms(lru_ref[...], nb_ref[...])
    h = h_ref[...] + _bdot(ya, wo_ref[:s5w, :]) + _bdot(yb, wo_ref[s5w:, :])
    o_ref[...] = h
    xn, rt_ref[...] = _route_tile(h, gf_ref[...], wr_ref[0], wr_ref[1], br_ref[...])
    xn_ref[...] = xn.astype(xn_ref.dtype)


def _ab_out(y5, lru, h, w_glu, b_glu, norm_a, norm_b, w_out, g_ffn, router):
    tp, d = h.shape
    s5w, lruw = y5.shape[1], lru.shape[1]
    wr, br = router
    tm = _pick_tile(tp, 512)
    row = lambda n: pl.BlockSpec((tm, n), lambda i: (i, 0))
    return pl.pallas_call(
        functools.partial(_ab_out_kernel, s5w=s5w),
        grid=(tp // tm,),
        in_specs=[row(s5w), row(lruw), row(d), _const_spec((s5w, s5w)), _const_spec((1, s5w)),
                  _const_spec((1, s5w)), _const_spec((1, lruw)), _const_spec(w_out.shape),
                  _const_spec((1, d)), _const_spec(wr.shape), _const_spec(br.shape)],
        out_specs=[row(d), row(d), row(ROUTE_LANES)],
        out_shape=[jax.ShapeDtypeStruct((tp, d), F32), jax.ShapeDtypeStruct((tp, d), F32),
                   jax.ShapeDtypeStruct((tp, ROUTE_LANES), F32)],
        compiler_params=_cparams(("parallel",)),
        name="ab_out",
    )(y5, lru, h, w_glu.astype(BF16), b_glu.reshape(1, s5w), norm_a.reshape(1, s5w),
      norm_b.reshape(1, lruw), w_out.astype(BF16), g_ffn.reshape(1, d), wr, br)


def _route_tile(h, g, wr_hi, wr_lo, br):
    xn = _rms(h, g)
    m = xn.shape[0]
    x_hi = xn.astype(BF16)
    x_lo = (xn - x_hi.astype(F32)).astype(BF16)
    top = jnp.dot(jnp.concatenate([x_hi, x_lo], axis=0), wr_hi, preferred_element_type=F32)
    lg = top[:m] + (top[m:] + jnp.dot(x_hi, wr_lo, preferred_element_type=F32)) + br
    lane = lax.broadcasted_iota(jnp.int32, lg.shape, 1).astype(F32)
    big = float(lg.shape[1])
    neg = -jnp.inf
    gl = jnp.where(lane < N_GROUPS, lg, neg)
    mg = jnp.max(gl, axis=-1, keepdims=True)
    gidx = jnp.min(jnp.where(gl == mg, lane, big), axis=-1, keepdims=True)
    pg_sel = 1.0 / jnp.sum(jnp.exp(gl - mg), axis=-1, keepdims=True)
    lo = N_GROUPS + EXPERTS_PER_GROUP * gidx
    le = jnp.where(lane >= lo, jnp.where(lane < lo + EXPERTS_PER_GROUP, lg, neg), neg)
    v1 = jnp.max(le, axis=-1, keepdims=True)
    i1 = jnp.min(jnp.where(le == v1, lane, big), axis=-1, keepdims=True)
    le2 = jnp.where(lane == i1, neg, le)
    v2 = jnp.max(le2, axis=-1, keepdims=True)
    i2 = jnp.min(jnp.where(le2 == v2, lane, big), axis=-1, keepdims=True)
    e2 = jnp.exp(v2 - v1)
    w1 = pg_sel / (1.0 + e2)
    w2 = w1 * e2
    rt = jnp.where(lane == 0.0, w1, jnp.where(lane == 1.0, w2, jnp.where(
        lane == 2.0, i1 - N_GROUPS, jnp.where(lane == 3.0, i2 - N_GROUPS, 0.0))))
    return xn, rt


def _router_table(wr_g, br_g, wr_e, br_e):
    d = wr_g.shape[0]
    wr = jnp.zeros((d, ROUTE_LANES), F32).at[:, :N_GROUPS].set(wr_g).at[:, N_GROUPS:N_GROUPS + N_EXPERTS].set(wr_e)
    br = jnp.zeros((1, ROUTE_LANES), F32).at[0, :N_GROUPS].set(br_g).at[0, N_GROUPS:N_GROUPS + N_EXPERTS].set(br_e)
    wr_hi = wr.astype(BF16)
    wr_lo = (wr - wr_hi.astype(F32)).astype(BF16)
    return jnp.stack([wr_hi, wr_lo]), br


def _gmm_kernel(te_ref, tv_ref, x_ref, wg_ref, wu_ref, wd_ref, keep_ref, o_ref, wg_bf, wu_bf, wd_bf, *, first):
    del keep_ref
    step = pl.program_id(0)
    i = step + first

    @pl.when(jnp.logical_or(step == 0, te_ref[i] != te_ref[jnp.maximum(i - 1, 0)]))
    def _():
        wg_bf[...] = wg_ref[0, 0].astype(BF16)
        wu_bf[...] = wu_ref[0, 0].astype(BF16)
        wd_bf[...] = wd_ref[0, 0].astype(BF16)

    @pl.when(tv_ref[i] != 0)
    def _():
        x = x_ref[...].astype(BF16)
        hg = jnp.dot(x, wg_bf[...], preferred_element_type=F32)
        hu = jnp.dot(x, wu_bf[...], preferred_element_type=F32)
        hid = hg * _sigmoid(hg) * hu
        o_ref[...] = _bdot(hid, wd_bf[...]).astype(o_ref.dtype)

    @pl.when(tv_ref[i] == 0)
    def _():
        o_ref[...] = jnp.zeros(o_ref.shape, o_ref.dtype)


def _invert_rows_kernel(dest_ref, init_ref, src_ref, blank_ref, zero_scr, sem, *, unroll):
    rows = zero_scr.shape[0]
    fills = [pltpu.make_async_copy(zero_scr, blank_ref.at[pl.ds(j * rows, rows), :], sem)
             for j in range(blank_ref.shape[0] // rows)]
    zero_scr[...] = jnp.zeros(zero_scr.shape, zero_scr.dtype)
    for cp in fills:
        cp.start()
    pltpu.sync_copy(init_ref, src_ref)
    n_tok = dest_ref.shape[0] // 2

    def put(t):
        src_ref[dest_ref[2 * t]] = t
        src_ref[dest_ref[2 * t + 1]] = t

    def body(i, _):
        for u in range(unroll):
            put(i * unroll + u)
        return 0

    lax.fori_loop(0, n_tok // unroll, body, 0)
    for t in range(n_tok - n_tok % unroll, n_tok):
        put(t)
    for cp in fills:
        cp.wait()


def _invert_rows(dest, init, width, dtype):
    smem = pl.BlockSpec(memory_space=pltpu.SMEM)
    anyspace = pl.BlockSpec(memory_space=pl.ANY)
    return pl.pallas_call(
        functools.partial(_invert_rows_kernel, unroll=8),
        in_specs=[smem, anyspace],
        out_specs=[smem, anyspace],
        out_shape=[jax.ShapeDtypeStruct(init.shape, jnp.int32), jax.ShapeDtypeStruct((init.shape[0], width), dtype)],
        scratch_shapes=[pltpu.VMEM((MOE_TILE, width), dtype), pltpu.SemaphoreType.DMA(())],
        name="moe_invert",
    )(dest, init)


def _moe(h, xn, rt, w_gate, w_up, w_down, layer):
    tp, d = h.shape
    f = w_gate.shape[-1]
    gate = rt[:, 0:2]
    eid = rt[:, 2:4].astype(jnp.int32)
    tmm = MOE_TILE
    na = 2 * tp
    e_flat = eid.reshape(na)
    onehot = (e_flat[:, None] == jnp.arange(N_EXPERTS, dtype=jnp.int32)[None, :]).astype(jnp.int32)
    csum = jnp.cumsum(onehot, axis=0)
    counts = csum[-1]
    padded = ((counts + tmm - 1) // tmm) * tmm
    ends = jnp.cumsum(padded)
    starts = ends - padded
    dest = jnp.sum(onehot * (csum + (starts - 1)[None, :]), axis=1)
    n_half = (-(-na // tmm) + N_EXPERTS + 1) // 2
    n_tiles = 2 * n_half
    nrows = n_tiles * tmm
    tile_start = jnp.arange(n_tiles, dtype=jnp.int32) * tmm
    tile_e = jnp.sum((ends[None, :] <= tile_start[:, None]).astype(jnp.int32), axis=1)
    tile_v = (tile_e < N_EXPERTS).astype(jnp.int32)
    tile_e = jnp.minimum(tile_e, N_EXPERTS - 1)
    src, ys = _invert_rows(dest, jnp.arange(nrows, dtype=jnp.int32) % tp, d, MOE_ROW_DTYPE)
    take = lambda a, i: a.at[i].get(mode="promise_in_bounds")
    for first in (0, n_half):
        xs = take(xn, src[first * tmm:(first + n_half) * tmm])
        wspec = lambda shape: pl.BlockSpec((1, 1) + shape, lambda i, te, tv: (layer, te[i + first], 0, 0))
        ys = pl.pallas_call(
            functools.partial(_gmm_kernel, first=first),
            grid_spec=pltpu.PrefetchScalarGridSpec(
                num_scalar_prefetch=2,
                grid=(n_half,),
                in_specs=[pl.BlockSpec((tmm, d), lambda i, te, tv: (i, 0)), wspec((d, f)), wspec((d, f)),
                          wspec((f, d)), pl.BlockSpec(memory_space=pl.ANY)],
                out_specs=pl.BlockSpec((tmm, d), lambda i, te, tv: (i + first, 0)),
                scratch_shapes=[pltpu.VMEM((d, f), BF16), pltpu.VMEM((d, f), BF16), pltpu.VMEM((f, d), BF16)],
            ),
            out_shape=jax.ShapeDtypeStruct((nrows, d), MOE_ROW_DTYPE),
            input_output_aliases={6: 0},
            compiler_params=_cparams(("arbitrary",)),
            name="moe_gmm",
        )(tile_e, tile_v, xs, w_gate, w_up, w_down, ys)
    d2 = dest.reshape(tp, 2)
    return h + gate[:, 0:1] * take(ys, d2[:, 0]) + gate[:, 1:2] * take(ys, d2[:, 1])


def _rw_pre_kernel(*refs, tm, lp, has_vres):
    if has_vres:
        (h_ref, hp_ref, g_ref, mu_ref, wr_ref, wk_ref, wv_ref, w0_ref, wl1_ref, wl2_ref,
         a0_ref, al1_ref, al2_ref, gl1_ref, gl2_ref, vf_ref, v0_ref, vl1_ref, vl2_ref,
         r_ref, k_ref, v_ref, w_ref, a_ref, gg_ref) = refs
    else:
        (h_ref, hp_ref, g_ref, mu_ref, wr_ref, wk_ref, wv_ref, w0_ref, wl1_ref, wl2_ref,
         a0_ref, al1_ref, al2_ref, gl1_ref, gl2_ref,
         r_ref, k_ref, v_ref, w_ref, a_ref, gg_ref) = refs
    i = pl.program_id(0)
    g = g_ref[...]
    x = _rms(h_ref[...], g)
    xp8 = _rms(hp_ref[...], g)
    row = lax.broadcasted_iota(jnp.int32, x.shape, 0)
    prev = jnp.where(row == 0, jnp.broadcast_to(xp8[7:8, :], x.shape), pltpu.roll(x, 1, axis=0))
    first = lax.rem(lp - lax.rem(i * tm, lp), lp)
    prev = jnp.where(row == first, 0.0, prev)
    xx = prev - x
    xr, xw, xk, xv, xa, xg = [x + xx * mu_ref[j:j + 1, :] for j in range(6)]
    r_ref[...] = _bdot(xr, wr_ref[...]).astype(r_ref.dtype)
    k_ref[...] = _bdot(xk, wk_ref[...]).astype(k_ref.dtype)
    v = _bdot(xv, wv_ref[...])
    if has_vres:
        mix = _sigmoid(v0_ref[...] + _bdot(_bdot(xv, vl1_ref[...]), vl2_ref[...]))
        v = v + (vf_ref[...] - v) * mix
    v_ref[...] = v.astype(v_ref.dtype)
    w_ref[...] = -_softplus(-(w0_ref[...] + _bdot(jnp.tanh(_bdot(xw, wl1_ref[...])), wl2_ref[...]))) - 0.5
    a_ref[...] = _sigmoid(a0_ref[...] + _bdot(_bdot(xa, al1_ref[...]), al2_ref[...]))
    gg_ref[...] = _bdot(_sigmoid(_bdot(xg, gl1_ref[...])), gl2_ref[...]).astype(gg_ref.dtype)


def _rw_pre(h, g, mu, w_r, w_k, w_v, w0, w_l1, w_l2, a0, a_l1, a_l2, g_l1, g_l2, lp, v_first, v_res):
    tp, d = h.shape
    tm = _pick_tile(tp, min(512, lp))
    has_vres = v_res is not None
    row = pl.BlockSpec((tm, d), lambda i: (i, 0))
    prev8 = pl.BlockSpec((8, d), lambda i: (jnp.maximum(i * (tm // 8) - 1, 0), 0))
    vec = _const_spec((1, d))
    mu8 = jnp.zeros((8, d), F32).at[:6].set(mu)
    bf = lambda w: w.astype(BF16)
    ins = [h, h, g.reshape(1, d), mu8, bf(w_r), bf(w_k), bf(w_v), w0.reshape(1, d), bf(w_l1), bf(w_l2),
           a0.reshape(1, d), bf(a_l1), bf(a_l2), bf(g_l1), bf(g_l2)]
    specs = [row, prev8, vec, _const_spec((8, d))] + [_const_spec((d, d))] * 3 + [
        vec, _const_spec(w_l1.shape), _const_spec(w_l2.shape),
        vec, _const_spec(a_l1.shape), _const_spec(a_l2.shape), _const_spec(g_l1.shape), _const_spec(g_l2.shape)]
    if has_vres:
        v0, v_l1, v_l2 = v_res
        ins += [v_first, v0.reshape(1, d), bf(v_l1), bf(v_l2)]
        specs += [row, vec, _const_spec(v_l1.shape), _const_spec(v_l2.shape)]
    return pl.pallas_call(
        functools.partial(_rw_pre_kernel, tm=tm, lp=lp, has_vres=has_vres),
        grid=(tp // tm,),
        in_specs=specs,
        out_specs=[row] * 6,
        out_shape=[jax.ShapeDtypeStruct((tp, d), dt) for dt in (BF16, BF16, BF16, F32, F32, BF16)],
        compiler_params=_cparams(("parallel",)),
        name="rwkv_pre",
    )(*ins)


def _rw_scan_kernel(r_ref, w_ref, k_ref, v_ref, a_ref, kk_ref, ka_ref, rk_ref, lw_ref, lb_ref,
                    o_ref, g_scr, h_scr, q_scr, y0_scr, *, n_chunks, unroll, pairs):
    c = RW_CHUNK
    nl = 2 * RW_HEAD
    c2 = 2 * c
    ri = lax.broadcasted_iota(jnp.int32, (c2, nl), 0)
    ci = lax.broadcasted_iota(jnp.int32, (c2, nl), 1)
    own = ((ri >= c) == (ci >= RW_HEAD)).astype(F32)
    t_in = jnp.bitwise_and(ri, c - 1)
    s_in = jnp.bitwise_and(ci, c - 1)
    strict = jnp.where(s_in < t_in, own, 0.0)
    incl = jnp.where(s_in <= t_in, own, 0.0)
    causal2 = jnp.concatenate([strict, incl], axis=0)
    eye_l = (ri == ci).astype(F32)
    same_head = own.astype(BF16)
    ti = lax.broadcasted_iota(jnp.int32, (c, c), 0)
    si = lax.broadcasted_iota(jnp.int32, (c, c), 1)
    tril_c = (si <= ti).astype(BF16)
    lanes = lambda p: slice(p * nl, (p + 1) * nl)

    def stack(x):
        return jnp.concatenate([x, x], axis=0) * own

    def offset(ch):
        return ch * c if isinstance(ch, int) else pl.multiple_of(ch * c, c)

    def groups(fn):
        for p in range(pairs):
            def body(i, _):
                fn([i * unroll + q for q in range(unroll)], p)
                return 0
            lax.fori_loop(0, n_chunks // unroll, body, 0)
            if n_chunks % unroll:
                fn(list(range(n_chunks - n_chunks % unroll, n_chunks)), p)

    def prep(chs, p):
        ln = lanes(p)
        kk_w, ka_w = kk_ref[:, ln], ka_ref[:, ln]
        offs = [offset(ch) for ch in chs]
        r = [r_ref[pl.ds(o, c), ln] for o in offs]
        k = [k_ref[pl.ds(o, c), ln] for o in offs]
        v = [v_ref[pl.ds(o, c), ln] for o in offs]
        a = [a_ref[pl.ds(o, c), ln] for o in offs]
        logw = [-jnp.exp(w_ref[pl.ds(o, c), ln]) for o in offs]
        kk = [x * kk_w for x in k]
        ss = _split_dot_many([x * x for x in kk], same_head, pieces=2)
        kk = [x / jnp.maximum(jnp.sqrt(q), 1e-12) for x, q in zip(kk, ss)]
        kmod = [x * (1.0 + (y - 1.0) * ka_w) for x, y in zip(k, a)]
        cum = _split_dot_many(logw, tril_c, rhs=True, pieces=2)
        p_incl = [jnp.exp(x) for x in cum]
        p_inv = [jnp.exp(-x) for x in cum]
        p_end = [jnp.exp(x[c - 1:c, :] - x) for x in cum]
        kka = [x * y for x, y in zip(kk, a)]
        a_s = [stack(-x * jnp.exp(y - z)) for x, y, z in zip(kk, cum, logw)]
        r_s = [stack(x * y) for x, y in zip(r, p_incl)]
        v_s = [stack(x).astype(BF16) for x in v]
        lhs = [jnp.concatenate([x, y], axis=0).astype(BF16) for x, y in zip(a_s, r_s)]
        rhs = [jnp.concatenate([stack(x * z), stack(y * z)], axis=0).astype(BF16) for x, y, z in zip(kka, kmod, p_inv)]
        big = [lax.dot_general(x, y, (((1,), (1,)), ((), ())), preferred_element_type=F32) for x, y in zip(lhs, rhs)]
        lpow = [(x[:c2, :c2] * strict).astype(BF16) for x in big]
        a_rb = [x[c2:, :c2] * incl for x in big]
        avk = [_bdot(x[:, c2:] * causal2, w) for x, w in zip(big, v_s)]
        x = [jnp.concatenate([p, q[:c2]], axis=1) for p, q in zip(a_s, avk)]
        x = [p + _bdot(q, p) for p, q in zip(x, lpow)]
        for _ in range(5):
            lpow = [jnp.dot(q, q, preferred_element_type=F32).astype(BF16) for q in lpow]
            x = [p + _bdot(q, p) for p, q in zip(x, lpow)]
        xb = [p.astype(BF16) for p in x]
        bh_t = [stack(p * q).T for p, q in zip(kka, p_end)]
        kh_t = [stack(p * q).T for p, q in zip(kmod, p_end)]
        both = [_bdot(jnp.concatenate([u, p], axis=0), z) for u, p, z in zip(a_rb, bh_t, xb)]
        qy = [jnp.concatenate([p, q[c2:]], axis=1) + z[:c2] for p, q, z in zip(r_s, avk, both)]
        gh = [z[c2:] + jnp.concatenate([eye_l * q[c - 1:c, :], _bdot(u, w)], axis=1)
              for z, q, u, w in zip(both, p_incl, kh_t, v_s)]
        for ch, u, q in zip(chs, qy, gh):
            q_scr[p, ch] = u[:c, :nl] + u[c:, :nl]
            y0_scr[p, ch] = u[:c, nl:] + u[c:, nl:]
            g_scr[p, ch] = q[:, :nl]
            h_scr[p, ch] = q[:, nl:]

    groups(prep)

    def advance(ch, states):
        both = _dot3_many([jnp.concatenate([q_scr[p, ch], g_scr[p, ch]], axis=0) for p in range(pairs)], states)
        return (tuple(both[p][c:] + h_scr[p, ch] for p in range(pairs)),
                tuple(both[p][:c] + y0_scr[p, ch] for p in range(pairs)))

    def finish(ch, ys):
        o = offset(ch)
        rk = []
        for p in range(pairs):
            ln = lanes(p)
            kmod = k_ref[pl.ds(o, c), ln] * (1.0 + (a_ref[pl.ds(o, c), ln] - 1.0) * ka_ref[:, ln])
            rk.append(r_ref[pl.ds(o, c), ln] * kmod * rk_ref[:, ln])
        sums = _split_dot_many([jnp.concatenate([u, u * u, q], axis=0) for u, q in zip(ys, rk)], same_head)
        for p, (u, s) in enumerate(zip(ys, sums)):
            ln = lanes(p)
            mean = s[:c] * (1.0 / RW_HEAD)
            var = s[c:2 * c] * (1.0 / RW_HEAD) - mean * mean
            o_ref[pl.ds(o, c), ln] = ((u - mean) * lax.rsqrt(var + LNX_EPS) * lw_ref[:, ln] + lb_ref[:, ln]
                                      + s[2 * c:] * v_ref[pl.ds(o, c), ln])

    def step(ch, carry):
        states, ys = carry
        finish(ch - 1, ys)
        return advance(ch, states)

    carry = advance(0, tuple(jnp.zeros((nl, nl), F32) for _ in range(pairs)))
    _, ys = lax.fori_loop(1, n_chunks, step, carry)
    finish(n_chunks - 1, ys)


def _rw_scan(r, w, k, v, a, k_k, k_a, r_k, lnx_w, lnx_b, bsz, lp):
    tp, d = r.shape
    nl = 2 * RW_HEAD
    pairs = RW_PAIRS
    bw = pairs * nl
    nhp = d // bw
    nc = lp // RW_CHUNK
    seq = pl.BlockSpec((lp, bw), lambda b, hp: (b, hp))
    vec = pl.BlockSpec((1, bw), lambda b, hp: (0, hp))
    c = RW_CHUNK
    return pl.pallas_call(
        functools.partial(_rw_scan_kernel, n_chunks=nc, unroll=RW_UNROLL, pairs=pairs),
        grid=(bsz, nhp),
        in_specs=[seq] * 5 + [vec] * 5,
        out_specs=seq,
        out_shape=jax.ShapeDtypeStruct((tp, d), F32),
        scratch_shapes=[pltpu.VMEM((pairs, nc, nl, nl), F32), pltpu.VMEM((pairs, nc, nl, nl), F32),
                        pltpu.VMEM((pairs, nc, c, nl), F32), pltpu.VMEM((pairs, nc, c, nl), F32)],
        compiler_params=_cparams(("parallel", "parallel")),
        name="rwkv_scan",
    )(r, w, k, v, a, k_k.reshape(1, d), k_a.reshape(1, d), r_k.reshape(1, d),
      lnx_w.reshape(1, d), lnx_b.reshape(1, d))


def _rw_out_kernel(y_ref, g_ref, h_ref, wo_ref, gf_ref, wr_ref, br_ref, o_ref, xn_ref, rt_ref):
    h = h_ref[...] + _bdot(y_ref[...] * g_ref[...], wo_ref[...])
    o_ref[...] = h
    xn, rt_ref[...] = _route_tile(h, gf_ref[...], wr_ref[0], wr_ref[1], br_ref[...])
    xn_ref[...] = xn.astype(xn_ref.dtype)


def _rw_out(y, g, h, w_o, g_ffn, router):
    tp, d = h.shape
    wr, br = router
    tm = _pick_tile(tp, 512)
    row = lambda n: pl.BlockSpec((tm, n), lambda i: (i, 0))
    return pl.pallas_call(
        _rw_out_kernel,
        grid=(tp // tm,),
        in_specs=[row(d), row(d), row(d), _const_spec((d, d)), _const_spec((1, d)), _const_spec(wr.shape),
                  _const_spec(br.shape)],
        out_specs=[row(d), row(d), row(ROUTE_LANES)],
        out_shape=[jax.ShapeDtypeStruct((tp, d), F32), jax.ShapeDtypeStruct((tp, d), F32),
                   jax.ShapeDtypeStruct((tp, ROUTE_LANES), F32)],
        compiler_params=_cparams(("parallel",)),
        name="rwkv_out",
    )(y, g, h, w_o.astype(BF16), g_ffn.reshape(1, d), wr, br)


def _final_norm_kernel(h_ref, g_ref, o_ref, *, first, rows):
    o_ref[0] = _rms(h_ref[first:first + rows, :], g_ref[...]).astype(o_ref.dtype)


def _final_norm(h, g, dtype, bsz, lp, first, rows):
    d = h.shape[1]
    return pl.pallas_call(
        functools.partial(_final_norm_kernel, first=first, rows=rows),
        grid=(bsz,),
        in_specs=[pl.BlockSpec((lp, d), lambda b: (b, 0)), _const_spec((1, d))],
        out_specs=pl.BlockSpec((1, rows, d), lambda b: (b, 0, 0)),
        out_shape=jax.ShapeDtypeStruct((bsz, rows, d), dtype),
        compiler_params=_cparams(("parallel",)),
        name="final_norm",
    )(h, g.reshape(1, d))


def kernel(x, meta_tokens, norm_mix, norm_ffn, norm_final, ab_w_in, ab_w_out, ab_norm_a, ab_norm_b, s5_lam_re, s5_lam_im, s5_log_dt, s5_b_re, s5_b_im, s5_c_re, s5_c_im, s5_d, s5_w_glu, s5_b_glu, lru_conv_w, lru_conv_b, lru_w_a, lru_b_a, lru_w_x, lru_b_x, lru_lam, rw_mu, rw_w_r, rw_w_k, rw_w_v, rw_w_o, rw_w0, rw_w_l1, rw_w_l2, rw_a0, rw_a_l1, rw_a_l2, rw_v0, rw_v_l1, rw_v_l2, rw_g_l1, rw_g_l2, rw_k_k, rw_k_a, rw_r_k, rw_lnx_w, rw_lnx_b, moe_router_g, moe_router_g_b, moe_router_e, moe_router_e_b, moe_w_gate, moe_w_up, moe_w_down):
    bsz, seq, d = x.shape
    n_meta = meta_tokens.shape[0]
    depth = norm_mix.shape[0]
    ltot = n_meta + seq
    lp = -(-ltot // SEQ_ALIGN) * SEQ_ALIGN
    s5w = s5_w_glu.shape[-1]
    lruw = lru_lam.shape[-1]
    meta = jnp.broadcast_to(meta_tokens.astype(F32)[None], (bsz, n_meta, d))
    h = jnp.concatenate([meta, x.astype(F32), jnp.zeros((bsz, lp - ltot, d), F32)], axis=1).reshape(bsz * lp, d)
    v_first = None
    flat = lambda a: a.reshape((-1,) + a.shape[2:])
    s5_tables = _s5_tables(flat(s5_lam_re), flat(s5_lam_im), flat(s5_log_dt), flat(s5_b_re), flat(s5_b_im),
                           flat(s5_c_re), flat(s5_c_im), flat(s5_d))
    for layer in range(depth):
        j = layer // 2
        router = _router_table(moe_router_g[layer], moe_router_g_b[layer], moe_router_e[layer], moe_router_e_b[layer])
        if layer % 2 == 0:
            u, gate, rec = _ab_in(h, norm_mix[layer], ab_w_in[j], s5w, lruw)
            y5 = _s5_scan(u, s5_tables, bsz, lp, j * (s5w // S5_LANES))
            lru = _lru(rec, gate, lru_conv_w[j], lru_conv_b[j], lru_w_a[j], lru_b_a[j], lru_w_x[j], lru_b_x[j],
                       lru_lam[j], bsz, lp)
            h, xn, rt = _ab_out(y5, lru, h, s5_w_glu[j], s5_b_glu[j], ab_norm_a[j], ab_norm_b[j], ab_w_out[j],
                                norm_ffn[layer], router)
        else:
            v_res = (rw_v0[j - 1], rw_v_l1[j - 1], rw_v_l2[j - 1]) if j > 0 else None
            r, k, v, w, a, g = _rw_pre(h, norm_mix[layer], rw_mu[j], rw_w_r[j], rw_w_k[j], rw_w_v[j], rw_w0[j],
                                       rw_w_l1[j], rw_w_l2[j], rw_a0[j], rw_a_l1[j], rw_a_l2[j], rw_g_l1[j],
                                       rw_g_l2[j], lp, v_first, v_res)
            if v_first is None:
                v_first = v
            y = _rw_scan(r, w, k, v, a, rw_k_k[j], rw_k_a[j], rw_r_k[j].reshape(-1), rw_lnx_w[j], rw_lnx_b[j],
                         bsz, lp)
            h, xn, rt = _rw_out(y, g, h, rw_w_o[j], norm_ffn[layer], router)
        h = _moe(h, xn, rt, moe_w_gate, moe_w_up, moe_w_down, layer)
    return _final_norm(h, norm_final, x.dtype, bsz, lp, n_meta, seq)
```

```python
import functools
import math

import jax
import jax.numpy as jnp
from jax import lax
from jax.experimental import pallas as pl
from jax.experimental.pallas import tpu as pltpu

F32 = jnp.float32
BF16 = jnp.bfloat16
HI = lax.Precision.HIGHEST

RMS_EPS = 1e-6
LNX_EPS = 64e-5
SEQ_ALIGN = 64
S5_CHUNK = 16
S5_LANES = 128
RW_CHUNK = 64
RW_HEAD = 64
RW_PAIRS = 2
RW_UNROLL = 11
LRU_C = 8.0
N_GROUPS = 4
EXPERTS_PER_GROUP = 4
N_EXPERTS = N_GROUPS * EXPERTS_PER_GROUP
MOE_TILE = 512
ROUTE_LANES = 128
MOE_ROW_DTYPE = BF16
VMEM_LIMIT = 56 * 1024 * 1024


def _cparams(sem):
    return pltpu.CompilerParams(dimension_semantics=sem, vmem_limit_bytes=VMEM_LIMIT)


def _pick_tile(n, target):
    best = 8
    for t in range(8, min(n, target) + 1, 8):
        if n % t == 0:
            best = t
    return best


def _const_spec(shape):
    nd = len(shape)
    return pl.BlockSpec(shape, lambda *_: (0,) * nd)


def _rms(x, g):
    return x * lax.rsqrt(jnp.mean(x * x, axis=-1, keepdims=True) + RMS_EPS) * g


def _gelu(x):
    return 0.5 * x * (1.0 + jnp.tanh(math.sqrt(2.0 / math.pi) * (x + 0.044715 * (x * x * x))))


def _sigmoid(x):
    return 1.0 / (1.0 + jnp.exp(-x))


def _softplus(x):
    return jnp.maximum(x, 0.0) + jnp.log(1.0 + jnp.exp(-jnp.abs(x)))


def _bdot(a, b):
    return jnp.dot(a.astype(BF16), b.astype(BF16), preferred_element_type=F32)


def _dot3_many(a_list, b_list):
    m = a_list[0].shape[0]
    dot = functools.partial(jnp.dot, preferred_element_type=F32)
    a_hi = [a.astype(BF16) for a in a_list]
    b_hi = [b.astype(BF16) for b in b_list]
    a_lo = [(a - h.astype(F32)).astype(BF16) for a, h in zip(a_list, a_hi)]
    b_lo = [(b - h.astype(F32)).astype(BF16) for b, h in zip(b_list, b_hi)]
    top = [dot(jnp.concatenate([h, l], axis=0), b) for h, l, b in zip(a_hi, a_lo, b_hi)]
    low = [dot(h, b) for h, b in zip(a_hi, b_lo)]
    return [t[:m] + (t[m:] + l) for t, l in zip(top, low)]


def _split_dot_many(xs, m, rhs=False, pieces=3):
    accs = [None] * len(xs)
    xs = list(xs)
    for _ in range(pieces):
        his = [x.astype(BF16) for x in xs]
        parts = [jnp.dot(m, hi, preferred_element_type=F32) if rhs else jnp.dot(hi, m, preferred_element_type=F32)
                 for hi in his]
        accs = [p if a is None else a + p for a, p in zip(accs, parts)]
        xs = [x - hi.astype(F32) for x, hi in zip(xs, his)]
    return accs


def _ab_in_kernel(h_ref, g_ref, w_ref, u_ref, gate_ref, rec_ref, *, s5w, lruw):
    xn = _rms(h_ref[...], g_ref[...])
    z = _bdot(xn, w_ref[...])
    u_ref[...] = z[:, :s5w].astype(u_ref.dtype)
    gate_ref[...] = z[:, s5w:s5w + lruw]
    rec_ref[...] = z[:, s5w + lruw:]


def _ab_in(h, g, w_in, s5w, lruw):
    tp, d = h.shape
    tm = _pick_tile(tp, 512)
    row = lambda n: pl.BlockSpec((tm, n), lambda i: (i, 0))
    return pl.pallas_call(
        functools.partial(_ab_in_kernel, s5w=s5w, lruw=lruw),
        grid=(tp // tm,),
        in_specs=[row(d), _const_spec((1, d)), _const_spec(w_in.shape)],
        out_specs=[row(s5w), row(lruw), row(lruw)],
        out_shape=[jax.ShapeDtypeStruct((tp, s5w), BF16),
                   jax.ShapeDtypeStruct((tp, lruw), F32),
                   jax.ShapeDtypeStruct((tp, lruw), F32)],
        compiler_params=_cparams(("parallel",)),
        name="ab_in",
    )(h, g.reshape(1, d), w_in.astype(BF16))


def _s5_tables(lam_re, lam_im, log_dt, b_re, b_im, c_re, c_im, d_skip):
    g, p = lam_re.shape
    hh = b_re.shape[-1]
    c = S5_CHUNK
    lr, li = lam_re.astype(F32), lam_im.astype(F32)
    dt = jnp.exp(log_dt.astype(F32))[:, None]
    mag = jnp.exp(lr * dt)
    abar_r = mag * jnp.cos(li * dt)
    abar_i = mag * jnp.sin(li * dt)
    den = lr * lr + li * li
    zr = ((abar_r - 1.0) * lr + abar_i * li) / den
    zi = (abar_i * lr - (abar_r - 1.0) * li) / den
    bbar_r = zr[..., None] * b_re - zi[..., None] * b_im
    bbar_i = zr[..., None] * b_im + zi[..., None] * b_re
    bbr_t, bbi_t = jnp.swapaxes(bbar_r, 1, 2), jnp.swapaxes(bbar_i, 1, 2)
    cr_t, ci_t = jnp.swapaxes(c_re, 1, 2), jnp.swapaxes(c_im, 1, 2)

    def powers(steps):
        st = steps.astype(F32)[None, :, None]
        pmag = jnp.exp(st * (lr * dt)[:, None, :])
        ang = st * (li * dt)[:, None, :]
        return pmag * jnp.cos(ang), pmag * jnp.sin(ang)

    down = (c - 1) - jnp.arange(c)
    rev_r, rev_i = powers(down)
    m1_r = rev_r[:, :, None, :] * bbr_t[:, None] - rev_i[:, :, None, :] * bbi_t[:, None]
    m1_i = rev_r[:, :, None, :] * bbi_t[:, None] + rev_i[:, :, None, :] * bbr_t[:, None]
    car = c_re[:, None] * rev_r[:, :, None, :] - c_im[:, None] * rev_i[:, :, None, :]
    cai = c_re[:, None] * rev_i[:, :, None, :] + c_im[:, None] * rev_r[:, :, None, :]
    kern = (jnp.einsum('gqhp,gpj->gqjh', car, bbar_r, precision=HI)
            - jnp.einsum('gqhp,gpj->gqjh', cai, bbar_i, precision=HI))
    is_tau0 = (down == 0).astype(F32)[None, :, None, None]
    kern = kern + is_tau0 * (d_skip[:, None, None, :] * jnp.eye(hh, dtype=F32)[None, None])
    up_r, up_i = powers(jnp.arange(1, c + 1))
    up_r, up_i = jnp.swapaxes(up_r, 1, 2)[..., None], jnp.swapaxes(up_i, 1, 2)[..., None]
    m2_r = cr_t[:, :, None, :] * up_r - ci_t[:, :, None, :] * up_i
    m2_i = -(cr_t[:, :, None, :] * up_i + ci_t[:, :, None, :] * up_r)
    adv_r, adv_i = powers(jnp.full((1,), c))
    gb = S5_LANES // hh
    nj = g // gb

    def rows_sgh(x):
        w = x.shape[-1]
        return jnp.transpose(x.reshape(nj, gb, c, hh, w), (0, 2, 1, 3, 4)).reshape(nj, c * gb * hh, w)

    def place(base, spread, row_group, col_group):
        out = jnp.einsum('jrw,wc->jrc', base.astype(BF16), spread.astype(BF16), preferred_element_type=BF16)
        rg = row_group(lax.broadcasted_iota(jnp.int32, out.shape[1:], 0))
        cg = col_group(lax.broadcasted_iota(jnp.int32, out.shape[1:], 1))
        return jnp.where(rg == cg, out, jnp.zeros((), BF16))

    grp_sgh = lambda r: (r // hh) % gb
    rep = lambda w: jnp.tile(jnp.eye(w, dtype=F32), (1, gb))
    m1 = jnp.concatenate([place(rows_sgh(m), rep(p), grp_sgh, lambda col: col // p) for m in (m1_r, m1_i)], axis=-1)
    krev = place(rows_sgh(kern), rep(hh), grp_sgh, lambda col: col // hh)
    ri = lax.broadcasted_iota(jnp.int32, (c * hh, c * gb * hh), 0)
    ci = lax.broadcasted_iota(jnp.int32, (c * hh, c * gb * hh), 1)
    spread_th = ((ri // hh == ci // (gb * hh)) & (ri % hh == ci % hh)).astype(F32)
    m2 = jnp.concatenate([place(m.reshape(nj, gb * p, c * hh), spread_th, lambda r: r // p,
                                lambda col: (col // hh) % gb) for m in (m2_r, m2_i)], axis=1)
    return (m1.astype(BF16), krev.astype(BF16), m2.astype(BF16),
            adv_r.reshape(nj, 1, gb * p), adv_i.reshape(nj, 1, gb * p))


def _s5_kernel(u_ref, m1_ref, kr_ref, m2_ref, ar_ref, ai_ref, y_ref, xe_ref, xin_ref, st_ref, *, n_chunks, nb):
    nl = S5_LANES
    csz = S5_CHUNK
    sw = ar_ref.shape[-1]

    @pl.when(pl.program_id(1) == 0)
    def _():
        st_ref[...] = jnp.zeros(st_ref.shape, F32)

    u = u_ref[0]
    xe_ref[...] = jnp.dot(u, m1_ref[0], preferred_element_type=F32)
    ar = jnp.broadcast_to(ar_ref[0], (nb, sw))
    ai = jnp.broadcast_to(ai_ref[0], (nb, sw))

    def body(c, carry):
        sr, si = carry
        off = pl.multiple_of(c * nb, nb)
        xin_ref[pl.ds(off, nb), :] = jnp.concatenate([sr, si], axis=1)
        e = xe_ref[pl.ds(off, nb), :]
        return (ar * sr - ai * si + e[:, :sw], ar * si + ai * sr + e[:, sw:])

    sr, si = lax.fori_loop(0, n_chunks, body, (st_ref[:, :sw], st_ref[:, sw:]))
    st_ref[:, :sw] = sr
    st_ref[:, sw:] = si
    y_ref[0] = _bdot(xin_ref[...], m2_ref[0])
    for t in range(csz):
        y_ref[0, :, t * nl:(t + 1) * nl] += jnp.dot(u[:, :(t + 1) * nl], kr_ref[0, (csz - 1 - t) * nl:, :],
                                                    preferred_element_type=F32)


def _s5_scan(u, tables, bsz, lp, blk0=0):
    m1, krev, m2, adv_r, adv_i = tables
    _, kin, sw2 = m1.shape
    c = S5_CHUNK
    nl = S5_LANES
    nj = u.shape[1] // nl
    nc = lp // c
    cpt = max(d for d in range(1, nc + 1) if nc % d == 0 and d * bsz <= 512)
    rows = cpt * bsz
    ug = jnp.transpose(u.reshape(bsz, nc, c, nj, nl), (3, 1, 0, 2, 4)).reshape(nj, nc * bsz, kin)
    y = pl.pallas_call(
        functools.partial(_s5_kernel, n_chunks=cpt, nb=bsz),
        grid=(nj, nc // cpt),
        in_specs=[pl.BlockSpec((1, rows, kin), lambda j, r: (j, r, 0)),
                  pl.BlockSpec((1, kin, sw2), lambda j, r: (j + blk0, 0, 0)),
                  pl.BlockSpec((1, kin, nl), lambda j, r: (j + blk0, 0, 0)),
                  pl.BlockSpec((1, sw2, kin), lambda j, r: (j + blk0, 0, 0)),
                  pl.BlockSpec((1, 1, sw2 // 2), lambda j, r: (j + blk0, 0, 0)),
                  pl.BlockSpec((1, 1, sw2 // 2), lambda j, r: (j + blk0, 0, 0))],
        out_specs=pl.BlockSpec((1, rows, kin), lambda j, r: (j, r, 0)),
        out_shape=jax.ShapeDtypeStruct((nj, nc * bsz, kin), F32),
        scratch_shapes=[pltpu.VMEM((rows, sw2), F32), pltpu.VMEM((rows, sw2), F32), pltpu.VMEM((bsz, sw2), F32)],
        compiler_params=_cparams(("parallel", "arbitrary")),
        name="s5_scan",
    )(ug, m1, krev, m2, adv_r, adv_i)
    y = jnp.transpose(y.reshape(nj, nc, bsz, c, nl), (2, 1, 3, 0, 4))
    return y.reshape(bsz * lp, nj * nl)


def _lru_kernel(rec_ref, gate_ref, cw_ref, cb_ref, wa_ref, ba_ref, wx_ref, bx_ref, lam_ref,
                o_ref, ext_ref, a_ref, b_ref, h_ref, *, tl):
    t = pl.program_id(1)

    @pl.when(t == 0)
    def _():
        ext_ref[0:8, :] = jnp.zeros((8, ext_ref.shape[1]), F32)
        h_ref[...] = jnp.zeros(h_ref.shape, F32)

    x = rec_ref[...]
    ext_ref[8:, :] = x
    xc = cb_ref[...] + cw_ref[3:4, :] * x
    for k in range(3):
        xc = xc + cw_ref[k:k + 1, :] * ext_ref[5 + k:5 + k + tl, :]
    ext_ref[0:8, :] = x[tl - 8:, :]
    r = _sigmoid(_bdot(xc, wa_ref[...]) + ba_ref[...])
    i = _sigmoid(_bdot(xc, wx_ref[...]) + bx_ref[...])
    log_a = (-LRU_C) * r * _softplus(-lam_ref[...])
    a = jnp.exp(log_a)
    a_ref[...] = a
    b_ref[...] = jnp.sqrt(1.0 - a * a) * (i * xc)

    sub = 8
    unroll = max(u for u in (4, 2, 1) if (tl // sub) % u == 0)
    row = lax.broadcasted_iota(jnp.int32, (sub, a.shape[1]), 0)

    def body(j, h):
        offs = [pl.multiple_of((j * unroll + q) * sub, sub) for q in range(unroll)]
        ab = [a_ref[pl.ds(o, sub), :] for o in offs]
        bb = [b_ref[pl.ds(o, sub), :] for o in offs]
        for dist in (1, 2, 4):
            keep = row >= dist
            bb = [jnp.where(keep, x * pltpu.roll(y, dist, axis=0) + y, y) for x, y in zip(ab, bb)]
            ab = [jnp.where(keep, x * pltpu.roll(x, dist, axis=0), x) for x in ab]
        for o, x, y in zip(offs, ab, bb):
            hb = x * h + y
            o_ref[pl.ds(o, sub), :] = hb * _gelu(gate_ref[pl.ds(o, sub), :])
            h = hb[sub - 1:sub, :]
        return h

    h_ref[...] = lax.fori_loop(0, tl // (sub * unroll), body, h_ref[...])


def _lru(rec, gate, conv_w, conv_b, w_a, b_a, w_x, b_x, lam, bsz, lp):
    tp, w = rec.shape
    tl = _pick_tile(lp, 1056)
    nt = lp // tl
    heads, hd, _ = w_a.shape

    def dense(wb):
        eye = jnp.eye(heads, dtype=F32)
        return jnp.einsum('hij,hg->higj', wb, eye).reshape(w, w).astype(BF16)

    row = pl.BlockSpec((tl, w), lambda b, t: (b * nt + t, 0))
    vec = _const_spec((1, w))
    return pl.pallas_call(
        functools.partial(_lru_kernel, tl=tl),
        grid=(bsz, nt),
        in_specs=[row, row, _const_spec((4, w)), vec, _const_spec((w, w)), vec, _const_spec((w, w)), vec, vec],
        out_specs=row,
        out_shape=jax.ShapeDtypeStruct((tp, w), F32),
        scratch_shapes=[pltpu.VMEM((tl + 8, w), F32), pltpu.VMEM((tl, w), F32),
                        pltpu.VMEM((tl, w), F32), pltpu.VMEM((1, w), F32)],
        compiler_params=_cparams(("parallel", "arbitrary")),
        name="rglru",
    )(rec, gate, conv_w, conv_b.reshape(1, w), dense(w_a), b_a.reshape(1, w), dense(w_x), b_x.reshape(1, w),
      lam.reshape(1, w))


def _ab_out_kernel(y5_ref, lru_ref, h_ref, wglu_ref, bglu_ref, na_ref, nb_ref, wo_ref, gf_ref, wr_ref, br_ref,
                   o_ref, xn_ref, rt_ref, *, s5w):
    y = _gelu(y5_ref[...])
    ya = y * _sigmoid(_bdot(y, wglu_ref[...]) + bglu_ref[...])
    ya = _rms(ya, na_ref[...])
    yb = _rms(lru_ref[...], nb_ref[...])
    h = h_ref[...] + _bdot(ya, wo_ref[:s5w, :]) + _bdot(yb, wo_ref[s5w:, :])
    o_ref[...] = h
    xn, rt_ref[...] = _route_tile(h, gf_ref[...], wr_ref[0], wr_ref[1], br_ref[...])
    xn_ref[...] = xn.astype(xn_ref.dtype)


def _ab_out(y5, lru, h, w_glu, b_glu, norm_a, norm_b, w_out, g_ffn, router):
    tp, d = h.shape
    s5w, lruw = y5.shape[1], lru.shape[1]
    wr, br = router
    tm = _pick_tile(tp, 512)
    row = lambda n: pl.BlockSpec((tm, n), lambda i: (i, 0))
    return pl.pallas_call(
        functools.partial(_ab_out_kernel, s5w=s5w),
        grid=(tp // tm,),
        in_specs=[row(s5w), row(lruw), row(d), _const_spec((s5w, s5w)), _const_spec((1, s5w)),
                  _const_spec((1, s5w)), _const_spec((1, lruw)), _const_spec(w_out.shape),
                  _const_spec((1, d)), _const_spec(wr.shape), _const_spec(br.shape)],
        out_specs=[row(d), row(d), row(ROUTE_LANES)],
        out_shape=[jax.ShapeDtypeStruct((tp, d), F32), jax.ShapeDtypeStruct((tp, d), F32),
                   jax.ShapeDtypeStruct((tp, ROUTE_LANES), F32)],
        compiler_params=_cparams(("parallel",)),
        name="ab_out",
    )(y5, lru, h, w_glu.astype(BF16), b_glu.reshape(1, s5w), norm_a.reshape(1, s5w),
      norm_b.reshape(1, lruw), w_out.astype(BF16), g_ffn.reshape(1, d), wr, br)


def _route_tile(h, g, wr_hi, wr_lo, br):
    xn = _rms(h, g)
    m = xn.shape[0]
    x_hi = xn.astype(BF16)
    x_lo = (xn - x_hi.astype(F32)).astype(BF16)
    top = jnp.dot(jnp.concatenate([x_hi, x_lo], axis=0), wr_hi, preferred_element_type=F32)
    lg = top[:m] + (top[m:] + jnp.dot(x_hi, wr_lo, preferred_element_type=F32)) + br
    lane = lax.broadcasted_iota(jnp.int32, lg.shape, 1).astype(F32)
    big = float(lg.shape[1])
    neg = -jnp.inf
    gl = jnp.where(lane < N_GROUPS, lg, neg)
    mg = jnp.max(gl, axis=-1, keepdims=True)
    gidx = jnp.min(jnp.where(gl == mg, lane, big), axis=-1, keepdims=True)
    pg_sel = 1.0 / jnp.sum(jnp.exp(gl - mg), axis=-1, keepdims=True)
    lo = N_GROUPS + EXPERTS_PER_GROUP * gidx
    le = jnp.where(lane >= lo, jnp.where(lane < lo + EXPERTS_PER_GROUP, lg, neg), neg)
    v1 = jnp.max(le, axis=-1, keepdims=True)
    i1 = jnp.min(jnp.where(le == v1, lane, big), axis=-1, keepdims=True)
    le2 = jnp.where(lane == i1, neg, le)
    v2 = jnp.max(le2, axis=-1, keepdims=True)
    i2 = jnp.min(jnp.where(le2 == v2, lane, big), axis=-1, keepdims=True)
    e2 = jnp.exp(v2 - v1)
    w1 = pg_sel / (1.0 + e2)
    w2 = w1 * e2
    rt = jnp.where(lane == 0.0, w1, jnp.where(lane == 1.0, w2, jnp.where(
        lane == 2.0, i1 - N_GROUPS, jnp.where(lane == 3.0, i2 - N_GROUPS, 0.0))))
    return xn, rt


def _router_table(wr_g, br_g, wr_e, br_e):
    d = wr_g.shape[0]
    wr = jnp.zeros((d, ROUTE_LANES), F32).at[:, :N_GROUPS].set(wr_g).at[:, N_GROUPS:N_GROUPS + N_EXPERTS].set(wr_e)
    br = jnp.zeros((1, ROUTE_LANES), F32).at[0, :N_GROUPS].set(br_g).at[0, N_GROUPS:N_GROUPS + N_EXPERTS].set(br_e)
    wr_hi = wr.astype(BF16)
    wr_lo = (wr - wr_hi.astype(F32)).astype(BF16)
    return jnp.stack([wr_hi, wr_lo]), br


def _gmm_kernel(te_ref, tv_ref, x_ref, wg_ref, wu_ref, wd_ref, keep_ref, o_ref, wg_bf, wu_bf, wd_bf, *, first):
    del keep_ref
    step = pl.program_id(0)
    i = step + first

    @pl.when(jnp.logical_or(step == 0, te_ref[i] != te_ref[jnp.maximum(i - 1, 0)]))
    def _():
        wg_bf[...] = wg_ref[0, 0].astype(BF16)
        wu_bf[...] = wu_ref[0, 0].astype(BF16)
        wd_bf[...] = wd_ref[0, 0].astype(BF16)

    @pl.when(tv_ref[i] != 0)
    def _():
        x = x_ref[...].astype(BF16)
        hg = jnp.dot(x, wg_bf[...], preferred_element_type=F32)
        hu = jnp.dot(x, wu_bf[...], preferred_element_type=F32)
        hid = hg * _sigmoid(hg) * hu
        o_ref[...] = _bdot(hid, wd_bf[...]).astype(o_ref.dtype)

    @pl.when(tv_ref[i] == 0)
    def _():
        o_ref[...] = jnp.zeros(o_ref.shape, o_ref.dtype)


def _invert_rows_kernel(dest_ref, init_ref, src_ref, blank_ref, zero_scr, sem, *, unroll):
    rows = zero_scr.shape[0]
    fills = [pltpu.make_async_copy(zero_scr, blank_ref.at[pl.ds(j * rows, rows), :], sem)
             for j in range(blank_ref.shape[0] // rows)]
    zero_scr[...] = jnp.zeros(zero_scr.shape, zero_scr.dtype)
    for cp in fills:
        cp.start()
    pltpu.sync_copy(init_ref, src_ref)
    n_tok = dest_ref.shape[0] // 2

    def put(t):
        src_ref[dest_ref[2 * t]] = t
        src_ref[dest_ref[2 * t + 1]] = t

    def body(i, _):
        for u in range(unroll):
            put(i * unroll + u)
        return 0

    lax.fori_loop(0, n_tok // unroll, body, 0)
    for t in range(n_tok - n_tok % unroll, n_tok):
        put(t)
    for cp in fills:
        cp.wait()


def _invert_rows(dest, init, width, dtype):
    smem = pl.BlockSpec(memory_space=pltpu.SMEM)
    anyspace = pl.BlockSpec(memory_space=pl.ANY)
    return pl.pallas_call(
        functools.partial(_invert_rows_kernel, unroll=8),
        in_specs=[smem, anyspace],
        out_specs=[smem, anyspace],
        out_shape=[jax.ShapeDtypeStruct(init.shape, jnp.int32), jax.ShapeDtypeStruct((init.shape[0], width), dtype)],
        scratch_shapes=[pltpu.VMEM((MOE_TILE, width), dtype), pltpu.SemaphoreType.DMA(())],
        name="moe_invert",
    )(dest, init)


def _moe(h, xn, rt, w_gate, w_up, w_down, layer):
    tp, d = h.shape
    f = w_gate.shape[-1]
    gate = rt[:, 0:2]
    eid = rt[:, 2:4].astype(jnp.int32)
    tmm = MOE_TILE
    na = 2 * tp
    e_flat = eid.reshape(na)
    onehot = (e_flat[:, None] == jnp.arange(N_EXPERTS, dtype=jnp.int32)[None, :]).astype(jnp.int32)
    csum = jnp.cumsum(onehot, axis=0)
    counts = csum[-1]
    padded = ((counts + tmm - 1) // tmm) * tmm
    ends = jnp.cumsum(padded)
    starts = ends - padded
    dest = jnp.sum(onehot * (csum + (starts - 1)[None, :]), axis=1)
    n_half = (-(-na // tmm) + N_EXPERTS + 1) // 2
    n_tiles = 2 * n_half
    nrows = n_tiles * tmm
    tile_start = jnp.arange(n_tiles, dtype=jnp.int32) * tmm
    tile_e = jnp.sum((ends[None, :] <= tile_start[:, None]).astype(jnp.int32), axis=1)
    tile_v = (tile_e < N_EXPERTS).astype(jnp.int32)
    tile_e = jnp.minimum(tile_e, N_EXPERTS - 1)
    src, ys = _invert_rows(dest, jnp.arange(nrows, dtype=jnp.int32) % tp, d, MOE_ROW_DTYPE)
    take = lambda a, i: a.at[i].get(mode="promise_in_bounds")
    n_first = n_tiles // 3
    for first, count in ((0, n_first), (n_first, n_tiles - n_first)):
        xs = take(xn, src[first * tmm:(first + count) * tmm])
        wspec = lambda shape: pl.BlockSpec((1, 1) + shape, lambda i, te, tv: (layer, te[i + first], 0, 0))
        ys = pl.pallas_call(
            functools.partial(_gmm_kernel, first=first),
            grid_spec=pltpu.PrefetchScalarGridSpec(
                num_scalar_prefetch=2,
                grid=(count,),
                in_specs=[pl.BlockSpec((tmm, d), lambda i, te, tv: (i, 0)), wspec((d, f)), wspec((d, f)),
                          wspec((f, d)), pl.BlockSpec(memory_space=pl.ANY)],
                out_specs=pl.BlockSpec((tmm, d), lambda i, te, tv: (i + first, 0)),
                scratch_shapes=[pltpu.VMEM((d, f), BF16), pltpu.VMEM((d, f), BF16), pltpu.VMEM((f, d), BF16)],
            ),
            out_shape=jax.ShapeDtypeStruct((nrows, d), MOE_ROW_DTYPE),
            input_output_aliases={6: 0},
            compiler_params=_cparams(("arbitrary",)),
            name="moe_gmm",
        )(tile_e, tile_v, xs, w_gate, w_up, w_down, ys)
    d2 = dest.reshape(tp, 2)
    return h + gate[:, 0:1] * take(ys, d2[:, 0]) + gate[:, 1:2] * take(ys, d2[:, 1])


def _rw_pre_kernel(*refs, tm, lp, has_vres):
    if has_vres:
        (h_ref, hp_ref, g_ref, mu_ref, wr_ref, wk_ref, wv_ref, w0_ref, wl1_ref, wl2_ref,
         a0_ref, al1_ref, al2_ref, gl1_ref, gl2_ref, vf_ref, v0_ref, vl1_ref, vl2_ref,
         r_ref, k_ref, v_ref, w_ref, a_ref, gg_ref) = refs
    else:
        (h_ref, hp_ref, g_ref, mu_ref, wr_ref, wk_ref, wv_ref, w0_ref, wl1_ref, wl2_ref,
         a0_ref, al1_ref, al2_ref, gl1_ref, gl2_ref,
         r_ref, k_ref, v_ref, w_ref, a_ref, gg_ref) = refs
    i = pl.program_id(0)
    g = g_ref[...]
    x = _rms(h_ref[...], g)
    xp8 = _rms(hp_ref[...], g)
    row = lax.broadcasted_iota(jnp.int32, x.shape, 0)
    prev = jnp.where(row == 0, jnp.broadcast_to(xp8[7:8, :], x.shape), pltpu.roll(x, 1, axis=0))
    first = lax.rem(lp - lax.rem(i * tm, lp), lp)
    prev = jnp.where(row == first, 0.0, prev)
    xx = prev - x
    xr, xw, xk, xv, xa, xg = [x + xx * mu_ref[j:j + 1, :] for j in range(6)]
    r_ref[...] = _bdot(xr, wr_ref[...]).astype(r_ref.dtype)
    k_ref[...] = _bdot(xk, wk_ref[...]).astype(k_ref.dtype)
    v = _bdot(xv, wv_ref[...])
    if has_vres:
        mix = _sigmoid(v0_ref[...] + _bdot(_bdot(xv, vl1_ref[...]), vl2_ref[...]))
        v = v + (vf_ref[...] - v) * mix
    v_ref[...] = v.astype(v_ref.dtype)
    w_ref[...] = -_softplus(-(w0_ref[...] + _bdot(jnp.tanh(_bdot(xw, wl1_ref[...])), wl2_ref[...]))) - 0.5
    a_ref[...] = _sigmoid(a0_ref[...] + _bdot(_bdot(xa, al1_ref[...]), al2_ref[...]))
    gg_ref[...] = _bdot(_sigmoid(_bdot(xg, gl1_ref[...])), gl2_ref[...]).astype(gg_ref.dtype)


def _rw_pre(h, g, mu, w_r, w_k, w_v, w0, w_l1, w_l2, a0, a_l1, a_l2, g_l1, g_l2, lp, v_first, v_res):
    tp, d = h.shape
    tm = _pick_tile(tp, min(512, lp))
    has_vres = v_res is not None
    row = pl.BlockSpec((tm, d), lambda i: (i, 0))
    prev8 = pl.BlockSpec((8, d), lambda i: (jnp.maximum(i * (tm // 8) - 1, 0), 0))
    vec = _const_spec((1, d))
    mu8 = jnp.zeros((8, d), F32).at[:6].set(mu)
    bf = lambda w: w.astype(BF16)
    ins = [h, h, g.reshape(1, d), mu8, bf(w_r), bf(w_k), bf(w_v), w0.reshape(1, d), bf(w_l1), bf(w_l2),
           a0.reshape(1, d), bf(a_l1), bf(a_l2), bf(g_l1), bf(g_l2)]
    specs = [row, prev8, vec, _const_spec((8, d))] + [_const_spec((d, d))] * 3 + [
        vec, _const_spec(w_l1.shape), _const_spec(w_l2.shape),
        vec, _const_spec(a_l1.shape), _const_spec(a_l2.shape), _const_spec(g_l1.shape), _const_spec(g_l2.shape)]
    if has_vres:
        v0, v_l1, v_l2 = v_res
        ins += [v_first, v0.reshape(1, d), bf(v_l1), bf(v_l2)]
        specs += [row, vec, _const_spec(v_l1.shape), _const_spec(v_l2.shape)]
    return pl.pallas_call(
        functools.partial(_rw_pre_kernel, tm=tm, lp=lp, has_vres=has_vres),
        grid=(tp // tm,),
        in_specs=specs,
        out_specs=[row] * 6,
        out_shape=[jax.ShapeDtypeStruct((tp, d), dt) for dt in (BF16, BF16, BF16, F32, F32, BF16)],
        compiler_params=_cparams(("parallel",)),
        name="rwkv_pre",
    )(*ins)


def _rw_scan_kernel(r_ref, w_ref, k_ref, v_ref, a_ref, kk_ref, ka_ref, rk_ref, lw_ref, lb_ref,
                    o_ref, g_scr, h_scr, q_scr, y0_scr, *, n_chunks, unroll, pairs):
    c = RW_CHUNK
    nl = 2 * RW_HEAD
    c2 = 2 * c
    ri = lax.broadcasted_iota(jnp.int32, (c2, nl), 0)
    ci = lax.broadcasted_iota(jnp.int32, (c2, nl), 1)
    own = ((ri >= c) == (ci >= RW_HEAD)).astype(F32)
    t_in = jnp.bitwise_and(ri, c - 1)
    s_in = jnp.bitwise_and(ci, c - 1)
    strict = jnp.where(s_in < t_in, own, 0.0)
    incl = jnp.where(s_in <= t_in, own, 0.0)
    causal2 = jnp.concatenate([strict, incl], axis=0)
    eye_l = (ri == ci).astype(F32)
    same_head = own.astype(BF16)
    ti = lax.broadcasted_iota(jnp.int32, (c, c), 0)
    si = lax.broadcasted_iota(jnp.int32, (c, c), 1)
    tril_c = (si <= ti).astype(BF16)
    lanes = lambda p: slice(p * nl, (p + 1) * nl)

    def stack(x):
        return jnp.concatenate([x, x], axis=0) * own

    def offset(ch):
        return ch * c if isinstance(ch, int) else pl.multiple_of(ch * c, c)

    def groups(fn):
        for p in range(pairs):
            def body(i, _):
                fn([i * unroll + q for q in range(unroll)], p)
                return 0
            lax.fori_loop(0, n_chunks // unroll, body, 0)
            if n_chunks % unroll:
                fn(list(range(n_chunks - n_chunks % unroll, n_chunks)), p)

    def prep(chs, p):
        ln = lanes(p)
        kk_w, ka_w = kk_ref[:, ln], ka_ref[:, ln]
        offs = [offset(ch) for ch in chs]
        r = [r_ref[pl.ds(o, c), ln] for o in offs]
        k = [k_ref[pl.ds(o, c), ln] for o in offs]
        v = [v_ref[pl.ds(o, c), ln] for o in offs]
        a = [a_ref[pl.ds(o, c), ln] for o in offs]
        logw = [-jnp.exp(w_ref[pl.ds(o, c), ln]) for o in offs]
        kk = [x * kk_w for x in k]
        ss = _split_dot_many([x * x for x in kk], same_head, pieces=2)
        kk = [x / jnp.maximum(jnp.sqrt(q), 1e-12) for x, q in zip(kk, ss)]
        kmod = [x * (1.0 + (y - 1.0) * ka_w) for x, y in zip(k, a)]
        cum = _split_dot_many(logw, tril_c, rhs=True, pieces=2)
        p_incl = [jnp.exp(x) for x in cum]
        p_inv = [jnp.exp(-x) for x in cum]
        p_end = [jnp.exp(x[c - 1:c, :] - x) for x in cum]
        kka = [x * y for x, y in zip(kk, a)]
        a_s = [stack(-x * jnp.exp(y - z)) for x, y, z in zip(kk, cum, logw)]
        r_s = [stack(x * y) for x, y in zip(r, p_incl)]
        v_s = [stack(x).astype(BF16) for x in v]
        lhs = [jnp.concatenate([x, y], axis=0).astype(BF16) for x, y in zip(a_s, r_s)]
        rhs = [jnp.concatenate([stack(x * z), stack(y * z)], axis=0).astype(BF16) for x, y, z in zip(kka, kmod, p_inv)]
        big = [lax.dot_general(x, y, (((1,), (1,)), ((), ())), preferred_element_type=F32) for x, y in zip(lhs, rhs)]
        lpow = [(x[:c2, :c2] * strict).astype(BF16) for x in big]
        a_rb = [x[c2:, :c2] * incl for x in big]
        avk = [_bdot(x[:, c2:] * causal2, w) for x, w in zip(big, v_s)]
        x = [jnp.concatenate([p, q[:c2]], axis=1) for p, q in zip(a_s, avk)]
        x = [p + _bdot(q, p) for p, q in zip(x, lpow)]
        for _ in range(5):
            lpow = [jnp.dot(q, q, preferred_element_type=F32).astype(BF16) for q in lpow]
            x = [p + _bdot(q, p) for p, q in zip(x, lpow)]
        xb = [p.astype(BF16) for p in x]
        bh_t = [stack(p * q).T for p, q in zip(kka, p_end)]
        kh_t = [stack(p * q).T for p, q in zip(kmod, p_end)]
        both = [_bdot(jnp.concatenate([u, p], axis=0), z) for u, p, z in zip(a_rb, bh_t, xb)]
        qy = [jnp.concatenate([p, q[c2:]], axis=1) + z[:c2] for p, q, z in zip(r_s, avk, both)]
        gh = [z[c2:] + jnp.concatenate([eye_l * q[c - 1:c, :], _bdot(u, w)], axis=1)
              for z, q, u, w in zip(both, p_incl, kh_t, v_s)]
        for ch, u, q in zip(chs, qy, gh):
            q_scr[p, ch] = u[:c, :nl] + u[c:, :nl]
            y0_scr[p, ch] = u[:c, nl:] + u[c:, nl:]
            g_scr[p, ch] = q[:, :nl]
            h_scr[p, ch] = q[:, nl:]

    groups(prep)

    def advance(ch, states):
        both = _dot3_many([jnp.concatenate([q_scr[p, ch], g_scr[p, ch]], axis=0) for p in range(pairs)], states)
        return (tuple(both[p][c:] + h_scr[p, ch] for p in range(pairs)),
                tuple(both[p][:c] + y0_scr[p, ch] for p in range(pairs)))

    def finish(ch, ys):
        o = offset(ch)
        rk = []
        for p in range(pairs):
            ln = lanes(p)
            kmod = k_ref[pl.ds(o, c), ln] * (1.0 + (a_ref[pl.ds(o, c), ln] - 1.0) * ka_ref[:, ln])
            rk.append(r_ref[pl.ds(o, c), ln] * kmod * rk_ref[:, ln])
        sums = _split_dot_many([jnp.concatenate([u, u * u, q], axis=0) for u, q in zip(ys, rk)], same_head)
        for p, (u, s) in enumerate(zip(ys, sums)):
            ln = lanes(p)
            mean = s[:c] * (1.0 / RW_HEAD)
            var = s[c:2 * c] * (1.0 / RW_HEAD) - mean * mean
            o_ref[pl.ds(o, c), ln] = ((u - mean) * lax.rsqrt(var + LNX_EPS) * lw_ref[:, ln] + lb_ref[:, ln]
                                      + s[2 * c:] * v_ref[pl.ds(o, c), ln])

    def step(ch, carry):
        states, ys = carry
        finish(ch - 1, ys)
        return advance(ch, states)

    carry = advance(0, tuple(jnp.zeros((nl, nl), F32) for _ in range(pairs)))
    _, ys = lax.fori_loop(1, n_chunks, step, carry)
    finish(n_chunks - 1, ys)


def _rw_scan(r, w, k, v, a, k_k, k_a, r_k, lnx_w, lnx_b, bsz, lp):
    tp, d = r.shape
    nl = 2 * RW_HEAD
    pairs = RW_PAIRS
    bw = pairs * nl
    nhp = d // bw
    nc = lp // RW_CHUNK
    seq = pl.BlockSpec((lp, bw), lambda b, hp: (b, hp))
    vec = pl.BlockSpec((1, bw), lambda b, hp: (0, hp))
    c = RW_CHUNK
    return pl.pallas_call(
        functools.partial(_rw_scan_kernel, n_chunks=nc, unroll=RW_UNROLL, pairs=pairs),
        grid=(bsz, nhp),
        in_specs=[seq] * 5 + [vec] * 5,
        out_specs=seq,
        out_shape=jax.ShapeDtypeStruct((tp, d), F32),
        scratch_shapes=[pltpu.VMEM((pairs, nc, nl, nl), F32), pltpu.VMEM((pairs, nc, nl, nl), F32),
                        pltpu.VMEM((pairs, nc, c, nl), F32), pltpu.VMEM((pairs, nc, c, nl), F32)],
        compiler_params=_cparams(("parallel", "parallel")),
        name="rwkv_scan",
    )(r, w, k, v, a, k_k.reshape(1, d), k_a.reshape(1, d), r_k.reshape(1, d),
      lnx_w.reshape(1, d), lnx_b.reshape(1, d))


def _rw_out_kernel(y_ref, g_ref, h_ref, wo_ref, gf_ref, wr_ref, br_ref, o_ref, xn_ref, rt_ref):
    h = h_ref[...] + _bdot(y_ref[...] * g_ref[...], wo_ref[...])
    o_ref[...] = h
    xn, rt_ref[...] = _route_tile(h, gf_ref[...], wr_ref[0], wr_ref[1], br_ref[...])
    xn_ref[...] = xn.astype(xn_ref.dtype)


def _rw_out(y, g, h, w_o, g_ffn, router):
    tp, d = h.shape
    wr, br = router
    tm = _pick_tile(tp, 512)
    row = lambda n: pl.BlockSpec((tm, n), lambda i: (i, 0))
    return pl.pallas_call(
        _rw_out_kernel,
        grid=(tp // tm,),
        in_specs=[row(d), row(d), row(d), _const_spec((d, d)), _const_spec((1, d)), _const_spec(wr.shape),
                  _const_spec(br.shape)],
        out_specs=[row(d), row(d), row(ROUTE_LANES)],
        out_shape=[jax.ShapeDtypeStruct((tp, d), F32), jax.ShapeDtypeStruct((tp, d), F32),
                   jax.ShapeDtypeStruct((tp, ROUTE_LANES), F32)],
        compiler_params=_cparams(("parallel",)),
        name="rwkv_out",
    )(y, g, h, w_o.astype(BF16), g_ffn.reshape(1, d), wr, br)


def _final_norm_kernel(h_ref, g_ref, o_ref, *, first, rows):
    o_ref[0] = _rms(h_ref[first:first + rows, :], g_ref[...]).astype(o_ref.dtype)


def _final_norm(h, g, dtype, bsz, lp, first, rows):
    d = h.shape[1]
    return pl.pallas_call(
        functools.partial(_final_norm_kernel, first=first, rows=rows),
        grid=(bsz,),
        in_specs=[pl.BlockSpec((lp, d), lambda b: (b, 0)), _const_spec((1, d))],
        out_specs=pl.BlockSpec((1, rows, d), lambda b: (b, 0, 0)),
        out_shape=jax.ShapeDtypeStruct((bsz, rows, d), dtype),
        compiler_params=_cparams(("parallel",)),
        name="final_norm",
    )(h, g.reshape(1, d))


def kernel(x, meta_tokens, norm_mix, norm_ffn, norm_final, ab_w_in, ab_w_out, ab_norm_a, ab_norm_b, s5_lam_re, s5_lam_im, s5_log_dt, s5_b_re, s5_b_im, s5_c_re, s5_c_im, s5_d, s5_w_glu, s5_b_glu, lru_conv_w, lru_conv_b, lru_w_a, lru_b_a, lru_w_x, lru_b_x, lru_lam, rw_mu, rw_w_r, rw_w_k, rw_w_v, rw_w_o, rw_w0, rw_w_l1, rw_w_l2, rw_a0, rw_a_l1, rw_a_l2, rw_v0, rw_v_l1, rw_v_l2, rw_g_l1, rw_g_l2, rw_k_k, rw_k_a, rw_r_k, rw_lnx_w, rw_lnx_b, moe_router_g, moe_router_g_b, moe_router_e, moe_router_e_b, moe_w_gate, moe_w_up, moe_w_down):
    bsz, seq, d = x.shape
    n_meta = meta_tokens.shape[0]
    depth = norm_mix.shape[0]
    ltot = n_meta + seq
    lp = -(-ltot // SEQ_ALIGN) * SEQ_ALIGN
    s5w = s5_w_glu.shape[-1]
    lruw = lru_lam.shape[-1]
    meta = jnp.broadcast_to(meta_tokens.astype(F32)[None], (bsz, n_meta, d))
    h = jnp.concatenate([meta, x.astype(F32), jnp.zeros((bsz, lp - ltot, d), F32)], axis=1).reshape(bsz * lp, d)
    v_first = None
    flat = lambda a: a.reshape((-1,) + a.shape[2:])
    s5_tables = _s5_tables(flat(s5_lam_re), flat(s5_lam_im), flat(s5_log_dt), flat(s5_b_re), flat(s5_b_im),
                           flat(s5_c_re), flat(s5_c_im), flat(s5_d))
    for layer in range(depth):
        j = layer // 2
        router = _router_table(moe_router_g[layer], moe_router_g_b[layer], moe_router_e[layer], moe_router_e_b[layer])
        if layer % 2 == 0:
            u, gate, rec = _ab_in(h, norm_mix[layer], ab_w_in[j], s5w, lruw)
            y5 = _s5_scan(u, s5_tables, bsz, lp, j * (s5w // S5_LANES))
            lru = _lru(rec, gate, lru_conv_w[j], lru_conv_b[j], lru_w_a[j], lru_b_a[j], lru_w_x[j], lru_b_x[j],
                       lru_lam[j], bsz, lp)
            h, xn, rt = _ab_out(y5, lru, h, s5_w_glu[j], s5_b_glu[j], ab_norm_a[j], ab_norm_b[j], ab_w_out[j],
                                norm_ffn[layer], router)
        else:
            v_res = (rw_v0[j - 1], rw_v_l1[j - 1], rw_v_l2[j - 1]) if j > 0 else None
            r, k, v, w, a, g = _rw_pre(h, norm_mix[layer], rw_mu[j], rw_w_r[j], rw_w_k[j], rw_w_v[j], rw_w0[j],
                                       rw_w_l1[j], rw_w_l2[j], rw_a0[j], rw_a_l1[j], rw_a_l2[j], rw_g_l1[j],
                                       rw_g_l2[j], lp, v_first, v_res)
            if v_first is None:
                v_first = v
            y = _rw_scan(r, w, k, v, a, rw_k_k[j], rw_k_a[j], rw_r_k[j].reshape(-1), rw_lnx_w[j], rw_lnx_b[j],
                         bsz, lp)
            h, xn, rt = _rw_out(y, g, h, rw_w_o[j], norm_ffn[layer], router)
        h = _moe(h, xn, rt, moe_w_gate, moe_w_up, moe_w_down, layer)
    return _final_norm(h, norm_final, x.dtype, bsz, lp, n_meta, seq)
```
